```python
import math
import jax, jax.numpy as jnp
from jax import lax
import numpy as np

D_MODEL = 2048
BATCH = 2
SEQ = 4096
DEPTH = 4
DEC_BATCH = 32
DEC_SEQ = 8
PAST_LEN = 16384
PAGE_SIZE = 128

HEAD_DIM = 64
N_HEADS = D_MODEL // HEAD_DIM
N_KV_HEADS = N_HEADS // 8
GQA_GROUP = N_HEADS // N_KV_HEADS
WINDOW = 128
ATTN_BLOCK = WINDOW
ROPE_THETA = 10000.0
SSM_WIDTH = D_MODEL
SSM_GROUP = 16
N_SSM_GROUPS = SSM_WIDTH // SSM_GROUP
SSM_STATE = 64
LOG_STEP_MIN = math.log(1e-3)
LOG_STEP_MAX = math.log(1e-1)
N_MEM = 256
N_XHEADS = 4
XHEAD_DIM = 128
D_FF = 5632
CONV_W = 3
N_ATTN_LAYERS = (DEPTH + 1) // 2
N_SSM_LAYERS = DEPTH // 2
NORM_EPS = 1e-6
NEG_INF = -1e30

kernel_name = 'swa_sink_s5_memxattn_convffn_step'


def rmsnorm(x, g):
    x32 = x.astype(jnp.float32)
    y = x32 * lax.rsqrt(jnp.mean(x32 * x32, axis=-1, keepdims=True) + NORM_EPS)
    return (y * g.astype(jnp.float32)).astype(x.dtype)


def rope(x, pos):
    half = HEAD_DIM // 2
    inv_freq = ROPE_THETA ** (-jnp.arange(half, dtype=jnp.float32) * 2.0 / HEAD_DIM)
    ang = pos.astype(jnp.float32)[:, None] * inv_freq[None, :]
    cos = jnp.cos(ang)[:, None, :]
    sin = jnp.sin(ang)[:, None, :]
    x1 = x[..., :half].astype(jnp.float32)
    x2 = x[..., half:].astype(jnp.float32)
    return jnp.concatenate([x1 * cos - x2 * sin, x2 * cos + x1 * sin], axis=-1).astype(x.dtype)


def split_qkv(h, w_qkv):
    B, L, _ = h.shape
    qkv = h @ w_qkv
    nq = N_HEADS * HEAD_DIM
    nk = N_KV_HEADS * HEAD_DIM
    q = qkv[..., :nq].reshape(B, L, N_HEADS, HEAD_DIM)
    k = qkv[..., nq:nq + nk].reshape(B, L, N_KV_HEADS, HEAD_DIM)
    v = qkv[..., nq + nk:].reshape(B, L, N_KV_HEADS, HEAD_DIM)
    return q, k, v


def sink_attention(q, k, v, mask, sinks):
    s = jnp.einsum('...qkgd,...skd->...kgqs', q.astype(jnp.float32), k.astype(jnp.float32)) * (HEAD_DIM ** -0.5)
    s = jnp.where(mask, s, NEG_INF)
    sk = jnp.broadcast_to(sinks.astype(jnp.float32).reshape(N_KV_HEADS, GQA_GROUP, 1, 1), s.shape[:-1] + (1,))
    p = jax.nn.softmax(jnp.concatenate([s, sk], axis=-1), axis=-1)[..., :-1]
    return jnp.einsum('...kgqs,...skd->...qkgd', p, v.astype(jnp.float32))


def swa_prompt(h, w_qkv, w_o, sinks):
    B, L, _ = h.shape
    q, k, v = split_qkv(h, w_qkv)
    pos = jnp.arange(L)
    q = rope(q, pos)
    k = rope(k, pos)
    nb = L // ATTN_BLOCK
    qb = q.reshape(B, nb, ATTN_BLOCK, N_KV_HEADS, GQA_GROUP, HEAD_DIM)
    kb = k.reshape(B, nb, ATTN_BLOCK, N_KV_HEADS, HEAD_DIM)
    vb = v.reshape(B, nb, ATTN_BLOCK, N_KV_HEADS, HEAD_DIM)

    def with_prev(t):
        prev = jnp.pad(t[:, :-1], ((0, 0), (1, 0), (0, 0), (0, 0), (0, 0)))
        return jnp.concatenate([prev, t], axis=2)

    kk, vv = with_prev(kb), with_prev(vb)
    qi = jnp.arange(ATTN_BLOCK)[:, None]
    si = jnp.arange(2 * ATTN_BLOCK)[None, :]
    d = ATTN_BLOCK + qi - si
    kpos = (jnp.arange(nb)[:, None, None] - 1) * ATTN_BLOCK + si[None]
    mask = (d >= 0)[None] & (d < WINDOW)[None] & (kpos >= 0)
    o = sink_attention(qb, kk, vv, mask[None, :, None, None], sinks)
    y = o.reshape(B, L, N_HEADS * HEAD_DIM).astype(h.dtype) @ w_o
    wb = min(WINDOW, L)
    return y, k[:, L - wb:], v[:, L - wb:]


def swa_sample(h, ck, cv, w_qkv, w_o, sinks):
    B, L, _ = h.shape
    wb = ck.shape[1]
    q, k, v = split_qkv(h, w_qkv)
    pos = PAST_LEN + jnp.arange(L)
    q = rope(q, pos)
    k = rope(k, pos)
    kk = jnp.concatenate([ck.astype(k.dtype), k], axis=1)
    vv = jnp.concatenate([cv.astype(v.dtype), v], axis=1)
    kpos = jnp.concatenate([PAST_LEN - wb + jnp.arange(wb), pos])
    d = pos[:, None] - kpos[None, :]
    mask = (d >= 0) & (d < WINDOW)
    o = sink_attention(q.reshape(B, L, N_KV_HEADS, GQA_GROUP, HEAD_DIM), kk, vv, mask, sinks)
    y = o.reshape(B, L, N_HEADS * HEAD_DIM).astype(h.dtype) @ w_o
    return y, kk[:, -wb:], vv[:, -wb:]


def _lin_combine(e1, e2):
    a1, b1 = e1
    a2, b2 = e2
    return a1 * a2, a2 * b1 + b2


def s5_mix(h, h0_re, h0_im, w_in, lam_re, lam_im, log_step, b_re, b_im, c_re, c_im, d_skip, w_glu):
    B, L, _ = h.shape
    f32 = jnp.float32
    u = (h @ w_in).astype(f32).reshape(B, L, N_SSM_GROUPS, SSM_GROUP)
    lam = lax.complex(lam_re.astype(f32), lam_im.astype(f32))
    dt = jnp.exp(log_step.astype(f32))[:, None]
    abar = jnp.exp(lam * dt)
    bbar = ((abar - 1.0) / lam)[..., None] * lax.complex(b_re.astype(f32), b_im.astype(f32))
    bu = jnp.einsum('gph,blgh->blgp', bbar, u.astype(jnp.complex64))
    h0 = lax.complex(h0_re.astype(f32), h0_im.astype(f32))
    bu = bu.at[:, 0].add(abar * h0)
    a = jnp.broadcast_to(abar, (1, L) + abar.shape)
    _, hs = lax.associative_scan(_lin_combine, (a, bu), axis=1)
    cc = lax.complex(c_re.astype(f32), c_im.astype(f32))
    y = jnp.einsum('ghp,blgp->blgh', cc, hs).real + d_skip.astype(f32) * u
    z = jax.nn.gelu(y.reshape(B, L, SSM_WIDTH)).astype(h.dtype)
    val, gate = jnp.split(z @ w_glu, 2, axis=-1)
    out = val * jax.nn.sigmoid(gate)
    h_last = hs[:, -1]
    return out, h_last.real, h_last.imag


def memory_kv(mem, g_mem, w_k, w_v):
    B, M, _ = mem.shape
    mn = rmsnorm(mem, g_mem)
    k = (mn @ w_k).reshape(B, M, N_XHEADS, XHEAD_DIM)
    v = (mn @ w_v).reshape(B, M, N_XHEADS, XHEAD_DIM)
    return k, v


def cross_attend(h, mk, mv, w_q, w_o):
    B, L, _ = h.shape
    q = (h @ w_q).reshape(B, L, N_XHEADS, XHEAD_DIM).astype(jnp.float32)
    s = jnp.einsum('blhd,bmhd->bhlm', q, mk.astype(jnp.float32)) * (XHEAD_DIM ** -0.5)
    p = jax.nn.softmax(s, axis=-1)
    o = jnp.einsum('bhlm,bmhd->blhd', p, mv.astype(jnp.float32))
    return o.reshape(B, L, N_XHEADS * XHEAD_DIM).astype(h.dtype) @ w_o


def conv_ffn(h, conv_state, w_up, conv_w, conv_b, w_down):
    L = h.shape[1]
    up = h @ w_up
    ext = jnp.concatenate([conv_state.astype(up.dtype), up], axis=1)
    c = conv_b + sum(conv_w[j] * ext[:, j:j + L] for j in range(CONV_W))
    a, v = jnp.split(c, 2, axis=-1)
    return (jax.nn.silu(a) * v) @ w_down, ext[:, L:]


def setup_inputs(seed: int = 0) -> dict:
    key = jax.random.key(seed)
    ks = iter(jax.random.split(key, 48))
    f32 = jnp.float32

    def nrm(shape, scale):
        return jax.random.normal(next(ks), shape, f32) * scale

    def gain(shape):
        return 1.0 + nrm(shape, 0.05)

    wb = min(WINDOW, PAST_LEN)
    qkv_cols = (N_HEADS + 2 * N_KV_HEADS) * HEAD_DIM
    xw = N_XHEADS * XHEAD_DIM
    lam_im = jnp.pi * jnp.arange(SSM_STATE, dtype=f32)[None, None, :] + nrm((N_SSM_LAYERS, N_SSM_GROUPS, SSM_STATE), 0.01)
    return {
        'x_prompt': nrm((BATCH, SEQ, D_MODEL), 1.0),
        'x_sample': nrm((DEC_BATCH, DEC_SEQ, D_MODEL), 1.0),
        'mem_prompt': nrm((BATCH, N_MEM, D_MODEL), 1.0),
        'cache_swa_k': nrm((N_ATTN_LAYERS, DEC_BATCH, wb, N_KV_HEADS, HEAD_DIM), 1.0),
        'cache_swa_v': nrm((N_ATTN_LAYERS, DEC_BATCH, wb, N_KV_HEADS, HEAD_DIM), 1.0),
        'state_ssm_re': nrm((N_SSM_LAYERS, DEC_BATCH, N_SSM_GROUPS, SSM_STATE), 0.1),
        'state_ssm_im': nrm((N_SSM_LAYERS, DEC_BATCH, N_SSM_GROUPS, SSM_STATE), 0.1),
        'state_ffn_conv': nrm((DEPTH, DEC_BATCH, CONV_W - 1, 2 * D_FF), 1.0),
        'cache_mem_k': nrm((DEPTH, DEC_BATCH, N_MEM, N_XHEADS, XHEAD_DIM), 1.0),
        'cache_mem_v': nrm((DEPTH, DEC_BATCH, N_MEM, N_XHEADS, XHEAD_DIM), 1.0),
        'g_mix_pre': gain((DEPTH, D_MODEL)),
        'g_mix_post': gain((DEPTH, D_MODEL)),
        'w_qkv': nrm((N_ATTN_LAYERS, D_MODEL, qkv_cols), D_MODEL ** -0.5),
        'w_attn_o': nrm((N_ATTN_LAYERS, N_HEADS * HEAD_DIM, D_MODEL), (N_HEADS * HEAD_DIM) ** -0.5),
        'attn_sinks': nrm((N_ATTN_LAYERS, N_HEADS), 1.0),
        'w_ssm_in': nrm((N_SSM_LAYERS, D_MODEL, SSM_WIDTH), D_MODEL ** -0.5),
        'ssm_lambda_re': -0.5 + nrm((N_SSM_LAYERS, N_SSM_GROUPS, SSM_STATE), 0.01),
        'ssm_lambda_im': lam_im,
        'ssm_log_step': jax.random.uniform(next(ks), (N_SSM_LAYERS, N_SSM_GROUPS), f32, LOG_STEP_MIN, LOG_STEP_MAX),
        'ssm_b_re': nrm((N_SSM_LAYERS, N_SSM_GROUPS, SSM_STATE, SSM_GROUP), (2 * SSM_GROUP) ** -0.5),
        'ssm_b_im': nrm((N_SSM_LAYERS, N_SSM_GROUPS, SSM_STATE, SSM_GROUP), (2 * SSM_GROUP) ** -0.5),
        'ssm_c_re': nrm((N_SSM_LAYERS, N_SSM_GROUPS, SSM_GROUP, SSM_STATE), SSM_STATE ** -0.5),
        'ssm_c_im': nrm((N_SSM_LAYERS, N_SSM_GROUPS, SSM_GROUP, SSM_STATE), SSM_STATE ** -0.5),
        'ssm_d': nrm((N_SSM_LAYERS, N_SSM_GROUPS, SSM_GROUP), 1.0),
        'w_ssm_glu': nrm((N_SSM_LAYERS, SSM_WIDTH, 2 * D_MODEL), SSM_WIDTH ** -0.5),
        'g_x_pre': gain((DEPTH, D_MODEL)),
        'g_x_post': gain((DEPTH, D_MODEL)),
        'g_mem': gain((DEPTH, D_MODEL)),
        'w_x_q': nrm((DEPTH, D_MODEL, xw), D_MODEL ** -0.5),
        'w_mem_k': nrm((DEPTH, D_MODEL, xw), D_MODEL ** -0.5),
        'w_mem_v': nrm((DEPTH, D_MODEL, xw), D_MODEL ** -0.5),
        'w_x_o': nrm((DEPTH, xw, D_MODEL), xw ** -0.5),
        'g_ffn_pre': gain((DEPTH, D_MODEL)),
        'g_ffn_post': gain((DEPTH, D_MODEL)),
        'w_ffn_up': nrm((DEPTH, D_MODEL, 2 * D_FF), D_MODEL ** -0.5),
        'ffn_conv_w': nrm((DEPTH, CONV_W, 2 * D_FF), CONV_W ** -0.5),
        'ffn_conv_b': nrm((DEPTH, 2 * D_FF), 0.02),
        'w_ffn_down': nrm((DEPTH, D_FF, D_MODEL), D_FF ** -0.5),
    }


def reference(x_prompt, x_sample, mem_prompt, cache_swa_k, cache_swa_v, state_ssm_re, state_ssm_im,
              state_ffn_conv, cache_mem_k, cache_mem_v, g_mix_pre, g_mix_post, w_qkv, w_attn_o, attn_sinks,
              w_ssm_in, ssm_lambda_re, ssm_lambda_im, ssm_log_step, ssm_b_re, ssm_b_im, ssm_c_re, ssm_c_im,
              ssm_d, w_ssm_glu, g_x_pre, g_x_post, g_mem, w_x_q, w_mem_k, w_mem_v, w_x_o,
              g_ffn_pre, g_ffn_post, w_ffn_up, ffn_conv_w, ffn_conv_b, w_ffn_down):
    xp, xs = x_prompt, x_sample
    swa_kp, swa_vp, swa_ks, swa_vs = [], [], [], []
    ssm_rp, ssm_ip, ssm_rs, ssm_is = [], [], [], []
    conv_p, conv_s, memk_p, memv_p = [], [], [], []
    for i in range(DEPTH):
        j = i // 2
        hp = rmsnorm(xp, g_mix_pre[i])
        hs = rmsnorm(xs, g_mix_pre[i])
        if i % 2 == 0:
            op, kp, vp = swa_prompt(hp, w_qkv[j], w_attn_o[j], attn_sinks[j])
            os_, kn, vn = swa_sample(hs, cache_swa_k[j], cache_swa_v[j], w_qkv[j], w_attn_o[j], attn_sinks[j])
            swa_kp.append(kp); swa_vp.append(vp); swa_ks.append(kn); swa_vs.append(vn)
        else:
            ssm_w = (w_ssm_in[j], ssm_lambda_re[j], ssm_lambda_im[j], ssm_log_step[j], ssm_b_re[j], ssm_b_im[j],
                     ssm_c_re[j], ssm_c_im[j], ssm_d[j], w_ssm_glu[j])
            zero_state = jnp.zeros((xp.shape[0], N_SSM_GROUPS, SSM_STATE), jnp.float32)
            op, rp, ip = s5_mix(hp, zero_state, zero_state, *ssm_w)
            os_, rn, im_ = s5_mix(hs, state_ssm_re[j], state_ssm_im[j], *ssm_w)
            ssm_rp.append(rp); ssm_ip.append(ip); ssm_rs.append(rn); ssm_is.append(im_)
        xp = xp + rmsnorm(op, g_mix_post[i])
        xs = xs + rmsnorm(os_, g_mix_post[i])
        mk, mv = memory_kv(mem_prompt, g_mem[i], w_mem_k[i], w_mem_v[i])
        memk_p.append(mk); memv_p.append(mv)
        xp = xp + rmsnorm(cross_attend(rmsnorm(xp, g_x_pre[i]), mk, mv, w_x_q[i], w_x_o[i]), g_x_post[i])
        xs = xs + rmsnorm(cross_attend(rmsnorm(xs, g_x_pre[i]), cache_mem_k[i], cache_mem_v[i], w_x_q[i], w_x_o[i]), g_x_post[i])
        zero_conv = jnp.zeros((xp.shape[0], CONV_W - 1, 2 * D_FF), xp.dtype)
        fp, cp = conv_ffn(rmsnorm(xp, g_ffn_pre[i]), zero_conv, w_ffn_up[i], ffn_conv_w[i], ffn_conv_b[i], w_ffn_down[i])
        fs, cs = conv_ffn(rmsnorm(xs, g_ffn_pre[i]), state_ffn_conv[i], w_ffn_up[i], ffn_conv_w[i], ffn_conv_b[i], w_ffn_down[i])
        conv_p.append(cp); conv_s.append(cs)
        xp = xp + rmsnorm(fp, g_ffn_post[i])
        xs = xs + rmsnorm(fs, g_ffn_post[i])
    return (xp, xs,
            jnp.stack(swa_kp), jnp.stack(swa_vp), jnp.stack(swa_ks), jnp.stack(swa_vs),
            jnp.stack(ssm_rp), jnp.stack(ssm_ip), jnp.stack(ssm_rs), jnp.stack(ssm_is),
            jnp.stack(conv_p), jnp.stack(conv_s), jnp.stack(memk_p), jnp.stack(memv_p))
```

```python
import functools
import math

import jax
import jax.numpy as jnp
from jax import lax
from jax.experimental import pallas as pl
from jax.experimental.pallas import tpu as pltpu

D_MODEL = 2048
SEQ = 4096
DEPTH = 4
DEC_SEQ = 8
PAST_LEN = 16384
HEAD_DIM = 64
N_HEADS = 32
N_KV_HEADS = 4
GQA_GROUP = 8
WINDOW = 128
ROPE_THETA = 10000.0
SSM_GROUP = 16
N_SSM_GROUPS = 128
SSM_STATE = 64
N_MEM = 256
N_XHEADS = 4
XHEAD_DIM = 128
D_FF = 5632
NORM_EPS = 1e-6
NEG_INF = -1e30

F32 = jnp.float32
BF16 = jnp.bfloat16

V7X_VMEM_BYTES = 64 * 1024 * 1024
VMEM_LIMIT = V7X_VMEM_BYTES - 8 * 1024 * 1024
LANES = 128
SUBLANES = 8
SSM_SUPER = 16
N_SUPER = N_SSM_GROUPS // SSM_SUPER
SUPER_IN = SSM_SUPER * SSM_GROUP
SUPER_STATE = SSM_SUPER * SSM_STATE
SLABS = SUPER_STATE // LANES


def _params(*sem):
    return pltpu.CompilerParams(dimension_semantics=sem, vmem_limit_bytes=VMEM_LIMIT)


def _rms(x, g):
    ms = jnp.mean(x * x, axis=-1, keepdims=True)
    return x * lax.rsqrt(ms + NORM_EPS) * g


def _bdot(a, b):
    return jnp.dot(a.astype(BF16), b.astype(BF16), preferred_element_type=F32)


def _once(shape, index_map):
    return pl.BlockSpec(shape, index_map, pipeline_mode=pl.Buffered(1))


def _norm_matmul_kernel(x_ref, g_ref, w_ref, o_ref, hn_ref):
    @pl.when(pl.program_id(1) == 0)
    def _():
        hn_ref[...] = _rms(x_ref[...], g_ref[...]).astype(BF16)

    o_ref[...] = jnp.dot(hn_ref[...], w_ref[...].astype(BF16),
                         preferred_element_type=F32).astype(o_ref.dtype)


def norm_matmul(x, g, layer, w, tm, tn, out_dtype=F32):
    g_layer, w_layer = layer
    M, D = x.shape
    N = w.shape[-1]
    return pl.pallas_call(
        _norm_matmul_kernel,
        grid=(M // tm, N // tn),
        in_specs=[
            pl.BlockSpec((tm, D), lambda i, j: (i, 0)),
            pl.BlockSpec((None, 1, D), lambda i, j: (g_layer, 0, 0)),
            pl.BlockSpec((None, D, tn), lambda i, j: (w_layer, 0, j)),
        ],
        out_specs=pl.BlockSpec((tm, tn), lambda i, j: (i, j)),
        out_shape=jax.ShapeDtypeStruct((M, N), out_dtype),
        scratch_shapes=[pltpu.VMEM((tm, D), BF16)],
        compiler_params=_params("arbitrary", "arbitrary"),
        name="norm_matmul",
    )(x, g.reshape(g.shape[0], 1, D), w)


def _matmul_post_kernel(a_ref, w_ref, g_ref, x_ref, o_ref, *, nk):
    k = pl.program_id(1)
    part = _bdot(a_ref[...], w_ref[...])

    @pl.when(k == 0)
    def _():
        o_ref[...] = part

    @pl.when(k > 0)
    def _():
        o_ref[...] += part

    @pl.when(k == nk - 1)
    def _():
        o_ref[...] = x_ref[...] + _rms(o_ref[...], g_ref[...])


def matmul_post(a, w, w_layer, g, g_layer, x, tm, tk):
    M, K = a.shape
    D = w.shape[-1]
    nk = K // tk
    return pl.pallas_call(
        functools.partial(_matmul_post_kernel, nk=nk),
        grid=(M // tm, nk),
        in_specs=[
            pl.BlockSpec((tm, tk), lambda i, k: (i, k)),
            pl.BlockSpec((None, tk, D), lambda i, k: (w_layer, k, 0)),
            pl.BlockSpec((None, 1, D), lambda i, k: (g_layer, 0, 0)),
            _once((tm, D), lambda i, k: (i, 0)),
        ],
        out_specs=pl.BlockSpec((tm, D), lambda i, k: (i, 0)),
        out_shape=jax.ShapeDtypeStruct((M, D), F32),
        compiler_params=_params("arbitrary", "arbitrary"),
        name="matmul_post",
    )(a, w, g.reshape(g.shape[0], 1, D), x)


def _glu_post_kernel(z_ref, wv_ref, wg_ref, g_ref, x_ref, o_ref, act_ref, *, nj):
    j = pl.program_id(1)
    z = z_ref[...]
    val = _bdot(z, wv_ref[...])
    gate = _bdot(z, wg_ref[...])
    act_ref[j] = val * jax.nn.sigmoid(gate)

    @pl.when(j == nj - 1)
    def _():
        y = jnp.concatenate([act_ref[jj] for jj in range(nj)], axis=1)
        o_ref[...] = x_ref[...] + _rms(y, g_ref[...])


def glu_post(z, w, w_layer, g, g_layer, x, tm, tn):
    M, K = z.shape
    D = D_MODEL
    nj = D // tn
    return pl.pallas_call(
        functools.partial(_glu_post_kernel, nj=nj),
        grid=(M // tm, nj),
        in_specs=[
            pl.BlockSpec((tm, K), lambda i, j: (i, 0)),
            pl.BlockSpec((None, K, tn), lambda i, j: (w_layer, 0, j)),
            pl.BlockSpec((None, K, tn), lambda i, j: (w_layer, 0, nj + j)),
            pl.BlockSpec((None, 1, D), lambda i, j: (g_layer, 0, 0)),
            _once((tm, D), lambda i, j: (i, 0)),
        ],
        out_specs=pl.BlockSpec((tm, D), lambda i, j: (i, 0)),
        out_shape=jax.ShapeDtypeStruct((M, D), F32),
        scratch_shapes=[pltpu.VMEM((nj, tm, tn), F32)],
        compiler_params=_params("arbitrary", "arbitrary"),
        name="glu_post",
    )(z, w, w, g.reshape(g.shape[0], 1, D), x)


def _rope_chunk(xc, cos, sin_signed):
    lane = lax.broadcasted_iota(jnp.int32, xc.shape, 1)
    first_half = (lane & (HEAD_DIM - 1)) < (HEAD_DIM // 2)
    partner = jnp.where(first_half, pltpu.roll(xc, LANES - HEAD_DIM // 2, 1),
                        pltpu.roll(xc, HEAD_DIM // 2, 1))
    return xc * cos + partner * sin_signed


def _rope_tables(pos):
    half = HEAD_DIM // 2
    inv_freq = ROPE_THETA ** (-jnp.arange(half, dtype=F32) * 2.0 / HEAD_DIM)
    ang = pos.astype(F32)[:, None] * inv_freq[None, :]
    cos, sin = jnp.cos(ang), jnp.sin(ang)
    cos128 = jnp.tile(cos, (1, LANES // half))
    sin128 = jnp.tile(jnp.concatenate([-sin, sin], axis=1), (1, LANES // HEAD_DIM))
    return cos128, sin128


def _swa_prompt_kernel(sink_ref, q_ref, kv_ref, cos_ref, sin_ref,
                       o_ref, kc_ref, vc_ref, kk_ref, vv_ref, *, layer):
    i = pl.program_id(1)
    blk = WINDOW
    cos, sn = cos_ref[...], sin_ref[...]
    kv = kv_ref[...]
    nkc = N_KV_HEADS * HEAD_DIM
    k = jnp.concatenate([_rope_chunk(kv[:, c * LANES:(c + 1) * LANES], cos, sn)
                         for c in range(nkc // LANES)], axis=1)
    v = kv[:, nkc:2 * nkc]
    kc_ref[...] = k
    vc_ref[...] = v

    cur = i % 2
    prv = 1 - cur

    @pl.when(i == 0)
    def _():
        kk_ref[1] = jnp.zeros((blk, nkc), BF16)
        vv_ref[1] = jnp.zeros((blk, nkc), BF16)

    kb, vb = k.astype(BF16), v.astype(BF16)
    kk_ref[cur] = kb
    vv_ref[cur] = vb
    kk = jnp.concatenate([kk_ref[prv], kb], axis=0)
    vv = jnp.concatenate([vv_ref[prv], vb], axis=0)
    qi = lax.broadcasted_iota(jnp.int32, (blk, 2 * blk), 0)
    si = lax.broadcasted_iota(jnp.int32, (blk, 2 * blk), 1)
    d = blk + qi - si
    mask = (d >= 0) & (d < WINDOW) & ((si >= blk) | (i > 0))

    outs = []
    for c in range(N_HEADS * HEAD_DIM // LANES):
        qc = _rope_chunk(q_ref[:, c * LANES:(c + 1) * LANES], cos, sn) * (HEAD_DIM ** -0.5)
        for hh in range(LANES // HEAD_DIM):
            h = c * (LANES // HEAD_DIM) + hh
            kh = h // GQA_GROUP
            qh = qc[:, hh * HEAD_DIM:(hh + 1) * HEAD_DIM].astype(BF16)
            s = lax.dot_general(qh, kk[:, kh * HEAD_DIM:(kh + 1) * HEAD_DIM],
                                (((1,), (1,)), ((), ())), preferred_element_type=F32)
            s = jnp.where(mask, s, NEG_INF)
            sk = sink_ref[layer, h]
            mx = jnp.maximum(jnp.max(s, axis=1, keepdims=True), sk)
            p = jnp.exp(s - mx)
            den = jnp.sum(p, axis=1, keepdims=True) + jnp.exp(sk - mx)
            oh = jnp.dot(p.astype(BF16), vv[:, kh * HEAD_DIM:(kh + 1) * HEAD_DIM],
                         preferred_element_type=F32)
            outs.append(oh / den)
    o_ref[...] = jnp.concatenate(outs, axis=1).astype(o_ref.dtype)


def swa_prompt(qkv, sinks, layer, cos, sin, batch):
    blk = WINDOW
    nb = SEQ // blk
    nq = N_HEADS * HEAD_DIM
    nkv = 2 * N_KV_HEADS * HEAD_DIM
    o, kc, vc = pl.pallas_call(
        functools.partial(_swa_prompt_kernel, layer=layer),
        grid=(batch, nb),
        in_specs=[
            pl.BlockSpec(memory_space=pltpu.SMEM),
            pl.BlockSpec((blk, nq), lambda b, i: (b * nb + i, 0)),
            pl.BlockSpec((blk, nkv), lambda b, i: (b * nb + i, nq // nkv)),
            pl.BlockSpec((blk, LANES), lambda b, i: (i, 0)),
            pl.BlockSpec((blk, LANES), lambda b, i: (i, 0)),
        ],
        out_specs=[
            pl.BlockSpec((blk, nq), lambda b, i: (b * nb + i, 0)),
            pl.BlockSpec((None, blk, nkv // 2), lambda b, i: (b, 0, 0)),
            pl.BlockSpec((None, blk, nkv // 2), lambda b, i: (b, 0, 0)),
        ],
        out_shape=[
            jax.ShapeDtypeStruct((batch * SEQ, nq), BF16),
            jax.ShapeDtypeStruct((batch, blk, nkv // 2), F32),
            jax.ShapeDtypeStruct((batch, blk, nkv // 2), F32),
        ],
        scratch_shapes=[pltpu.VMEM((2, blk, nkv // 2), BF16),
                        pltpu.VMEM((2, blk, nkv // 2), BF16)],
        compiler_params=_params("arbitrary", "arbitrary"),
        name="swa_prompt",
    )(sinks, qkv, qkv, cos, sin)
    return o, kc, vc


def _swa_sample_kernel(sink_ref, qkv_ref, ck_ref, cv_ref, cos_ref, sin_ref,
                       o_ref, nk_ref, nv_ref, *, nb):
    L = DEC_SEQ
    wb = WINDOW
    nq = N_HEADS * HEAD_DIM
    nkc = N_KV_HEADS * HEAD_DIM
    cos, sn = cos_ref[...], sin_ref[...]
    rows = GQA_GROUP * L
    t_q = lax.broadcasted_iota(jnp.int32, (rows, wb), 0) % L
    c_k = lax.broadcasted_iota(jnp.int32, (rows, wb), 1)
    mask_c = c_k >= t_q + 1
    t_q2 = lax.broadcasted_iota(jnp.int32, (rows, L), 0) % L
    t_k2 = lax.broadcasted_iota(jnp.int32, (rows, L), 1)
    mask_n = t_k2 <= t_q2
    for b in range(nb):
        x = qkv_ref[b * L:(b + 1) * L, :]
        qr = jnp.concatenate([_rope_chunk(x[:, c * LANES:(c + 1) * LANES], cos, sn)
                              for c in range(nq // LANES)], axis=1) * (HEAD_DIM ** -0.5)
        kn = jnp.concatenate([_rope_chunk(x[:, nq + c * LANES:nq + (c + 1) * LANES], cos, sn)
                              for c in range(nkc // LANES)], axis=1)
        vn = x[:, nq + nkc:nq + 2 * nkc]
        ck = ck_ref[b]
        cv = cv_ref[b]
        nk_ref[b, 0:wb - L, :] = ck[L:wb, :]
        nk_ref[b, wb - L:wb, :] = kn
        nv_ref[b, 0:wb - L, :] = cv[L:wb, :]
        nv_ref[b, wb - L:wb, :] = vn
        ckb, cvb, knb, vnb = ck.astype(BF16), cv.astype(BF16), kn.astype(BF16), vn.astype(BF16)
        outs = []
        for kh in range(N_KV_HEADS):
            hs = slice(kh * HEAD_DIM, (kh + 1) * HEAD_DIM)
            qs = jnp.concatenate(
                [qr[:, (kh * GQA_GROUP + j) * HEAD_DIM:(kh * GQA_GROUP + j + 1) * HEAD_DIM]
                 for j in range(GQA_GROUP)], axis=0).astype(BF16)
            dn = (((1,), (1,)), ((), ()))
            s_c = lax.dot_general(qs, ckb[:, hs], dn, preferred_element_type=F32)
            s_n = lax.dot_general(qs, knb[:, hs], dn, preferred_element_type=F32)
            s_c = jnp.where(mask_c, s_c, NEG_INF)
            s_n = jnp.where(mask_n, s_n, NEG_INF)
            sk = sink_ref[kh]
            mx = jnp.maximum(jnp.maximum(jnp.max(s_c, axis=1, keepdims=True),
                                         jnp.max(s_n, axis=1, keepdims=True)), sk)
            p_c = jnp.exp(s_c - mx)
            p_n = jnp.exp(s_n - mx)
            den = (jnp.sum(p_c, axis=1, keepdims=True) + jnp.sum(p_n, axis=1, keepdims=True)
                   + jnp.exp(sk - mx))
            o = (jnp.dot(p_c.astype(BF16), cvb[:, hs], preferred_element_type=F32)
                 + jnp.dot(p_n.astype(BF16), vnb[:, hs], preferred_element_type=F32)) / den
            outs.extend(o[j * L:(j + 1) * L, :] for j in range(GQA_GROUP))
        o_ref[b * L:(b + 1) * L, :] = jnp.concatenate(outs, axis=1).astype(o_ref.dtype)


def swa_sample(qkv, cache_k, cache_v, layer, sink_rows, cos, sin, nb=8):
    nbatch = cache_k.shape[1]
    nq = N_HEADS * HEAD_DIM
    nkc = N_KV_HEADS * HEAD_DIM
    L = DEC_SEQ
    return pl.pallas_call(
        functools.partial(_swa_sample_kernel, nb=nb),
        grid=(nbatch // nb,),
        in_specs=[
            pl.BlockSpec((N_KV_HEADS, GQA_GROUP * L, 1), lambda g: (0, 0, 0)),
            pl.BlockSpec((nb * L, nq + 2 * nkc), lambda g: (g, 0)),
            pl.BlockSpec((None, nb, WINDOW, nkc), lambda g: (layer, g, 0, 0)),
            pl.BlockSpec((None, nb, WINDOW, nkc), lambda g: (layer, g, 0, 0)),
            pl.BlockSpec((L, LANES), lambda g: (0, 0)),
            pl.BlockSpec((L, LANES), lambda g: (0, 0)),
        ],
        out_specs=[
            pl.BlockSpec((nb * L, nq), lambda g: (g, 0)),
            pl.BlockSpec((nb, WINDOW, nkc), lambda g: (g, 0, 0)),
            pl.BlockSpec((nb, WINDOW, nkc), lambda g: (g, 0, 0)),
        ],
        out_shape=[
            jax.ShapeDtypeStruct((nbatch * L, nq), BF16),
            jax.ShapeDtypeStruct((nbatch, WINDOW, nkc), F32),
            jax.ShapeDtypeStruct((nbatch, WINDOW, nkc), F32),
        ],
        compiler_params=_params("arbitrary"),
        name="swa_sample",
    )(sink_rows, qkv, cache_k, cache_v, cos, sin)


def _xattn_heads(q, mk, mv):
    outs = []
    for h in range(N_XHEADS):
        hs = slice(h * XHEAD_DIM, (h + 1) * XHEAD_DIM)
        qh = (q[:, hs] * (XHEAD_DIM ** -0.5)).astype(BF16)
        s = lax.dot_general(qh, mk[:, hs].astype(BF16), (((1,), (1,)), ((), ())),
                            preferred_element_type=F32)
        mx = jnp.max(s, axis=1, keepdims=True)
        p = jnp.exp(s - mx)
        den = jnp.sum(p, axis=1, keepdims=True)
        outs.append(jnp.dot(p.astype(BF16), mv[:, hs].astype(BF16),
                            preferred_element_type=F32) / den)
    return jnp.concatenate(outs, axis=1)


def _xattn_prompt_kernel(q_ref, mk_ref, mv_ref, o_ref):
    o_ref[...] = _xattn_heads(q_ref[...], mk_ref[...], mv_ref[...]).astype(o_ref.dtype)


def xattn_prompt(q, mk, mv, batch, tq):
    nq = SEQ // tq
    xw = N_XHEADS * XHEAD_DIM
    return pl.pallas_call(
        _xattn_prompt_kernel,
        grid=(batch, nq),
        in_specs=[
            pl.BlockSpec((tq, xw), lambda b, i: (b * nq + i, 0)),
            pl.BlockSpec((N_MEM, xw), lambda b, i: (b, 0)),
            pl.BlockSpec((N_MEM, xw), lambda b, i: (b, 0)),
        ],
        out_specs=pl.BlockSpec((tq, xw), lambda b, i: (b * nq + i, 0)),
        out_shape=jax.ShapeDtypeStruct((batch * SEQ, xw), BF16),
        compiler_params=_params("arbitrary", "arbitrary"),
        name="xattn_prompt",
    )(q, mk, mv)


def _xattn_sample_kernel(q_ref, mk_ref, mv_ref, o_ref, *, nb):
    L = DEC_SEQ
    for b in range(nb):
        o_ref[b * L:(b + 1) * L, :] = _xattn_heads(
            q_ref[b * L:(b + 1) * L, :], mk_ref[b], mv_ref[b]).astype(o_ref.dtype)


def xattn_sample(q, cache_k, cache_v, layer, nb=8):
    nbatch = cache_k.shape[1]
    L = DEC_SEQ
    xw = N_XHEADS * XHEAD_DIM
    return pl.pallas_call(
        functools.partial(_xattn_sample_kernel, nb=nb),
        grid=(nbatch // nb,),
        in_specs=[
            pl.BlockSpec((nb * L, xw), lambda g: (g, 0)),
            pl.BlockSpec((None, nb, N_MEM, xw), lambda g: (layer, g, 0, 0)),
            pl.BlockSpec((None, nb, N_MEM, xw), lambda g: (layer, g, 0, 0)),
        ],
        out_specs=pl.BlockSpec((nb * L, xw), lambda g: (g, 0)),
        out_shape=jax.ShapeDtypeStruct((nbatch * L, xw), BF16),
        compiler_params=_params("arbitrary"),
        name="xattn_sample",
    )(q, cache_k, cache_v)


def _s5_prompt_kernel(u_ref, h0r_ref, h0i_ref, a_ref, b_ref, c_ref, d_ref,
                      z_ref, hr_ref, hi_ref, x_ref, hst_ref, *, T):
    c = pl.program_id(1)
    nslab = 2 * SLABS

    @pl.when(c == 0)
    def _():
        for sb in range(N_SUPER):
            hst_ref[2 * sb] = h0r_ref[sb]
            hst_ref[2 * sb + 1] = h0i_ref[sb]

    for sb in range(N_SUPER):
        bu = _bdot(u_ref[:, sb * SUPER_IN:(sb + 1) * SUPER_IN], b_ref[sb])
        for s in range(nslab):
            x_ref[pl.ds((sb * nslab + s) * T, T), :] = bu[:, s * LANES:(s + 1) * LANES]

    a = [a_ref[k] for k in range(2 * N_SUPER)]
    h_init = tuple(hst_ref[k] for k in range(2 * N_SUPER))

    def step(t, h):
        new = []
        for sb in range(N_SUPER):
            ir = pl.ds((sb * nslab) * T + t, SUBLANES, stride=T)
            ii = pl.ds((sb * nslab + SLABS) * T + t, SUBLANES, stride=T)
            ar, ai = a[2 * sb], a[2 * sb + 1]
            hr, hi = h[2 * sb], h[2 * sb + 1]
            nr = ar * hr - ai * hi + x_ref[ir, :]
            ni = ar * hi + ai * hr + x_ref[ii, :]
            x_ref[ir, :] = nr
            x_ref[ii, :] = ni
            new += [nr, ni]
        return tuple(new)

    h_fin = lax.fori_loop(0, T, step, h_init)
    for sb in range(N_SUPER):
        hst_ref[2 * sb] = h_fin[2 * sb]
        hst_ref[2 * sb + 1] = h_fin[2 * sb + 1]
        hr_ref[sb] = h_fin[2 * sb]
        hi_ref[sb] = h_fin[2 * sb + 1]

    for sb in range(N_SUPER):
        hcat = jnp.concatenate([x_ref[pl.ds((sb * nslab + s) * T, T), :] for s in range(nslab)],
                               axis=1)
        cols = slice(sb * SUPER_IN, (sb + 1) * SUPER_IN)
        y = _bdot(hcat, c_ref[sb]) + d_ref[:, cols] * u_ref[:, cols]
        z_ref[:, cols] = jax.nn.gelu(y).astype(z_ref.dtype)


def s5_prompt(u, h0r, h0i, a_tiles, b_blk, c_blk, d_row, batch, T):
    nc = SEQ // T
    st = (None, N_SUPER, SUBLANES, LANES)
    return pl.pallas_call(
        functools.partial(_s5_prompt_kernel, T=T),
        grid=(batch, nc),
        in_specs=[
            pl.BlockSpec((T, D_MODEL), lambda b, c: (b * nc + c, 0)),
            pl.BlockSpec(st, lambda b, c: (b, 0, 0, 0)),
            pl.BlockSpec(st, lambda b, c: (b, 0, 0, 0)),
            _once((2 * N_SUPER, SUBLANES, LANES), lambda b, c: (0, 0, 0)),
            _once((N_SUPER, SUPER_IN, 2 * SUPER_STATE), lambda b, c: (0, 0, 0)),
            _once((N_SUPER, 2 * SUPER_STATE, SUPER_IN), lambda b, c: (0, 0, 0)),
            _once((1, D_MODEL), lambda b, c: (0, 0)),
        ],
        out_specs=[
            pl.BlockSpec((T, D_MODEL), lambda b, c: (b * nc + c, 0)),
            pl.BlockSpec(st, lambda b, c: (b, 0, 0, 0)),
            pl.BlockSpec(st, lambda b, c: (b, 0, 0, 0)),
        ],
        out_shape=[
            jax.ShapeDtypeStruct((batch * SEQ, D_MODEL), BF16),
            jax.ShapeDtypeStruct((batch, N_SUPER, SUBLANES, LANES), F32),
            jax.ShapeDtypeStruct((batch, N_SUPER, SUBLANES, LANES), F32),
        ],
        scratch_shapes=[
            pltpu.VMEM((N_SUPER * 2 * SLABS * T, LANES), F32),
            pltpu.VMEM((2 * N_SUPER, SUBLANES, LANES), F32),
        ],
        compiler_params=_params("arbitrary", "arbitrary"),
        name="s5_prompt",
    )(u, h0r, h0i, a_tiles, b_blk, c_blk, d_row)


def _s5_sample_kernel(u_ref, h0r_ref, h0i_ref, ar_ref, ai_ref, b_ref, c_ref, d_ref,
                      z_ref, hr_ref, hi_ref, x_ref, *, nbatch):
    L = DEC_SEQ
    R = nbatch * L
    for sb in range(N_SUPER):
        cols = slice(sb * SUPER_IN, (sb + 1) * SUPER_IN)
        bu = _bdot(u_ref[:, cols], b_ref[sb])
        for s in range(2 * SLABS):
            x_ref[s * R:(s + 1) * R, :] = bu[:, s * LANES:(s + 1) * LANES]
        for s in range(SLABS):
            st = slice(sb * SUPER_STATE + s * LANES, sb * SUPER_STATE + (s + 1) * LANES)
            ar, ai = ar_ref[:, st], ai_ref[:, st]
            hr, hi = h0r_ref[:, st], h0i_ref[:, st]
            for t in range(L):
                rr = pl.ds(s * R + t, nbatch, stride=L)
                ri = pl.ds((SLABS + s) * R + t, nbatch, stride=L)
                nr = ar * hr - ai * hi + x_ref[rr, :]
                ni = ar * hi + ai * hr + x_ref[ri, :]
                x_ref[rr, :] = nr
                x_ref[ri, :] = ni
                hr, hi = nr, ni
            hr_ref[:, st] = hr
            hi_ref[:, st] = hi
        hcat = jnp.concatenate([x_ref[s * R:(s + 1) * R, :] for s in range(2 * SLABS)], axis=1)
        y = _bdot(hcat, c_ref[sb]) + d_ref[:, cols] * u_ref[:, cols]
        z_ref[:, cols] = jax.nn.gelu(y).astype(z_ref.dtype)


def s5_sample(u, h0r, h0i, a_re, a_im, b_blk, c_blk, d_row):
    rows = u.shape[0]
    nbatch = rows // DEC_SEQ
    nstate = N_SSM_GROUPS * SSM_STATE
    full = lambda shape: pl.BlockSpec(shape, lambda i: (0,) * len(shape))
    return pl.pallas_call(
        functools.partial(_s5_sample_kernel, nbatch=nbatch),
        grid=(1,),
        in_specs=[
            full((rows, D_MODEL)), full((nbatch, nstate)), full((nbatch, nstate)),
            full((1, nstate)), full((1, nstate)),
            full((N_SUPER, SUPER_IN, 2 * SUPER_STATE)),
            full((N_SUPER, 2 * SUPER_STATE, SUPER_IN)),
            full((1, D_MODEL)),
        ],
        out_specs=[full((rows, D_MODEL)), full((nbatch, nstate)), full((nbatch, nstate))],
        out_shape=[
            jax.ShapeDtypeStruct((rows, D_MODEL), BF16),
            jax.ShapeDtypeStruct((nbatch, nstate), F32),
            jax.ShapeDtypeStruct((nbatch, nstate), F32),
        ],
        scratch_shapes=[pltpu.VMEM((2 * SLABS * rows, LANES), F32)],
        compiler_params=_params("arbitrary"),
        name="s5_sample",
    )(u, h0r, h0i, a_re, a_im, b_blk, c_blk, d_row)


def _s5_weights(lam_re, lam_im, log_step, b_re, b_im, c_re, c_im):
    lam = lax.complex(lam_re.astype(F32), lam_im.astype(F32))
    dt = jnp.exp(log_step.astype(F32))[:, None]
    abar = jnp.exp(lam * dt)
    bbar = ((abar - 1.0) / lam)[..., None] * lax.complex(b_re.astype(F32), b_im.astype(F32))
    eye = jnp.eye(SSM_SUPER, dtype=F32)

    def b_layout(m):
        m = m.reshape(N_SUPER, SSM_SUPER, SSM_STATE, SSM_GROUP)
        return jnp.einsum('sgph,gk->sghkp', m, eye).reshape(N_SUPER, SUPER_IN, SUPER_STATE)

    def c_layout(m):
        m = m.reshape(N_SUPER, SSM_SUPER, SSM_GROUP, SSM_STATE)
        return jnp.einsum('sghp,gk->sgpkh', m, eye).reshape(N_SUPER, SUPER_STATE, SUPER_IN)

    b_blk = jnp.concatenate([b_layout(bbar.real), b_layout(bbar.imag)], axis=2).astype(BF16)
    c_blk = jnp.concatenate([c_layout(c_re.astype(F32)), -c_layout(c_im.astype(F32))],
                            axis=1).astype(BF16)
    return abar.real, abar.imag, b_blk, c_blk


def _conv_gate(ua, uv, p1a, p2a, p1v, p2v, cwa_ref, cwv_ref, cba_ref, cbv_ref):
    ca = cba_ref[...] + cwa_ref[0:1, :] * p2a + cwa_ref[1:2, :] * p1a + cwa_ref[2:3, :] * ua
    cv = cbv_ref[...] + cwv_ref[0:1, :] * p2v + cwv_ref[1:2, :] * p1v + cwv_ref[2:3, :] * uv
    return (jax.nn.silu(ca) * cv).astype(BF16)


def _ffn_prompt_kernel(x_ref, gpre_ref, wua_ref, wuv_ref, cwa_ref, cwv_ref, cba_ref, cbv_ref,
                       wd_ref, gpost_ref, o_ref, sa_ref, sv_ref, hn_ref, ca_ref, cv_ref,
                       *, nc, blocks_per_seq):
    i = pl.program_id(0)
    c = pl.program_id(1)

    @pl.when(c == 0)
    def _():
        hn_ref[...] = _rms(x_ref[...], gpre_ref[...]).astype(BF16)

    hn = hn_ref[...]
    ua = jnp.dot(hn, wua_ref[...].astype(BF16), preferred_element_type=F32)
    uv = jnp.dot(hn, wuv_ref[...].astype(BF16), preferred_element_type=F32)
    tm = ua.shape[0]
    row = lax.broadcasted_iota(jnp.int32, ua.shape, 0)

    @pl.when(i % blocks_per_seq == 0)
    def _():
        ca_ref[c] = jnp.zeros(ca_ref.shape[1:], F32)
        cv_ref[c] = jnp.zeros(cv_ref.shape[1:], F32)

    def shifted(u, carry_ref):
        carry = carry_ref[c]
        p1 = jnp.where(row == 0, carry[1:2, :], pltpu.roll(u, 1, 0))
        p2 = jnp.where(row == 0, carry[0:1, :],
                       jnp.where(row == 1, carry[1:2, :], pltpu.roll(u, 2, 0)))
        tail = u[tm - SUBLANES:tm, :]
        carry_ref[c] = pltpu.roll(tail, 2, 0)
        return p1, p2, tail[SUBLANES - 2:SUBLANES, :]

    p1a, p2a, ta = shifted(ua, ca_ref)
    p1v, p2v, tv = shifted(uv, cv_ref)
    sa_ref[c] = ta
    sv_ref[c] = tv
    act = _conv_gate(ua, uv, p1a, p2a, p1v, p2v, cwa_ref, cwv_ref, cba_ref, cbv_ref)
    part = jnp.dot(act, wd_ref[...].astype(BF16), preferred_element_type=F32)

    @pl.when(c == 0)
    def _():
        o_ref[...] = part

    @pl.when(c > 0)
    def _():
        o_ref[...] += part

    @pl.when(c == nc - 1)
    def _():
        o_ref[...] = x_ref[...] + _rms(o_ref[...], gpost_ref[...])


def ffn_prompt(x, layer, g_pre, w_up, conv_w, conv_b, w_down, g_post, batch, tm, tf):
    M, D = x.shape
    nc = D_FF // tf
    bps = SEQ // tm
    conv_b3 = conv_b.reshape(DEPTH, 1, 2 * D_FF)
    y, sa, sv = pl.pallas_call(
        functools.partial(_ffn_prompt_kernel, nc=nc, blocks_per_seq=bps),
        grid=(M // tm, nc),
        in_specs=[
            _once((tm, D), lambda i, c: (i, 0)),
            pl.BlockSpec((None, 1, D), lambda i, c: (layer, 0, 0)),
            pl.BlockSpec((None, D, tf), lambda i, c: (layer, 0, c)),
            pl.BlockSpec((None, D, tf), lambda i, c: (layer, 0, nc + c)),
            pl.BlockSpec((None, 3, tf), lambda i, c: (layer, 0, c)),
            pl.BlockSpec((None, 3, tf), lambda i, c: (layer, 0, nc + c)),
            pl.BlockSpec((None, 1, tf), lambda i, c: (layer, 0, c)),
            pl.BlockSpec((None, 1, tf), lambda i, c: (layer, 0, nc + c)),
            pl.BlockSpec((None, tf, D), lambda i, c: (layer, c, 0)),
            pl.BlockSpec((None, 1, D), lambda i, c: (layer, 0, 0)),
        ],
        out_specs=[
            _once((tm, D), lambda i, c: (i, 0)),
            pl.BlockSpec((None, nc, 2, tf), lambda i, c: (i // bps, 0, 0, 0)),
            pl.BlockSpec((None, nc, 2, tf), lambda i, c: (i // bps, 0, 0, 0)),
        ],
        out_shape=[
            jax.ShapeDtypeStruct((M, D), F32),
            jax.ShapeDtypeStruct((batch, nc, 2, tf), F32),
            jax.ShapeDtypeStruct((batch, nc, 2, tf), F32),
        ],
        scratch_shapes=[
            pltpu.VMEM((tm, D), BF16),
            pltpu.VMEM((nc, SUBLANES, tf), F32),
            pltpu.VMEM((nc, SUBLANES, tf), F32),
        ],
        compiler_params=_params("arbitrary", "arbitrary"),
        name="ffn_prompt",
    )(x, g_pre.reshape(DEPTH, 1, D), w_up, w_up, conv_w, conv_w, conv_b3, conv_b3,
      w_down, g_post.reshape(DEPTH, 1, D))
    sa = sa.transpose(0, 2, 1, 3).reshape(batch, 2, D_FF)
    sv = sv.transpose(0, 2, 1, 3).reshape(batch, 2, D_FF)
    return y, jnp.concatenate([sa, sv], axis=-1)


def _ffn_sample_kernel(x_ref, gpre_ref, wua_ref, wuv_ref, cwa_ref, cwv_ref, cba_ref, cbv_ref,
                       wd_ref, gpost_ref, s1a_ref, s2a_ref, s1v_ref, s2v_ref,
                       o_ref, ua_ref, uv_ref, hn_ref, *, nc):
    c = pl.program_id(0)

    @pl.when(c == 0)
    def _():
        hn_ref[...] = _rms(x_ref[...], gpre_ref[...]).astype(BF16)

    hn = hn_ref[...]
    ua = jnp.dot(hn, wua_ref[...].astype(BF16), preferred_element_type=F32)
    uv = jnp.dot(hn, wuv_ref[...].astype(BF16), preferred_element_type=F32)
    ua_ref[...] = ua
    uv_ref[...] = uv
    t = lax.broadcasted_iota(jnp.int32, ua.shape, 0) % DEC_SEQ

    def shifted(u, s1_ref, s2_ref):
        p1 = jnp.where(t == 0, s1_ref[...], pltpu.roll(u, 1, 0))
        p2 = jnp.where(t <= 1, s2_ref[...], pltpu.roll(u, 2, 0))
        return p1, p2

    p1a, p2a = shifted(ua, s1a_ref, s2a_ref)
    p1v, p2v = shifted(uv, s1v_ref, s2v_ref)
    act = _conv_gate(ua, uv, p1a, p2a, p1v, p2v, cwa_ref, cwv_ref, cba_ref, cbv_ref)
    part = jnp.dot(act, wd_ref[...].astype(BF16), preferred_element_type=F32)

    @pl.when(c == 0)
    def _():
        o_ref[...] = part

    @pl.when(c > 0)
    def _():
        o_ref[...] += part

    @pl.when(c == nc - 1)
    def _():
        o_ref[...] = x_ref[...] + _rms(o_ref[...], gpost_ref[...])


def ffn_sample(x, layer, g_pre, w_up, conv_w, conv_b, w_down, g_post, conv_state, tf):
    M, D = x.shape
    nbatch = M // DEC_SEQ
    nc = D_FF // tf
    conv_b3 = conv_b.reshape(DEPTH, 1, 2 * D_FF)
    zeros = jnp.zeros((nbatch, DEC_SEQ - 2, 2 * D_FF), F32)
    s1 = jnp.concatenate([conv_state[:, 1:2], conv_state[:, 0:1] * 0, zeros], axis=1)
    s2 = jnp.concatenate([conv_state, zeros], axis=1)
    s1 = s1.reshape(M, 2 * D_FF)
    s2 = s2.reshape(M, 2 * D_FF)
    y, ua, uv = pl.pallas_call(
        functools.partial(_ffn_sample_kernel, nc=nc),
        grid=(nc,),
        in_specs=[
            pl.BlockSpec((M, D), lambda c: (0, 0)),
            pl.BlockSpec((None, 1, D), lambda c: (layer, 0, 0)),
            pl.BlockSpec((None, D, tf), lambda c: (layer, 0, c)),
            pl.BlockSpec((None, D, tf), lambda c: (layer, 0, nc + c)),
            pl.BlockSpec((None, 3, tf), lambda c: (layer, 0, c)),
            pl.BlockSpec((None, 3, tf), lambda c: (layer, 0, nc + c)),
            pl.BlockSpec((None, 1, tf), lambda c: (layer, 0, c)),
            pl.BlockSpec((None, 1, tf), lambda c: (layer, 0, nc + c)),
            pl.BlockSpec((None, tf, D), lambda c: (layer, c, 0)),
            pl.BlockSpec((None, 1, D), lambda c: (layer, 0, 0)),
            pl.BlockSpec((M, tf), lambda c: (0, c)),
            pl.BlockSpec((M, tf), lambda c: (0, c)),
            pl.BlockSpec((M, tf), lambda c: (0, nc + c)),
            pl.BlockSpec((M, tf), lambda c: (0, nc + c)),
        ],
        out_specs=[
            pl.BlockSpec((M, D), lambda c: (0, 0)),
            pl.BlockSpec((M, tf), lambda c: (0, c)),
            pl.BlockSpec((M, tf), lambda c: (0, c)),
        ],
        out_shape=[
            jax.ShapeDtypeStruct((M, D), F32),
            jax.ShapeDtypeStruct((M, D_FF), F32),
            jax.ShapeDtypeStruct((M, D_FF), F32),
        ],
        scratch_shapes=[pltpu.VMEM((M, D), BF16)],
        compiler_params=_params("arbitrary"),
        name="ffn_sample",
    )(x, g_pre.reshape(DEPTH, 1, D), w_up, w_up, conv_w, conv_w, conv_b3, conv_b3,
      w_down, g_post.reshape(DEPTH, 1, D), s1, s2, s1, s2)
    up = jnp.concatenate([ua, uv], axis=-1).reshape(nbatch, DEC_SEQ, 2 * D_FF)
    return y, up[:, DEC_SEQ - 2:]


def kernel(x_prompt, x_sample, mem_prompt, cache_swa_k, cache_swa_v, state_ssm_re, state_ssm_im, state_ffn_conv, cache_mem_k, cache_mem_v, g_mix_pre, g_mix_post, w_qkv, w_attn_o, attn_sinks, w_ssm_in, ssm_lambda_re, ssm_lambda_im, ssm_log_step, ssm_b_re, ssm_b_im, ssm_c_re, ssm_c_im, ssm_d, w_ssm_glu, g_x_pre, g_x_post, g_mem, w_x_q, w_mem_k, w_mem_v, w_x_o, g_ffn_pre, g_ffn_post, w_ffn_up, ffn_conv_w, ffn_conv_b, w_ffn_down):
    B = x_prompt.shape[0]
    SB = x_sample.shape[0]
    xw = N_XHEADS * XHEAD_DIM
    nkc = N_KV_HEADS * HEAD_DIM
    xp = x_prompt.reshape(B * SEQ, D_MODEL)
    xs = x_sample.reshape(SB * DEC_SEQ, D_MODEL)
    mem = mem_prompt.reshape(B * N_MEM, D_MODEL)
    MS = SB * DEC_SEQ
    TMP = 1024
    cos_p, sin_p = _rope_tables(jnp.arange(SEQ))
    cos_s, sin_s = _rope_tables(PAST_LEN + jnp.arange(DEC_SEQ))
    ck_all = cache_swa_k.reshape(cache_swa_k.shape[0], SB, WINDOW, nkc)
    cv_all = cache_swa_v.reshape(cache_swa_v.shape[0], SB, WINDOW, nkc)
    cmk_all = cache_mem_k.reshape(DEPTH, SB, N_MEM, xw)
    cmv_all = cache_mem_v.reshape(DEPTH, SB, N_MEM, xw)

    swa_kp, swa_vp, swa_ks, swa_vs = [], [], [], []
    ssm_rp, ssm_ip, ssm_rs, ssm_is = [], [], [], []
    conv_p, conv_s, memk_p, memv_p = [], [], [], []
    for i in range(DEPTH):
        j = i // 2
        if i % 2 == 0:
            qkv_p = norm_matmul(xp, g_mix_pre, (i, j), w_qkv, TMP, 512)
            qkv_s = norm_matmul(xs, g_mix_pre, (i, j), w_qkv, MS, 512)
            op, kp, vp = swa_prompt(qkv_p, attn_sinks, j, cos_p, sin_p, B)
            sink_rows = jnp.repeat(attn_sinks[j].reshape(N_KV_HEADS, GQA_GROUP), DEC_SEQ,
                                   axis=1)[..., None]
            os_, kn, vn = swa_sample(qkv_s, ck_all, cv_all, j, sink_rows, cos_s, sin_s)
            swa_kp.append(kp.reshape(B, WINDOW, N_KV_HEADS, HEAD_DIM))
            swa_vp.append(vp.reshape(B, WINDOW, N_KV_HEADS, HEAD_DIM))
            swa_ks.append(kn.reshape(SB, WINDOW, N_KV_HEADS, HEAD_DIM))
            swa_vs.append(vn.reshape(SB, WINDOW, N_KV_HEADS, HEAD_DIM))
            xp = matmul_post(op, w_attn_o, j, g_mix_post, i, xp, TMP, 512)
            xs = matmul_post(os_, w_attn_o, j, g_mix_post, i, xs, MS, 512)
        else:
            a_re, a_im, b_blk, c_blk = _s5_weights(
                ssm_lambda_re[j], ssm_lambda_im[j], ssm_log_step[j], ssm_b_re[j], ssm_b_im[j],
                ssm_c_re[j], ssm_c_im[j])
            d_row = ssm_d[j].reshape(1, D_MODEL)
            up_ = norm_matmul(xp, g_mix_pre, (i, j), w_ssm_in, TMP, 512)
            us_ = norm_matmul(xs, g_mix_pre, (i, j), w_ssm_in, MS, 512)
            a_tiles = jnp.stack([a_re.reshape(N_SUPER, SUBLANES, LANES),
                                 a_im.reshape(N_SUPER, SUBLANES, LANES)], axis=1
                                ).reshape(2 * N_SUPER, SUBLANES, LANES)
            zero_state = jnp.zeros((B, N_SUPER, SUBLANES, LANES), F32)
            zp, rp, ip = s5_prompt(up_, zero_state, zero_state, a_tiles, b_blk, c_blk, d_row,
                                   B, 128)
            nstate = N_SSM_GROUPS * SSM_STATE
            zs, rn, im_ = s5_sample(us_, state_ssm_re[j].reshape(SB, nstate),
                                    state_ssm_im[j].reshape(SB, nstate),
                                    a_re.reshape(1, nstate), a_im.reshape(1, nstate),
                                    b_blk, c_blk, d_row)
            ssm_rp.append(rp.reshape(B, N_SSM_GROUPS, SSM_STATE))
            ssm_ip.append(ip.reshape(B, N_SSM_GROUPS, SSM_STATE))
            ssm_rs.append(rn.reshape(SB, N_SSM_GROUPS, SSM_STATE))
            ssm_is.append(im_.reshape(SB, N_SSM_GROUPS, SSM_STATE))
            xp = glu_post(zp, w_ssm_glu, j, g_mix_post, i, xp, TMP, 256)
            xs = glu_post(zs, w_ssm_glu, j, g_mix_post, i, xs, MS, 256)
        mk = norm_matmul(mem, g_mem, (i, i), w_mem_k, B * N_MEM, xw)
        mv = norm_matmul(mem, g_mem, (i, i), w_mem_v, B * N_MEM, xw)
        memk_p.append(mk.reshape(B, N_MEM, N_XHEADS, XHEAD_DIM))
        memv_p.append(mv.reshape(B, N_MEM, N_XHEADS, XHEAD_DIM))
        qp = norm_matmul(xp, g_x_pre, (i, i), w_x_q, TMP, xw)
        qs = norm_matmul(xs, g_x_pre, (i, i), w_x_q, MS, xw)
        ap = xattn_prompt(qp, mk, mv, B, 512)
        as_ = xattn_sample(qs, cmk_all, cmv_all, i)
        xp = matmul_post(ap, w_x_o, i, g_x_post, i, xp, TMP, xw)
        xs = matmul_post(as_, w_x_o, i, g_x_post, i, xs, MS, xw)
        xp, cp = ffn_prompt(xp, i, g_ffn_pre, w_ffn_up, ffn_conv_w, ffn_conv_b, w_ffn_down,
                            g_ffn_post, B, TMP, 256)
        xs, cs = ffn_sample(xs, i, g_ffn_pre, w_ffn_up, ffn_conv_w, ffn_conv_b, w_ffn_down,
                            g_ffn_post, state_ffn_conv[i], 512)
        conv_p.append(cp)
        conv_s.append(cs)
    return (xp.reshape(B, SEQ, D_MODEL), xs.reshape(SB, DEC_SEQ, D_MODEL),
            jnp.stack(swa_kp), jnp.stack(swa_vp), jnp.stack(swa_ks), jnp.stack(swa_vs),
            jnp.stack(ssm_rp), jnp.stack(ssm_ip), jnp.stack(ssm_rs), jnp.stack(ssm_is),
            jnp.stack(conv_p), jnp.stack(conv_s), jnp.stack(memk_p), jnp.stack(memv_p))
```

```python
import functools
import math

import jax
import jax.numpy as jnp
from jax import lax
from jax.experimental import pallas as pl
from jax.experimental.pallas import tpu as pltpu

D_MODEL = 2048
SEQ = 4096
DEPTH = 4
DEC_SEQ = 8
PAST_LEN = 16384
HEAD_DIM = 64
N_HEADS = 32
N_KV_HEADS = 4
GQA_GROUP = 8
WINDOW = 128
ROPE_THETA = 10000.0
SSM_GROUP = 16
N_SSM_GROUPS = 128
SSM_STATE = 64
N_MEM = 256
N_XHEADS = 4
XHEAD_DIM = 128
D_FF = 5632
NORM_EPS = 1e-6
NEG_INF = -1e30

F32 = jnp.float32
BF16 = jnp.bfloat16

V7X_VMEM_BYTES = 64 * 1024 * 1024
VMEM_LIMIT = V7X_VMEM_BYTES - 8 * 1024 * 1024
LANES = 128
SUBLANES = 8
SSM_SUPER = 16
N_SUPER = N_SSM_GROUPS // SSM_SUPER
SUPER_IN = SSM_SUPER * SSM_GROUP
SUPER_STATE = SSM_SUPER * SSM_STATE
SLABS = SUPER_STATE // LANES


def _params(*sem):
    return pltpu.CompilerParams(dimension_semantics=sem, vmem_limit_bytes=VMEM_LIMIT)


def _rms(x, g):
    ms = jnp.mean(x * x, axis=-1, keepdims=True)
    return x * lax.rsqrt(ms + NORM_EPS) * g


def _bdot(a, b):
    return jnp.dot(a.astype(BF16), b.astype(BF16), preferred_element_type=F32)


def _once(shape, index_map):
    return pl.BlockSpec(shape, index_map, pipeline_mode=pl.Buffered(1))


def _norm_matmul_kernel(x_ref, g_ref, w_ref, o_ref, hn_ref):
    @pl.when(pl.program_id(1) == 0)
    def _():
        hn_ref[...] = _rms(x_ref[...], g_ref[...]).astype(BF16)

    o_ref[...] = jnp.dot(hn_ref[...], w_ref[...].astype(BF16),
                         preferred_element_type=F32).astype(o_ref.dtype)


def norm_matmul(x, g, layer, w, tm, tn, out_dtype=F32):
    g_layer, w_layer = layer
    M, D = x.shape
    N = w.shape[-1]
    return pl.pallas_call(
        _norm_matmul_kernel,
        grid=(M // tm, N // tn),
        in_specs=[
            pl.BlockSpec((tm, D), lambda i, j: (i, 0)),
            pl.BlockSpec((None, 1, D), lambda i, j: (g_layer, 0, 0)),
            pl.BlockSpec((None, D, tn), lambda i, j: (w_layer, 0, j)),
        ],
        out_specs=pl.BlockSpec((tm, tn), lambda i, j: (i, j)),
        out_shape=jax.ShapeDtypeStruct((M, N), out_dtype),
        scratch_shapes=[pltpu.VMEM((tm, D), BF16)],
        compiler_params=_params("arbitrary", "arbitrary"),
        name="norm_matmul",
    )(x, g.reshape(g.shape[0], 1, D), w)


def _matmul_post_kernel(a_ref, w_ref, g_ref, x_ref, o_ref, *, nk):
    k = pl.program_id(1)
    part = _bdot(a_ref[...], w_ref[...])

    @pl.when(k == 0)
    def _():
        o_ref[...] = part

    @pl.when(k > 0)
    def _():
        o_ref[...] += part

    @pl.when(k == nk - 1)
    def _():
        o_ref[...] = x_ref[...] + _rms(o_ref[...], g_ref[...])


def matmul_post(a, w, w_layer, g, g_layer, x, tm, tk):
    M, K = a.shape
    D = w.shape[-1]
    nk = K // tk
    return pl.pallas_call(
        functools.partial(_matmul_post_kernel, nk=nk),
        grid=(M // tm, nk),
        in_specs=[
            pl.BlockSpec((tm, tk), lambda i, k: (i, k)),
            pl.BlockSpec((None, tk, D), lambda i, k: (w_layer, k, 0)),
            pl.BlockSpec((None, 1, D), lambda i, k: (g_layer, 0, 0)),
            _once((tm, D), lambda i, k: (i, 0)),
        ],
        out_specs=pl.BlockSpec((tm, D), lambda i, k: (i, 0)),
        out_shape=jax.ShapeDtypeStruct((M, D), F32),
        compiler_params=_params("arbitrary", "arbitrary"),
        name="matmul_post",
    )(a, w, g.reshape(g.shape[0], 1, D), x)


def _glu_post_kernel(z_ref, wv_ref, wg_ref, g_ref, x_ref, o_ref, act_ref, *, nj):
    j = pl.program_id(1)
    z = z_ref[...]
    val = _bdot(z, wv_ref[...])
    gate = _bdot(z, wg_ref[...])
    act_ref[j] = val * jax.nn.sigmoid(gate)

    @pl.when(j == nj - 1)
    def _():
        y = jnp.concatenate([act_ref[jj] for jj in range(nj)], axis=1)
        o_ref[...] = x_ref[...] + _rms(y, g_ref[...])


def glu_post(z, w, w_layer, g, g_layer, x, tm, tn):
    M, K = z.shape
    D = D_MODEL
    nj = D // tn
    return pl.pallas_call(
        functools.partial(_glu_post_kernel, nj=nj),
        grid=(M // tm, nj),
        in_specs=[
            pl.BlockSpec((tm, K), lambda i, j: (i, 0)),
            pl.BlockSpec((None, K, tn), lambda i, j: (w_layer, 0, j)),
            pl.BlockSpec((None, K, tn), lambda i, j: (w_layer, 0, nj + j)),
            pl.BlockSpec((None, 1, D), lambda i, j: (g_layer, 0, 0)),
            _once((tm, D), lambda i, j: (i, 0)),
        ],
        out_specs=pl.BlockSpec((tm, D), lambda i, j: (i, 0)),
        out_shape=jax.ShapeDtypeStruct((M, D), F32),
        scratch_shapes=[pltpu.VMEM((nj, tm, tn), F32)],
        compiler_params=_params("arbitrary", "arbitrary"),
        name="glu_post",
    )(z, w, w, g.reshape(g.shape[0], 1, D), x)


def _rope_chunk(xc, cos, sin_signed):
    lane = lax.broadcasted_iota(jnp.int32, xc.shape, 1)
    first_half = (lane & (HEAD_DIM - 1)) < (HEAD_DIM // 2)
    partner = jnp.where(first_half, pltpu.roll(xc, LANES - HEAD_DIM // 2, 1),
                        pltpu.roll(xc, HEAD_DIM // 2, 1))
    return xc * cos + partner * sin_signed


def _rope_tables(pos):
    half = HEAD_DIM // 2
    inv_freq = ROPE_THETA ** (-jnp.arange(half, dtype=F32) * 2.0 / HEAD_DIM)
    ang = pos.astype(F32)[:, None] * inv_freq[None, :]
    cos, sin = jnp.cos(ang), jnp.sin(ang)
    cos128 = jnp.tile(cos, (1, LANES // half))
    sin128 = jnp.tile(jnp.concatenate([-sin, sin], axis=1), (1, LANES // HEAD_DIM))
    return cos128, sin128


def _swa_prompt_kernel(sink_ref, q_ref, kv_ref, cos_ref, sin_ref,
                       o_ref, kc_ref, vc_ref, kk_ref, vv_ref, *, layer):
    i = pl.program_id(1)
    blk = WINDOW
    cos, sn = cos_ref[...], sin_ref[...]
    kv = kv_ref[...]
    nkc = N_KV_HEADS * HEAD_DIM
    k = jnp.concatenate([_rope_chunk(kv[:, c * LANES:(c + 1) * LANES], cos, sn)
                         for c in range(nkc // LANES)], axis=1)
    v = kv[:, nkc:2 * nkc]
    kc_ref[...] = k
    vc_ref[...] = v

    cur = i % 2
    prv = 1 - cur

    @pl.when(i == 0)
    def _():
        kk_ref[1] = jnp.zeros((blk, nkc), BF16)
        vv_ref[1] = jnp.zeros((blk, nkc), BF16)

    kb, vb = k.astype(BF16), v.astype(BF16)
    kk_ref[cur] = kb
    vv_ref[cur] = vb
    kk = jnp.concatenate([kk_ref[prv], kb], axis=0)
    vv = jnp.concatenate([vv_ref[prv], vb], axis=0)
    rows = GQA_GROUP * blk
    qi = lax.broadcasted_iota(jnp.int32, (rows, 2 * blk), 0) & (blk - 1)
    si = lax.broadcasted_iota(jnp.int32, (rows, 2 * blk), 1)
    d = blk + qi - si
    mask = (d >= 0) & (d < WINDOW) & ((si >= blk) | (i > 0))

    chunks_per_group = GQA_GROUP * HEAD_DIM // LANES
    outs = []
    for kh in range(N_KV_HEADS):
        hs = slice(kh * HEAD_DIM, (kh + 1) * HEAD_DIM)
        pieces = []
        for c in range(kh * chunks_per_group, (kh + 1) * chunks_per_group):
            qc = (_rope_chunk(q_ref[:, c * LANES:(c + 1) * LANES], cos, sn)
                  * (HEAD_DIM ** -0.5)).astype(BF16)
            pieces += [qc[:, hh * HEAD_DIM:(hh + 1) * HEAD_DIM] for hh in range(LANES // HEAD_DIM)]
        qs = jnp.concatenate(pieces, axis=0)
        sk = jnp.concatenate([jnp.full((blk, 1), sink_ref[layer, kh * GQA_GROUP + j], F32)
                              for j in range(GQA_GROUP)], axis=0)
        s = lax.dot_general(qs, kk[:, hs], (((1,), (1,)), ((), ())),
                            preferred_element_type=F32)
        s = jnp.where(mask, s, NEG_INF)
        mx = jnp.maximum(jnp.max(s, axis=1, keepdims=True), sk)
        p = jnp.exp(s - mx)
        den = jnp.sum(p, axis=1, keepdims=True) + jnp.exp(sk - mx)
        o = jnp.dot(p.astype(BF16), vv[:, hs], preferred_element_type=F32) / den
        outs += [o[j * blk:(j + 1) * blk, :] for j in range(GQA_GROUP)]
    o_ref[...] = jnp.concatenate(outs, axis=1).astype(o_ref.dtype)


def swa_prompt(qkv, sinks, layer, cos, sin, batch):
    blk = WINDOW
    nb = SEQ // blk
    nq = N_HEADS * HEAD_DIM
    nkv = 2 * N_KV_HEADS * HEAD_DIM
    o, kc, vc = pl.pallas_call(
        functools.partial(_swa_prompt_kernel, layer=layer),
        grid=(batch, nb),
        in_specs=[
            pl.BlockSpec(memory_space=pltpu.SMEM),
            pl.BlockSpec((blk, nq), lambda b, i: (b * nb + i, 0)),
            pl.BlockSpec((blk, nkv), lambda b, i: (b * nb + i, nq // nkv)),
            pl.BlockSpec((blk, LANES), lambda b, i: (i, 0)),
            pl.BlockSpec((blk, LANES), lambda b, i: (i, 0)),
        ],
        out_specs=[
            pl.BlockSpec((blk, nq), lambda b, i: (b * nb + i, 0)),
            pl.BlockSpec((None, blk, nkv // 2), lambda b, i: (b, 0, 0)),
            pl.BlockSpec((None, blk, nkv // 2), lambda b, i: (b, 0, 0)),
        ],
        out_shape=[
            jax.ShapeDtypeStruct((batch * SEQ, nq), BF16),
            jax.ShapeDtypeStruct((batch, blk, nkv // 2), F32),
            jax.ShapeDtypeStruct((batch, blk, nkv // 2), F32),
        ],
        scratch_shapes=[pltpu.VMEM((2, blk, nkv // 2), BF16),
                        pltpu.VMEM((2, blk, nkv // 2), BF16)],
        compiler_params=_params("arbitrary", "arbitrary"),
        name="swa_prompt",
    )(sinks, qkv, qkv, cos, sin)
    return o, kc, vc


def _swa_sample_kernel(sink_ref, qkv_ref, ck_ref, cv_ref, cos_ref, sin_ref,
                       o_ref, nk_ref, nv_ref, *, nb):
    L = DEC_SEQ
    wb = WINDOW
    nq = N_HEADS * HEAD_DIM
    nkc = N_KV_HEADS * HEAD_DIM
    cos, sn = cos_ref[...], sin_ref[...]
    rows = GQA_GROUP * L
    t_q = lax.broadcasted_iota(jnp.int32, (rows, wb), 0) % L
    c_k = lax.broadcasted_iota(jnp.int32, (rows, wb), 1)
    mask_c = c_k >= t_q + 1
    t_q2 = lax.broadcasted_iota(jnp.int32, (rows, L), 0) % L
    t_k2 = lax.broadcasted_iota(jnp.int32, (rows, L), 1)
    mask_n = t_k2 <= t_q2
    for b in range(nb):
        x = qkv_ref[b * L:(b + 1) * L, :]
        qr = jnp.concatenate([_rope_chunk(x[:, c * LANES:(c + 1) * LANES], cos, sn)
                              for c in range(nq // LANES)], axis=1) * (HEAD_DIM ** -0.5)
        kn = jnp.concatenate([_rope_chunk(x[:, nq + c * LANES:nq + (c + 1) * LANES], cos, sn)
                              for c in range(nkc // LANES)], axis=1)
        vn = x[:, nq + nkc:nq + 2 * nkc]
        ck = ck_ref[b]
        cv = cv_ref[b]
        nk_ref[b, 0:wb - L, :] = ck[L:wb, :]
        nk_ref[b, wb - L:wb, :] = kn
        nv_ref[b, 0:wb - L, :] = cv[L:wb, :]
        nv_ref[b, wb - L:wb, :] = vn
        ckb, cvb, knb, vnb = ck.astype(BF16), cv.astype(BF16), kn.astype(BF16), vn.astype(BF16)
        outs = []
        for kh in range(N_KV_HEADS):
            hs = slice(kh * HEAD_DIM, (kh + 1) * HEAD_DIM)
            qs = jnp.concatenate(
                [qr[:, (kh * GQA_GROUP + j) * HEAD_DIM:(kh * GQA_GROUP + j + 1) * HEAD_DIM]
                 for j in range(GQA_GROUP)], axis=0).astype(BF16)
            dn = (((1,), (1,)), ((), ()))
            s_c = lax.dot_general(qs, ckb[:, hs], dn, preferred_element_type=F32)
            s_n = lax.dot_general(qs, knb[:, hs], dn, preferred_element_type=F32)
            s_c = jnp.where(mask_c, s_c, NEG_INF)
            s_n = jnp.where(mask_n, s_n, NEG_INF)
            sk = sink_ref[kh]
            mx = jnp.maximum(jnp.maximum(jnp.max(s_c, axis=1, keepdims=True),
                                         jnp.max(s_n, axis=1, keepdims=True)), sk)
            p_c = jnp.exp(s_c - mx)
            p_n = jnp.exp(s_n - mx)
            den = (jnp.sum(p_c, axis=1, keepdims=True) + jnp.sum(p_n, axis=1, keepdims=True)
                   + jnp.exp(sk - mx))
            o = (jnp.dot(p_c.astype(BF16), cvb[:, hs], preferred_element_type=F32)
                 + jnp.dot(p_n.astype(BF16), vnb[:, hs], preferred_element_type=F32)) / den
            outs.extend(o[j * L:(j + 1) * L, :] for j in range(GQA_GROUP))
        o_ref[b * L:(b + 1) * L, :] = jnp.concatenate(outs, axis=1).astype(o_ref.dtype)


def swa_sample(qkv, cache_k, cache_v, layer, sink_rows, cos, sin, nb=8):
    nbatch = cache_k.shape[1]
    nq = N_HEADS * HEAD_DIM
    nkc = N_KV_HEADS * HEAD_DIM
    L = DEC_SEQ
    return pl.pallas_call(
        functools.partial(_swa_sample_kernel, nb=nb),
        grid=(nbatch // nb,),
        in_specs=[
            pl.BlockSpec((N_KV_HEADS, GQA_GROUP * L, 1), lambda g: (0, 0, 0)),
            pl.BlockSpec((nb * L, nq + 2 * nkc), lambda g: (g, 0)),
            pl.BlockSpec((None, nb, WINDOW, nkc), lambda g: (layer, g, 0, 0)),
            pl.BlockSpec((None, nb, WINDOW, nkc), lambda g: (layer, g, 0, 0)),
            pl.BlockSpec((L, LANES), lambda g: (0, 0)),
            pl.BlockSpec((L, LANES), lambda g: (0, 0)),
        ],
        out_specs=[
            pl.BlockSpec((nb * L, nq), lambda g: (g, 0)),
            pl.BlockSpec((nb, WINDOW, nkc), lambda g: (g, 0, 0)),
            pl.BlockSpec((nb, WINDOW, nkc), lambda g: (g, 0, 0)),
        ],
        out_shape=[
            jax.ShapeDtypeStruct((nbatch * L, nq), BF16),
            jax.ShapeDtypeStruct((nbatch, WINDOW, nkc), F32),
            jax.ShapeDtypeStruct((nbatch, WINDOW, nkc), F32),
        ],
        compiler_params=_params("arbitrary"),
        name="swa_sample",
    )(sink_rows, qkv, cache_k, cache_v, cos, sin)


def _xattn_heads(q, mk, mv):
    outs = []
    for h in range(N_XHEADS):
        hs = slice(h * XHEAD_DIM, (h + 1) * XHEAD_DIM)
        qh = (q[:, hs] * (XHEAD_DIM ** -0.5)).astype(BF16)
        s = lax.dot_general(qh, mk[:, hs].astype(BF16), (((1,), (1,)), ((), ())),
                            preferred_element_type=F32)
        mx = jnp.max(s, axis=1, keepdims=True)
        p = jnp.exp(s - mx)
        den = jnp.sum(p, axis=1, keepdims=True)
        outs.append(jnp.dot(p.astype(BF16), mv[:, hs].astype(BF16),
                            preferred_element_type=F32) / den)
    return jnp.concatenate(outs, axis=1)


def _xattn_prompt_kernel(q_ref, mk_ref, mv_ref, o_ref):
    o_ref[...] = _xattn_heads(q_ref[...], mk_ref[...], mv_ref[...]).astype(o_ref.dtype)


def xattn_prompt(q, mk, mv, batch, tq):
    nq = SEQ // tq
    xw = N_XHEADS * XHEAD_DIM
    return pl.pallas_call(
        _xattn_prompt_kernel,
        grid=(batch, nq),
        in_specs=[
            pl.BlockSpec((tq, xw), lambda b, i: (b * nq + i, 0)),
            pl.BlockSpec((N_MEM, xw), lambda b, i: (b, 0)),
            pl.BlockSpec((N_MEM, xw), lambda b, i: (b, 0)),
        ],
        out_specs=pl.BlockSpec((tq, xw), lambda b, i: (b * nq + i, 0)),
        out_shape=jax.ShapeDtypeStruct((batch * SEQ, xw), BF16),
        compiler_params=_params("arbitrary", "arbitrary"),
        name="xattn_prompt",
    )(q, mk, mv)


def _xattn_sample_kernel(q_ref, mk_ref, mv_ref, o_ref, *, nb):
    L = DEC_SEQ
    for b in range(nb):
        o_ref[b * L:(b + 1) * L, :] = _xattn_heads(
            q_ref[b * L:(b + 1) * L, :], mk_ref[b], mv_ref[b]).astype(o_ref.dtype)


def xattn_sample(q, cache_k, cache_v, layer, nb=8):
    nbatch = cache_k.shape[1]
    L = DEC_SEQ
    xw = N_XHEADS * XHEAD_DIM
    return pl.pallas_call(
        functools.partial(_xattn_sample_kernel, nb=nb),
        grid=(nbatch // nb,),
        in_specs=[
            pl.BlockSpec((nb * L, xw), lambda g: (g, 0)),
            pl.BlockSpec((None, nb, N_MEM, xw), lambda g: (layer, g, 0, 0)),
            pl.BlockSpec((None, nb, N_MEM, xw), lambda g: (layer, g, 0, 0)),
        ],
        out_specs=pl.BlockSpec((nb * L, xw), lambda g: (g, 0)),
        out_shape=jax.ShapeDtypeStruct((nbatch * L, xw), BF16),
        compiler_params=_params("arbitrary"),
        name="xattn_sample",
    )(q, cache_k, cache_v)


def _s5_prompt_kernel(u_ref, h0r_ref, h0i_ref, a_ref, b_ref, c_ref, d_ref,
                      z_ref, hr_ref, hi_ref, x_ref, hst_ref, *, T):
    c = pl.program_id(1)
    nslab = 2 * SLABS
    G = T // SUBLANES

    @pl.when(c == 0)
    def _():
        for sb in range(N_SUPER):
            hst_ref[2 * sb] = h0r_ref[sb]
            hst_ref[2 * sb + 1] = h0i_ref[sb]

    def slab_rows(sb, s):
        return pl.ds((sb * nslab + s) * SUBLANES, SUBLANES)

    for sb in range(N_SUPER):
        bu = _bdot(u_ref[:, sb * SUPER_IN:(sb + 1) * SUPER_IN], b_ref[sb])
        for s in range(nslab):
            x_ref[:, slab_rows(sb, s), :] = bu[:, s * LANES:(s + 1) * LANES].reshape(
                G, SUBLANES, LANES)

    a = [a_ref[k] for k in range(2 * N_SUPER)]
    h_init = tuple(hst_ref[k] for k in range(2 * N_SUPER))

    def step(t, h):
        g = lax.shift_right_logical(t, 3)
        r = lax.bitwise_and(t, SUBLANES - 1)
        new = []
        for sb in range(N_SUPER):
            ir = pl.ds((sb * nslab) * SUBLANES + r, SUBLANES, stride=SUBLANES)
            ii = pl.ds((sb * nslab + SLABS) * SUBLANES + r, SUBLANES, stride=SUBLANES)
            ar, ai = a[2 * sb], a[2 * sb + 1]
            hr, hi = h[2 * sb], h[2 * sb + 1]
            nr = ar * hr - ai * hi + x_ref[g, ir, :]
            ni = ar * hi + ai * hr + x_ref[g, ii, :]
            x_ref[g, ir, :] = nr
            x_ref[g, ii, :] = ni
            new += [nr, ni]
        return tuple(new)

    h_fin = lax.fori_loop(0, T, step, h_init)
    for sb in range(N_SUPER):
        hst_ref[2 * sb] = h_fin[2 * sb]
        hst_ref[2 * sb + 1] = h_fin[2 * sb + 1]
        hr_ref[sb] = h_fin[2 * sb]
        hi_ref[sb] = h_fin[2 * sb + 1]

    for sb in range(N_SUPER):
        hcat = jnp.concatenate([x_ref[:, slab_rows(sb, s), :].reshape(T, LANES)
                                for s in range(nslab)], axis=1)
        cols = slice(sb * SUPER_IN, (sb + 1) * SUPER_IN)
        y = _bdot(hcat, c_ref[sb]) + d_ref[:, cols] * u_ref[:, cols]
        z_ref[:, cols] = jax.nn.gelu(y).astype(z_ref.dtype)


def s5_prompt(u, h0r, h0i, a_tiles, b_blk, c_blk, d_row, batch, T):
    nc = SEQ // T
    st = (None, N_SUPER, SUBLANES, LANES)
    return pl.pallas_call(
        functools.partial(_s5_prompt_kernel, T=T),
        grid=(batch, nc),
        in_specs=[
            pl.BlockSpec((T, D_MODEL), lambda b, c: (b * nc + c, 0)),
            pl.BlockSpec(st, lambda b, c: (b, 0, 0, 0)),
            pl.BlockSpec(st, lambda b, c: (b, 0, 0, 0)),
            _once((2 * N_SUPER, SUBLANES, LANES), lambda b, c: (0, 0, 0)),
            _once((N_SUPER, SUPER_IN, 2 * SUPER_STATE), lambda b, c: (0, 0, 0)),
            _once((N_SUPER, 2 * SUPER_STATE, SUPER_IN), lambda b, c: (0, 0, 0)),
            _once((1, D_MODEL), lambda b, c: (0, 0)),
        ],
        out_specs=[
            pl.BlockSpec((T, D_MODEL), lambda b, c: (b * nc + c, 0)),
            pl.BlockSpec(st, lambda b, c: (b, 0, 0, 0)),
            pl.BlockSpec(st, lambda b, c: (b, 0, 0, 0)),
        ],
        out_shape=[
            jax.ShapeDtypeStruct((batch * SEQ, D_MODEL), BF16),
            jax.ShapeDtypeStruct((batch, N_SUPER, SUBLANES, LANES), F32),
            jax.ShapeDtypeStruct((batch, N_SUPER, SUBLANES, LANES), F32),
        ],
        scratch_shapes=[
            pltpu.VMEM((T // SUBLANES, N_SUPER * 2 * SLABS * SUBLANES, LANES), F32),
            pltpu.VMEM((2 * N_SUPER, SUBLANES, LANES), F32),
        ],
        compiler_params=_params("arbitrary", "arbitrary"),
        name="s5_prompt",
    )(u, h0r, h0i, a_tiles, b_blk, c_blk, d_row)


def _s5_sample_kernel(u_ref, h0r_ref, h0i_ref, ar_ref, ai_ref, b_ref, c_ref, d_ref,
                      z_ref, hr_ref, hi_ref, x_ref, *, nbatch):
    L = DEC_SEQ
    R = nbatch * L
    for sb in range(N_SUPER):
        cols = slice(sb * SUPER_IN, (sb + 1) * SUPER_IN)
        bu = _bdot(u_ref[:, cols], b_ref[sb])
        for s in range(2 * SLABS):
            x_ref[s * R:(s + 1) * R, :] = bu[:, s * LANES:(s + 1) * LANES]
        for s in range(SLABS):
            st = slice(sb * SUPER_STATE + s * LANES, sb * SUPER_STATE + (s + 1) * LANES)
            ar, ai = ar_ref[:, st], ai_ref[:, st]
            hr, hi = h0r_ref[:, st], h0i_ref[:, st]
            for t in range(L):
                rr = pl.ds(s * R + t, nbatch, stride=L)
                ri = pl.ds((SLABS + s) * R + t, nbatch, stride=L)
                nr = ar * hr - ai * hi + x_ref[rr, :]
                ni = ar * hi + ai * hr + x_ref[ri, :]
                x_ref[rr, :] = nr
                x_ref[ri, :] = ni
                hr, hi = nr, ni
            hr_ref[:, st] = hr
            hi_ref[:, st] = hi
        hcat = jnp.concatenate([x_ref[s * R:(s + 1) * R, :] for s in range(2 * SLABS)], axis=1)
        y = _bdot(hcat, c_ref[sb]) + d_ref[:, cols] * u_ref[:, cols]
        z_ref[:, cols] = jax.nn.gelu(y).astype(z_ref.dtype)


def s5_sample(u, h0r, h0i, a_re, a_im, b_blk, c_blk, d_row):
    rows = u.shape[0]
    nbatch = rows // DEC_SEQ
    nstate = N_SSM_GROUPS * SSM_STATE
    full = lambda shape: pl.BlockSpec(shape, lambda i: (0,) * len(shape))
    return pl.pallas_call(
        functools.partial(_s5_sample_kernel, nbatch=nbatch),
        grid=(1,),
        in_specs=[
            full((rows, D_MODEL)), full((nbatch, nstate)), full((nbatch, nstate)),
            full((1, nstate)), full((1, nstate)),
            full((N_SUPER, SUPER_IN, 2 * SUPER_STATE)),
            full((N_SUPER, 2 * SUPER_STATE, SUPER_IN)),
            full((1, D_MODEL)),
        ],
        out_specs=[full((rows, D_MODEL)), full((nbatch, nstate)), full((nbatch, nstate))],
        out_shape=[
            jax.ShapeDtypeStruct((rows, D_MODEL), BF16),
            jax.ShapeDtypeStruct((nbatch, nstate), F32),
            jax.ShapeDtypeStruct((nbatch, nstate), F32),
        ],
        scratch_shapes=[pltpu.VMEM((2 * SLABS * rows, LANES), F32)],
        compiler_params=_params("arbitrary"),
        name="s5_sample",
    )(u, h0r, h0i, a_re, a_im, b_blk, c_blk, d_row)


def _s5_weights(lam_re, lam_im, log_step, b_re, b_im, c_re, c_im):
    lam = lax.complex(lam_re.astype(F32), lam_im.astype(F32))
    dt = jnp.exp(log_step.astype(F32))[:, None]
    abar = jnp.exp(lam * dt)
    bbar = ((abar - 1.0) / lam)[..., None] * lax.complex(b_re.astype(F32), b_im.astype(F32))
    eye = jnp.eye(SSM_SUPER, dtype=F32)

    def b_layout(m):
        m = m.reshape(N_SUPER, SSM_SUPER, SSM_STATE, SSM_GROUP)
        return jnp.einsum('sgph,gk->sghkp', m, eye).reshape(N_SUPER, SUPER_IN, SUPER_STATE)

    def c_layout(m):
        m = m.reshape(N_SUPER, SSM_SUPER, SSM_GROUP, SSM_STATE)
        return jnp.einsum('sghp,gk->sgpkh', m, eye).reshape(N_SUPER, SUPER_STATE, SUPER_IN)

    b_blk = jnp.concatenate([b_layout(bbar.real), b_layout(bbar.imag)], axis=2).astype(BF16)
    c_blk = jnp.concatenate([c_layout(c_re.astype(F32)), -c_layout(c_im.astype(F32))],
                            axis=1).astype(BF16)
    return abar.real, abar.imag, b_blk, c_blk


def _conv_gate(ua, uv, p1a, p2a, p1v, p2v, cwa_ref, cwv_ref, cba_ref, cbv_ref):
    ca = cba_ref[...] + cwa_ref[0:1, :] * p2a + cwa_ref[1:2, :] * p1a + cwa_ref[2:3, :] * ua
    cv = cbv_ref[...] + cwv_ref[0:1, :] * p2v + cwv_ref[1:2, :] * p1v + cwv_ref[2:3, :] * uv
    return (jax.nn.silu(ca) * cv).astype(BF16)


def _ffn_prompt_kernel(x_ref, gpre_ref, wua_ref, wuv_ref, cwa_ref, cwv_ref, cba_ref, cbv_ref,
                       wd_ref, gpost_ref, o_ref, sa_ref, sv_ref,
                       hn_ref, ua_ref, uv_ref, act_ref, ha_ref, hv_ref,
                       *, nc, blocks_per_seq, rc, rm):
    i = pl.program_id(0)
    c = pl.program_id(1)
    tm, D = hn_ref.shape
    hdr = SUBLANES

    @pl.when(c == 0)
    def _():
        hn_ref[...] = _rms(x_ref[...], gpre_ref[...]).astype(BF16)
        o_ref[...] = jnp.zeros(o_ref.shape, F32)

    slot = i % 2

    @pl.when(i % blocks_per_seq == 0)
    def _():
        ha_ref[1 - slot, c] = jnp.zeros(ha_ref.shape[2:], F32)
        hv_ref[1 - slot, c] = jnp.zeros(hv_ref.shape[2:], F32)

    ua_ref[0:hdr, :] = ha_ref[1 - slot, c]
    uv_ref[0:hdr, :] = hv_ref[1 - slot, c]
    wa = wua_ref[...].astype(BF16)
    wv = wuv_ref[...].astype(BF16)
    wd = wd_ref[...].astype(BF16)

    def conv(buf_ref, w_ref, b_ref, r0):
        return (b_ref[...] + w_ref[0:1, :] * buf_ref[r0 + hdr - 2:r0 + hdr - 2 + rc, :]
                + w_ref[1:2, :] * buf_ref[r0 + hdr - 1:r0 + hdr - 1 + rc, :]
                + w_ref[2:3, :] * buf_ref[r0 + hdr:r0 + hdr + rc, :])

    for k in range(tm // rm):
        rows = slice(k * rm, (k + 1) * rm)
        hk = hn_ref[rows, :]
        ua_ref[hdr + k * rm:hdr + (k + 1) * rm, :] = jnp.dot(hk, wa, preferred_element_type=F32)
        uv_ref[hdr + k * rm:hdr + (k + 1) * rm, :] = jnp.dot(hk, wv, preferred_element_type=F32)
        for r0 in range(k * rm, (k + 1) * rm, rc):
            ca = conv(ua_ref, cwa_ref, cba_ref, r0)
            cv = conv(uv_ref, cwv_ref, cbv_ref, r0)
            act_ref[r0:r0 + rc, :] = (jax.nn.silu(ca) * cv).astype(BF16)
        o_ref[rows, :] += jnp.dot(act_ref[rows, :], wd, preferred_element_type=F32)

    ta = ua_ref[tm:tm + hdr, :]
    tv = uv_ref[tm:tm + hdr, :]
    ha_ref[slot, c] = ta
    hv_ref[slot, c] = tv
    sa_ref[c] = ta[hdr - 2:hdr, :]
    sv_ref[c] = tv[hdr - 2:hdr, :]

    @pl.when(c == nc - 1)
    def _():
        o_ref[...] = x_ref[...] + _rms(o_ref[...], gpost_ref[...])


FFN_ROW_CHUNK = 64
FFN_ROW_GROUP = 256


def ffn_prompt(x, layer, g_pre, w_up, conv_w, conv_b, w_down, g_post, batch, tm, tf):
    M, D = x.shape
    nc = D_FF // tf
    bps = SEQ // tm
    conv_b3 = conv_b.reshape(DEPTH, 1, 2 * D_FF)
    y, sa, sv = pl.pallas_call(
        functools.partial(_ffn_prompt_kernel, nc=nc, blocks_per_seq=bps, rc=FFN_ROW_CHUNK,
                          rm=FFN_ROW_GROUP),
        grid=(M // tm, nc),
        in_specs=[
            _once((tm, D), lambda i, c: (i, 0)),
            pl.BlockSpec((None, 1, D), lambda i, c: (layer, 0, 0)),
            pl.BlockSpec((None, D, tf), lambda i, c: (layer, 0, c)),
            pl.BlockSpec((None, D, tf), lambda i, c: (layer, 0, nc + c)),
            pl.BlockSpec((None, 3, tf), lambda i, c: (layer, 0, c)),
            pl.BlockSpec((None, 3, tf), lambda i, c: (layer, 0, nc + c)),
            pl.BlockSpec((None, 1, tf), lambda i, c: (layer, 0, c)),
            pl.BlockSpec((None, 1, tf), lambda i, c: (layer, 0, nc + c)),
            pl.BlockSpec((None, tf, D), lambda i, c: (layer, c, 0)),
            pl.BlockSpec((None, 1, D), lambda i, c: (layer, 0, 0)),
        ],
        out_specs=[
            _once((tm, D), lambda i, c: (i, 0)),
            pl.BlockSpec((None, nc, 2, tf), lambda i, c: (i // bps, 0, 0, 0)),
            pl.BlockSpec((None, nc, 2, tf), lambda i, c: (i // bps, 0, 0, 0)),
        ],
        out_shape=[
            jax.ShapeDtypeStruct((M, D), F32),
            jax.ShapeDtypeStruct((batch, nc, 2, tf), F32),
            jax.ShapeDtypeStruct((batch, nc, 2, tf), F32),
        ],
        scratch_shapes=[
            pltpu.VMEM((tm, D), BF16),
            pltpu.VMEM((tm + SUBLANES, tf), F32),
            pltpu.VMEM((tm + SUBLANES, tf), F32),
            pltpu.VMEM((tm, tf), BF16),
            pltpu.VMEM((2, nc, SUBLANES, tf), F32),
            pltpu.VMEM((2, nc, SUBLANES, tf), F32),
        ],
        compiler_params=_params("arbitrary", "arbitrary"),
        name="ffn_prompt",
    )(x, g_pre.reshape(DEPTH, 1, D), w_up, w_up, conv_w, conv_w, conv_b3, conv_b3,
      w_down, g_post.reshape(DEPTH, 1, D))
    sa = sa.transpose(0, 2, 1, 3).reshape(batch, 2, D_FF)
    sv = sv.transpose(0, 2, 1, 3).reshape(batch, 2, D_FF)
    return y, jnp.concatenate([sa, sv], axis=-1)


def _ffn_sample_kernel(x_ref, gpre_ref, wua_ref, wuv_ref, cwa_ref, cwv_ref, cba_ref, cbv_ref,
                       wd_ref, gpost_ref, s1a_ref, s2a_ref, s1v_ref, s2v_ref,
                       o_ref, ua_ref, uv_ref, hn_ref, *, nc):
    c = pl.program_id(0)

    @pl.when(c == 0)
    def _():
        hn_ref[...] = _rms(x_ref[...], gpre_ref[...]).astype(BF16)

    hn = hn_ref[...]
    ua = jnp.dot(hn, wua_ref[...].astype(BF16), preferred_element_type=F32)
    uv = jnp.dot(hn, wuv_ref[...].astype(BF16), preferred_element_type=F32)
    ua_ref[...] = ua
    uv_ref[...] = uv
    t = lax.broadcasted_iota(jnp.int32, ua.shape, 0) % DEC_SEQ

    def shifted(u, s1_ref, s2_ref):
        p1 = jnp.where(t == 0, s1_ref[...], pltpu.roll(u, 1, 0))
        p2 = jnp.where(t <= 1, s2_ref[...], pltpu.roll(u, 2, 0))
        return p1, p2

    p1a, p2a = shifted(ua, s1a_ref, s2a_ref)
    p1v, p2v = shifted(uv, s1v_ref, s2v_ref)
    act = _conv_gate(ua, uv, p1a, p2a, p1v, p2v, cwa_ref, cwv_ref, cba_ref, cbv_ref)
    part = jnp.dot(act, wd_ref[...].astype(BF16), preferred_element_type=F32)

    @pl.when(c == 0)
    def _():
        o_ref[...] = part

    @pl.when(c > 0)
    def _():
        o_ref[...] += part

    @pl.when(c == nc - 1)
    def _():
        o_ref[...] = x_ref[...] + _rms(o_ref[...], gpost_ref[...])


def ffn_sample(x, layer, g_pre, w_up, conv_w, conv_b, w_down, g_post, conv_state, tf):
    M, D = x.shape
    nbatch = M // DEC_SEQ
    nc = D_FF // tf
    conv_b3 = conv_b.reshape(DEPTH, 1, 2 * D_FF)
    zeros = jnp.zeros((nbatch, DEC_SEQ - 2, 2 * D_FF), F32)
    s1 = jnp.concatenate([conv_state[:, 1:2], conv_state[:, 0:1] * 0, zeros], axis=1)
    s2 = jnp.concatenate([conv_state, zeros], axis=1)
    s1 = s1.reshape(M, 2 * D_FF)
    s2 = s2.reshape(M, 2 * D_FF)
    y, ua, uv = pl.pallas_call(
        functools.partial(_ffn_sample_kernel, nc=nc),
        grid=(nc,),
        in_specs=[
            pl.BlockSpec((M, D), lambda c: (0, 0)),
            pl.BlockSpec((None, 1, D), lambda c: (layer, 0, 0)),
            pl.BlockSpec((None, D, tf), lambda c: (layer, 0, c)),
            pl.BlockSpec((None, D, tf), lambda c: (layer, 0, nc + c)),
            pl.BlockSpec((None, 3, tf), lambda c: (layer, 0, c)),
            pl.BlockSpec((None, 3, tf), lambda c: (layer, 0, nc + c)),
            pl.BlockSpec((None, 1, tf), lambda c: (layer, 0, c)),
            pl.BlockSpec((None, 1, tf), lambda c: (layer, 0, nc + c)),
            pl.BlockSpec((None, tf, D), lambda c: (layer, c, 0)),
            pl.BlockSpec((None, 1, D), lambda c: (layer, 0, 0)),
            pl.BlockSpec((M, tf), lambda c: (0, c)),
            pl.BlockSpec((M, tf), lambda c: (0, c)),
            pl.BlockSpec((M, tf), lambda c: (0, nc + c)),
            pl.BlockSpec((M, tf), lambda c: (0, nc + c)),
        ],
        out_specs=[
            pl.BlockSpec((M, D), lambda c: (0, 0)),
            pl.BlockSpec((M, tf), lambda c: (0, c)),
            pl.BlockSpec((M, tf), lambda c: (0, c)),
        ],
        out_shape=[
            jax.ShapeDtypeStruct((M, D), F32),
            jax.ShapeDtypeStruct((M, D_FF), F32),
            jax.ShapeDtypeStruct((M, D_FF), F32),
        ],
        scratch_shapes=[pltpu.VMEM((M, D), BF16)],
        compiler_params=_params("arbitrary"),
        name="ffn_sample",
    )(x, g_pre.reshape(DEPTH, 1, D), w_up, w_up, conv_w, conv_w, conv_b3, conv_b3,
      w_down, g_post.reshape(DEPTH, 1, D), s1, s2, s1, s2)
    up = jnp.concatenate([ua, uv], axis=-1).reshape(nbatch, DEC_SEQ, 2 * D_FF)
    return y, up[:, DEC_SEQ - 2:]


def kernel(x_prompt, x_sample, mem_prompt, cache_swa_k, cache_swa_v, state_ssm_re, state_ssm_im, state_ffn_conv, cache_mem_k, cache_mem_v, g_mix_pre, g_mix_post, w_qkv, w_attn_o, attn_sinks, w_ssm_in, ssm_lambda_re, ssm_lambda_im, ssm_log_step, ssm_b_re, ssm_b_im, ssm_c_re, ssm_c_im, ssm_d, w_ssm_glu, g_x_pre, g_x_post, g_mem, w_x_q, w_mem_k, w_mem_v, w_x_o, g_ffn_pre, g_ffn_post, w_ffn_up, ffn_conv_w, ffn_conv_b, w_ffn_down):
    B = x_prompt.shape[0]
    SB = x_sample.shape[0]
    xw = N_XHEADS * XHEAD_DIM
    nkc = N_KV_HEADS * HEAD_DIM
    xp = x_prompt.reshape(B * SEQ, D_MODEL)
    xs = x_sample.reshape(SB * DEC_SEQ, D_MODEL)
    mem = mem_prompt.reshape(B * N_MEM, D_MODEL)
    MS = SB * DEC_SEQ
    TMP = 1024
    cos_p, sin_p = _rope_tables(jnp.arange(SEQ))
    cos_s, sin_s = _rope_tables(PAST_LEN + jnp.arange(DEC_SEQ))
    ck_all = cache_swa_k.reshape(cache_swa_k.shape[0], SB, WINDOW, nkc)
    cv_all = cache_swa_v.reshape(cache_swa_v.shape[0], SB, WINDOW, nkc)
    cmk_all = cache_mem_k.reshape(DEPTH, SB, N_MEM, xw)
    cmv_all = cache_mem_v.reshape(DEPTH, SB, N_MEM, xw)

    swa_kp, swa_vp, swa_ks, swa_vs = [], [], [], []
    ssm_rp, ssm_ip, ssm_rs, ssm_is = [], [], [], []
    conv_p, conv_s, memk_p, memv_p = [], [], [], []
    for i in range(DEPTH):
        j = i // 2
        if i % 2 == 0:
            qkv_p = norm_matmul(xp, g_mix_pre, (i, j), w_qkv, TMP, 512)
            qkv_s = norm_matmul(xs, g_mix_pre, (i, j), w_qkv, MS, 512)
            op, kp, vp = swa_prompt(qkv_p, attn_sinks, j, cos_p, sin_p, B)
            sink_rows = jnp.repeat(attn_sinks[j].reshape(N_KV_HEADS, GQA_GROUP), DEC_SEQ,
                                   axis=1)[..., None]
            os_, kn, vn = swa_sample(qkv_s, ck_all, cv_all, j, sink_rows, cos_s, sin_s)
            swa_kp.append(kp.reshape(B, WINDOW, N_KV_HEADS, HEAD_DIM))
            swa_vp.append(vp.reshape(B, WINDOW, N_KV_HEADS, HEAD_DIM))
            swa_ks.append(kn.reshape(SB, WINDOW, N_KV_HEADS, HEAD_DIM))
            swa_vs.append(vn.reshape(SB, WINDOW, N_KV_HEADS, HEAD_DIM))
            xp = matmul_post(op, w_attn_o, j, g_mix_post, i, xp, TMP, 512)
            xs = matmul_post(os_, w_attn_o, j, g_mix_post, i, xs, MS, 512)
        else:
            a_re, a_im, b_blk, c_blk = _s5_weights(
                ssm_lambda_re[j], ssm_lambda_im[j], ssm_log_step[j], ssm_b_re[j], ssm_b_im[j],
                ssm_c_re[j], ssm_c_im[j])
            d_row = ssm_d[j].reshape(1, D_MODEL)
            up_ = norm_matmul(xp, g_mix_pre, (i, j), w_ssm_in, TMP, 512)
            us_ = norm_matmul(xs, g_mix_pre, (i, j), w_ssm_in, MS, 512)
            a_tiles = jnp.stack([a_re.reshape(N_SUPER, SUBLANES, LANES),
                                 a_im.reshape(N_SUPER, SUBLANES, LANES)], axis=1
                                ).reshape(2 * N_SUPER, SUBLANES, LANES)
            zero_state = jnp.zeros((B, N_SUPER, SUBLANES, LANES), F32)
            zp, rp, ip = s5_prompt(up_, zero_state, zero_state, a_tiles, b_blk, c_blk, d_row,
                                   B, 256)
            nstate = N_SSM_GROUPS * SSM_STATE
            zs, rn, im_ = s5_sample(us_, state_ssm_re[j].reshape(SB, nstate),
                                    state_ssm_im[j].reshape(SB, nstate),
                                    a_re.reshape(1, nstate), a_im.reshape(1, nstate),
                                    b_blk, c_blk, d_row)
            ssm_rp.append(rp.reshape(B, N_SSM_GROUPS, SSM_STATE))
            ssm_ip.append(ip.reshape(B, N_SSM_GROUPS, SSM_STATE))
            ssm_rs.append(rn.reshape(SB, N_SSM_GROUPS, SSM_STATE))
            ssm_is.append(im_.reshape(SB, N_SSM_GROUPS, SSM_STATE))
            xp = glu_post(zp, w_ssm_glu, j, g_mix_post, i, xp, TMP, 256)
            xs = glu_post(zs, w_ssm_glu, j, g_mix_post, i, xs, MS, 256)
        mk = norm_matmul(mem, g_mem, (i, i), w_mem_k, B * N_MEM, xw)
        mv = norm_matmul(mem, g_mem, (i, i), w_mem_v, B * N_MEM, xw)
        memk_p.append(mk.reshape(B, N_MEM, N_XHEADS, XHEAD_DIM))
        memv_p.append(mv.reshape(B, N_MEM, N_XHEADS, XHEAD_DIM))
        qp = norm_matmul(xp, g_x_pre, (i, i), w_x_q, TMP, xw)
        qs = norm_matmul(xs, g_x_pre, (i, i), w_x_q, MS, xw)
        ap = xattn_prompt(qp, mk, mv, B, 512)
        as_ = xattn_sample(qs, cmk_all, cmv_all, i)
        xp = matmul_post(ap, w_x_o, i, g_x_post, i, xp, TMP, xw)
        xs = matmul_post(as_, w_x_o, i, g_x_post, i, xs, MS, xw)
        xp, cp = ffn_prompt(xp, i, g_ffn_pre, w_ffn_up, ffn_conv_w, ffn_conv_b, w_ffn_down,
                            g_ffn_post, B, TMP, 256)
        xs, cs = ffn_sample(xs, i, g_ffn_pre, w_ffn_up, ffn_conv_w, ffn_conv_b, w_ffn_down,
                            g_ffn_post, state_ffn_conv[i], 512)
        conv_p.append(cp)
        conv_s.append(cs)
    return (xp.reshape(B, SEQ, D_MODEL), xs.reshape(SB, DEC_SEQ, D_MODEL),
            jnp.stack(swa_kp), jnp.stack(swa_vp), jnp.stack(swa_ks), jnp.stack(swa_vs),
            jnp.stack(ssm_rp), jnp.stack(ssm_ip), jnp.stack(ssm_rs), jnp.stack(ssm_is),
            jnp.stack(conv_p), jnp.stack(conv_s), jnp.stack(memk_p), jnp.stack(memv_p))
```

```python
import functools
import math

import jax
import jax.numpy as jnp
from jax import lax
from jax.experimental import pallas as pl
from jax.experimental.pallas import tpu as pltpu

D_MODEL = 2048
SEQ = 4096
DEPTH = 4
DEC_SEQ = 8
PAST_LEN = 16384
HEAD_DIM = 64
N_HEADS = 32
N_KV_HEADS = 4
GQA_GROUP = 8
WINDOW = 128
ROPE_THETA = 10000.0
SSM_GROUP = 16
N_SSM_GROUPS = 128
SSM_STATE = 64
N_MEM = 256
N_XHEADS = 4
XHEAD_DIM = 128
D_FF = 5632
NORM_EPS = 1e-6
NEG_INF = -1e30

F32 = jnp.float32
BF16 = jnp.bfloat16

V7X_VMEM_BYTES = 64 * 1024 * 1024
VMEM_LIMIT = V7X_VMEM_BYTES - 8 * 1024 * 1024
LANES = 128
SUBLANES = 8
SSM_SUPER = 16
N_SUPER = N_SSM_GROUPS // SSM_SUPER
SUPER_IN = SSM_SUPER * SSM_GROUP
SUPER_STATE = SSM_SUPER * SSM_STATE
SLABS = SUPER_STATE // LANES


def _params(*sem):
    return pltpu.CompilerParams(dimension_semantics=sem, vmem_limit_bytes=VMEM_LIMIT)


def _rms(x, g):
    ms = jnp.mean(x * x, axis=-1, keepdims=True)
    return x * lax.rsqrt(ms + NORM_EPS) * g


def _bdot(a, b):
    return jnp.dot(a.astype(BF16), b.astype(BF16), preferred_element_type=F32)


def _once(shape, index_map):
    return pl.BlockSpec(shape, index_map, pipeline_mode=pl.Buffered(1))


def _norm_matmul_kernel(x_ref, g_ref, w_ref, o_ref, hn_ref):
    @pl.when(pl.program_id(1) == 0)
    def _():
        hn_ref[...] = _rms(x_ref[...], g_ref[...]).astype(BF16)

    o_ref[...] = jnp.dot(hn_ref[...], w_ref[...].astype(BF16),
                         preferred_element_type=F32).astype(o_ref.dtype)


def norm_matmul(x, g, layer, w, tm, tn, out_dtype=F32):
    g_layer, w_layer = layer
    M, D = x.shape
    N = w.shape[-1]
    return pl.pallas_call(
        _norm_matmul_kernel,
        grid=(M // tm, N // tn),
        in_specs=[
            pl.BlockSpec((tm, D), lambda i, j: (i, 0)),
            pl.BlockSpec((None, 1, D), lambda i, j: (g_layer, 0, 0)),
            pl.BlockSpec((None, D, tn), lambda i, j: (w_layer, 0, j)),
        ],
        out_specs=pl.BlockSpec((tm, tn), lambda i, j: (i, j)),
        out_shape=jax.ShapeDtypeStruct((M, N), out_dtype),
        scratch_shapes=[pltpu.VMEM((tm, D), BF16)],
        compiler_params=_params("arbitrary", "arbitrary"),
        name="norm_matmul",
    )(x, g.reshape(g.shape[0], 1, D), w)


def _matmul_post_kernel(a_ref, w_ref, g_ref, x_ref, o_ref, *, nk):
    k = pl.program_id(1)
    part = _bdot(a_ref[...], w_ref[...])

    @pl.when(k == 0)
    def _():
        o_ref[...] = part

    @pl.when(k > 0)
    def _():
        o_ref[...] += part

    @pl.when(k == nk - 1)
    def _():
        o_ref[...] = x_ref[...] + _rms(o_ref[...], g_ref[...])


def matmul_post(a, w, w_layer, g, g_layer, x, tm, tk):
    M, K = a.shape
    D = w.shape[-1]
    nk = K // tk
    return pl.pallas_call(
        functools.partial(_matmul_post_kernel, nk=nk),
        grid=(M // tm, nk),
        in_specs=[
            pl.BlockSpec((tm, tk), lambda i, k: (i, k)),
            pl.BlockSpec((None, tk, D), lambda i, k: (w_layer, k, 0)),
            pl.BlockSpec((None, 1, D), lambda i, k: (g_layer, 0, 0)),
            _once((tm, D), lambda i, k: (i, 0)),
        ],
        out_specs=pl.BlockSpec((tm, D), lambda i, k: (i, 0)),
        out_shape=jax.ShapeDtypeStruct((M, D), F32),
        compiler_params=_params("arbitrary", "arbitrary"),
        name="matmul_post",
    )(a, w, g.reshape(g.shape[0], 1, D), x)


def _glu_post_kernel(z_ref, wv_ref, wg_ref, g_ref, x_ref, o_ref, act_ref, *, nj):
    j = pl.program_id(1)
    z = z_ref[...]
    val = _bdot(z, wv_ref[...])
    gate = _bdot(z, wg_ref[...])
    act_ref[j] = val * jax.nn.sigmoid(gate)

    @pl.when(j == nj - 1)
    def _():
        y = jnp.concatenate([act_ref[jj] for jj in range(nj)], axis=1)
        o_ref[...] = x_ref[...] + _rms(y, g_ref[...])


def glu_post(z, w, w_layer, g, g_layer, x, tm, tn):
    M, K = z.shape
    D = D_MODEL
    nj = D // tn
    return pl.pallas_call(
        functools.partial(_glu_post_kernel, nj=nj),
        grid=(M // tm, nj),
        in_specs=[
            pl.BlockSpec((tm, K), lambda i, j: (i, 0)),
            pl.BlockSpec((None, K, tn), lambda i, j: (w_layer, 0, j)),
            pl.BlockSpec((None, K, tn), lambda i, j: (w_layer, 0, nj + j)),
            pl.BlockSpec((None, 1, D), lambda i, j: (g_layer, 0, 0)),
            _once((tm, D), lambda i, j: (i, 0)),
        ],
        out_specs=pl.BlockSpec((tm, D), lambda i, j: (i, 0)),
        out_shape=jax.ShapeDtypeStruct((M, D), F32),
        scratch_shapes=[pltpu.VMEM((nj, tm, tn), F32)],
        compiler_params=_params("arbitrary", "arbitrary"),
        name="glu_post",
    )(z, w, w, g.reshape(g.shape[0], 1, D), x)


def _rope_chunk(xc, cos, sin_signed):
    lane = lax.broadcasted_iota(jnp.int32, xc.shape, 1)
    first_half = (lane & (HEAD_DIM - 1)) < (HEAD_DIM // 2)
    partner = jnp.where(first_half, pltpu.roll(xc, LANES - HEAD_DIM // 2, 1),
                        pltpu.roll(xc, HEAD_DIM // 2, 1))
    return xc * cos + partner * sin_signed


def _rope_tables(pos):
    half = HEAD_DIM // 2
    inv_freq = ROPE_THETA ** (-jnp.arange(half, dtype=F32) * 2.0 / HEAD_DIM)
    ang = pos.astype(F32)[:, None] * inv_freq[None, :]
    cos, sin = jnp.cos(ang), jnp.sin(ang)
    cos128 = jnp.tile(cos, (1, LANES // half))
    sin128 = jnp.tile(jnp.concatenate([-sin, sin], axis=1), (1, LANES // HEAD_DIM))
    return cos128, sin128


def _swa_prompt_kernel(sink_ref, q_ref, kv_ref, cos_ref, sin_ref,
                       o_ref, kc_ref, vc_ref, kk_ref, vv_ref, *, layer):
    i = pl.program_id(1)
    blk = WINDOW
    cos, sn = cos_ref[...], sin_ref[...]
    kv = kv_ref[...]
    nkc = N_KV_HEADS * HEAD_DIM
    k = jnp.concatenate([_rope_chunk(kv[:, c * LANES:(c + 1) * LANES], cos, sn)
                         for c in range(nkc // LANES)], axis=1)
    v = kv[:, nkc:2 * nkc]
    kc_ref[...] = k
    vc_ref[...] = v

    cur = i % 2
    prv = 1 - cur

    @pl.when(i == 0)
    def _():
        kk_ref[1] = jnp.zeros((blk, nkc), BF16)
        vv_ref[1] = jnp.zeros((blk, nkc), BF16)

    kb, vb = k.astype(BF16), v.astype(BF16)
    kk_ref[cur] = kb
    vv_ref[cur] = vb
    kk = jnp.concatenate([kk_ref[prv], kb], axis=0)
    vv = jnp.concatenate([vv_ref[prv], vb], axis=0)
    rows = GQA_GROUP * blk
    qi = lax.broadcasted_iota(jnp.int32, (rows, 2 * blk), 0) & (blk - 1)
    si = lax.broadcasted_iota(jnp.int32, (rows, 2 * blk), 1)
    d = blk + qi - si
    mask = (d >= 0) & (d < WINDOW) & ((si >= blk) | (i > 0))

    chunks_per_group = GQA_GROUP * HEAD_DIM // LANES
    outs = []
    for kh in range(N_KV_HEADS):
        hs = slice(kh * HEAD_DIM, (kh + 1) * HEAD_DIM)
        pieces = []
        for c in range(kh * chunks_per_group, (kh + 1) * chunks_per_group):
            qc = (_rope_chunk(q_ref[:, c * LANES:(c + 1) * LANES], cos, sn)
                  * (HEAD_DIM ** -0.5)).astype(BF16)
            pieces += [qc[:, hh * HEAD_DIM:(hh + 1) * HEAD_DIM] for hh in range(LANES // HEAD_DIM)]
        qs = jnp.concatenate(pieces, axis=0)
        sk = jnp.concatenate([jnp.full((blk, 1), sink_ref[layer, kh * GQA_GROUP + j], F32)
                              for j in range(GQA_GROUP)], axis=0)
        s = lax.dot_general(qs, kk[:, hs], (((1,), (1,)), ((), ())),
                            preferred_element_type=F32)
        s = jnp.where(mask, s, NEG_INF)
        mx = jnp.maximum(jnp.max(s, axis=1, keepdims=True), sk)
        p = jnp.exp(s - mx)
        den = jnp.sum(p, axis=1, keepdims=True) + jnp.exp(sk - mx)
        o = jnp.dot(p.astype(BF16), vv[:, hs], preferred_element_type=F32) / den
        outs += [o[j * blk:(j + 1) * blk, :] for j in range(GQA_GROUP)]
    o_ref[...] = jnp.concatenate(outs, axis=1).astype(o_ref.dtype)


def swa_prompt(qkv, sinks, layer, cos, sin, batch):
    blk = WINDOW
    nb = SEQ // blk
    nq = N_HEADS * HEAD_DIM
    nkv = 2 * N_KV_HEADS * HEAD_DIM
    o, kc, vc = pl.pallas_call(
        functools.partial(_swa_prompt_kernel, layer=layer),
        grid=(batch, nb),
        in_specs=[
            pl.BlockSpec(memory_space=pltpu.SMEM),
            pl.BlockSpec((blk, nq), lambda b, i: (b * nb + i, 0)),
            pl.BlockSpec((blk, nkv), lambda b, i: (b * nb + i, nq // nkv)),
            pl.BlockSpec((blk, LANES), lambda b, i: (i, 0)),
            pl.BlockSpec((blk, LANES), lambda b, i: (i, 0)),
        ],
        out_specs=[
            pl.BlockSpec((blk, nq), lambda b, i: (b * nb + i, 0)),
            pl.BlockSpec((None, blk, nkv // 2), lambda b, i: (b, 0, 0)),
            pl.BlockSpec((None, blk, nkv // 2), lambda b, i: (b, 0, 0)),
        ],
        out_shape=[
            jax.ShapeDtypeStruct((batch * SEQ, nq), BF16),
            jax.ShapeDtypeStruct((batch, blk, nkv // 2), F32),
            jax.ShapeDtypeStruct((batch, blk, nkv // 2), F32),
        ],
        scratch_shapes=[pltpu.VMEM((2, blk, nkv // 2), BF16),
                        pltpu.VMEM((2, blk, nkv // 2), BF16)],
        compiler_params=_params("arbitrary", "arbitrary"),
        name="swa_prompt",
    )(sinks, qkv, qkv, cos, sin)
    return o, kc, vc


def _swa_sample_kernel(sink_ref, qkv_ref, ck_ref, cv_ref, cos_ref, sin_ref,
                       o_ref, nk_ref, nv_ref, *, nb):
    L = DEC_SEQ
    wb = WINDOW
    nq = N_HEADS * HEAD_DIM
    nkc = N_KV_HEADS * HEAD_DIM
    cos, sn = cos_ref[...], sin_ref[...]
    rows = GQA_GROUP * L
    t_q = lax.broadcasted_iota(jnp.int32, (rows, wb), 0) % L
    c_k = lax.broadcasted_iota(jnp.int32, (rows, wb), 1)
    mask_c = c_k >= t_q + 1
    t_q2 = lax.broadcasted_iota(jnp.int32, (rows, L), 0) % L
    t_k2 = lax.broadcasted_iota(jnp.int32, (rows, L), 1)
    mask_n = t_k2 <= t_q2
    for b in range(nb):
        x = qkv_ref[b * L:(b + 1) * L, :]
        qr = jnp.concatenate([_rope_chunk(x[:, c * LANES:(c + 1) * LANES], cos, sn)
                              for c in range(nq // LANES)], axis=1) * (HEAD_DIM ** -0.5)
        kn = jnp.concatenate([_rope_chunk(x[:, nq + c * LANES:nq + (c + 1) * LANES], cos, sn)
                              for c in range(nkc // LANES)], axis=1)
        vn = x[:, nq + nkc:nq + 2 * nkc]
        ck = ck_ref[b]
        cv = cv_ref[b]
        nk_ref[b, 0:wb - L, :] = ck[L:wb, :]
        nk_ref[b, wb - L:wb, :] = kn
        nv_ref[b, 0:wb - L, :] = cv[L:wb, :]
        nv_ref[b, wb - L:wb, :] = vn
        ckb, cvb, knb, vnb = ck.astype(BF16), cv.astype(BF16), kn.astype(BF16), vn.astype(BF16)
        outs = []
        for kh in range(N_KV_HEADS):
            hs = slice(kh * HEAD_DIM, (kh + 1) * HEAD_DIM)
            qs = jnp.concatenate(
                [qr[:, (kh * GQA_GROUP + j) * HEAD_DIM:(kh * GQA_GROUP + j + 1) * HEAD_DIM]
                 for j in range(GQA_GROUP)], axis=0).astype(BF16)
            dn = (((1,), (1,)), ((), ()))
            s_c = lax.dot_general(qs, ckb[:, hs], dn, preferred_element_type=F32)
            s_n = lax.dot_general(qs, knb[:, hs], dn, preferred_element_type=F32)
            s_c = jnp.where(mask_c, s_c, NEG_INF)
            s_n = jnp.where(mask_n, s_n, NEG_INF)
            sk = sink_ref[kh]
            mx = jnp.maximum(jnp.maximum(jnp.max(s_c, axis=1, keepdims=True),
                                         jnp.max(s_n, axis=1, keepdims=True)), sk)
            p_c = jnp.exp(s_c - mx)
            p_n = jnp.exp(s_n - mx)
            den = (jnp.sum(p_c, axis=1, keepdims=True) + jnp.sum(p_n, axis=1, keepdims=True)
                   + jnp.exp(sk - mx))
            o = (jnp.dot(p_c.astype(BF16), cvb[:, hs], preferred_element_type=F32)
                 + jnp.dot(p_n.astype(BF16), vnb[:, hs], preferred_element_type=F32)) / den
            outs.extend(o[j * L:(j + 1) * L, :] for j in range(GQA_GROUP))
        o_ref[b * L:(b + 1) * L, :] = jnp.concatenate(outs, axis=1).astype(o_ref.dtype)


def swa_sample(qkv, cache_k, cache_v, layer, sink_rows, cos, sin, nb=8):
    nbatch = cache_k.shape[1]
    nq = N_HEADS * HEAD_DIM
    nkc = N_KV_HEADS * HEAD_DIM
    L = DEC_SEQ
    return pl.pallas_call(
        functools.partial(_swa_sample_kernel, nb=nb),
        grid=(nbatch // nb,),
        in_specs=[
            pl.BlockSpec((N_KV_HEADS, GQA_GROUP * L, 1), lambda g: (0, 0, 0)),
            pl.BlockSpec((nb * L, nq + 2 * nkc), lambda g: (g, 0)),
            pl.BlockSpec((None, nb, WINDOW, nkc), lambda g: (layer, g, 0, 0)),
            pl.BlockSpec((None, nb, WINDOW, nkc), lambda g: (layer, g, 0, 0)),
            pl.BlockSpec((L, LANES), lambda g: (0, 0)),
            pl.BlockSpec((L, LANES), lambda g: (0, 0)),
        ],
        out_specs=[
            pl.BlockSpec((nb * L, nq), lambda g: (g, 0)),
            pl.BlockSpec((nb, WINDOW, nkc), lambda g: (g, 0, 0)),
            pl.BlockSpec((nb, WINDOW, nkc), lambda g: (g, 0, 0)),
        ],
        out_shape=[
            jax.ShapeDtypeStruct((nbatch * L, nq), BF16),
            jax.ShapeDtypeStruct((nbatch, WINDOW, nkc), F32),
            jax.ShapeDtypeStruct((nbatch, WINDOW, nkc), F32),
        ],
        compiler_params=_params("arbitrary"),
        name="swa_sample",
    )(sink_rows, qkv, cache_k, cache_v, cos, sin)


def _xattn_heads(q, mk, mv):
    outs = []
    for h in range(N_XHEADS):
        hs = slice(h * XHEAD_DIM, (h + 1) * XHEAD_DIM)
        qh = (q[:, hs] * (XHEAD_DIM ** -0.5)).astype(BF16)
        s = lax.dot_general(qh, mk[:, hs].astype(BF16), (((1,), (1,)), ((), ())),
                            preferred_element_type=F32)
        mx = jnp.max(s, axis=1, keepdims=True)
        p = jnp.exp(s - mx)
        den = jnp.sum(p, axis=1, keepdims=True)
        outs.append(jnp.dot(p.astype(BF16), mv[:, hs].astype(BF16),
                            preferred_element_type=F32) / den)
    return jnp.concatenate(outs, axis=1)


def _xattn_prompt_kernel(q_ref, mk_ref, mv_ref, o_ref):
    o_ref[...] = _xattn_heads(q_ref[...], mk_ref[...], mv_ref[...]).astype(o_ref.dtype)


def xattn_prompt(q, mk, mv, batch, tq):
    nq = SEQ // tq
    xw = N_XHEADS * XHEAD_DIM
    return pl.pallas_call(
        _xattn_prompt_kernel,
        grid=(batch, nq),
        in_specs=[
            pl.BlockSpec((tq, xw), lambda b, i: (b * nq + i, 0)),
            pl.BlockSpec((N_MEM, xw), lambda b, i: (b, 0)),
            pl.BlockSpec((N_MEM, xw), lambda b, i: (b, 0)),
        ],
        out_specs=pl.BlockSpec((tq, xw), lambda b, i: (b * nq + i, 0)),
        out_shape=jax.ShapeDtypeStruct((batch * SEQ, xw), BF16),
        compiler_params=_params("arbitrary", "arbitrary"),
        name="xattn_prompt",
    )(q, mk, mv)


def _xattn_sample_kernel(q_ref, mk_ref, mv_ref, o_ref, *, nb):
    L = DEC_SEQ
    for b in range(nb):
        o_ref[b * L:(b + 1) * L, :] = _xattn_heads(
            q_ref[b * L:(b + 1) * L, :], mk_ref[b], mv_ref[b]).astype(o_ref.dtype)


def xattn_sample(q, cache_k, cache_v, layer, nb=8):
    nbatch = cache_k.shape[1]
    L = DEC_SEQ
    xw = N_XHEADS * XHEAD_DIM
    return pl.pallas_call(
        functools.partial(_xattn_sample_kernel, nb=nb),
        grid=(nbatch // nb,),
        in_specs=[
            pl.BlockSpec((nb * L, xw), lambda g: (g, 0)),
            pl.BlockSpec((None, nb, N_MEM, xw), lambda g: (layer, g, 0, 0)),
            pl.BlockSpec((None, nb, N_MEM, xw), lambda g: (layer, g, 0, 0)),
        ],
        out_specs=pl.BlockSpec((nb * L, xw), lambda g: (g, 0)),
        out_shape=jax.ShapeDtypeStruct((nbatch * L, xw), BF16),
        compiler_params=_params("arbitrary"),
        name="xattn_sample",
    )(q, cache_k, cache_v)


def _s5_prompt_kernel(u_ref, h0r_ref, h0i_ref, a_ref, b_ref, c_ref, d_ref,
                      z_ref, hr_ref, hi_ref, x_ref, hst_ref, *, T):
    c = pl.program_id(1)
    nslab = 2 * SLABS
    G = T // SUBLANES

    @pl.when(c == 0)
    def _():
        for sb in range(N_SUPER):
            hst_ref[2 * sb] = h0r_ref[sb]
            hst_ref[2 * sb + 1] = h0i_ref[sb]

    def slab_rows(sb, s):
        return pl.ds((sb * nslab + s) * SUBLANES, SUBLANES)

    for sb in range(N_SUPER):
        bu = _bdot(u_ref[:, sb * SUPER_IN:(sb + 1) * SUPER_IN], b_ref[sb])
        for s in range(nslab):
            x_ref[:, slab_rows(sb, s), :] = bu[:, s * LANES:(s + 1) * LANES].reshape(
                G, SUBLANES, LANES)

    a = [a_ref[k] for k in range(2 * N_SUPER)]
    h_init = tuple(hst_ref[k] for k in range(2 * N_SUPER))

    def step(t, h):
        g = lax.shift_right_logical(t, 3)
        r = lax.bitwise_and(t, SUBLANES - 1)
        new = []
        for sb in range(N_SUPER):
            ir = pl.ds((sb * nslab) * SUBLANES + r, SUBLANES, stride=SUBLANES)
            ii = pl.ds((sb * nslab + SLABS) * SUBLANES + r, SUBLANES, stride=SUBLANES)
            ar, ai = a[2 * sb], a[2 * sb + 1]
            hr, hi = h[2 * sb], h[2 * sb + 1]
            nr = ar * hr - ai * hi + x_ref[g, ir, :]
            ni = ar * hi + ai * hr + x_ref[g, ii, :]
            x_ref[g, ir, :] = nr
            x_ref[g, ii, :] = ni
            new += [nr, ni]
        return tuple(new)

    h_fin = lax.fori_loop(0, T, step, h_init)
    for sb in range(N_SUPER):
        hst_ref[2 * sb] = h_fin[2 * sb]
        hst_ref[2 * sb + 1] = h_fin[2 * sb + 1]
        hr_ref[sb] = h_fin[2 * sb]
        hi_ref[sb] = h_fin[2 * sb + 1]

    for sb in range(N_SUPER):
        hcat = jnp.concatenate([x_ref[:, slab_rows(sb, s), :].reshape(T, LANES)
                                for s in range(nslab)], axis=1)
        cols = slice(sb * SUPER_IN, (sb + 1) * SUPER_IN)
        y = _bdot(hcat, c_ref[sb]) + d_ref[:, cols] * u_ref[:, cols]
        z_ref[:, cols] = jax.nn.gelu(y).astype(z_ref.dtype)


def s5_prompt(u, h0r, h0i, a_tiles, b_blk, c_blk, d_row, batch, T):
    nc = SEQ // T
    st = (None, N_SUPER, SUBLANES, LANES)
    return pl.pallas_call(
        functools.partial(_s5_prompt_kernel, T=T),
        grid=(batch, nc),
        in_specs=[
            pl.BlockSpec((T, D_MODEL), lambda b, c: (b * nc + c, 0)),
            pl.BlockSpec(st, lambda b, c: (b, 0, 0, 0)),
            pl.BlockSpec(st, lambda b, c: (b, 0, 0, 0)),
            _once((2 * N_SUPER, SUBLANES, LANES), lambda b, c: (0, 0, 0)),
            _once((N_SUPER, SUPER_IN, 2 * SUPER_STATE), lambda b, c: (0, 0, 0)),
            _once((N_SUPER, 2 * SUPER_STATE, SUPER_IN), lambda b, c: (0, 0, 0)),
            _once((1, D_MODEL), lambda b, c: (0, 0)),
        ],
        out_specs=[
            pl.BlockSpec((T, D_MODEL), lambda b, c: (b * nc + c, 0)),
            pl.BlockSpec(st, lambda b, c: (b, 0, 0, 0)),
            pl.BlockSpec(st, lambda b, c: (b, 0, 0, 0)),
        ],
        out_shape=[
            jax.ShapeDtypeStruct((batch * SEQ, D_MODEL), BF16),
            jax.ShapeDtypeStruct((batch, N_SUPER, SUBLANES, LANES), F32),
            jax.ShapeDtypeStruct((batch, N_SUPER, SUBLANES, LANES), F32),
        ],
        scratch_shapes=[
            pltpu.VMEM((T // SUBLANES, N_SUPER * 2 * SLABS * SUBLANES, LANES), F32),
            pltpu.VMEM((2 * N_SUPER, SUBLANES, LANES), F32),
        ],
        compiler_params=_params("arbitrary", "arbitrary"),
        name="s5_prompt",
    )(u, h0r, h0i, a_tiles, b_blk, c_blk, d_row)


def _s5_sample_kernel(u_ref, h0r_ref, h0i_ref, ar_ref, ai_ref, b_ref, c_ref, d_ref,
                      z_ref, hr_ref, hi_ref, x_ref, *, nbatch):
    L = DEC_SEQ
    R = nbatch * L
    for sb in range(N_SUPER):
        cols = slice(sb * SUPER_IN, (sb + 1) * SUPER_IN)
        bu = _bdot(u_ref[:, cols], b_ref[sb])
        for s in range(2 * SLABS):
            x_ref[s * R:(s + 1) * R, :] = bu[:, s * LANES:(s + 1) * LANES]
        for s in range(SLABS):
            st = slice(sb * SUPER_STATE + s * LANES, sb * SUPER_STATE + (s + 1) * LANES)
            ar, ai = ar_ref[:, st], ai_ref[:, st]
            hr, hi = h0r_ref[:, st], h0i_ref[:, st]
            for t in range(L):
                rr = pl.ds(s * R + t, nbatch, stride=L)
                ri = pl.ds((SLABS + s) * R + t, nbatch, stride=L)
                nr = ar * hr - ai * hi + x_ref[rr, :]
                ni = ar * hi + ai * hr + x_ref[ri, :]
                x_ref[rr, :] = nr
                x_ref[ri, :] = ni
                hr, hi = nr, ni
            hr_ref[:, st] = hr
            hi_ref[:, st] = hi
        hcat = jnp.concatenate([x_ref[s * R:(s + 1) * R, :] for s in range(2 * SLABS)], axis=1)
        y = _bdot(hcat, c_ref[sb]) + d_ref[:, cols] * u_ref[:, cols]
        z_ref[:, cols] = jax.nn.gelu(y).astype(z_ref.dtype)


def s5_sample(u, h0r, h0i, a_re, a_im, b_blk, c_blk, d_row):
    rows = u.shape[0]
    nbatch = rows // DEC_SEQ
    nstate = N_SSM_GROUPS * SSM_STATE
    full = lambda shape: pl.BlockSpec(shape, lambda i: (0,) * len(shape))
    return pl.pallas_call(
        functools.partial(_s5_sample_kernel, nbatch=nbatch),
        grid=(1,),
        in_specs=[
            full((rows, D_MODEL)), full((nbatch, nstate)), full((nbatch, nstate)),
            full((1, nstate)), full((1, nstate)),
            full((N_SUPER, SUPER_IN, 2 * SUPER_STATE)),
            full((N_SUPER, 2 * SUPER_STATE, SUPER_IN)),
            full((1, D_MODEL)),
        ],
        out_specs=[full((rows, D_MODEL)), full((nbatch, nstate)), full((nbatch, nstate))],
        out_shape=[
            jax.ShapeDtypeStruct((rows, D_MODEL), BF16),
            jax.ShapeDtypeStruct((nbatch, nstate), F32),
            jax.ShapeDtypeStruct((nbatch, nstate), F32),
        ],
        scratch_shapes=[pltpu.VMEM((2 * SLABS * rows, LANES), F32)],
        compiler_params=_params("arbitrary"),
        name="s5_sample",
    )(u, h0r, h0i, a_re, a_im, b_blk, c_blk, d_row)


def _s5_weights(lam_re, lam_im, log_step, b_re, b_im, c_re, c_im):
    lam = lax.complex(lam_re.astype(F32), lam_im.astype(F32))
    dt = jnp.exp(log_step.astype(F32))[:, None]
    abar = jnp.exp(lam * dt)
    bbar = ((abar - 1.0) / lam)[..., None] * lax.complex(b_re.astype(F32), b_im.astype(F32))
    eye = jnp.eye(SSM_SUPER, dtype=F32)

    def b_layout(m):
        m = m.reshape(N_SUPER, SSM_SUPER, SSM_STATE, SSM_GROUP)
        return jnp.einsum('sgph,gk->sghkp', m, eye).reshape(N_SUPER, SUPER_IN, SUPER_STATE)

    def c_layout(m):
        m = m.reshape(N_SUPER, SSM_SUPER, SSM_GROUP, SSM_STATE)
        return jnp.einsum('sghp,gk->sgpkh', m, eye).reshape(N_SUPER, SUPER_STATE, SUPER_IN)

    b_blk = jnp.concatenate([b_layout(bbar.real), b_layout(bbar.imag)], axis=2).astype(BF16)
    c_blk = jnp.concatenate([c_layout(c_re.astype(F32)), -c_layout(c_im.astype(F32))],
                            axis=1).astype(BF16)
    return abar.real, abar.imag, b_blk, c_blk


def _conv_gate(ua, uv, p1a, p2a, p1v, p2v, cwa_ref, cwv_ref, cba_ref, cbv_ref):
    ca = cba_ref[...] + cwa_ref[0:1, :] * p2a + cwa_ref[1:2, :] * p1a + cwa_ref[2:3, :] * ua
    cv = cbv_ref[...] + cwv_ref[0:1, :] * p2v + cwv_ref[1:2, :] * p1v + cwv_ref[2:3, :] * uv
    return (jax.nn.silu(ca) * cv).astype(BF16)


def _ffn_prompt_kernel(x_ref, gpre_ref, wua_ref, wuv_ref, cwa_ref, cwv_ref, cba_ref, cbv_ref,
                       wd_ref, gpost_ref, o_ref, sa_ref, sv_ref,
                       hn_ref, ua_ref, uv_ref, act_ref, ha_ref, hv_ref,
                       *, nc, blocks_per_seq, rc, rm):
    i = pl.program_id(0)
    c = pl.program_id(1)
    tm, D = hn_ref.shape
    hdr = SUBLANES

    @pl.when(c == 0)
    def _():
        hn_ref[...] = _rms(x_ref[...], gpre_ref[...]).astype(BF16)
        o_ref[...] = jnp.zeros(o_ref.shape, F32)

    slot = i % 2

    @pl.when(i % blocks_per_seq == 0)
    def _():
        ha_ref[1 - slot, c] = jnp.zeros(ha_ref.shape[2:], F32)
        hv_ref[1 - slot, c] = jnp.zeros(hv_ref.shape[2:], F32)

    ua_ref[0:hdr, :] = ha_ref[1 - slot, c]
    uv_ref[0:hdr, :] = hv_ref[1 - slot, c]
    wa = wua_ref[...].astype(BF16)
    wv = wuv_ref[...].astype(BF16)
    wd = wd_ref[...].astype(BF16)

    def conv(buf_ref, w_ref, b_ref, r0):
        return (b_ref[...] + w_ref[0:1, :] * buf_ref[r0 + hdr - 2:r0 + hdr - 2 + rc, :]
                + w_ref[1:2, :] * buf_ref[r0 + hdr - 1:r0 + hdr - 1 + rc, :]
                + w_ref[2:3, :] * buf_ref[r0 + hdr:r0 + hdr + rc, :])

    for k in range(tm // rm):
        rows = slice(k * rm, (k + 1) * rm)
        hk = hn_ref[rows, :]
        ua_ref[hdr + k * rm:hdr + (k + 1) * rm, :] = jnp.dot(hk, wa, preferred_element_type=F32)
        uv_ref[hdr + k * rm:hdr + (k + 1) * rm, :] = jnp.dot(hk, wv, preferred_element_type=F32)
        for r0 in range(k * rm, (k + 1) * rm, rc):
            ca = conv(ua_ref, cwa_ref, cba_ref, r0)
            cv = conv(uv_ref, cwv_ref, cbv_ref, r0)
            act_ref[r0:r0 + rc, :] = (jax.nn.silu(ca) * cv).astype(BF16)
        o_ref[rows, :] += jnp.dot(act_ref[rows, :], wd, preferred_element_type=F32)

    ta = ua_ref[tm:tm + hdr, :]
    tv = uv_ref[tm:tm + hdr, :]
    ha_ref[slot, c] = ta
    hv_ref[slot, c] = tv
    sa_ref[c] = ta[hdr - 2:hdr, :]
    sv_ref[c] = tv[hdr - 2:hdr, :]

    @pl.when(c == nc - 1)
    def _():
        o_ref[...] = x_ref[...] + _rms(o_ref[...], gpost_ref[...])


FFN_ROW_CHUNK = 64
FFN_ROW_GROUP = 256


def ffn_prompt(x, layer, g_pre, w_up, conv_w, conv_b, w_down, g_post, batch, tm, tf):
    M, D = x.shape
    nc = D_FF // tf
    bps = SEQ // tm
    conv_b3 = conv_b.reshape(DEPTH, 1, 2 * D_FF)
    y, sa, sv = pl.pallas_call(
        functools.partial(_ffn_prompt_kernel, nc=nc, blocks_per_seq=bps, rc=FFN_ROW_CHUNK,
                          rm=FFN_ROW_GROUP),
        grid=(M // tm, nc),
        in_specs=[
            _once((tm, D), lambda i, c: (i, 0)),
            pl.BlockSpec((None, 1, D), lambda i, c: (layer, 0, 0)),
            pl.BlockSpec((None, D, tf), lambda i, c: (layer, 0, c)),
            pl.BlockSpec((None, D, tf), lambda i, c: (layer, 0, nc + c)),
            pl.BlockSpec((None, 3, tf), lambda i, c: (layer, 0, c)),
            pl.BlockSpec((None, 3, tf), lambda i, c: (layer, 0, nc + c)),
            pl.BlockSpec((None, 1, tf), lambda i, c: (layer, 0, c)),
            pl.BlockSpec((None, 1, tf), lambda i, c: (layer, 0, nc + c)),
            pl.BlockSpec((None, tf, D), lambda i, c: (layer, c, 0)),
            pl.BlockSpec((None, 1, D), lambda i, c: (layer, 0, 0)),
        ],
        out_specs=[
            _once((tm, D), lambda i, c: (i, 0)),
            pl.BlockSpec((None, nc, 2, tf), lambda i, c: (i // bps, 0, 0, 0)),
            pl.BlockSpec((None, nc, 2, tf), lambda i, c: (i // bps, 0, 0, 0)),
        ],
        out_shape=[
            jax.ShapeDtypeStruct((M, D), F32),
            jax.ShapeDtypeStruct((batch, nc, 2, tf), F32),
            jax.ShapeDtypeStruct((batch, nc, 2, tf), F32),
        ],
        scratch_shapes=[
            pltpu.VMEM((tm, D), BF16),
            pltpu.VMEM((tm + SUBLANES, tf), F32),
            pltpu.VMEM((tm + SUBLANES, tf), F32),
            pltpu.VMEM((tm, tf), BF16),
            pltpu.VMEM((2, nc, SUBLANES, tf), F32),
            pltpu.VMEM((2, nc, SUBLANES, tf), F32),
        ],
        compiler_params=_params("arbitrary", "arbitrary"),
        name="ffn_prompt",
    )(x, g_pre.reshape(DEPTH, 1, D), w_up, w_up, conv_w, conv_w, conv_b3, conv_b3,
      w_down, g_post.reshape(DEPTH, 1, D))
    sa = sa.transpose(0, 2, 1, 3).reshape(batch, 2, D_FF)
    sv = sv.transpose(0, 2, 1, 3).reshape(batch, 2, D_FF)
    return y, jnp.concatenate([sa, sv], axis=-1)


def _ffn_sample_kernel(x_ref, gpre_ref, wua_ref, wuv_ref, cwa_ref, cwv_ref, cba_ref, cbv_ref,
                       wd_ref, gpost_ref, s1a_ref, s2a_ref, s1v_ref, s2v_ref,
                       o_ref, ua_ref, uv_ref, hn_ref, *, nc):
    c = pl.program_id(0)

    @pl.when(c == 0)
    def _():
        hn_ref[...] = _rms(x_ref[...], gpre_ref[...]).astype(BF16)

    hn = hn_ref[...]
    ua = jnp.dot(hn, wua_ref[...].astype(BF16), preferred_element_type=F32)
    uv = jnp.dot(hn, wuv_ref[...].astype(BF16), preferred_element_type=F32)
    ua_ref[...] = ua
    uv_ref[...] = uv
    t = lax.broadcasted_iota(jnp.int32, ua.shape, 0) % DEC_SEQ

    def shifted(u, s1_ref, s2_ref):
        p1 = jnp.where(t == 0, s1_ref[...], pltpu.roll(u, 1, 0))
        p2 = jnp.where(t <= 1, s2_ref[...], pltpu.roll(u, 2, 0))
        return p1, p2

    p1a, p2a = shifted(ua, s1a_ref, s2a_ref)
    p1v, p2v = shifted(uv, s1v_ref, s2v_ref)
    act = _conv_gate(ua, uv, p1a, p2a, p1v, p2v, cwa_ref, cwv_ref, cba_ref, cbv_ref)
    part = jnp.dot(act, wd_ref[...].astype(BF16), preferred_element_type=F32)

    @pl.when(c == 0)
    def _():
        o_ref[...] = part

    @pl.when(c > 0)
    def _():
        o_ref[...] += part

    @pl.when(c == nc - 1)
    def _():
        o_ref[...] = x_ref[...] + _rms(o_ref[...], gpost_ref[...])


def ffn_sample(x, layer, g_pre, w_up, conv_w, conv_b, w_down, g_post, conv_state, tf):
    M, D = x.shape
    nbatch = M // DEC_SEQ
    nc = D_FF // tf
    conv_b3 = conv_b.reshape(DEPTH, 1, 2 * D_FF)
    zeros = jnp.zeros((nbatch, DEC_SEQ - 2, 2 * D_FF), F32)
    s1 = jnp.concatenate([conv_state[:, 1:2], conv_state[:, 0:1] * 0, zeros], axis=1)
    s2 = jnp.concatenate([conv_state, zeros], axis=1)
    s1 = s1.reshape(M, 2 * D_FF)
    s2 = s2.reshape(M, 2 * D_FF)
    y, ua, uv = pl.pallas_call(
        functools.partial(_ffn_sample_kernel, nc=nc),
        grid=(nc,),
        in_specs=[
            pl.BlockSpec((M, D), lambda c: (0, 0)),
            pl.BlockSpec((None, 1, D), lambda c: (layer, 0, 0)),
            pl.BlockSpec((None, D, tf), lambda c: (layer, 0, c)),
            pl.BlockSpec((None, D, tf), lambda c: (layer, 0, nc + c)),
            pl.BlockSpec((None, 3, tf), lambda c: (layer, 0, c)),
            pl.BlockSpec((None, 3, tf), lambda c: (layer, 0, nc + c)),
            pl.BlockSpec((None, 1, tf), lambda c: (layer, 0, c)),
            pl.BlockSpec((None, 1, tf), lambda c: (layer, 0, nc + c)),
            pl.BlockSpec((None, tf, D), lambda c: (layer, c, 0)),
            pl.BlockSpec((None, 1, D), lambda c: (layer, 0, 0)),
            pl.BlockSpec((M, tf), lambda c: (0, c)),
            pl.BlockSpec((M, tf), lambda c: (0, c)),
            pl.BlockSpec((M, tf), lambda c: (0, nc + c)),
            pl.BlockSpec((M, tf), lambda c: (0, nc + c)),
        ],
        out_specs=[
            pl.BlockSpec((M, D), lambda c: (0, 0)),
            pl.BlockSpec((M, tf), lambda c: (0, c)),
            pl.BlockSpec((M, tf), lambda c: (0, c)),
        ],
        out_shape=[
            jax.ShapeDtypeStruct((M, D), F32),
            jax.ShapeDtypeStruct((M, D_FF), F32),
            jax.ShapeDtypeStruct((M, D_FF), F32),
        ],
        scratch_shapes=[pltpu.VMEM((M, D), BF16)],
        compiler_params=_params("arbitrary"),
        name="ffn_sample",
    )(x, g_pre.reshape(DEPTH, 1, D), w_up, w_up, conv_w, conv_w, conv_b3, conv_b3,
      w_down, g_post.reshape(DEPTH, 1, D), s1, s2, s1, s2)
    up = jnp.concatenate([ua, uv], axis=-1).reshape(nbatch, DEC_SEQ, 2 * D_FF)
    return y, up[:, DEC_SEQ - 2:]


def kernel(x_prompt, x_sample, mem_prompt, cache_swa_k, cache_swa_v, state_ssm_re, state_ssm_im, state_ffn_conv, cache_mem_k, cache_mem_v, g_mix_pre, g_mix_post, w_qkv, w_attn_o, attn_sinks, w_ssm_in, ssm_lambda_re, ssm_lambda_im, ssm_log_step, ssm_b_re, ssm_b_im, ssm_c_re, ssm_c_im, ssm_d, w_ssm_glu, g_x_pre, g_x_post, g_mem, w_x_q, w_mem_k, w_mem_v, w_x_o, g_ffn_pre, g_ffn_post, w_ffn_up, ffn_conv_w, ffn_conv_b, w_ffn_down):
    B = x_prompt.shape[0]
    SB = x_sample.shape[0]
    xw = N_XHEADS * XHEAD_DIM
    nkc = N_KV_HEADS * HEAD_DIM
    xp = x_prompt.reshape(B * SEQ, D_MODEL)
    xs = x_sample.reshape(SB * DEC_SEQ, D_MODEL)
    mem = mem_prompt.reshape(B * N_MEM, D_MODEL)
    MS = SB * DEC_SEQ
    TMP = 1024
    cos_p, sin_p = _rope_tables(jnp.arange(SEQ))
    cos_s, sin_s = _rope_tables(PAST_LEN + jnp.arange(DEC_SEQ))
    ck_all = cache_swa_k.reshape(cache_swa_k.shape[0], SB, WINDOW, nkc)
    cv_all = cache_swa_v.reshape(cache_swa_v.shape[0], SB, WINDOW, nkc)
    cmk_all = cache_mem_k.reshape(DEPTH, SB, N_MEM, xw)
    cmv_all = cache_mem_v.reshape(DEPTH, SB, N_MEM, xw)
    w_up_bf = w_ffn_up.astype(BF16)
    w_down_bf = w_ffn_down.astype(BF16)

    swa_kp, swa_vp, swa_ks, swa_vs = [], [], [], []
    ssm_rp, ssm_ip, ssm_rs, ssm_is = [], [], [], []
    conv_p, conv_s, memk_p, memv_p = [], [], [], []
    for i in range(DEPTH):
        j = i // 2
        if i % 2 == 0:
            qkv_p = norm_matmul(xp, g_mix_pre, (i, j), w_qkv, TMP, 512)
            qkv_s = norm_matmul(xs, g_mix_pre, (i, j), w_qkv, MS, 512)
            op, kp, vp = swa_prompt(qkv_p, attn_sinks, j, cos_p, sin_p, B)
            sink_rows = jnp.repeat(attn_sinks[j].reshape(N_KV_HEADS, GQA_GROUP), DEC_SEQ,
                                   axis=1)[..., None]
            os_, kn, vn = swa_sample(qkv_s, ck_all, cv_all, j, sink_rows, cos_s, sin_s)
            swa_kp.append(kp.reshape(B, WINDOW, N_KV_HEADS, HEAD_DIM))
            swa_vp.append(vp.reshape(B, WINDOW, N_KV_HEADS, HEAD_DIM))
            swa_ks.append(kn.reshape(SB, WINDOW, N_KV_HEADS, HEAD_DIM))
            swa_vs.append(vn.reshape(SB, WINDOW, N_KV_HEADS, HEAD_DIM))
            xp = matmul_post(op, w_attn_o, j, g_mix_post, i, xp, TMP, 512)
            xs = matmul_post(os_, w_attn_o, j, g_mix_post, i, xs, MS, 512)
        else:
            a_re, a_im, b_blk, c_blk = _s5_weights(
                ssm_lambda_re[j], ssm_lambda_im[j], ssm_log_step[j], ssm_b_re[j], ssm_b_im[j],
                ssm_c_re[j], ssm_c_im[j])
            d_row = ssm_d[j].reshape(1, D_MODEL)
            up_ = norm_matmul(xp, g_mix_pre, (i, j), w_ssm_in, TMP, 512)
            us_ = norm_matmul(xs, g_mix_pre, (i, j), w_ssm_in, MS, 512)
            a_tiles = jnp.stack([a_re.reshape(N_SUPER, SUBLANES, LANES),
                                 a_im.reshape(N_SUPER, SUBLANES, LANES)], axis=1
                                ).reshape(2 * N_SUPER, SUBLANES, LANES)
            zero_state = jnp.zeros((B, N_SUPER, SUBLANES, LANES), F32)
            zp, rp, ip = s5_prompt(up_, zero_state, zero_state, a_tiles, b_blk, c_blk, d_row,
                                   B, 256)
            nstate = N_SSM_GROUPS * SSM_STATE
            zs, rn, im_ = s5_sample(us_, state_ssm_re[j].reshape(SB, nstate),
                                    state_ssm_im[j].reshape(SB, nstate),
                                    a_re.reshape(1, nstate), a_im.reshape(1, nstate),
                                    b_blk, c_blk, d_row)
            ssm_rp.append(rp.reshape(B, N_SSM_GROUPS, SSM_STATE))
            ssm_ip.append(ip.reshape(B, N_SSM_GROUPS, SSM_STATE))
            ssm_rs.append(rn.reshape(SB, N_SSM_GROUPS, SSM_STATE))
            ssm_is.append(im_.reshape(SB, N_SSM_GROUPS, SSM_STATE))
            xp = glu_post(zp, w_ssm_glu, j, g_mix_post, i, xp, TMP, 256)
            xs = glu_post(zs, w_ssm_glu, j, g_mix_post, i, xs, MS, 256)
        mk = norm_matmul(mem, g_mem, (i, i), w_mem_k, B * N_MEM, xw)
        mv = norm_matmul(mem, g_mem, (i, i), w_mem_v, B * N_MEM, xw)
        memk_p.append(mk.reshape(B, N_MEM, N_XHEADS, XHEAD_DIM))
        memv_p.append(mv.reshape(B, N_MEM, N_XHEADS, XHEAD_DIM))
        qp = norm_matmul(xp, g_x_pre, (i, i), w_x_q, TMP, xw)
        qs = norm_matmul(xs, g_x_pre, (i, i), w_x_q, MS, xw)
        ap = xattn_prompt(qp, mk, mv, B, 512)
        as_ = xattn_sample(qs, cmk_all, cmv_all, i)
        xp = matmul_post(ap, w_x_o, i, g_x_post, i, xp, TMP, xw)
        xs = matmul_post(as_, w_x_o, i, g_x_post, i, xs, MS, xw)
        xp, cp = ffn_prompt(xp, i, g_ffn_pre, w_up_bf, ffn_conv_w, ffn_conv_b, w_down_bf,
                            g_ffn_post, B, TMP, 512)
        xs, cs = ffn_sample(xs, i, g_ffn_pre, w_up_bf, ffn_conv_w, ffn_conv_b, w_down_bf,
                            g_ffn_post, state_ffn_conv[i], 512)
        conv_p.append(cp)
        conv_s.append(cs)
    return (xp.reshape(B, SEQ, D_MODEL), xs.reshape(SB, DEC_SEQ, D_MODEL),
            jnp.stack(swa_kp), jnp.stack(swa_vp), jnp.stack(swa_ks), jnp.stack(swa_vs),
            jnp.stack(ssm_rp), jnp.stack(ssm_ip), jnp.stack(ssm_rs), jnp.stack(ssm_is),
            jnp.stack(conv_p), jnp.stack(conv_s), jnp.stack(memk_p), jnp.stack(memv_p))
```

```python
import functools
import math

import jax
import jax.numpy as jnp
from jax import lax
from jax.experimental import pallas as pl
from jax.experimental.pallas import tpu as pltpu

D_MODEL = 2048
SEQ = 4096
DEPTH = 4
DEC_SEQ = 8
PAST_LEN = 16384
HEAD_DIM = 64
N_HEADS = 32
N_KV_HEADS = 4
GQA_GROUP = 8
WINDOW = 128
ROPE_THETA = 10000.0
SSM_GROUP = 16
N_SSM_GROUPS = 128
SSM_STATE = 64
N_MEM = 256
N_XHEADS = 4
XHEAD_DIM = 128
D_FF = 5632
NORM_EPS = 1e-6
NEG_INF = -1e30

F32 = jnp.float32
BF16 = jnp.bfloat16

V7X_VMEM_BYTES = 64 * 1024 * 1024
VMEM_LIMIT = V7X_VMEM_BYTES - 8 * 1024 * 1024
LANES = 128
SUBLANES = 8
SSM_SUPER = 16
N_SUPER = N_SSM_GROUPS // SSM_SUPER
SUPER_IN = SSM_SUPER * SSM_GROUP
SUPER_STATE = SSM_SUPER * SSM_STATE
SLABS = SUPER_STATE // LANES


def _params(*sem):
    return pltpu.CompilerParams(dimension_semantics=sem, vmem_limit_bytes=VMEM_LIMIT)


def _rms(x, g):
    ms = jnp.mean(x * x, axis=-1, keepdims=True)
    return x * lax.rsqrt(ms + NORM_EPS) * g


def _bdot(a, b):
    return jnp.dot(a.astype(BF16), b.astype(BF16), preferred_element_type=F32)


def _once(shape, index_map):
    return pl.BlockSpec(shape, index_map, pipeline_mode=pl.Buffered(1))


ROW_GROUP = 256


def _resident(shape, index_map):
    return pl.BlockSpec(shape, index_map, pipeline_mode=pl.Buffered(1))


def _row_groups(tm):
    rg = min(ROW_GROUP, tm)
    return [slice(k * rg, (k + 1) * rg) for k in range(tm // rg)]


def _norm_matmul_kernel(x_ref, g_ref, w_ref, o_ref):
    g = g_ref[...]
    for rows in _row_groups(x_ref.shape[0]):
        hn = _rms(x_ref[rows, :], g).astype(BF16)
        o_ref[rows, :] = jnp.dot(hn, w_ref[...], preferred_element_type=F32).astype(o_ref.dtype)


def norm_matmul(x, g, g_layer, w, w_layer, tm, out_dtype=F32):
    M, D = x.shape
    N = w.shape[-1]
    return pl.pallas_call(
        _norm_matmul_kernel,
        grid=(M // tm,),
        in_specs=[
            pl.BlockSpec((tm, D), lambda i: (i, 0)),
            pl.BlockSpec((None, 1, D), lambda i: (g_layer, 0, 0)),
            _resident((None, D, N), lambda i: (w_layer, 0, 0)),
        ],
        out_specs=pl.BlockSpec((tm, N), lambda i: (i, 0)),
        out_shape=jax.ShapeDtypeStruct((M, N), out_dtype),
        compiler_params=_params("arbitrary"),
        name="norm_matmul",
    )(x, g.reshape(g.shape[0], 1, D), w)


def _matmul_post_kernel(a_ref, w_ref, g_ref, x_ref, o_ref):
    g = g_ref[...]
    for rows in _row_groups(x_ref.shape[0]):
        y = jnp.dot(a_ref[rows, :].astype(BF16), w_ref[...], preferred_element_type=F32)
        o_ref[rows, :] = x_ref[rows, :] + _rms(y, g)


def matmul_post(a, w, w_layer, g, g_layer, x, tm):
    M, K = a.shape
    D = w.shape[-1]
    return pl.pallas_call(
        _matmul_post_kernel,
        grid=(M // tm,),
        in_specs=[
            pl.BlockSpec((tm, K), lambda i: (i, 0)),
            _resident((None, K, D), lambda i: (w_layer, 0, 0)),
            pl.BlockSpec((None, 1, D), lambda i: (g_layer, 0, 0)),
            pl.BlockSpec((tm, D), lambda i: (i, 0)),
        ],
        out_specs=pl.BlockSpec((tm, D), lambda i: (i, 0)),
        out_shape=jax.ShapeDtypeStruct((M, D), F32),
        compiler_params=_params("arbitrary"),
        name="matmul_post",
    )(a, w, g.reshape(g.shape[0], 1, D), x)


GLU_COLS = 512


def _glu_post_kernel(z_ref, w_ref, g_ref, x_ref, o_ref, y_ref):
    D = o_ref.shape[1]
    g = g_ref[...]
    for rows in _row_groups(x_ref.shape[0]):
        z = z_ref[rows, :]
        for n in range(D // GLU_COLS):
            cols = slice(n * GLU_COLS, (n + 1) * GLU_COLS)
            gcols = slice(D + n * GLU_COLS, D + (n + 1) * GLU_COLS)
            val = jnp.dot(z, w_ref[:, cols], preferred_element_type=F32)
            gate = jnp.dot(z, w_ref[:, gcols], preferred_element_type=F32)
            y_ref[rows, cols] = val * jax.nn.sigmoid(gate)
        o_ref[rows, :] = x_ref[rows, :] + _rms(y_ref[rows, :], g)


def glu_post(z, w, w_layer, g, g_layer, x, tm):
    M, K = z.shape
    D = D_MODEL
    return pl.pallas_call(
        _glu_post_kernel,
        grid=(M // tm,),
        in_specs=[
            pl.BlockSpec((tm, K), lambda i: (i, 0)),
            _resident((None, K, 2 * D), lambda i: (w_layer, 0, 0)),
            pl.BlockSpec((None, 1, D), lambda i: (g_layer, 0, 0)),
            pl.BlockSpec((tm, D), lambda i: (i, 0)),
        ],
        out_specs=pl.BlockSpec((tm, D), lambda i: (i, 0)),
        out_shape=jax.ShapeDtypeStruct((M, D), F32),
        scratch_shapes=[pltpu.VMEM((tm, D), F32)],
        compiler_params=_params("arbitrary"),
        name="glu_post",
    )(z, w, g.reshape(g.shape[0], 1, D), x)


def _rope_chunk(xc, cos, sin_signed):
    lane = lax.broadcasted_iota(jnp.int32, xc.shape, 1)
    first_half = (lane & (HEAD_DIM - 1)) < (HEAD_DIM // 2)
    partner = jnp.where(first_half, pltpu.roll(xc, LANES - HEAD_DIM // 2, 1),
                        pltpu.roll(xc, HEAD_DIM // 2, 1))
    return xc * cos + partner * sin_signed


def _rope_tables(pos):
    half = HEAD_DIM // 2
    inv_freq = ROPE_THETA ** (-jnp.arange(half, dtype=F32) * 2.0 / HEAD_DIM)
    ang = pos.astype(F32)[:, None] * inv_freq[None, :]
    cos, sin = jnp.cos(ang), jnp.sin(ang)
    cos128 = jnp.tile(cos, (1, LANES // half))
    sin128 = jnp.tile(jnp.concatenate([-sin, sin], axis=1), (1, LANES // HEAD_DIM))
    return cos128, sin128


def _swa_prompt_kernel(sink_ref, q_ref, kv_ref, cos_ref, sin_ref,
                       o_ref, kc_ref, vc_ref, kk_ref, vv_ref, *, layer):
    i = pl.program_id(1)
    blk = WINDOW
    cos, sn = cos_ref[...], sin_ref[...]
    kv = kv_ref[...]
    nkc = N_KV_HEADS * HEAD_DIM
    k = jnp.concatenate([_rope_chunk(kv[:, c * LANES:(c + 1) * LANES], cos, sn)
                         for c in range(nkc // LANES)], axis=1)
    v = kv[:, nkc:2 * nkc]
    kc_ref[...] = k
    vc_ref[...] = v

    cur = i % 2
    prv = 1 - cur

    @pl.when(i == 0)
    def _():
        kk_ref[1] = jnp.zeros((blk, nkc), BF16)
        vv_ref[1] = jnp.zeros((blk, nkc), BF16)

    kb, vb = k.astype(BF16), v.astype(BF16)
    kk_ref[cur] = kb
    vv_ref[cur] = vb
    kk = jnp.concatenate([kk_ref[prv], kb], axis=0)
    vv = jnp.concatenate([vv_ref[prv], vb], axis=0)
    rows = GQA_GROUP * blk
    qi = lax.broadcasted_iota(jnp.int32, (rows, 2 * blk), 0) & (blk - 1)
    si = lax.broadcasted_iota(jnp.int32, (rows, 2 * blk), 1)
    d = blk + qi - si
    mask = (d >= 0) & (d < WINDOW) & ((si >= blk) | (i > 0))

    chunks_per_group = GQA_GROUP * HEAD_DIM // LANES
    outs = []
    for kh in range(N_KV_HEADS):
        hs = slice(kh * HEAD_DIM, (kh + 1) * HEAD_DIM)
        pieces = []
        for c in range(kh * chunks_per_group, (kh + 1) * chunks_per_group):
            qc = (_rope_chunk(q_ref[:, c * LANES:(c + 1) * LANES], cos, sn)
                  * (HEAD_DIM ** -0.5)).astype(BF16)
            pieces += [qc[:, hh * HEAD_DIM:(hh + 1) * HEAD_DIM] for hh in range(LANES // HEAD_DIM)]
        qs = jnp.concatenate(pieces, axis=0)
        sk = jnp.concatenate([jnp.full((blk, 1), sink_ref[layer, kh * GQA_GROUP + j], F32)
                              for j in range(GQA_GROUP)], axis=0)
        s = lax.dot_general(qs, kk[:, hs], (((1,), (1,)), ((), ())),
                            preferred_element_type=F32)
        s = jnp.where(mask, s, NEG_INF)
        mx = jnp.maximum(jnp.max(s, axis=1, keepdims=True), sk)
        p = jnp.exp(s - mx)
        den = jnp.sum(p, axis=1, keepdims=True) + jnp.exp(sk - mx)
        o = jnp.dot(p.astype(BF16), vv[:, hs], preferred_element_type=F32) / den
        outs += [o[j * blk:(j + 1) * blk, :] for j in range(GQA_GROUP)]
    o_ref[...] = jnp.concatenate(outs, axis=1).astype(o_ref.dtype)


def swa_prompt(qkv, sinks, layer, cos, sin, batch):
    blk = WINDOW
    nb = SEQ // blk
    nq = N_HEADS * HEAD_DIM
    nkv = 2 * N_KV_HEADS * HEAD_DIM
    o, kc, vc = pl.pallas_call(
        functools.partial(_swa_prompt_kernel, layer=layer),
        grid=(batch, nb),
        in_specs=[
            pl.BlockSpec(memory_space=pltpu.SMEM),
            pl.BlockSpec((blk, nq), lambda b, i: (b * nb + i, 0)),
            pl.BlockSpec((blk, nkv), lambda b, i: (b * nb + i, nq // nkv)),
            pl.BlockSpec((blk, LANES), lambda b, i: (i, 0)),
            pl.BlockSpec((blk, LANES), lambda b, i: (i, 0)),
        ],
        out_specs=[
            pl.BlockSpec((blk, nq), lambda b, i: (b * nb + i, 0)),
            pl.BlockSpec((None, blk, nkv // 2), lambda b, i: (b, 0, 0)),
            pl.BlockSpec((None, blk, nkv // 2), lambda b, i: (b, 0, 0)),
        ],
        out_shape=[
            jax.ShapeDtypeStruct((batch * SEQ, nq), BF16),
            jax.ShapeDtypeStruct((batch, blk, nkv // 2), F32),
            jax.ShapeDtypeStruct((batch, blk, nkv // 2), F32),
        ],
        scratch_shapes=[pltpu.VMEM((2, blk, nkv // 2), BF16),
                        pltpu.VMEM((2, blk, nkv // 2), BF16)],
        compiler_params=_params("arbitrary", "arbitrary"),
        name="swa_prompt",
    )(sinks, qkv, qkv, cos, sin)
    return o, kc, vc


def _swa_sample_kernel(sink_ref, qkv_ref, ck_ref, cv_ref, cos_ref, sin_ref,
                       o_ref, nk_ref, nv_ref, *, nb):
    L = DEC_SEQ
    wb = WINDOW
    nq = N_HEADS * HEAD_DIM
    nkc = N_KV_HEADS * HEAD_DIM
    cos, sn = cos_ref[...], sin_ref[...]
    rows = GQA_GROUP * L
    t_q = lax.broadcasted_iota(jnp.int32, (rows, wb), 0) % L
    c_k = lax.broadcasted_iota(jnp.int32, (rows, wb), 1)
    mask_c = c_k >= t_q + 1
    t_q2 = lax.broadcasted_iota(jnp.int32, (rows, L), 0) % L
    t_k2 = lax.broadcasted_iota(jnp.int32, (rows, L), 1)
    mask_n = t_k2 <= t_q2
    for b in range(nb):
        x = qkv_ref[b * L:(b + 1) * L, :]
        qr = jnp.concatenate([_rope_chunk(x[:, c * LANES:(c + 1) * LANES], cos, sn)
                              for c in range(nq // LANES)], axis=1) * (HEAD_DIM ** -0.5)
        kn = jnp.concatenate([_rope_chunk(x[:, nq + c * LANES:nq + (c + 1) * LANES], cos, sn)
                              for c in range(nkc // LANES)], axis=1)
        vn = x[:, nq + nkc:nq + 2 * nkc]
        ck = ck_ref[b]
        cv = cv_ref[b]
        nk_ref[b, 0:wb - L, :] = ck[L:wb, :]
        nk_ref[b, wb - L:wb, :] = kn
        nv_ref[b, 0:wb - L, :] = cv[L:wb, :]
        nv_ref[b, wb - L:wb, :] = vn
        ckb, cvb, knb, vnb = ck.astype(BF16), cv.astype(BF16), kn.astype(BF16), vn.astype(BF16)
        outs = []
        for kh in range(N_KV_HEADS):
            hs = slice(kh * HEAD_DIM, (kh + 1) * HEAD_DIM)
            qs = jnp.concatenate(
                [qr[:, (kh * GQA_GROUP + j) * HEAD_DIM:(kh * GQA_GROUP + j + 1) * HEAD_DIM]
                 for j in range(GQA_GROUP)], axis=0).astype(BF16)
            dn = (((1,), (1,)), ((), ()))
            s_c = lax.dot_general(qs, ckb[:, hs], dn, preferred_element_type=F32)
            s_n = lax.dot_general(qs, knb[:, hs], dn, preferred_element_type=F32)
            s_c = jnp.where(mask_c, s_c, NEG_INF)
            s_n = jnp.where(mask_n, s_n, NEG_INF)
            sk = sink_ref[kh]
            mx = jnp.maximum(jnp.maximum(jnp.max(s_c, axis=1, keepdims=True),
                                         jnp.max(s_n, axis=1, keepdims=True)), sk)
            p_c = jnp.exp(s_c - mx)
            p_n = jnp.exp(s_n - mx)
            den = (jnp.sum(p_c, axis=1, keepdims=True) + jnp.sum(p_n, axis=1, keepdims=True)
                   + jnp.exp(sk - mx))
            o = (jnp.dot(p_c.astype(BF16), cvb[:, hs], preferred_element_type=F32)
                 + jnp.dot(p_n.astype(BF16), vnb[:, hs], preferred_element_type=F32)) / den
            outs.extend(o[j * L:(j + 1) * L, :] for j in range(GQA_GROUP))
        o_ref[b * L:(b + 1) * L, :] = jnp.concatenate(outs, axis=1).astype(o_ref.dtype)


def swa_sample(qkv, cache_k, cache_v, layer, sink_rows, cos, sin, nb=8):
    nbatch = cache_k.shape[1]
    nq = N_HEADS * HEAD_DIM
    nkc = N_KV_HEADS * HEAD_DIM
    L = DEC_SEQ
    return pl.pallas_call(
        functools.partial(_swa_sample_kernel, nb=nb),
        grid=(nbatch // nb,),
        in_specs=[
            pl.BlockSpec((N_KV_HEADS, GQA_GROUP * L, 1), lambda g: (0, 0, 0)),
            pl.BlockSpec((nb * L, nq + 2 * nkc), lambda g: (g, 0)),
            pl.BlockSpec((None, nb, WINDOW, nkc), lambda g: (layer, g, 0, 0)),
            pl.BlockSpec((None, nb, WINDOW, nkc), lambda g: (layer, g, 0, 0)),
            pl.BlockSpec((L, LANES), lambda g: (0, 0)),
            pl.BlockSpec((L, LANES), lambda g: (0, 0)),
        ],
        out_specs=[
            pl.BlockSpec((nb * L, nq), lambda g: (g, 0)),
            pl.BlockSpec((nb, WINDOW, nkc), lambda g: (g, 0, 0)),
            pl.BlockSpec((nb, WINDOW, nkc), lambda g: (g, 0, 0)),
        ],
        out_shape=[
            jax.ShapeDtypeStruct((nbatch * L, nq), BF16),
            jax.ShapeDtypeStruct((nbatch, WINDOW, nkc), F32),
            jax.ShapeDtypeStruct((nbatch, WINDOW, nkc), F32),
        ],
        compiler_params=_params("arbitrary"),
        name="swa_sample",
    )(sink_rows, qkv, cache_k, cache_v, cos, sin)


def _xattn_heads(q, mk, mv):
    outs = []
    for h in range(N_XHEADS):
        hs = slice(h * XHEAD_DIM, (h + 1) * XHEAD_DIM)
        qh = (q[:, hs] * (XHEAD_DIM ** -0.5)).astype(BF16)
        s = lax.dot_general(qh, mk[:, hs].astype(BF16), (((1,), (1,)), ((), ())),
                            preferred_element_type=F32)
        mx = jnp.max(s, axis=1, keepdims=True)
        p = jnp.exp(s - mx)
        den = jnp.sum(p, axis=1, keepdims=True)
        outs.append(jnp.dot(p.astype(BF16), mv[:, hs].astype(BF16),
                            preferred_element_type=F32) / den)
    return jnp.concatenate(outs, axis=1)


def _xattn_prompt_kernel(x_ref, gpre_ref, wq_ref, mk_ref, mv_ref, wo_ref, gpost_ref, o_ref):
    mk = mk_ref[...].astype(BF16)
    mv = mv_ref[...].astype(BF16)
    gpre, gpost = gpre_ref[...], gpost_ref[...]
    for rows in _row_groups(x_ref.shape[0]):
        x = x_ref[rows, :]
        q = jnp.dot(_rms(x, gpre).astype(BF16), wq_ref[...], preferred_element_type=F32)
        a = _xattn_heads(q, mk, mv).astype(BF16)
        y = jnp.dot(a, wo_ref[...], preferred_element_type=F32)
        o_ref[rows, :] = x + _rms(y, gpost)


def xattn_prompt(x, layer, g_pre, w_q, mk, mv, w_o, g_post, batch, tm):
    nq = SEQ // tm
    xw = N_XHEADS * XHEAD_DIM
    D = D_MODEL
    return pl.pallas_call(
        _xattn_prompt_kernel,
        grid=(batch, nq),
        in_specs=[
            pl.BlockSpec((tm, D), lambda b, i: (b * nq + i, 0)),
            pl.BlockSpec((None, 1, D), lambda b, i: (layer, 0, 0)),
            _resident((None, D, xw), lambda b, i: (layer, 0, 0)),
            pl.BlockSpec((N_MEM, xw), lambda b, i: (b, 0)),
            pl.BlockSpec((N_MEM, xw), lambda b, i: (b, 0)),
            _resident((None, xw, D), lambda b, i: (layer, 0, 0)),
            pl.BlockSpec((None, 1, D), lambda b, i: (layer, 0, 0)),
        ],
        out_specs=pl.BlockSpec((tm, D), lambda b, i: (b * nq + i, 0)),
        out_shape=jax.ShapeDtypeStruct((batch * SEQ, D), F32),
        compiler_params=_params("arbitrary", "arbitrary"),
        name="xattn_prompt",
    )(x, g_pre.reshape(DEPTH, 1, D), w_q, mk, mv, w_o, g_post.reshape(DEPTH, 1, D))


def _xattn_sample_kernel(q_ref, mk_ref, mv_ref, o_ref, *, nb):
    L = DEC_SEQ
    for b in range(nb):
        o_ref[b * L:(b + 1) * L, :] = _xattn_heads(
            q_ref[b * L:(b + 1) * L, :], mk_ref[b], mv_ref[b]).astype(o_ref.dtype)


def xattn_sample(q, cache_k, cache_v, layer, nb=8):
    nbatch = cache_k.shape[1]
    L = DEC_SEQ
    xw = N_XHEADS * XHEAD_DIM
    return pl.pallas_call(
        functools.partial(_xattn_sample_kernel, nb=nb),
        grid=(nbatch // nb,),
        in_specs=[
            pl.BlockSpec((nb * L, xw), lambda g: (g, 0)),
            pl.BlockSpec((None, nb, N_MEM, xw), lambda g: (layer, g, 0, 0)),
            pl.BlockSpec((None, nb, N_MEM, xw), lambda g: (layer, g, 0, 0)),
        ],
        out_specs=pl.BlockSpec((nb * L, xw), lambda g: (g, 0)),
        out_shape=jax.ShapeDtypeStruct((nbatch * L, xw), BF16),
        compiler_params=_params("arbitrary"),
        name="xattn_sample",
    )(q, cache_k, cache_v)


def _s5_prompt_kernel(u_ref, h0r_ref, h0i_ref, a_ref, b_ref, c_ref, d_ref,
                      z_ref, hr_ref, hi_ref, x_ref, hst_ref, *, T):
    c = pl.program_id(1)
    nslab = 2 * SLABS
    G = T // SUBLANES

    @pl.when(c == 0)
    def _():
        for sb in range(N_SUPER):
            hst_ref[2 * sb] = h0r_ref[sb]
            hst_ref[2 * sb + 1] = h0i_ref[sb]

    def slab_rows(sb, s):
        return pl.ds((sb * nslab + s) * SUBLANES, SUBLANES)

    for sb in range(N_SUPER):
        bu = _bdot(u_ref[:, sb * SUPER_IN:(sb + 1) * SUPER_IN], b_ref[sb])
        for s in range(nslab):
            x_ref[:, slab_rows(sb, s), :] = bu[:, s * LANES:(s + 1) * LANES].reshape(
                G, SUBLANES, LANES)

    a = [a_ref[k] for k in range(2 * N_SUPER)]
    h_init = tuple(hst_ref[k] for k in range(2 * N_SUPER))

    def step(t, h):
        g = lax.shift_right_logical(t, 3)
        r = lax.bitwise_and(t, SUBLANES - 1)
        new = []
        for sb in range(N_SUPER):
            ir = pl.ds((sb * nslab) * SUBLANES + r, SUBLANES, stride=SUBLANES)
            ii = pl.ds((sb * nslab + SLABS) * SUBLANES + r, SUBLANES, stride=SUBLANES)
            ar, ai = a[2 * sb], a[2 * sb + 1]
            hr, hi = h[2 * sb], h[2 * sb + 1]
            nr = ar * hr - ai * hi + x_ref[g, ir, :]
            ni = ar * hi + ai * hr + x_ref[g, ii, :]
            x_ref[g, ir, :] = nr
            x_ref[g, ii, :] = ni
            new += [nr, ni]
        return tuple(new)

    h_fin = lax.fori_loop(0, T, step, h_init)
    for sb in range(N_SUPER):
        hst_ref[2 * sb] = h_fin[2 * sb]
        hst_ref[2 * sb + 1] = h_fin[2 * sb + 1]
        hr_ref[sb] = h_fin[2 * sb]
        hi_ref[sb] = h_fin[2 * sb + 1]

    for sb in range(N_SUPER):
        hcat = jnp.concatenate([x_ref[:, slab_rows(sb, s), :].reshape(T, LANES)
                                for s in range(nslab)], axis=1)
        cols = slice(sb * SUPER_IN, (sb + 1) * SUPER_IN)
        y = _bdot(hcat, c_ref[sb]) + d_ref[:, cols] * u_ref[:, cols]
        z_ref[:, cols] = jax.nn.gelu(y).astype(z_ref.dtype)


def s5_prompt(u, h0r, h0i, a_tiles, b_blk, c_blk, d_row, batch, T):
    nc = SEQ // T
    st = (None, N_SUPER, SUBLANES, LANES)
    return pl.pallas_call(
        functools.partial(_s5_prompt_kernel, T=T),
        grid=(batch, nc),
        in_specs=[
            pl.BlockSpec((T, D_MODEL), lambda b, c: (b * nc + c, 0)),
            pl.BlockSpec(st, lambda b, c: (b, 0, 0, 0)),
            pl.BlockSpec(st, lambda b, c: (b, 0, 0, 0)),
            _once((2 * N_SUPER, SUBLANES, LANES), lambda b, c: (0, 0, 0)),
            _once((N_SUPER, SUPER_IN, 2 * SUPER_STATE), lambda b, c: (0, 0, 0)),
            _once((N_SUPER, 2 * SUPER_STATE, SUPER_IN), lambda b, c: (0, 0, 0)),
            _once((1, D_MODEL), lambda b, c: (0, 0)),
        ],
        out_specs=[
            pl.BlockSpec((T, D_MODEL), lambda b, c: (b * nc + c, 0)),
            pl.BlockSpec(st, lambda b, c: (b, 0, 0, 0)),
            pl.BlockSpec(st, lambda b, c: (b, 0, 0, 0)),
        ],
        out_shape=[
            jax.ShapeDtypeStruct((batch * SEQ, D_MODEL), BF16),
            jax.ShapeDtypeStruct((batch, N_SUPER, SUBLANES, LANES), F32),
            jax.ShapeDtypeStruct((batch, N_SUPER, SUBLANES, LANES), F32),
        ],
        scratch_shapes=[
            pltpu.VMEM((T // SUBLANES, N_SUPER * 2 * SLABS * SUBLANES, LANES), F32),
            pltpu.VMEM((2 * N_SUPER, SUBLANES, LANES), F32),
        ],
        compiler_params=_params("arbitrary", "arbitrary"),
        name="s5_prompt",
    )(u, h0r, h0i, a_tiles, b_blk, c_blk, d_row)


def _s5_sample_kernel(u_ref, h0r_ref, h0i_ref, ar_ref, ai_ref, b_ref, c_ref, d_ref,
                      z_ref, hr_ref, hi_ref, x_ref, *, nbatch):
    L = DEC_SEQ
    R = nbatch * L
    for sb in range(N_SUPER):
        cols = slice(sb * SUPER_IN, (sb + 1) * SUPER_IN)
        bu = _bdot(u_ref[:, cols], b_ref[sb])
        for s in range(2 * SLABS):
            x_ref[s * R:(s + 1) * R, :] = bu[:, s * LANES:(s + 1) * LANES]
        for s in range(SLABS):
            st = slice(sb * SUPER_STATE + s * LANES, sb * SUPER_STATE + (s + 1) * LANES)
            ar, ai = ar_ref[:, st], ai_ref[:, st]
            hr, hi = h0r_ref[:, st], h0i_ref[:, st]
            for t in range(L):
                rr = pl.ds(s * R + t, nbatch, stride=L)
                ri = pl.ds((SLABS + s) * R + t, nbatch, stride=L)
                nr = ar * hr - ai * hi + x_ref[rr, :]
                ni = ar * hi + ai * hr + x_ref[ri, :]
                x_ref[rr, :] = nr
                x_ref[ri, :] = ni
                hr, hi = nr, ni
            hr_ref[:, st] = hr
            hi_ref[:, st] = hi
        hcat = jnp.concatenate([x_ref[s * R:(s + 1) * R, :] for s in range(2 * SLABS)], axis=1)
        y = _bdot(hcat, c_ref[sb]) + d_ref[:, cols] * u_ref[:, cols]
        z_ref[:, cols] = jax.nn.gelu(y).astype(z_ref.dtype)


def s5_sample(u, h0r, h0i, a_re, a_im, b_blk, c_blk, d_row):
    rows = u.shape[0]
    nbatch = rows // DEC_SEQ
    nstate = N_SSM_GROUPS * SSM_STATE
    full = lambda shape: pl.BlockSpec(shape, lambda i: (0,) * len(shape))
    return pl.pallas_call(
        functools.partial(_s5_sample_kernel, nbatch=nbatch),
        grid=(1,),
        in_specs=[
            full((rows, D_MODEL)), full((nbatch, nstate)), full((nbatch, nstate)),
            full((1, nstate)), full((1, nstate)),
            full((N_SUPER, SUPER_IN, 2 * SUPER_STATE)),
            full((N_SUPER, 2 * SUPER_STATE, SUPER_IN)),
            full((1, D_MODEL)),
        ],
        out_specs=[full((rows, D_MODEL)), full((nbatch, nstate)), full((nbatch, nstate))],
        out_shape=[
            jax.ShapeDtypeStruct((rows, D_MODEL), BF16),
            jax.ShapeDtypeStruct((nbatch, nstate), F32),
            jax.ShapeDtypeStruct((nbatch, nstate), F32),
        ],
        scratch_shapes=[pltpu.VMEM((2 * SLABS * rows, LANES), F32)],
        compiler_params=_params("arbitrary"),
        name="s5_sample",
    )(u, h0r, h0i, a_re, a_im, b_blk, c_blk, d_row)


def _s5_weights(lam_re, lam_im, log_step, b_re, b_im, c_re, c_im):
    lam = lax.complex(lam_re.astype(F32), lam_im.astype(F32))
    dt = jnp.exp(log_step.astype(F32))[:, None]
    abar = jnp.exp(lam * dt)
    bbar = ((abar - 1.0) / lam)[..., None] * lax.complex(b_re.astype(F32), b_im.astype(F32))
    eye = jnp.eye(SSM_SUPER, dtype=F32)

    def b_layout(m):
        m = m.reshape(N_SUPER, SSM_SUPER, SSM_STATE, SSM_GROUP)
        return jnp.einsum('sgph,gk->sghkp', m, eye).reshape(N_SUPER, SUPER_IN, SUPER_STATE)

    def c_layout(m):
        m = m.reshape(N_SUPER, SSM_SUPER, SSM_GROUP, SSM_STATE)
        return jnp.einsum('sghp,gk->sgpkh', m, eye).reshape(N_SUPER, SUPER_STATE, SUPER_IN)

    b_blk = jnp.concatenate([b_layout(bbar.real), b_layout(bbar.imag)], axis=2).astype(BF16)
    c_blk = jnp.concatenate([c_layout(c_re.astype(F32)), -c_layout(c_im.astype(F32))],
                            axis=1).astype(BF16)
    return abar.real, abar.imag, b_blk, c_blk


def _conv_gate(ua, uv, p1a, p2a, p1v, p2v, cwa_ref, cwv_ref, cba_ref, cbv_ref):
    ca = cba_ref[...] + cwa_ref[0:1, :] * p2a + cwa_ref[1:2, :] * p1a + cwa_ref[2:3, :] * ua
    cv = cbv_ref[...] + cwv_ref[0:1, :] * p2v + cwv_ref[1:2, :] * p1v + cwv_ref[2:3, :] * uv
    return (jax.nn.silu(ca) * cv).astype(BF16)


def _ffn_prompt_kernel(x_ref, gpre_ref, wua_ref, wuv_ref, cwa_ref, cwv_ref, cba_ref, cbv_ref,
                       wd_ref, gpost_ref, o_ref, sa_ref, sv_ref,
                       hn_ref, ua_ref, uv_ref, act_ref, ha_ref, hv_ref,
                       *, nc, blocks_per_seq, rc, rm):
    i = pl.program_id(0)
    c = pl.program_id(1)
    tm, D = hn_ref.shape
    hdr = SUBLANES

    @pl.when(c == 0)
    def _():
        hn_ref[...] = _rms(x_ref[...], gpre_ref[...]).astype(BF16)
        o_ref[...] = jnp.zeros(o_ref.shape, F32)

    slot = i % 2

    @pl.when(i % blocks_per_seq == 0)
    def _():
        ha_ref[1 - slot, c] = jnp.zeros(ha_ref.shape[2:], F32)
        hv_ref[1 - slot, c] = jnp.zeros(hv_ref.shape[2:], F32)

    ua_ref[0:hdr, :] = ha_ref[1 - slot, c]
    uv_ref[0:hdr, :] = hv_ref[1 - slot, c]
    wa = wua_ref[...].astype(BF16)
    wv = wuv_ref[...].astype(BF16)
    wd = wd_ref[...].astype(BF16)

    def conv(buf_ref, w_ref, b_ref, r0):
        return (b_ref[...] + w_ref[0:1, :] * buf_ref[r0 + hdr - 2:r0 + hdr - 2 + rc, :]
                + w_ref[1:2, :] * buf_ref[r0 + hdr - 1:r0 + hdr - 1 + rc, :]
                + w_ref[2:3, :] * buf_ref[r0 + hdr:r0 + hdr + rc, :])

    for k in range(tm // rm):
        rows = slice(k * rm, (k + 1) * rm)
        hk = hn_ref[rows, :]
        ua_ref[hdr + k * rm:hdr + (k + 1) * rm, :] = jnp.dot(hk, wa, preferred_element_type=F32)
        uv_ref[hdr + k * rm:hdr + (k + 1) * rm, :] = jnp.dot(hk, wv, preferred_element_type=F32)
        for r0 in range(k * rm, (k + 1) * rm, rc):
            ca = conv(ua_ref, cwa_ref, cba_ref, r0)
            cv = conv(uv_ref, cwv_ref, cbv_ref, r0)
            act_ref[r0:r0 + rc, :] = (jax.nn.silu(ca) * cv).astype(BF16)
        o_ref[rows, :] += jnp.dot(act_ref[rows, :], wd, preferred_element_type=F32)

    ta = ua_ref[tm:tm + hdr, :]
    tv = uv_ref[tm:tm + hdr, :]
    ha_ref[slot, c] = ta
    hv_ref[slot, c] = tv
    sa_ref[c] = ta[hdr - 2:hdr, :]
    sv_ref[c] = tv[hdr - 2:hdr, :]

    @pl.when(c == nc - 1)
    def _():
        o_ref[...] = x_ref[...] + _rms(o_ref[...], gpost_ref[...])


FFN_ROW_CHUNK = 64
FFN_ROW_GROUP = 256


def ffn_prompt(x, layer, g_pre, w_up, conv_w, conv_b, w_down, g_post, batch, tm, tf):
    M, D = x.shape
    nc = D_FF // tf
    bps = SEQ // tm
    conv_b3 = conv_b.reshape(DEPTH, 1, 2 * D_FF)
    y, sa, sv = pl.pallas_call(
        functools.partial(_ffn_prompt_kernel, nc=nc, blocks_per_seq=bps, rc=FFN_ROW_CHUNK,
                          rm=FFN_ROW_GROUP),
        grid=(M // tm, nc),
        in_specs=[
            _once((tm, D), lambda i, c: (i, 0)),
            pl.BlockSpec((None, 1, D), lambda i, c: (layer, 0, 0)),
            pl.BlockSpec((None, D, tf), lambda i, c: (layer, 0, c)),
            pl.BlockSpec((None, D, tf), lambda i, c: (layer, 0, nc + c)),
            pl.BlockSpec((None, 3, tf), lambda i, c: (layer, 0, c)),
            pl.BlockSpec((None, 3, tf), lambda i, c: (layer, 0, nc + c)),
            pl.BlockSpec((None, 1, tf), lambda i, c: (layer, 0, c)),
            pl.BlockSpec((None, 1, tf), lambda i, c: (layer, 0, nc + c)),
            pl.BlockSpec((None, tf, D), lambda i, c: (layer, c, 0)),
            pl.BlockSpec((None, 1, D), lambda i, c: (layer, 0, 0)),
        ],
        out_specs=[
            _once((tm, D), lambda i, c: (i, 0)),
            pl.BlockSpec((None, nc, 2, tf), lambda i, c: (i // bps, 0, 0, 0)),
            pl.BlockSpec((None, nc, 2, tf), lambda i, c: (i // bps, 0, 0, 0)),
        ],
        out_shape=[
            jax.ShapeDtypeStruct((M, D), F32),
            jax.ShapeDtypeStruct((batch, nc, 2, tf), F32),
            jax.ShapeDtypeStruct((batch, nc, 2, tf), F32),
        ],
        scratch_shapes=[
            pltpu.VMEM((tm, D), BF16),
            pltpu.VMEM((tm + SUBLANES, tf), F32),
            pltpu.VMEM((tm + SUBLANES, tf), F32),
            pltpu.VMEM((tm, tf), BF16),
            pltpu.VMEM((2, nc, SUBLANES, tf), F32),
            pltpu.VMEM((2, nc, SUBLANES, tf), F32),
        ],
        compiler_params=_params("arbitrary", "arbitrary"),
        name="ffn_prompt",
    )(x, g_pre.reshape(DEPTH, 1, D), w_up, w_up, conv_w, conv_w, conv_b3, conv_b3,
      w_down, g_post.reshape(DEPTH, 1, D))
    sa = sa.transpose(0, 2, 1, 3).reshape(batch, 2, D_FF)
    sv = sv.transpose(0, 2, 1, 3).reshape(batch, 2, D_FF)
    return y, jnp.concatenate([sa, sv], axis=-1)


def _ffn_sample_kernel(x_ref, gpre_ref, wua_ref, wuv_ref, cwa_ref, cwv_ref, cba_ref, cbv_ref,
                       wd_ref, gpost_ref, s1a_ref, s2a_ref, s1v_ref, s2v_ref,
                       o_ref, ua_ref, uv_ref, hn_ref, *, nc):
    c = pl.program_id(0)

    @pl.when(c == 0)
    def _():
        hn_ref[...] = _rms(x_ref[...], gpre_ref[...]).astype(BF16)

    hn = hn_ref[...]
    ua = jnp.dot(hn, wua_ref[...].astype(BF16), preferred_element_type=F32)
    uv = jnp.dot(hn, wuv_ref[...].astype(BF16), preferred_element_type=F32)
    ua_ref[...] = ua
    uv_ref[...] = uv
    t = lax.broadcasted_iota(jnp.int32, ua.shape, 0) % DEC_SEQ

    def shifted(u, s1_ref, s2_ref):
        p1 = jnp.where(t == 0, s1_ref[...], pltpu.roll(u, 1, 0))
        p2 = jnp.where(t <= 1, s2_ref[...], pltpu.roll(u, 2, 0))
        return p1, p2

    p1a, p2a = shifted(ua, s1a_ref, s2a_ref)
    p1v, p2v = shifted(uv, s1v_ref, s2v_ref)
    act = _conv_gate(ua, uv, p1a, p2a, p1v, p2v, cwa_ref, cwv_ref, cba_ref, cbv_ref)
    part = jnp.dot(act, wd_ref[...].astype(BF16), preferred_element_type=F32)

    @pl.when(c == 0)
    def _():
        o_ref[...] = part

    @pl.when(c > 0)
    def _():
        o_ref[...] += part

    @pl.when(c == nc - 1)
    def _():
        o_ref[...] = x_ref[...] + _rms(o_ref[...], gpost_ref[...])


def ffn_sample(x, layer, g_pre, w_up, conv_w, conv_b, w_down, g_post, conv_state, tf):
    M, D = x.shape
    nbatch = M // DEC_SEQ
    nc = D_FF // tf
    conv_b3 = conv_b.reshape(DEPTH, 1, 2 * D_FF)
    zeros = jnp.zeros((nbatch, DEC_SEQ - 2, 2 * D_FF), F32)
    s1 = jnp.concatenate([conv_state[:, 1:2], conv_state[:, 0:1] * 0, zeros], axis=1)
    s2 = jnp.concatenate([conv_state, zeros], axis=1)
    s1 = s1.reshape(M, 2 * D_FF)
    s2 = s2.reshape(M, 2 * D_FF)
    y, ua, uv = pl.pallas_call(
        functools.partial(_ffn_sample_kernel, nc=nc),
        grid=(nc,),
        in_specs=[
            pl.BlockSpec((M, D), lambda c: (0, 0)),
            pl.BlockSpec((None, 1, D), lambda c: (layer, 0, 0)),
            pl.BlockSpec((None, D, tf), lambda c: (layer, 0, c)),
            pl.BlockSpec((None, D, tf), lambda c: (layer, 0, nc + c)),
            pl.BlockSpec((None, 3, tf), lambda c: (layer, 0, c)),
            pl.BlockSpec((None, 3, tf), lambda c: (layer, 0, nc + c)),
            pl.BlockSpec((None, 1, tf), lambda c: (layer, 0, c)),
            pl.BlockSpec((None, 1, tf), lambda c: (layer, 0, nc + c)),
            pl.BlockSpec((None, tf, D), lambda c: (layer, c, 0)),
            pl.BlockSpec((None, 1, D), lambda c: (layer, 0, 0)),
            pl.BlockSpec((M, tf), lambda c: (0, c)),
            pl.BlockSpec((M, tf), lambda c: (0, c)),
            pl.BlockSpec((M, tf), lambda c: (0, nc + c)),
            pl.BlockSpec((M, tf), lambda c: (0, nc + c)),
        ],
        out_specs=[
            pl.BlockSpec((M, D), lambda c: (0, 0)),
            pl.BlockSpec((M, tf), lambda c: (0, c)),
            pl.BlockSpec((M, tf), lambda c: (0, c)),
        ],
        out_shape=[
            jax.ShapeDtypeStruct((M, D), F32),
            jax.ShapeDtypeStruct((M, D_FF), F32),
            jax.ShapeDtypeStruct((M, D_FF), F32),
        ],
        scratch_shapes=[pltpu.VMEM((M, D), BF16)],
        compiler_params=_params("arbitrary"),
        name="ffn_sample",
    )(x, g_pre.reshape(DEPTH, 1, D), w_up, w_up, conv_w, conv_w, conv_b3, conv_b3,
      w_down, g_post.reshape(DEPTH, 1, D), s1, s2, s1, s2)
    up = jnp.concatenate([ua, uv], axis=-1).reshape(nbatch, DEC_SEQ, 2 * D_FF)
    return y, up[:, DEC_SEQ - 2:]


def kernel(x_prompt, x_sample, mem_prompt, cache_swa_k, cache_swa_v, state_ssm_re, state_ssm_im, state_ffn_conv, cache_mem_k, cache_mem_v, g_mix_pre, g_mix_post, w_qkv, w_attn_o, attn_sinks, w_ssm_in, ssm_lambda_re, ssm_lambda_im, ssm_log_step, ssm_b_re, ssm_b_im, ssm_c_re, ssm_c_im, ssm_d, w_ssm_glu, g_x_pre, g_x_post, g_mem, w_x_q, w_mem_k, w_mem_v, w_x_o, g_ffn_pre, g_ffn_post, w_ffn_up, ffn_conv_w, ffn_conv_b, w_ffn_down):
    B = x_prompt.shape[0]
    SB = x_sample.shape[0]
    xw = N_XHEADS * XHEAD_DIM
    nkc = N_KV_HEADS * HEAD_DIM
    xp = x_prompt.reshape(B * SEQ, D_MODEL)
    xs = x_sample.reshape(SB * DEC_SEQ, D_MODEL)
    mem = mem_prompt.reshape(B * N_MEM, D_MODEL)
    MS = SB * DEC_SEQ
    TMP = 1024
    cos_p, sin_p = _rope_tables(jnp.arange(SEQ))
    cos_s, sin_s = _rope_tables(PAST_LEN + jnp.arange(DEC_SEQ))
    ck_all = cache_swa_k.reshape(cache_swa_k.shape[0], SB, WINDOW, nkc)
    cv_all = cache_swa_v.reshape(cache_swa_v.shape[0], SB, WINDOW, nkc)
    cmk_all = cache_mem_k.reshape(DEPTH, SB, N_MEM, xw)
    cmv_all = cache_mem_v.reshape(DEPTH, SB, N_MEM, xw)
    w_up_bf = w_ffn_up.astype(BF16)
    w_down_bf = w_ffn_down.astype(BF16)
    w_qkv_bf = w_qkv.astype(BF16)
    w_attn_o_bf = w_attn_o.astype(BF16)
    w_ssm_in_bf = w_ssm_in.astype(BF16)
    w_glu_bf = w_ssm_glu.astype(BF16)
    w_x_q_bf = w_x_q.astype(BF16)
    w_x_o_bf = w_x_o.astype(BF16)
    w_mem_kv_bf = jnp.concatenate([w_mem_k, w_mem_v], axis=-1).astype(BF16)
    TD = 512

    swa_kp, swa_vp, swa_ks, swa_vs = [], [], [], []
    ssm_rp, ssm_ip, ssm_rs, ssm_is = [], [], [], []
    conv_p, conv_s, memk_p, memv_p = [], [], [], []
    for i in range(DEPTH):
        j = i // 2
        if i % 2 == 0:
            qkv_p = norm_matmul(xp, g_mix_pre, i, w_qkv_bf, j, TD)
            qkv_s = norm_matmul(xs, g_mix_pre, i, w_qkv_bf, j, MS)
            op, kp, vp = swa_prompt(qkv_p, attn_sinks, j, cos_p, sin_p, B)
            sink_rows = jnp.repeat(attn_sinks[j].reshape(N_KV_HEADS, GQA_GROUP), DEC_SEQ,
                                   axis=1)[..., None]
            os_, kn, vn = swa_sample(qkv_s, ck_all, cv_all, j, sink_rows, cos_s, sin_s)
            swa_kp.append(kp.reshape(B, WINDOW, N_KV_HEADS, HEAD_DIM))
            swa_vp.append(vp.reshape(B, WINDOW, N_KV_HEADS, HEAD_DIM))
            swa_ks.append(kn.reshape(SB, WINDOW, N_KV_HEADS, HEAD_DIM))
            swa_vs.append(vn.reshape(SB, WINDOW, N_KV_HEADS, HEAD_DIM))
            xp = matmul_post(op, w_attn_o_bf, j, g_mix_post, i, xp, TD)
            xs = matmul_post(os_, w_attn_o_bf, j, g_mix_post, i, xs, MS)
        else:
            a_re, a_im, b_blk, c_blk = _s5_weights(
                ssm_lambda_re[j], ssm_lambda_im[j], ssm_log_step[j], ssm_b_re[j], ssm_b_im[j],
                ssm_c_re[j], ssm_c_im[j])
            d_row = ssm_d[j].reshape(1, D_MODEL)
            up_ = norm_matmul(xp, g_mix_pre, i, w_ssm_in_bf, j, TD)
            us_ = norm_matmul(xs, g_mix_pre, i, w_ssm_in_bf, j, MS)
            a_tiles = jnp.stack([a_re.reshape(N_SUPER, SUBLANES, LANES),
                                 a_im.reshape(N_SUPER, SUBLANES, LANES)], axis=1
                                ).reshape(2 * N_SUPER, SUBLANES, LANES)
            zero_state = jnp.zeros((B, N_SUPER, SUBLANES, LANES), F32)
            zp, rp, ip = s5_prompt(up_, zero_state, zero_state, a_tiles, b_blk, c_blk, d_row,
                                   B, 256)
            nstate = N_SSM_GROUPS * SSM_STATE
            zs, rn, im_ = s5_sample(us_, state_ssm_re[j].reshape(SB, nstate),
                                    state_ssm_im[j].reshape(SB, nstate),
                                    a_re.reshape(1, nstate), a_im.reshape(1, nstate),
                                    b_blk, c_blk, d_row)
            ssm_rp.append(rp.reshape(B, N_SSM_GROUPS, SSM_STATE))
            ssm_ip.append(ip.reshape(B, N_SSM_GROUPS, SSM_STATE))
            ssm_rs.append(rn.reshape(SB, N_SSM_GROUPS, SSM_STATE))
            ssm_is.append(im_.reshape(SB, N_SSM_GROUPS, SSM_STATE))
            xp = glu_post(zp, w_glu_bf, j, g_mix_post, i, xp, TD)
            xs = glu_post(zs, w_glu_bf, j, g_mix_post, i, xs, MS)
        mkv = norm_matmul(mem, g_mem, i, w_mem_kv_bf, i, B * N_MEM)
        mk, mv = mkv[:, :xw], mkv[:, xw:]
        memk_p.append(mk.reshape(B, N_MEM, N_XHEADS, XHEAD_DIM))
        memv_p.append(mv.reshape(B, N_MEM, N_XHEADS, XHEAD_DIM))
        xp = xattn_prompt(xp, i, g_x_pre, w_x_q_bf, mk, mv, w_x_o_bf, g_x_post, B, TD)
        qs = norm_matmul(xs, g_x_pre, i, w_x_q_bf, i, MS)
        as_ = xattn_sample(qs, cmk_all, cmv_all, i)
        xs = matmul_post(as_, w_x_o_bf, i, g_x_post, i, xs, MS)
        xp, cp = ffn_prompt(xp, i, g_ffn_pre, w_up_bf, ffn_conv_w, ffn_conv_b, w_down_bf,
                            g_ffn_post, B, TMP, 512)
        xs, cs = ffn_sample(xs, i, g_ffn_pre, w_up_bf, ffn_conv_w, ffn_conv_b, w_down_bf,
                            g_ffn_post, state_ffn_conv[i], 512)
        conv_p.append(cp)
        conv_s.append(cs)
    return (xp.reshape(B, SEQ, D_MODEL), xs.reshape(SB, DEC_SEQ, D_MODEL),
            jnp.stack(swa_kp), jnp.stack(swa_vp), jnp.stack(swa_ks), jnp.stack(swa_vs),
            jnp.stack(ssm_rp), jnp.stack(ssm_ip), jnp.stack(ssm_rs), jnp.stack(ssm_is),
            jnp.stack(conv_p), jnp.stack(conv_s), jnp.stack(memk_p), jnp.stack(memv_p))
```

```python
import functools
import math

import jax
import jax.numpy as jnp
from jax import lax
from jax.experimental import pallas as pl
from jax.experimental.pallas import tpu as pltpu

D_MODEL = 2048
SEQ = 4096
DEPTH = 4
DEC_SEQ = 8
PAST_LEN = 16384
HEAD_DIM = 64
N_HEADS = 32
N_KV_HEADS = 4
GQA_GROUP = 8
WINDOW = 128
ROPE_THETA = 10000.0
SSM_GROUP = 16
N_SSM_GROUPS = 128
SSM_STATE = 64
N_MEM = 256
N_XHEADS = 4
XHEAD_DIM = 128
D_FF = 5632
NORM_EPS = 1e-6
NEG_INF = -1e30

F32 = jnp.float32
BF16 = jnp.bfloat16

V7X_VMEM_BYTES = 64 * 1024 * 1024
VMEM_LIMIT = V7X_VMEM_BYTES - 8 * 1024 * 1024
LANES = 128
SUBLANES = 8
SSM_SUPER = 16
N_SUPER = N_SSM_GROUPS // SSM_SUPER
SUPER_IN = SSM_SUPER * SSM_GROUP
SUPER_STATE = SSM_SUPER * SSM_STATE
SLABS = SUPER_STATE // LANES


def _params(*sem):
    return pltpu.CompilerParams(dimension_semantics=sem, vmem_limit_bytes=VMEM_LIMIT)


def _rms(x, g):
    ms = jnp.mean(x * x, axis=-1, keepdims=True)
    return x * lax.rsqrt(ms + NORM_EPS) * g


def _bdot(a, b):
    return jnp.dot(a.astype(BF16), b.astype(BF16), preferred_element_type=F32)


def _once(shape, index_map):
    return pl.BlockSpec(shape, index_map, pipeline_mode=pl.Buffered(1))


ROW_GROUP = 256


def _resident(shape, index_map):
    return pl.BlockSpec(shape, index_map, pipeline_mode=pl.Buffered(1))


def _row_groups(tm):
    rg = min(ROW_GROUP, tm)
    return [slice(k * rg, (k + 1) * rg) for k in range(tm // rg)]


def _norm_matmul_kernel(x_ref, g_ref, w_ref, o_ref):
    g = g_ref[...]
    for rows in _row_groups(x_ref.shape[0]):
        hn = _rms(x_ref[rows, :], g).astype(BF16)
        o_ref[rows, :] = jnp.dot(hn, w_ref[...], preferred_element_type=F32).astype(o_ref.dtype)


def norm_matmul(x, g, g_layer, w, w_layer, tm, out_dtype=F32):
    M, D = x.shape
    N = w.shape[-1]
    return pl.pallas_call(
        _norm_matmul_kernel,
        grid=(M // tm,),
        in_specs=[
            pl.BlockSpec((tm, D), lambda i: (i, 0)),
            pl.BlockSpec((None, 1, D), lambda i: (g_layer, 0, 0)),
            _resident((None, D, N), lambda i: (w_layer, 0, 0)),
        ],
        out_specs=pl.BlockSpec((tm, N), lambda i: (i, 0)),
        out_shape=jax.ShapeDtypeStruct((M, N), out_dtype),
        compiler_params=_params("arbitrary"),
        name="norm_matmul",
    )(x, g.reshape(g.shape[0], 1, D), w)


def _matmul_post_kernel(a_ref, w_ref, g_ref, x_ref, o_ref):
    g = g_ref[...]
    for rows in _row_groups(x_ref.shape[0]):
        y = jnp.dot(a_ref[rows, :].astype(BF16), w_ref[...], preferred_element_type=F32)
        o_ref[rows, :] = x_ref[rows, :] + _rms(y, g)


def matmul_post(a, w, w_layer, g, g_layer, x, tm):
    M, K = a.shape
    D = w.shape[-1]
    return pl.pallas_call(
        _matmul_post_kernel,
        grid=(M // tm,),
        in_specs=[
            pl.BlockSpec((tm, K), lambda i: (i, 0)),
            _resident((None, K, D), lambda i: (w_layer, 0, 0)),
            pl.BlockSpec((None, 1, D), lambda i: (g_layer, 0, 0)),
            pl.BlockSpec((tm, D), lambda i: (i, 0)),
        ],
        out_specs=pl.BlockSpec((tm, D), lambda i: (i, 0)),
        out_shape=jax.ShapeDtypeStruct((M, D), F32),
        compiler_params=_params("arbitrary"),
        name="matmul_post",
    )(a, w, g.reshape(g.shape[0], 1, D), x)


GLU_COLS = 512


def _glu_post_kernel(z_ref, w_ref, g_ref, x_ref, o_ref, y_ref):
    D = o_ref.shape[1]
    g = g_ref[...]
    for rows in _row_groups(x_ref.shape[0]):
        z = z_ref[rows, :]
        for n in range(D // GLU_COLS):
            cols = slice(n * GLU_COLS, (n + 1) * GLU_COLS)
            gcols = slice(D + n * GLU_COLS, D + (n + 1) * GLU_COLS)
            val = jnp.dot(z, w_ref[:, cols], preferred_element_type=F32)
            gate = jnp.dot(z, w_ref[:, gcols], preferred_element_type=F32)
            y_ref[rows, cols] = val * jax.nn.sigmoid(gate)
        o_ref[rows, :] = x_ref[rows, :] + _rms(y_ref[rows, :], g)


def glu_post(z, w, w_layer, g, g_layer, x, tm):
    M, K = z.shape
    D = D_MODEL
    return pl.pallas_call(
        _glu_post_kernel,
        grid=(M // tm,),
        in_specs=[
            pl.BlockSpec((tm, K), lambda i: (i, 0)),
            _resident((None, K, 2 * D), lambda i: (w_layer, 0, 0)),
            pl.BlockSpec((None, 1, D), lambda i: (g_layer, 0, 0)),
            pl.BlockSpec((tm, D), lambda i: (i, 0)),
        ],
        out_specs=pl.BlockSpec((tm, D), lambda i: (i, 0)),
        out_shape=jax.ShapeDtypeStruct((M, D), F32),
        scratch_shapes=[pltpu.VMEM((tm, D), F32)],
        compiler_params=_params("arbitrary"),
        name="glu_post",
    )(z, w, g.reshape(g.shape[0], 1, D), x)


def _rope_chunk(xc, cos, sin_signed):
    lane = lax.broadcasted_iota(jnp.int32, xc.shape, 1)
    first_half = (lane & (HEAD_DIM - 1)) < (HEAD_DIM // 2)
    partner = jnp.where(first_half, pltpu.roll(xc, LANES - HEAD_DIM // 2, 1),
                        pltpu.roll(xc, HEAD_DIM // 2, 1))
    return xc * cos + partner * sin_signed


def _rope_tables(pos):
    half = HEAD_DIM // 2
    inv_freq = ROPE_THETA ** (-jnp.arange(half, dtype=F32) * 2.0 / HEAD_DIM)
    ang = pos.astype(F32)[:, None] * inv_freq[None, :]
    cos, sin = jnp.cos(ang), jnp.sin(ang)
    cos128 = jnp.tile(cos, (1, LANES // half))
    sin128 = jnp.tile(jnp.concatenate([-sin, sin], axis=1), (1, LANES // HEAD_DIM))
    return cos128, sin128


def _swa_prompt_kernel(sink_ref, q_ref, kv_ref, cos_ref, sin_ref,
                       o_ref, kc_ref, vc_ref, kk_ref, vv_ref, *, layer):
    i = pl.program_id(1)
    blk = WINDOW
    cos, sn = cos_ref[...], sin_ref[...]
    kv = kv_ref[...]
    nkc = N_KV_HEADS * HEAD_DIM
    k = jnp.concatenate([_rope_chunk(kv[:, c * LANES:(c + 1) * LANES], cos, sn)
                         for c in range(nkc // LANES)], axis=1)
    v = kv[:, nkc:2 * nkc]
    kc_ref[...] = k
    vc_ref[...] = v

    cur = i % 2
    prv = 1 - cur

    @pl.when(i == 0)
    def _():
        kk_ref[1] = jnp.zeros((blk, nkc), BF16)
        vv_ref[1] = jnp.zeros((blk, nkc), BF16)

    kb, vb = k.astype(BF16), v.astype(BF16)
    kk_ref[cur] = kb
    vv_ref[cur] = vb
    kk = jnp.concatenate([kk_ref[prv], kb], axis=0)
    vv = jnp.concatenate([vv_ref[prv], vb], axis=0)
    rows = GQA_GROUP * blk
    qi = lax.broadcasted_iota(jnp.int32, (rows, 2 * blk), 0) & (blk - 1)
    si = lax.broadcasted_iota(jnp.int32, (rows, 2 * blk), 1)
    d = blk + qi - si
    mask = (d >= 0) & (d < WINDOW) & ((si >= blk) | (i > 0))

    chunks_per_group = GQA_GROUP * HEAD_DIM // LANES

    def scores(kh):
        hs = slice(kh * HEAD_DIM, (kh + 1) * HEAD_DIM)
        pieces = []
        for c in range(kh * chunks_per_group, (kh + 1) * chunks_per_group):
            qc = (_rope_chunk(q_ref[:, c * LANES:(c + 1) * LANES], cos, sn)
                  * (HEAD_DIM ** -0.5)).astype(BF16)
            pieces += [qc[:, hh * HEAD_DIM:(hh + 1) * HEAD_DIM] for hh in range(LANES // HEAD_DIM)]
        qs = jnp.concatenate(pieces, axis=0)
        return lax.dot_general(qs, kk[:, hs], (((1,), (1,)), ((), ())),
                               preferred_element_type=F32)

    ahead = 2
    s_all = [scores(kh) for kh in range(min(ahead, N_KV_HEADS))]
    outs = []
    for kh in range(N_KV_HEADS):
        if kh + ahead < N_KV_HEADS:
            s_all.append(scores(kh + ahead))
        hs = slice(kh * HEAD_DIM, (kh + 1) * HEAD_DIM)
        sk = jnp.concatenate([jnp.full((blk, 1), sink_ref[layer, kh * GQA_GROUP + j], F32)
                              for j in range(GQA_GROUP)], axis=0)
        s = jnp.where(mask, s_all[kh], NEG_INF)
        mx = jnp.maximum(jnp.max(s, axis=1, keepdims=True), sk)
        p = jnp.exp(s - mx)
        den = jnp.sum(p, axis=1, keepdims=True) + jnp.exp(sk - mx)
        o = jnp.dot(p.astype(BF16), vv[:, hs], preferred_element_type=F32) / den
        outs += [o[j * blk:(j + 1) * blk, :] for j in range(GQA_GROUP)]
    o_ref[...] = jnp.concatenate(outs, axis=1).astype(o_ref.dtype)


def swa_prompt(qkv, sinks, layer, cos, sin, batch):
    blk = WINDOW
    nb = SEQ // blk
    nq = N_HEADS * HEAD_DIM
    nkv = 2 * N_KV_HEADS * HEAD_DIM
    o, kc, vc = pl.pallas_call(
        functools.partial(_swa_prompt_kernel, layer=layer),
        grid=(batch, nb),
        in_specs=[
            pl.BlockSpec(memory_space=pltpu.SMEM),
            pl.BlockSpec((blk, nq), lambda b, i: (b * nb + i, 0)),
            pl.BlockSpec((blk, nkv), lambda b, i: (b * nb + i, nq // nkv)),
            pl.BlockSpec((blk, LANES), lambda b, i: (i, 0)),
            pl.BlockSpec((blk, LANES), lambda b, i: (i, 0)),
        ],
        out_specs=[
            pl.BlockSpec((blk, nq), lambda b, i: (b * nb + i, 0)),
            pl.BlockSpec((None, blk, nkv // 2), lambda b, i: (b, 0, 0)),
            pl.BlockSpec((None, blk, nkv // 2), lambda b, i: (b, 0, 0)),
        ],
        out_shape=[
            jax.ShapeDtypeStruct((batch * SEQ, nq), BF16),
            jax.ShapeDtypeStruct((batch, blk, nkv // 2), F32),
            jax.ShapeDtypeStruct((batch, blk, nkv // 2), F32),
        ],
        scratch_shapes=[pltpu.VMEM((2, blk, nkv // 2), BF16),
                        pltpu.VMEM((2, blk, nkv // 2), BF16)],
        compiler_params=_params("arbitrary", "arbitrary"),
        name="swa_prompt",
    )(sinks, qkv, qkv, cos, sin)
    return o, kc, vc


def _swa_sample_kernel(sink_ref, qkv_ref, ck_ref, cv_ref, cos_ref, sin_ref,
                       o_ref, nk_ref, nv_ref, *, nb):
    L = DEC_SEQ
    wb = WINDOW
    nq = N_HEADS * HEAD_DIM
    nkc = N_KV_HEADS * HEAD_DIM
    cos, sn = cos_ref[...], sin_ref[...]
    rows = GQA_GROUP * L
    t_q = lax.broadcasted_iota(jnp.int32, (rows, wb), 0) % L
    c_k = lax.broadcasted_iota(jnp.int32, (rows, wb), 1)
    mask_c = c_k >= t_q + 1
    t_q2 = lax.broadcasted_iota(jnp.int32, (rows, L), 0) % L
    t_k2 = lax.broadcasted_iota(jnp.int32, (rows, L), 1)
    mask_n = t_k2 <= t_q2
    for b in range(nb):
        x = qkv_ref[b * L:(b + 1) * L, :]
        qr = jnp.concatenate([_rope_chunk(x[:, c * LANES:(c + 1) * LANES], cos, sn)
                              for c in range(nq // LANES)], axis=1) * (HEAD_DIM ** -0.5)
        kn = jnp.concatenate([_rope_chunk(x[:, nq + c * LANES:nq + (c + 1) * LANES], cos, sn)
                              for c in range(nkc // LANES)], axis=1)
        vn = x[:, nq + nkc:nq + 2 * nkc]
        ck = ck_ref[b]
        cv = cv_ref[b]
        nk_ref[b, 0:wb - L, :] = ck[L:wb, :]
        nk_ref[b, wb - L:wb, :] = kn
        nv_ref[b, 0:wb - L, :] = cv[L:wb, :]
        nv_ref[b, wb - L:wb, :] = vn
        ckb, cvb, knb, vnb = ck.astype(BF16), cv.astype(BF16), kn.astype(BF16), vn.astype(BF16)
        outs = []
        for kh in range(N_KV_HEADS):
            hs = slice(kh * HEAD_DIM, (kh + 1) * HEAD_DIM)
            qs = jnp.concatenate(
                [qr[:, (kh * GQA_GROUP + j) * HEAD_DIM:(kh * GQA_GROUP + j + 1) * HEAD_DIM]
                 for j in range(GQA_GROUP)], axis=0).astype(BF16)
            dn = (((1,), (1,)), ((), ()))
            s_c = lax.dot_general(qs, ckb[:, hs], dn, preferred_element_type=F32)
            s_n = lax.dot_general(qs, knb[:, hs], dn, preferred_element_type=F32)
            s_c = jnp.where(mask_c, s_c, NEG_INF)
            s_n = jnp.where(mask_n, s_n, NEG_INF)
            sk = sink_ref[kh]
            mx = jnp.maximum(jnp.maximum(jnp.max(s_c, axis=1, keepdims=True),
                                         jnp.max(s_n, axis=1, keepdims=True)), sk)
            p_c = jnp.exp(s_c - mx)
            p_n = jnp.exp(s_n - mx)
            den = (jnp.sum(p_c, axis=1, keepdims=True) + jnp.sum(p_n, axis=1, keepdims=True)
                   + jnp.exp(sk - mx))
            o = (jnp.dot(p_c.astype(BF16), cvb[:, hs], preferred_element_type=F32)
                 + jnp.dot(p_n.astype(BF16), vnb[:, hs], preferred_element_type=F32)) / den
            outs.extend(o[j * L:(j + 1) * L, :] for j in range(GQA_GROUP))
        o_ref[b * L:(b + 1) * L, :] = jnp.concatenate(outs, axis=1).astype(o_ref.dtype)


def swa_sample(qkv, cache_k, cache_v, layer, sink_rows, cos, sin, nb=8):
    nbatch = cache_k.shape[1]
    nq = N_HEADS * HEAD_DIM
    nkc = N_KV_HEADS * HEAD_DIM
    L = DEC_SEQ
    return pl.pallas_call(
        functools.partial(_swa_sample_kernel, nb=nb),
        grid=(nbatch // nb,),
        in_specs=[
            pl.BlockSpec((N_KV_HEADS, GQA_GROUP * L, 1), lambda g: (0, 0, 0)),
            pl.BlockSpec((nb * L, nq + 2 * nkc), lambda g: (g, 0)),
            pl.BlockSpec((None, nb, WINDOW, nkc), lambda g: (layer, g, 0, 0)),
            pl.BlockSpec((None, nb, WINDOW, nkc), lambda g: (layer, g, 0, 0)),
            pl.BlockSpec((L, LANES), lambda g: (0, 0)),
            pl.BlockSpec((L, LANES), lambda g: (0, 0)),
        ],
        out_specs=[
            pl.BlockSpec((nb * L, nq), lambda g: (g, 0)),
            pl.BlockSpec((nb, WINDOW, nkc), lambda g: (g, 0, 0)),
            pl.BlockSpec((nb, WINDOW, nkc), lambda g: (g, 0, 0)),
        ],
        out_shape=[
            jax.ShapeDtypeStruct((nbatch * L, nq), BF16),
            jax.ShapeDtypeStruct((nbatch, WINDOW, nkc), F32),
            jax.ShapeDtypeStruct((nbatch, WINDOW, nkc), F32),
        ],
        compiler_params=_params("arbitrary"),
        name="swa_sample",
    )(sink_rows, qkv, cache_k, cache_v, cos, sin)


def _xattn_heads(q, mk, mv):
    outs = []
    for h in range(N_XHEADS):
        hs = slice(h * XHEAD_DIM, (h + 1) * XHEAD_DIM)
        qh = (q[:, hs] * (XHEAD_DIM ** -0.5)).astype(BF16)
        s = lax.dot_general(qh, mk[:, hs].astype(BF16), (((1,), (1,)), ((), ())),
                            preferred_element_type=F32)
        mx = jnp.max(s, axis=1, keepdims=True)
        p = jnp.exp(s - mx)
        den = jnp.sum(p, axis=1, keepdims=True)
        outs.append(jnp.dot(p.astype(BF16), mv[:, hs].astype(BF16),
                            preferred_element_type=F32) / den)
    return jnp.concatenate(outs, axis=1)


def _xattn_prompt_kernel(x_ref, gpre_ref, wq_ref, mk_ref, mv_ref, wo_ref, gpost_ref, o_ref):
    mk = mk_ref[...].astype(BF16)
    mv = mv_ref[...].astype(BF16)
    gpre, gpost = gpre_ref[...], gpost_ref[...]
    for rows in _row_groups(x_ref.shape[0]):
        x = x_ref[rows, :]
        q = jnp.dot(_rms(x, gpre).astype(BF16), wq_ref[...], preferred_element_type=F32)
        a = _xattn_heads(q, mk, mv).astype(BF16)
        y = jnp.dot(a, wo_ref[...], preferred_element_type=F32)
        o_ref[rows, :] = x + _rms(y, gpost)


def xattn_prompt(x, layer, g_pre, w_q, mk, mv, w_o, g_post, batch, tm):
    nq = SEQ // tm
    xw = N_XHEADS * XHEAD_DIM
    D = D_MODEL
    return pl.pallas_call(
        _xattn_prompt_kernel,
        grid=(batch, nq),
        in_specs=[
            pl.BlockSpec((tm, D), lambda b, i: (b * nq + i, 0)),
            pl.BlockSpec((None, 1, D), lambda b, i: (layer, 0, 0)),
            _resident((None, D, xw), lambda b, i: (layer, 0, 0)),
            pl.BlockSpec((N_MEM, xw), lambda b, i: (b, 0)),
            pl.BlockSpec((N_MEM, xw), lambda b, i: (b, 0)),
            _resident((None, xw, D), lambda b, i: (layer, 0, 0)),
            pl.BlockSpec((None, 1, D), lambda b, i: (layer, 0, 0)),
        ],
        out_specs=pl.BlockSpec((tm, D), lambda b, i: (b * nq + i, 0)),
        out_shape=jax.ShapeDtypeStruct((batch * SEQ, D), F32),
        compiler_params=_params("arbitrary", "arbitrary"),
        name="xattn_prompt",
    )(x, g_pre.reshape(DEPTH, 1, D), w_q, mk, mv, w_o, g_post.reshape(DEPTH, 1, D))


def _xattn_sample_kernel(q_ref, mk_ref, mv_ref, o_ref, *, nb):
    L = DEC_SEQ
    for b in range(nb):
        o_ref[b * L:(b + 1) * L, :] = _xattn_heads(
            q_ref[b * L:(b + 1) * L, :], mk_ref[b], mv_ref[b]).astype(o_ref.dtype)


def xattn_sample(q, cache_k, cache_v, layer, nb=8):
    nbatch = cache_k.shape[1]
    L = DEC_SEQ
    xw = N_XHEADS * XHEAD_DIM
    return pl.pallas_call(
        functools.partial(_xattn_sample_kernel, nb=nb),
        grid=(nbatch // nb,),
        in_specs=[
            pl.BlockSpec((nb * L, xw), lambda g: (g, 0)),
            pl.BlockSpec((None, nb, N_MEM, xw), lambda g: (layer, g, 0, 0)),
            pl.BlockSpec((None, nb, N_MEM, xw), lambda g: (layer, g, 0, 0)),
        ],
        out_specs=pl.BlockSpec((nb * L, xw), lambda g: (g, 0)),
        out_shape=jax.ShapeDtypeStruct((nbatch * L, xw), BF16),
        compiler_params=_params("arbitrary"),
        name="xattn_sample",
    )(q, cache_k, cache_v)


def _s5_prompt_kernel(u_ref, h0r_ref, h0i_ref, a_ref, b_ref, c_ref, d_ref,
                      z_ref, hr_ref, hi_ref, x_ref, hst_ref, *, T):
    c = pl.program_id(1)
    nslab = 2 * SLABS
    G = T // SUBLANES

    @pl.when(c == 0)
    def _():
        for sb in range(N_SUPER):
            hst_ref[2 * sb] = h0r_ref[sb]
            hst_ref[2 * sb + 1] = h0i_ref[sb]

    def slab_rows(sb, s):
        return pl.ds((sb * nslab + s) * SUBLANES, SUBLANES)

    for sb in range(N_SUPER):
        bu = _bdot(u_ref[:, sb * SUPER_IN:(sb + 1) * SUPER_IN], b_ref[sb])
        for s in range(nslab):
            x_ref[:, slab_rows(sb, s), :] = bu[:, s * LANES:(s + 1) * LANES].reshape(
                G, SUBLANES, LANES)

    a = [a_ref[k] for k in range(2 * N_SUPER)]
    h_init = tuple(hst_ref[k] for k in range(2 * N_SUPER))

    def step(t, h):
        g = lax.shift_right_logical(t, 3)
        r = lax.bitwise_and(t, SUBLANES - 1)
        new = []
        for sb in range(N_SUPER):
            ir = pl.ds((sb * nslab) * SUBLANES + r, SUBLANES, stride=SUBLANES)
            ii = pl.ds((sb * nslab + SLABS) * SUBLANES + r, SUBLANES, stride=SUBLANES)
            ar, ai = a[2 * sb], a[2 * sb + 1]
            hr, hi = h[2 * sb], h[2 * sb + 1]
            nr = ar * hr - ai * hi + x_ref[g, ir, :]
            ni = ar * hi + ai * hr + x_ref[g, ii, :]
            x_ref[g, ir, :] = nr
            x_ref[g, ii, :] = ni
            new += [nr, ni]
        return tuple(new)

    h_fin = lax.fori_loop(0, T, step, h_init)
    for sb in range(N_SUPER):
        hst_ref[2 * sb] = h_fin[2 * sb]
        hst_ref[2 * sb + 1] = h_fin[2 * sb + 1]
        hr_ref[sb] = h_fin[2 * sb]
        hi_ref[sb] = h_fin[2 * sb + 1]

    for sb in range(N_SUPER):
        hcat = jnp.concatenate([x_ref[:, slab_rows(sb, s), :].reshape(T, LANES)
                                for s in range(nslab)], axis=1)
        cols = slice(sb * SUPER_IN, (sb + 1) * SUPER_IN)
        y = _bdot(hcat, c_ref[sb]) + d_ref[:, cols] * u_ref[:, cols]
        z_ref[:, cols] = jax.nn.gelu(y).astype(z_ref.dtype)


def s5_prompt(u, h0r, h0i, a_tiles, b_blk, c_blk, d_row, batch, T):
    nc = SEQ // T
    st = (None, N_SUPER, SUBLANES, LANES)
    return pl.pallas_call(
        functools.partial(_s5_prompt_kernel, T=T),
        grid=(batch, nc),
        in_specs=[
            pl.BlockSpec((T, D_MODEL), lambda b, c: (b * nc + c, 0)),
            pl.BlockSpec(st, lambda b, c: (b, 0, 0, 0)),
            pl.BlockSpec(st, lambda b, c: (b, 0, 0, 0)),
            _once((2 * N_SUPER, SUBLANES, LANES), lambda b, c: (0, 0, 0)),
            _once((N_SUPER, SUPER_IN, 2 * SUPER_STATE), lambda b, c: (0, 0, 0)),
            _once((N_SUPER, 2 * SUPER_STATE, SUPER_IN), lambda b, c: (0, 0, 0)),
            _once((1, D_MODEL), lambda b, c: (0, 0)),
        ],
        out_specs=[
            pl.BlockSpec((T, D_MODEL), lambda b, c: (b * nc + c, 0)),
            pl.BlockSpec(st, lambda b, c: (b, 0, 0, 0)),
            pl.BlockSpec(st, lambda b, c: (b, 0, 0, 0)),
        ],
        out_shape=[
            jax.ShapeDtypeStruct((batch * SEQ, D_MODEL), BF16),
            jax.ShapeDtypeStruct((batch, N_SUPER, SUBLANES, LANES), F32),
            jax.ShapeDtypeStruct((batch, N_SUPER, SUBLANES, LANES), F32),
        ],
        scratch_shapes=[
            pltpu.VMEM((T // SUBLANES, N_SUPER * 2 * SLABS * SUBLANES, LANES), F32),
            pltpu.VMEM((2 * N_SUPER, SUBLANES, LANES), F32),
        ],
        compiler_params=_params("arbitrary", "arbitrary"),
        name="s5_prompt",
    )(u, h0r, h0i, a_tiles, b_blk, c_blk, d_row)


def _s5_sample_kernel(u_ref, h0r_ref, h0i_ref, ar_ref, ai_ref, b_ref, c_ref, d_ref,
                      z_ref, hr_ref, hi_ref, x_ref, *, nbatch):
    L = DEC_SEQ
    R = nbatch * L
    for sb in range(N_SUPER):
        cols = slice(sb * SUPER_IN, (sb + 1) * SUPER_IN)
        bu = _bdot(u_ref[:, cols], b_ref[sb])
        for s in range(2 * SLABS):
            x_ref[s * R:(s + 1) * R, :] = bu[:, s * LANES:(s + 1) * LANES]
        for s in range(SLABS):
            st = slice(sb * SUPER_STATE + s * LANES, sb * SUPER_STATE + (s + 1) * LANES)
            ar, ai = ar_ref[:, st], ai_ref[:, st]
            hr, hi = h0r_ref[:, st], h0i_ref[:, st]
            for t in range(L):
                rr = pl.ds(s * R + t, nbatch, stride=L)
                ri = pl.ds((SLABS + s) * R + t, nbatch, stride=L)
                nr = ar * hr - ai * hi + x_ref[rr, :]
                ni = ar * hi + ai * hr + x_ref[ri, :]
                x_ref[rr, :] = nr
                x_ref[ri, :] = ni
                hr, hi = nr, ni
            hr_ref[:, st] = hr
            hi_ref[:, st] = hi
        hcat = jnp.concatenate([x_ref[s * R:(s + 1) * R, :] for s in range(2 * SLABS)], axis=1)
        y = _bdot(hcat, c_ref[sb]) + d_ref[:, cols] * u_ref[:, cols]
        z_ref[:, cols] = jax.nn.gelu(y).astype(z_ref.dtype)


def s5_sample(u, h0r, h0i, a_re, a_im, b_blk, c_blk, d_row):
    rows = u.shape[0]
    nbatch = rows // DEC_SEQ
    nstate = N_SSM_GROUPS * SSM_STATE
    full = lambda shape: pl.BlockSpec(shape, lambda i: (0,) * len(shape))
    return pl.pallas_call(
        functools.partial(_s5_sample_kernel, nbatch=nbatch),
        grid=(1,),
        in_specs=[
            full((rows, D_MODEL)), full((nbatch, nstate)), full((nbatch, nstate)),
            full((1, nstate)), full((1, nstate)),
            full((N_SUPER, SUPER_IN, 2 * SUPER_STATE)),
            full((N_SUPER, 2 * SUPER_STATE, SUPER_IN)),
            full((1, D_MODEL)),
        ],
        out_specs=[full((rows, D_MODEL)), full((nbatch, nstate)), full((nbatch, nstate))],
        out_shape=[
            jax.ShapeDtypeStruct((rows, D_MODEL), BF16),
            jax.ShapeDtypeStruct((nbatch, nstate), F32),
            jax.ShapeDtypeStruct((nbatch, nstate), F32),
        ],
        scratch_shapes=[pltpu.VMEM((2 * SLABS * rows, LANES), F32)],
        compiler_params=_params("arbitrary"),
        name="s5_sample",
    )(u, h0r, h0i, a_re, a_im, b_blk, c_blk, d_row)


def _s5_weights(lam_re, lam_im, log_step, b_re, b_im, c_re, c_im):
    lam = lax.complex(lam_re.astype(F32), lam_im.astype(F32))
    dt = jnp.exp(log_step.astype(F32))[:, None]
    abar = jnp.exp(lam * dt)
    bbar = ((abar - 1.0) / lam)[..., None] * lax.complex(b_re.astype(F32), b_im.astype(F32))
    eye = jnp.eye(SSM_SUPER, dtype=F32)

    def b_layout(m):
        m = m.reshape(N_SUPER, SSM_SUPER, SSM_STATE, SSM_GROUP)
        return jnp.einsum('sgph,gk->sghkp', m, eye).reshape(N_SUPER, SUPER_IN, SUPER_STATE)

    def c_layout(m):
        m = m.reshape(N_SUPER, SSM_SUPER, SSM_GROUP, SSM_STATE)
        return jnp.einsum('sghp,gk->sgpkh', m, eye).reshape(N_SUPER, SUPER_STATE, SUPER_IN)

    b_blk = jnp.concatenate([b_layout(bbar.real), b_layout(bbar.imag)], axis=2).astype(BF16)
    c_blk = jnp.concatenate([c_layout(c_re.astype(F32)), -c_layout(c_im.astype(F32))],
                            axis=1).astype(BF16)
    return abar.real, abar.imag, b_blk, c_blk


def _conv_gate(ua, uv, p1a, p2a, p1v, p2v, cwa_ref, cwv_ref, cba_ref, cbv_ref):
    ca = cba_ref[...] + cwa_ref[0:1, :] * p2a + cwa_ref[1:2, :] * p1a + cwa_ref[2:3, :] * ua
    cv = cbv_ref[...] + cwv_ref[0:1, :] * p2v + cwv_ref[1:2, :] * p1v + cwv_ref[2:3, :] * uv
    return (jax.nn.silu(ca) * cv).astype(BF16)


def _ffn_prompt_kernel(x_ref, gpre_ref, wua_ref, wuv_ref, cwa_ref, cwv_ref, cba_ref, cbv_ref,
                       wd_ref, gpost_ref, o_ref, sa_ref, sv_ref,
                       hn_ref, ua_ref, uv_ref, act_ref, ha_ref, hv_ref,
                       *, nc, blocks_per_seq, rc, rm):
    i = pl.program_id(0)
    c = pl.program_id(1)
    tm, D = hn_ref.shape
    hdr = SUBLANES

    @pl.when(c == 0)
    def _():
        hn_ref[...] = _rms(x_ref[...], gpre_ref[...]).astype(BF16)
        o_ref[...] = jnp.zeros(o_ref.shape, F32)

    slot = i % 2

    @pl.when(i % blocks_per_seq == 0)
    def _():
        ha_ref[1 - slot, c] = jnp.zeros(ha_ref.shape[2:], F32)
        hv_ref[1 - slot, c] = jnp.zeros(hv_ref.shape[2:], F32)

    ua_ref[0:hdr, :] = ha_ref[1 - slot, c]
    uv_ref[0:hdr, :] = hv_ref[1 - slot, c]
    wa = wua_ref[...].astype(BF16)
    wv = wuv_ref[...].astype(BF16)
    wd = wd_ref[...].astype(BF16)

    def conv(buf_ref, w_ref, b_ref, r0):
        return (b_ref[...] + w_ref[0:1, :] * buf_ref[r0 + hdr - 2:r0 + hdr - 2 + rc, :]
                + w_ref[1:2, :] * buf_ref[r0 + hdr - 1:r0 + hdr - 1 + rc, :]
                + w_ref[2:3, :] * buf_ref[r0 + hdr:r0 + hdr + rc, :])

    def up_proj(k):
        hk = hn_ref[k * rm:(k + 1) * rm, :]
        ua_ref[hdr + k * rm:hdr + (k + 1) * rm, :] = jnp.dot(hk, wa, preferred_element_type=F32)
        uv_ref[hdr + k * rm:hdr + (k + 1) * rm, :] = jnp.dot(hk, wv, preferred_element_type=F32)

    ngroups = tm // rm
    up_proj(0)
    for k in range(ngroups):
        if k + 1 < ngroups:
            up_proj(k + 1)
        rows = slice(k * rm, (k + 1) * rm)
        for r0 in range(k * rm, (k + 1) * rm, rc):
            ca = conv(ua_ref, cwa_ref, cba_ref, r0)
            cv = conv(uv_ref, cwv_ref, cbv_ref, r0)
            act_ref[r0:r0 + rc, :] = (jax.nn.silu(ca) * cv).astype(BF16)
        o_ref[rows, :] += jnp.dot(act_ref[rows, :], wd, preferred_element_type=F32)

    ta = ua_ref[tm:tm + hdr, :]
    tv = uv_ref[tm:tm + hdr, :]
    ha_ref[slot, c] = ta
    hv_ref[slot, c] = tv
    sa_ref[c] = ta[hdr - 2:hdr, :]
    sv_ref[c] = tv[hdr - 2:hdr, :]

    @pl.when(c == nc - 1)
    def _():
        o_ref[...] = x_ref[...] + _rms(o_ref[...], gpost_ref[...])


FFN_ROW_CHUNK = 64
FFN_ROW_GROUP = 256


def ffn_prompt(x, layer, g_pre, w_up, conv_w, conv_b, w_down, g_post, batch, tm, tf):
    M, D = x.shape
    nc = D_FF // tf
    bps = SEQ // tm
    conv_b3 = conv_b.reshape(DEPTH, 1, 2 * D_FF)
    y, sa, sv = pl.pallas_call(
        functools.partial(_ffn_prompt_kernel, nc=nc, blocks_per_seq=bps, rc=FFN_ROW_CHUNK,
                          rm=FFN_ROW_GROUP),
        grid=(M // tm, nc),
        in_specs=[
            _once((tm, D), lambda i, c: (i, 0)),
            pl.BlockSpec((None, 1, D), lambda i, c: (layer, 0, 0)),
            pl.BlockSpec((None, D, tf), lambda i, c: (layer, 0, c)),
            pl.BlockSpec((None, D, tf), lambda i, c: (layer, 0, nc + c)),
            pl.BlockSpec((None, 3, tf), lambda i, c: (layer, 0, c)),
            pl.BlockSpec((None, 3, tf), lambda i, c: (layer, 0, nc + c)),
            pl.BlockSpec((None, 1, tf), lambda i, c: (layer, 0, c)),
            pl.BlockSpec((None, 1, tf), lambda i, c: (layer, 0, nc + c)),
            pl.BlockSpec((None, tf, D), lambda i, c: (layer, c, 0)),
            pl.BlockSpec((None, 1, D), lambda i, c: (layer, 0, 0)),
        ],
        out_specs=[
            _once((tm, D), lambda i, c: (i, 0)),
            pl.BlockSpec((None, nc, 2, tf), lambda i, c: (i // bps, 0, 0, 0)),
            pl.BlockSpec((None, nc, 2, tf), lambda i, c: (i // bps, 0, 0, 0)),
        ],
        out_shape=[
            jax.ShapeDtypeStruct((M, D), F32),
            jax.ShapeDtypeStruct((batch, nc, 2, tf), F32),
            jax.ShapeDtypeStruct((batch, nc, 2, tf), F32),
        ],
        scratch_shapes=[
            pltpu.VMEM((tm, D), BF16),
            pltpu.VMEM((tm + SUBLANES, tf), F32),
            pltpu.VMEM((tm + SUBLANES, tf), F32),
            pltpu.VMEM((tm, tf), BF16),
            pltpu.VMEM((2, nc, SUBLANES, tf), F32),
            pltpu.VMEM((2, nc, SUBLANES, tf), F32),
        ],
        compiler_params=_params("arbitrary", "arbitrary"),
        name="ffn_prompt",
    )(x, g_pre.reshape(DEPTH, 1, D), w_up, w_up, conv_w, conv_w, conv_b3, conv_b3,
      w_down, g_post.reshape(DEPTH, 1, D))
    sa = sa.transpose(0, 2, 1, 3).reshape(batch, 2, D_FF)
    sv = sv.transpose(0, 2, 1, 3).reshape(batch, 2, D_FF)
    return y, jnp.concatenate([sa, sv], axis=-1)


def _ffn_sample_kernel(x_ref, gpre_ref, wua_ref, wuv_ref, cwa_ref, cwv_ref, cba_ref, cbv_ref,
                       wd_ref, gpost_ref, s1a_ref, s2a_ref, s1v_ref, s2v_ref,
                       o_ref, ua_ref, uv_ref, hn_ref, *, nc):
    c = pl.program_id(0)

    @pl.when(c == 0)
    def _():
        hn_ref[...] = _rms(x_ref[...], gpre_ref[...]).astype(BF16)

    hn = hn_ref[...]
    ua = jnp.dot(hn, wua_ref[...].astype(BF16), preferred_element_type=F32)
    uv = jnp.dot(hn, wuv_ref[...].astype(BF16), preferred_element_type=F32)
    ua_ref[...] = ua
    uv_ref[...] = uv
    t = lax.broadcasted_iota(jnp.int32, ua.shape, 0) % DEC_SEQ

    def shifted(u, s1_ref, s2_ref):
        p1 = jnp.where(t == 0, s1_ref[...], pltpu.roll(u, 1, 0))
        p2 = jnp.where(t <= 1, s2_ref[...], pltpu.roll(u, 2, 0))
        return p1, p2

    p1a, p2a = shifted(ua, s1a_ref, s2a_ref)
    p1v, p2v = shifted(uv, s1v_ref, s2v_ref)
    act = _conv_gate(ua, uv, p1a, p2a, p1v, p2v, cwa_ref, cwv_ref, cba_ref, cbv_ref)
    part = jnp.dot(act, wd_ref[...].astype(BF16), preferred_element_type=F32)

    @pl.when(c == 0)
    def _():
        o_ref[...] = part

    @pl.when(c > 0)
    def _():
        o_ref[...] += part

    @pl.when(c == nc - 1)
    def _():
        o_ref[...] = x_ref[...] + _rms(o_ref[...], gpost_ref[...])


def ffn_sample(x, layer, g_pre, w_up, conv_w, conv_b, w_down, g_post, conv_state, tf):
    M, D = x.shape
    nbatch = M // DEC_SEQ
    nc = D_FF // tf
    conv_b3 = conv_b.reshape(DEPTH, 1, 2 * D_FF)
    zeros = jnp.zeros((nbatch, DEC_SEQ - 2, 2 * D_FF), F32)
    s1 = jnp.concatenate([conv_state[:, 1:2], conv_state[:, 0:1] * 0, zeros], axis=1)
    s2 = jnp.concatenate([conv_state, zeros], axis=1)
    s1 = s1.reshape(M, 2 * D_FF)
    s2 = s2.reshape(M, 2 * D_FF)
    y, ua, uv = pl.pallas_call(
        functools.partial(_ffn_sample_kernel, nc=nc),
        grid=(nc,),
        in_specs=[
            pl.BlockSpec((M, D), lambda c: (0, 0)),
            pl.BlockSpec((None, 1, D), lambda c: (layer, 0, 0)),
            pl.BlockSpec((None, D, tf), lambda c: (layer, 0, c)),
            pl.BlockSpec((None, D, tf), lambda c: (layer, 0, nc + c)),
            pl.BlockSpec((None, 3, tf), lambda c: (layer, 0, c)),
            pl.BlockSpec((None, 3, tf), lambda c: (layer, 0, nc + c)),
            pl.BlockSpec((None, 1, tf), lambda c: (layer, 0, c)),
            pl.BlockSpec((None, 1, tf), lambda c: (layer, 0, nc + c)),
            pl.BlockSpec((None, tf, D), lambda c: (layer, c, 0)),
            pl.BlockSpec((None, 1, D), lambda c: (layer, 0, 0)),
            pl.BlockSpec((M, tf), lambda c: (0, c)),
            pl.BlockSpec((M, tf), lambda c: (0, c)),
            pl.BlockSpec((M, tf), lambda c: (0, nc + c)),
            pl.BlockSpec((M, tf), lambda c: (0, nc + c)),
        ],
        out_specs=[
            pl.BlockSpec((M, D), lambda c: (0, 0)),
            pl.BlockSpec((M, tf), lambda c: (0, c)),
            pl.BlockSpec((M, tf), lambda c: (0, c)),
        ],
        out_shape=[
            jax.ShapeDtypeStruct((M, D), F32),
            jax.ShapeDtypeStruct((M, D_FF), F32),
            jax.ShapeDtypeStruct((M, D_FF), F32),
        ],
        scratch_shapes=[pltpu.VMEM((M, D), BF16)],
        compiler_params=_params("arbitrary"),
        name="ffn_sample",
    )(x, g_pre.reshape(DEPTH, 1, D), w_up, w_up, conv_w, conv_w, conv_b3, conv_b3,
      w_down, g_post.reshape(DEPTH, 1, D), s1, s2, s1, s2)
    up = jnp.concatenate([ua, uv], axis=-1).reshape(nbatch, DEC_SEQ, 2 * D_FF)
    return y, up[:, DEC_SEQ - 2:]


def kernel(x_prompt, x_sample, mem_prompt, cache_swa_k, cache_swa_v, state_ssm_re, state_ssm_im, state_ffn_conv, cache_mem_k, cache_mem_v, g_mix_pre, g_mix_post, w_qkv, w_attn_o, attn_sinks, w_ssm_in, ssm_lambda_re, ssm_lambda_im, ssm_log_step, ssm_b_re, ssm_b_im, ssm_c_re, ssm_c_im, ssm_d, w_ssm_glu, g_x_pre, g_x_post, g_mem, w_x_q, w_mem_k, w_mem_v, w_x_o, g_ffn_pre, g_ffn_post, w_ffn_up, ffn_conv_w, ffn_conv_b, w_ffn_down):
    B = x_prompt.shape[0]
    SB = x_sample.shape[0]
    xw = N_XHEADS * XHEAD_DIM
    nkc = N_KV_HEADS * HEAD_DIM
    xp = x_prompt.reshape(B * SEQ, D_MODEL)
    xs = x_sample.reshape(SB * DEC_SEQ, D_MODEL)
    mem = mem_prompt.reshape(B * N_MEM, D_MODEL)
    MS = SB * DEC_SEQ
    TMP = 1024
    cos_p, sin_p = _rope_tables(jnp.arange(SEQ))
    cos_s, sin_s = _rope_tables(PAST_LEN + jnp.arange(DEC_SEQ))
    ck_all = cache_swa_k.reshape(cache_swa_k.shape[0], SB, WINDOW, nkc)
    cv_all = cache_swa_v.reshape(cache_swa_v.shape[0], SB, WINDOW, nkc)
    cmk_all = cache_mem_k.reshape(DEPTH, SB, N_MEM, xw)
    cmv_all = cache_mem_v.reshape(DEPTH, SB, N_MEM, xw)
    w_up_bf = w_ffn_up.astype(BF16)
    w_down_bf = w_ffn_down.astype(BF16)
    w_qkv_bf = w_qkv.astype(BF16)
    w_attn_o_bf = w_attn_o.astype(BF16)
    w_ssm_in_bf = w_ssm_in.astype(BF16)
    w_glu_bf = w_ssm_glu.astype(BF16)
    w_x_q_bf = w_x_q.astype(BF16)
    w_x_o_bf = w_x_o.astype(BF16)
    w_mem_kv_bf = jnp.concatenate([w_mem_k, w_mem_v], axis=-1).astype(BF16)
    TD = 512

    swa_kp, swa_vp, swa_ks, swa_vs = [], [], [], []
    ssm_rp, ssm_ip, ssm_rs, ssm_is = [], [], [], []
    conv_p, conv_s, memk_p, memv_p = [], [], [], []
    for i in range(DEPTH):
        j = i // 2
        if i % 2 == 0:
            qkv_p = norm_matmul(xp, g_mix_pre, i, w_qkv_bf, j, TD)
            qkv_s = norm_matmul(xs, g_mix_pre, i, w_qkv_bf, j, MS)
            op, kp, vp = swa_prompt(qkv_p, attn_sinks, j, cos_p, sin_p, B)
            sink_rows = jnp.repeat(attn_sinks[j].reshape(N_KV_HEADS, GQA_GROUP), DEC_SEQ,
                                   axis=1)[..., None]
            os_, kn, vn = swa_sample(qkv_s, ck_all, cv_all, j, sink_rows, cos_s, sin_s)
            swa_kp.append(kp.reshape(B, WINDOW, N_KV_HEADS, HEAD_DIM))
            swa_vp.append(vp.reshape(B, WINDOW, N_KV_HEADS, HEAD_DIM))
            swa_ks.append(kn.reshape(SB, WINDOW, N_KV_HEADS, HEAD_DIM))
            swa_vs.append(vn.reshape(SB, WINDOW, N_KV_HEADS, HEAD_DIM))
            xp = matmul_post(op, w_attn_o_bf, j, g_mix_post, i, xp, TD)
            xs = matmul_post(os_, w_attn_o_bf, j, g_mix_post, i, xs, MS)
        else:
            a_re, a_im, b_blk, c_blk = _s5_weights(
                ssm_lambda_re[j], ssm_lambda_im[j], ssm_log_step[j], ssm_b_re[j], ssm_b_im[j],
                ssm_c_re[j], ssm_c_im[j])
            d_row = ssm_d[j].reshape(1, D_MODEL)
            up_ = norm_matmul(xp, g_mix_pre, i, w_ssm_in_bf, j, TD)
            us_ = norm_matmul(xs, g_mix_pre, i, w_ssm_in_bf, j, MS)
            a_tiles = jnp.stack([a_re.reshape(N_SUPER, SUBLANES, LANES),
                                 a_im.reshape(N_SUPER, SUBLANES, LANES)], axis=1
                                ).reshape(2 * N_SUPER, SUBLANES, LANES)
            zero_state = jnp.zeros((B, N_SUPER, SUBLANES, LANES), F32)
            zp, rp, ip = s5_prompt(up_, zero_state, zero_state, a_tiles, b_blk, c_blk, d_row,
                                   B, 256)
            nstate = N_SSM_GROUPS * SSM_STATE
            zs, rn, im_ = s5_sample(us_, state_ssm_re[j].reshape(SB, nstate),
                                    state_ssm_im[j].reshape(SB, nstate),
                                    a_re.reshape(1, nstate), a_im.reshape(1, nstate),
                                    b_blk, c_blk, d_row)
            ssm_rp.append(rp.reshape(B, N_SSM_GROUPS, SSM_STATE))
            ssm_ip.append(ip.reshape(B, N_SSM_GROUPS, SSM_STATE))
            ssm_rs.append(rn.reshape(SB, N_SSM_GROUPS, SSM_STATE))
            ssm_is.append(im_.reshape(SB, N_SSM_GROUPS, SSM_STATE))
            xp = glu_post(zp, w_glu_bf, j, g_mix_post, i, xp, TD)
            xs = glu_post(zs, w_glu_bf, j, g_mix_post, i, xs, MS)
        mkv = norm_matmul(mem, g_mem, i, w_mem_kv_bf, i, B * N_MEM)
        mk, mv = mkv[:, :xw], mkv[:, xw:]
        memk_p.append(mk.reshape(B, N_MEM, N_XHEADS, XHEAD_DIM))
        memv_p.append(mv.reshape(B, N_MEM, N_XHEADS, XHEAD_DIM))
        xp = xattn_prompt(xp, i, g_x_pre, w_x_q_bf, mk, mv, w_x_o_bf, g_x_post, B, TD)
        qs = norm_matmul(xs, g_x_pre, i, w_x_q_bf, i, MS)
        as_ = xattn_sample(qs, cmk_all, cmv_all, i)
        xs = matmul_post(as_, w_x_o_bf, i, g_x_post, i, xs, MS)
        xp, cp = ffn_prompt(xp, i, g_ffn_pre, w_up_bf, ffn_conv_w, ffn_conv_b, w_down_bf,
                            g_ffn_post, B, TMP, 512)
        xs, cs = ffn_sample(xs, i, g_ffn_pre, w_up_bf, ffn_conv_w, ffn_conv_b, w_down_bf,
                            g_ffn_post, state_ffn_conv[i], 512)
        conv_p.append(cp)
        conv_s.append(cs)
    return (xp.reshape(B, SEQ, D_MODEL), xs.reshape(SB, DEC_SEQ, D_MODEL),
            jnp.stack(swa_kp), jnp.stack(swa_vp), jnp.stack(swa_ks), jnp.stack(swa_vs),
            jnp.stack(ssm_rp), jnp.stack(ssm_ip), jnp.stack(ssm_rs), jnp.stack(ssm_is),
            jnp.stack(conv_p), jnp.stack(conv_s), jnp.stack(memk_p), jnp.stack(memv_p))
```

```python
import functools
import math

import jax
import jax.numpy as jnp
from jax import lax
from jax.experimental import pallas as pl
from jax.experimental.pallas import tpu as pltpu

D_MODEL = 2048
SEQ = 4096
DEPTH = 4
DEC_SEQ = 8
PAST_LEN = 16384
HEAD_DIM = 64
N_HEADS = 32
N_KV_HEADS = 4
GQA_GROUP = 8
WINDOW = 128
ROPE_THETA = 10000.0
SSM_GROUP = 16
N_SSM_GROUPS = 128
SSM_STATE = 64
N_MEM = 256
N_XHEADS = 4
XHEAD_DIM = 128
D_FF = 5632
NORM_EPS = 1e-6
NEG_INF = -1e30

F32 = jnp.float32
BF16 = jnp.bfloat16

V7X_VMEM_BYTES = 64 * 1024 * 1024
VMEM_LIMIT = V7X_VMEM_BYTES - 8 * 1024 * 1024
LANES = 128
SUBLANES = 8
SSM_SUPER = 16
N_SUPER = N_SSM_GROUPS // SSM_SUPER
SUPER_IN = SSM_SUPER * SSM_GROUP
SUPER_STATE = SSM_SUPER * SSM_STATE
SLABS = SUPER_STATE // LANES


def _params(*sem, vmem=VMEM_LIMIT):
    return pltpu.CompilerParams(dimension_semantics=sem, vmem_limit_bytes=vmem)


def _rms(x, g):
    ms = jnp.mean(x * x, axis=-1, keepdims=True)
    return x * lax.rsqrt(ms + NORM_EPS) * g


def _bdot(a, b):
    return jnp.dot(a.astype(BF16), b.astype(BF16), preferred_element_type=F32)


def _once(shape, index_map):
    return pl.BlockSpec(shape, index_map, pipeline_mode=pl.Buffered(1))


ROW_GROUP = 256


def _resident(shape, index_map):
    return pl.BlockSpec(shape, index_map, pipeline_mode=pl.Buffered(1))


def _row_groups(tm):
    rg = min(ROW_GROUP, tm)
    return [slice(k * rg, (k + 1) * rg) for k in range(tm // rg)]


def _norm_matmul_kernel(x_ref, g_ref, w_ref, o_ref):
    g = g_ref[...]
    for rows in _row_groups(x_ref.shape[0]):
        hn = _rms(x_ref[rows, :], g).astype(BF16)
        o_ref[rows, :] = jnp.dot(hn, w_ref[...], preferred_element_type=F32).astype(o_ref.dtype)


def norm_matmul(x, g, g_layer, w, w_layer, tm, out_dtype=F32):
    M, D = x.shape
    N = w.shape[-1]
    return pl.pallas_call(
        _norm_matmul_kernel,
        grid=(M // tm,),
        in_specs=[
            pl.BlockSpec((tm, D), lambda i: (i, 0)),
            pl.BlockSpec((None, 1, D), lambda i: (g_layer, 0, 0)),
            _resident((None, D, N), lambda i: (w_layer, 0, 0)),
        ],
        out_specs=pl.BlockSpec((tm, N), lambda i: (i, 0)),
        out_shape=jax.ShapeDtypeStruct((M, N), out_dtype),
        compiler_params=_params("arbitrary"),
        name="norm_matmul",
    )(x, g.reshape(g.shape[0], 1, D), w)


def _matmul_post_kernel(a_ref, w_ref, g_ref, x_ref, o_ref):
    g = g_ref[...]
    for rows in _row_groups(x_ref.shape[0]):
        y = jnp.dot(a_ref[rows, :].astype(BF16), w_ref[...], preferred_element_type=F32)
        o_ref[rows, :] = x_ref[rows, :] + _rms(y, g)


def matmul_post(a, w, w_layer, g, g_layer, x, tm):
    M, K = a.shape
    D = w.shape[-1]
    return pl.pallas_call(
        _matmul_post_kernel,
        grid=(M // tm,),
        in_specs=[
            pl.BlockSpec((tm, K), lambda i: (i, 0)),
            _resident((None, K, D), lambda i: (w_layer, 0, 0)),
            pl.BlockSpec((None, 1, D), lambda i: (g_layer, 0, 0)),
            pl.BlockSpec((tm, D), lambda i: (i, 0)),
        ],
        out_specs=pl.BlockSpec((tm, D), lambda i: (i, 0)),
        out_shape=jax.ShapeDtypeStruct((M, D), F32),
        compiler_params=_params("arbitrary"),
        name="matmul_post",
    )(a, w, g.reshape(g.shape[0], 1, D), x)


GLU_COLS = 512


def _glu_post_kernel(z_ref, w_ref, g_ref, x_ref, o_ref, y_ref):
    D = o_ref.shape[1]
    g = g_ref[...]
    for rows in _row_groups(x_ref.shape[0]):
        z = z_ref[rows, :]
        for n in range(D // GLU_COLS):
            cols = slice(n * GLU_COLS, (n + 1) * GLU_COLS)
            gcols = slice(D + n * GLU_COLS, D + (n + 1) * GLU_COLS)
            val = jnp.dot(z, w_ref[:, cols], preferred_element_type=F32)
            gate = jnp.dot(z, w_ref[:, gcols], preferred_element_type=F32)
            y_ref[rows, cols] = val * jax.nn.sigmoid(gate)
        o_ref[rows, :] = x_ref[rows, :] + _rms(y_ref[rows, :], g)


def glu_post(z, w, w_layer, g, g_layer, x, tm):
    M, K = z.shape
    D = D_MODEL
    return pl.pallas_call(
        _glu_post_kernel,
        grid=(M // tm,),
        in_specs=[
            pl.BlockSpec((tm, K), lambda i: (i, 0)),
            _resident((None, K, 2 * D), lambda i: (w_layer, 0, 0)),
            pl.BlockSpec((None, 1, D), lambda i: (g_layer, 0, 0)),
            pl.BlockSpec((tm, D), lambda i: (i, 0)),
        ],
        out_specs=pl.BlockSpec((tm, D), lambda i: (i, 0)),
        out_shape=jax.ShapeDtypeStruct((M, D), F32),
        scratch_shapes=[pltpu.VMEM((tm, D), F32)],
        compiler_params=_params("arbitrary"),
        name="glu_post",
    )(z, w, g.reshape(g.shape[0], 1, D), x)


def _rope_chunk(xc, cos, sin_signed):
    lane = lax.broadcasted_iota(jnp.int32, xc.shape, 1)
    first_half = (lane & (HEAD_DIM - 1)) < (HEAD_DIM // 2)
    partner = jnp.where(first_half, pltpu.roll(xc, LANES - HEAD_DIM // 2, 1),
                        pltpu.roll(xc, HEAD_DIM // 2, 1))
    return xc * cos + partner * sin_signed


def _rope_tables(pos):
    half = HEAD_DIM // 2
    inv_freq = ROPE_THETA ** (-jnp.arange(half, dtype=F32) * 2.0 / HEAD_DIM)
    ang = pos.astype(F32)[:, None] * inv_freq[None, :]
    cos, sin = jnp.cos(ang), jnp.sin(ang)
    cos128 = jnp.tile(cos, (1, LANES // half))
    sin128 = jnp.tile(jnp.concatenate([-sin, sin], axis=1), (1, LANES // HEAD_DIM))
    return cos128, sin128


def _swa_prompt_kernel(sink_ref, q_ref, kv_ref, cos_ref, sin_ref,
                       o_ref, kc_ref, vc_ref, kk_ref, vv_ref, *, layer):
    i = pl.program_id(1)
    blk = WINDOW
    cos, sn = cos_ref[...], sin_ref[...]
    kv = kv_ref[...]
    nkc = N_KV_HEADS * HEAD_DIM
    k = jnp.concatenate([_rope_chunk(kv[:, c * LANES:(c + 1) * LANES], cos, sn)
                         for c in range(nkc // LANES)], axis=1)
    v = kv[:, nkc:2 * nkc]
    kc_ref[...] = k
    vc_ref[...] = v

    cur = i % 2
    prv = 1 - cur

    @pl.when(i == 0)
    def _():
        kk_ref[1] = jnp.zeros((blk, nkc), BF16)
        vv_ref[1] = jnp.zeros((blk, nkc), BF16)

    kb, vb = k.astype(BF16), v.astype(BF16)
    kk_ref[cur] = kb
    vv_ref[cur] = vb
    kk = jnp.concatenate([kk_ref[prv], kb], axis=0)
    vv = jnp.concatenate([vv_ref[prv], vb], axis=0)
    rows = GQA_GROUP * blk
    qi = lax.broadcasted_iota(jnp.int32, (rows, 2 * blk), 0) & (blk - 1)
    si = lax.broadcasted_iota(jnp.int32, (rows, 2 * blk), 1)
    d = blk + qi - si
    mask = (d >= 0) & (d < WINDOW) & ((si >= blk) | (i > 0))

    chunks_per_group = GQA_GROUP * HEAD_DIM // LANES

    def scores(kh):
        hs = slice(kh * HEAD_DIM, (kh + 1) * HEAD_DIM)
        pieces = []
        for c in range(kh * chunks_per_group, (kh + 1) * chunks_per_group):
            qc = (_rope_chunk(q_ref[:, c * LANES:(c + 1) * LANES], cos, sn)
                  * (HEAD_DIM ** -0.5)).astype(BF16)
            pieces += [qc[:, hh * HEAD_DIM:(hh + 1) * HEAD_DIM] for hh in range(LANES // HEAD_DIM)]
        qs = jnp.concatenate(pieces, axis=0)
        return lax.dot_general(qs, kk[:, hs], (((1,), (1,)), ((), ())),
                               preferred_element_type=F32)

    ahead = 2
    s_all = [scores(kh) for kh in range(min(ahead, N_KV_HEADS))]
    outs = []
    for kh in range(N_KV_HEADS):
        if kh + ahead < N_KV_HEADS:
            s_all.append(scores(kh + ahead))
        hs = slice(kh * HEAD_DIM, (kh + 1) * HEAD_DIM)
        sk = jnp.concatenate([jnp.full((blk, 1), sink_ref[layer, kh * GQA_GROUP + j], F32)
                              for j in range(GQA_GROUP)], axis=0)
        s = jnp.where(mask, s_all[kh], NEG_INF)
        mx = jnp.maximum(jnp.max(s, axis=1, keepdims=True), sk)
        p = jnp.exp(s - mx)
        den = jnp.sum(p, axis=1, keepdims=True) + jnp.exp(sk - mx)
        o = jnp.dot(p.astype(BF16), vv[:, hs], preferred_element_type=F32) / den
        outs += [o[j * blk:(j + 1) * blk, :] for j in range(GQA_GROUP)]
    o_ref[...] = jnp.concatenate(outs, axis=1).astype(o_ref.dtype)


def swa_prompt(qkv, sinks, layer, cos, sin, batch):
    blk = WINDOW
    nb = SEQ // blk
    nq = N_HEADS * HEAD_DIM
    nkv = 2 * N_KV_HEADS * HEAD_DIM
    o, kc, vc = pl.pallas_call(
        functools.partial(_swa_prompt_kernel, layer=layer),
        grid=(batch, nb),
        in_specs=[
            pl.BlockSpec(memory_space=pltpu.SMEM),
            pl.BlockSpec((blk, nq), lambda b, i: (b * nb + i, 0)),
            pl.BlockSpec((blk, nkv), lambda b, i: (b * nb + i, nq // nkv)),
            pl.BlockSpec((blk, LANES), lambda b, i: (i, 0)),
            pl.BlockSpec((blk, LANES), lambda b, i: (i, 0)),
        ],
        out_specs=[
            pl.BlockSpec((blk, nq), lambda b, i: (b * nb + i, 0)),
            pl.BlockSpec((None, blk, nkv // 2), lambda b, i: (b, 0, 0)),
            pl.BlockSpec((None, blk, nkv // 2), lambda b, i: (b, 0, 0)),
        ],
        out_shape=[
            jax.ShapeDtypeStruct((batch * SEQ, nq), BF16),
            jax.ShapeDtypeStruct((batch, blk, nkv // 2), F32),
            jax.ShapeDtypeStruct((batch, blk, nkv // 2), F32),
        ],
        scratch_shapes=[pltpu.VMEM((2, blk, nkv // 2), BF16),
                        pltpu.VMEM((2, blk, nkv // 2), BF16)],
        compiler_params=_params("arbitrary", "arbitrary"),
        name="swa_prompt",
    )(sinks, qkv, qkv, cos, sin)
    return o, kc, vc


def _swa_sample_kernel(sink_ref, qkv_ref, ck_ref, cv_ref, cos_ref, sin_ref,
                       o_ref, nk_ref, nv_ref, *, nb):
    L = DEC_SEQ
    wb = WINDOW
    nq = N_HEADS * HEAD_DIM
    nkc = N_KV_HEADS * HEAD_DIM
    cos, sn = cos_ref[...], sin_ref[...]
    rows = GQA_GROUP * L
    t_q = lax.broadcasted_iota(jnp.int32, (rows, wb), 0) % L
    c_k = lax.broadcasted_iota(jnp.int32, (rows, wb), 1)
    mask_c = c_k >= t_q + 1
    t_q2 = lax.broadcasted_iota(jnp.int32, (rows, L), 0) % L
    t_k2 = lax.broadcasted_iota(jnp.int32, (rows, L), 1)
    mask_n = t_k2 <= t_q2
    for b in range(nb):
        x = qkv_ref[b * L:(b + 1) * L, :]
        qr = jnp.concatenate([_rope_chunk(x[:, c * LANES:(c + 1) * LANES], cos, sn)
                              for c in range(nq // LANES)], axis=1) * (HEAD_DIM ** -0.5)
        kn = jnp.concatenate([_rope_chunk(x[:, nq + c * LANES:nq + (c + 1) * LANES], cos, sn)
                              for c in range(nkc // LANES)], axis=1)
        vn = x[:, nq + nkc:nq + 2 * nkc]
        ck = ck_ref[b]
        cv = cv_ref[b]
        nk_ref[b, 0:wb - L, :] = ck[L:wb, :]
        nk_ref[b, wb - L:wb, :] = kn
        nv_ref[b, 0:wb - L, :] = cv[L:wb, :]
        nv_ref[b, wb - L:wb, :] = vn
        ckb, cvb, knb, vnb = ck.astype(BF16), cv.astype(BF16), kn.astype(BF16), vn.astype(BF16)
        outs = []
        for kh in range(N_KV_HEADS):
            hs = slice(kh * HEAD_DIM, (kh + 1) * HEAD_DIM)
            qs = jnp.concatenate(
                [qr[:, (kh * GQA_GROUP + j) * HEAD_DIM:(kh * GQA_GROUP + j + 1) * HEAD_DIM]
                 for j in range(GQA_GROUP)], axis=0).astype(BF16)
            dn = (((1,), (1,)), ((), ()))
            s_c = lax.dot_general(qs, ckb[:, hs], dn, preferred_element_type=F32)
            s_n = lax.dot_general(qs, knb[:, hs], dn, preferred_element_type=F32)
            s_c = jnp.where(mask_c, s_c, NEG_INF)
            s_n = jnp.where(mask_n, s_n, NEG_INF)
            sk = sink_ref[kh]
            mx = jnp.maximum(jnp.maximum(jnp.max(s_c, axis=1, keepdims=True),
                                         jnp.max(s_n, axis=1, keepdims=True)), sk)
            p_c = jnp.exp(s_c - mx)
            p_n = jnp.exp(s_n - mx)
            den = (jnp.sum(p_c, axis=1, keepdims=True) + jnp.sum(p_n, axis=1, keepdims=True)
                   + jnp.exp(sk - mx))
            o = (jnp.dot(p_c.astype(BF16), cvb[:, hs], preferred_element_type=F32)
                 + jnp.dot(p_n.astype(BF16), vnb[:, hs], preferred_element_type=F32)) / den
            outs.extend(o[j * L:(j + 1) * L, :] for j in range(GQA_GROUP))
        o_ref[b * L:(b + 1) * L, :] = jnp.concatenate(outs, axis=1).astype(o_ref.dtype)


def swa_sample(qkv, cache_k, cache_v, layer, sink_rows, cos, sin, nb=8):
    nbatch = cache_k.shape[1]
    nq = N_HEADS * HEAD_DIM
    nkc = N_KV_HEADS * HEAD_DIM
    L = DEC_SEQ
    return pl.pallas_call(
        functools.partial(_swa_sample_kernel, nb=nb),
        grid=(nbatch // nb,),
        in_specs=[
            pl.BlockSpec((N_KV_HEADS, GQA_GROUP * L, 1), lambda g: (0, 0, 0)),
            pl.BlockSpec((nb * L, nq + 2 * nkc), lambda g: (g, 0)),
            pl.BlockSpec((None, nb, WINDOW, nkc), lambda g: (layer, g, 0, 0)),
            pl.BlockSpec((None, nb, WINDOW, nkc), lambda g: (layer, g, 0, 0)),
            pl.BlockSpec((L, LANES), lambda g: (0, 0)),
            pl.BlockSpec((L, LANES), lambda g: (0, 0)),
        ],
        out_specs=[
            pl.BlockSpec((nb * L, nq), lambda g: (g, 0)),
            pl.BlockSpec((nb, WINDOW, nkc), lambda g: (g, 0, 0)),
            pl.BlockSpec((nb, WINDOW, nkc), lambda g: (g, 0, 0)),
        ],
        out_shape=[
            jax.ShapeDtypeStruct((nbatch * L, nq), BF16),
            jax.ShapeDtypeStruct((nbatch, WINDOW, nkc), F32),
            jax.ShapeDtypeStruct((nbatch, WINDOW, nkc), F32),
        ],
        compiler_params=_params("arbitrary"),
        name="swa_sample",
    )(sink_rows, qkv, cache_k, cache_v, cos, sin)


def _xattn_heads(q, mk, mv):
    heads = [slice(h * XHEAD_DIM, (h + 1) * XHEAD_DIM) for h in range(N_XHEADS)]
    scores = [lax.dot_general((q[:, hs] * (XHEAD_DIM ** -0.5)).astype(BF16), mk[:, hs].astype(BF16),
                              (((1,), (1,)), ((), ())), preferred_element_type=F32)
              for hs in heads]
    outs = []
    for hs, s in zip(heads, scores):
        mx = jnp.max(s, axis=1, keepdims=True)
        p = jnp.exp(s - mx)
        den = jnp.sum(p, axis=1, keepdims=True)
        outs.append(jnp.dot(p.astype(BF16), mv[:, hs].astype(BF16),
                            preferred_element_type=F32) / den)
    return jnp.concatenate(outs, axis=1)


def _xattn_prompt_kernel(x_ref, gpre_ref, wq_ref, mk_ref, mv_ref, wo_ref, gpost_ref, o_ref):
    mk = mk_ref[...].astype(BF16)
    mv = mv_ref[...].astype(BF16)
    gpre, gpost = gpre_ref[...], gpost_ref[...]
    for rows in _row_groups(x_ref.shape[0]):
        x = x_ref[rows, :]
        q = jnp.dot(_rms(x, gpre).astype(BF16), wq_ref[...], preferred_element_type=F32)
        a = _xattn_heads(q, mk, mv).astype(BF16)
        y = jnp.dot(a, wo_ref[...], preferred_element_type=F32)
        o_ref[rows, :] = x + _rms(y, gpost)


def xattn_prompt(x, layer, g_pre, w_q, mk, mv, w_o, g_post, batch, tm):
    nq = SEQ // tm
    xw = N_XHEADS * XHEAD_DIM
    D = D_MODEL
    return pl.pallas_call(
        _xattn_prompt_kernel,
        grid=(batch, nq),
        in_specs=[
            pl.BlockSpec((tm, D), lambda b, i: (b * nq + i, 0)),
            pl.BlockSpec((None, 1, D), lambda b, i: (layer, 0, 0)),
            _resident((None, D, xw), lambda b, i: (layer, 0, 0)),
            pl.BlockSpec((N_MEM, xw), lambda b, i: (b, 0)),
            pl.BlockSpec((N_MEM, xw), lambda b, i: (b, 0)),
            _resident((None, xw, D), lambda b, i: (layer, 0, 0)),
            pl.BlockSpec((None, 1, D), lambda b, i: (layer, 0, 0)),
        ],
        out_specs=pl.BlockSpec((tm, D), lambda b, i: (b * nq + i, 0)),
        out_shape=jax.ShapeDtypeStruct((batch * SEQ, D), F32),
        compiler_params=_params("arbitrary", "arbitrary"),
        name="xattn_prompt",
    )(x, g_pre.reshape(DEPTH, 1, D), w_q, mk, mv, w_o, g_post.reshape(DEPTH, 1, D))


def _xattn_sample_kernel(q_ref, mk_ref, mv_ref, o_ref, *, nb):
    L = DEC_SEQ
    for b in range(nb):
        o_ref[b * L:(b + 1) * L, :] = _xattn_heads(
            q_ref[b * L:(b + 1) * L, :], mk_ref[b], mv_ref[b]).astype(o_ref.dtype)


def xattn_sample(q, cache_k, cache_v, layer, nb=8):
    nbatch = cache_k.shape[1]
    L = DEC_SEQ
    xw = N_XHEADS * XHEAD_DIM
    return pl.pallas_call(
        functools.partial(_xattn_sample_kernel, nb=nb),
        grid=(nbatch // nb,),
        in_specs=[
            pl.BlockSpec((nb * L, xw), lambda g: (g, 0)),
            pl.BlockSpec((None, nb, N_MEM, xw), lambda g: (layer, g, 0, 0)),
            pl.BlockSpec((None, nb, N_MEM, xw), lambda g: (layer, g, 0, 0)),
        ],
        out_specs=pl.BlockSpec((nb * L, xw), lambda g: (g, 0)),
        out_shape=jax.ShapeDtypeStruct((nbatch * L, xw), BF16),
        compiler_params=_params("arbitrary"),
        name="xattn_sample",
    )(q, cache_k, cache_v)


def _s5_prompt_kernel(u_ref, h0r_ref, h0i_ref, a_ref, b_ref, c_ref, d_ref,
                      z_ref, hr_ref, hi_ref, x_ref, hst_ref, *, T):
    c = pl.program_id(1)
    nslab = 2 * SLABS
    G = T // SUBLANES

    @pl.when(c == 0)
    def _():
        for sb in range(N_SUPER):
            hst_ref[2 * sb] = h0r_ref[sb]
            hst_ref[2 * sb + 1] = h0i_ref[sb]

    def slab_rows(sb, s):
        return pl.ds((sb * nslab + s) * SUBLANES, SUBLANES)

    for sb in range(N_SUPER):
        bu = _bdot(u_ref[:, sb * SUPER_IN:(sb + 1) * SUPER_IN], b_ref[sb])
        for s in range(nslab):
            x_ref[:, slab_rows(sb, s), :] = bu[:, s * LANES:(s + 1) * LANES].reshape(
                G, SUBLANES, LANES)

    a = [a_ref[k] for k in range(2 * N_SUPER)]
    h_init = tuple(hst_ref[k] for k in range(2 * N_SUPER))

    def step(t, h):
        g = lax.shift_right_logical(t, 3)
        r = lax.bitwise_and(t, SUBLANES - 1)
        new = []
        for sb in range(N_SUPER):
            ir = pl.ds((sb * nslab) * SUBLANES + r, SUBLANES, stride=SUBLANES)
            ii = pl.ds((sb * nslab + SLABS) * SUBLANES + r, SUBLANES, stride=SUBLANES)
            ar, ai = a[2 * sb], a[2 * sb + 1]
            hr, hi = h[2 * sb], h[2 * sb + 1]
            nr = ar * hr - ai * hi + x_ref[g, ir, :]
            ni = ar * hi + ai * hr + x_ref[g, ii, :]
            x_ref[g, ir, :] = nr
            x_ref[g, ii, :] = ni
            new += [nr, ni]
        return tuple(new)

    h_fin = lax.fori_loop(0, T, step, h_init)
    for sb in range(N_SUPER):
        hst_ref[2 * sb] = h_fin[2 * sb]
        hst_ref[2 * sb + 1] = h_fin[2 * sb + 1]
        hr_ref[sb] = h_fin[2 * sb]
        hi_ref[sb] = h_fin[2 * sb + 1]

    for sb in range(N_SUPER):
        hcat = jnp.concatenate([x_ref[:, slab_rows(sb, s), :].reshape(T, LANES)
                                for s in range(nslab)], axis=1)
        cols = slice(sb * SUPER_IN, (sb + 1) * SUPER_IN)
        y = _bdot(hcat, c_ref[sb]) + d_ref[:, cols] * u_ref[:, cols]
        z_ref[:, cols] = jax.nn.gelu(y).astype(z_ref.dtype)


def s5_prompt(u, h0r, h0i, a_tiles, b_blk, c_blk, d_row, batch, T):
    nc = SEQ // T
    st = (None, N_SUPER, SUBLANES, LANES)
    return pl.pallas_call(
        functools.partial(_s5_prompt_kernel, T=T),
        grid=(batch, nc),
        in_specs=[
            pl.BlockSpec((T, D_MODEL), lambda b, c: (b * nc + c, 0)),
            pl.BlockSpec(st, lambda b, c: (b, 0, 0, 0)),
            pl.BlockSpec(st, lambda b, c: (b, 0, 0, 0)),
            _once((2 * N_SUPER, SUBLANES, LANES), lambda b, c: (0, 0, 0)),
            _once((N_SUPER, SUPER_IN, 2 * SUPER_STATE), lambda b, c: (0, 0, 0)),
            _once((N_SUPER, 2 * SUPER_STATE, SUPER_IN), lambda b, c: (0, 0, 0)),
            _once((1, D_MODEL), lambda b, c: (0, 0)),
        ],
        out_specs=[
            pl.BlockSpec((T, D_MODEL), lambda b, c: (b * nc + c, 0)),
            pl.BlockSpec(st, lambda b, c: (b, 0, 0, 0)),
            pl.BlockSpec(st, lambda b, c: (b, 0, 0, 0)),
        ],
        out_shape=[
            jax.ShapeDtypeStruct((batch * SEQ, D_MODEL), BF16),
            jax.ShapeDtypeStruct((batch, N_SUPER, SUBLANES, LANES), F32),
            jax.ShapeDtypeStruct((batch, N_SUPER, SUBLANES, LANES), F32),
        ],
        scratch_shapes=[
            pltpu.VMEM((T // SUBLANES, N_SUPER * 2 * SLABS * SUBLANES, LANES), F32),
            pltpu.VMEM((2 * N_SUPER, SUBLANES, LANES), F32),
        ],
        compiler_params=_params("arbitrary", "arbitrary"),
        name="s5_prompt",
    )(u, h0r, h0i, a_tiles, b_blk, c_blk, d_row)


def _s5_sample_kernel(u_ref, h0r_ref, h0i_ref, ar_ref, ai_ref, b_ref, c_ref, d_ref,
                      z_ref, hr_ref, hi_ref, x_ref, *, nbatch):
    L = DEC_SEQ
    R = nbatch * L
    for sb in range(N_SUPER):
        cols = slice(sb * SUPER_IN, (sb + 1) * SUPER_IN)
        bu = _bdot(u_ref[:, cols], b_ref[sb])
        for s in range(2 * SLABS):
            x_ref[s * R:(s + 1) * R, :] = bu[:, s * LANES:(s + 1) * LANES]
        for s in range(SLABS):
            st = slice(sb * SUPER_STATE + s * LANES, sb * SUPER_STATE + (s + 1) * LANES)
            ar, ai = ar_ref[:, st], ai_ref[:, st]
            hr, hi = h0r_ref[:, st], h0i_ref[:, st]
            for t in range(L):
                rr = pl.ds(s * R + t, nbatch, stride=L)
                ri = pl.ds((SLABS + s) * R + t, nbatch, stride=L)
                nr = ar * hr - ai * hi + x_ref[rr, :]
                ni = ar * hi + ai * hr + x_ref[ri, :]
                x_ref[rr, :] = nr
                x_ref[ri, :] = ni
                hr, hi = nr, ni
            hr_ref[:, st] = hr
            hi_ref[:, st] = hi
        hcat = jnp.concatenate([x_ref[s * R:(s + 1) * R, :] for s in range(2 * SLABS)], axis=1)
        y = _bdot(hcat, c_ref[sb]) + d_ref[:, cols] * u_ref[:, cols]
        z_ref[:, cols] = jax.nn.gelu(y).astype(z_ref.dtype)


def s5_sample(u, h0r, h0i, a_re, a_im, b_blk, c_blk, d_row):
    rows = u.shape[0]
    nbatch = rows // DEC_SEQ
    nstate = N_SSM_GROUPS * SSM_STATE
    full = lambda shape: pl.BlockSpec(shape, lambda i: (0,) * len(shape))
    return pl.pallas_call(
        functools.partial(_s5_sample_kernel, nbatch=nbatch),
        grid=(1,),
        in_specs=[
            full((rows, D_MODEL)), full((nbatch, nstate)), full((nbatch, nstate)),
            full((1, nstate)), full((1, nstate)),
            full((N_SUPER, SUPER_IN, 2 * SUPER_STATE)),
            full((N_SUPER, 2 * SUPER_STATE, SUPER_IN)),
            full((1, D_MODEL)),
        ],
        out_specs=[full((rows, D_MODEL)), full((nbatch, nstate)), full((nbatch, nstate))],
        out_shape=[
            jax.ShapeDtypeStruct((rows, D_MODEL), BF16),
            jax.ShapeDtypeStruct((nbatch, nstate), F32),
            jax.ShapeDtypeStruct((nbatch, nstate), F32),
        ],
        scratch_shapes=[pltpu.VMEM((2 * SLABS * rows, LANES), F32)],
        compiler_params=_params("arbitrary"),
        name="s5_sample",
    )(u, h0r, h0i, a_re, a_im, b_blk, c_blk, d_row)


def _s5_weights(lam_re, lam_im, log_step, b_re, b_im, c_re, c_im):
    lam = lax.complex(lam_re.astype(F32), lam_im.astype(F32))
    dt = jnp.exp(log_step.astype(F32))[:, None]
    abar = jnp.exp(lam * dt)
    bbar = ((abar - 1.0) / lam)[..., None] * lax.complex(b_re.astype(F32), b_im.astype(F32))
    eye = jnp.eye(SSM_SUPER, dtype=F32)

    def b_layout(m):
        m = m.reshape(N_SUPER, SSM_SUPER, SSM_STATE, SSM_GROUP)
        return jnp.einsum('sgph,gk->sghkp', m, eye).reshape(N_SUPER, SUPER_IN, SUPER_STATE)

    def c_layout(m):
        m = m.reshape(N_SUPER, SSM_SUPER, SSM_GROUP, SSM_STATE)
        return jnp.einsum('sghp,gk->sgpkh', m, eye).reshape(N_SUPER, SUPER_STATE, SUPER_IN)

    b_blk = jnp.concatenate([b_layout(bbar.real), b_layout(bbar.imag)], axis=2).astype(BF16)
    c_blk = jnp.concatenate([c_layout(c_re.astype(F32)), -c_layout(c_im.astype(F32))],
                            axis=1).astype(BF16)
    return abar.real, abar.imag, b_blk, c_blk


def _ffn_prompt_kernel(x_ref, gpre_ref, wua_ref, wuv_ref, cwa_ref, cwv_ref, cba_ref, cbv_ref,
                       wd_ref, gpost_ref, o_ref, sa_ref, sv_ref,
                       hn_ref, ua_ref, uv_ref, act_ref, ha_ref, hv_ref,
                       *, nc, blocks_per_seq, rc, rm):
    i = pl.program_id(0)
    c = pl.program_id(1)
    tm, D = hn_ref.shape
    hdr = SUBLANES

    @pl.when(c == 0)
    def _():
        hn_ref[...] = _rms(x_ref[...], gpre_ref[...]).astype(BF16)
        o_ref[...] = jnp.zeros(o_ref.shape, F32)

    slot = i % 2

    @pl.when(i % blocks_per_seq == 0)
    def _():
        ha_ref[1 - slot, c] = jnp.zeros(ha_ref.shape[2:], F32)
        hv_ref[1 - slot, c] = jnp.zeros(hv_ref.shape[2:], F32)

    ua_ref[0:hdr, :] = ha_ref[1 - slot, c]
    uv_ref[0:hdr, :] = hv_ref[1 - slot, c]
    wa = wua_ref[...].astype(BF16)
    wv = wuv_ref[...].astype(BF16)
    wd = wd_ref[...].astype(BF16)

    def conv(buf_ref, w_ref, b_ref, r0):
        return (b_ref[...] + w_ref[0:1, :] * buf_ref[r0 + hdr - 2:r0 + hdr - 2 + rc, :]
                + w_ref[1:2, :] * buf_ref[r0 + hdr - 1:r0 + hdr - 1 + rc, :]
                + w_ref[2:3, :] * buf_ref[r0 + hdr:r0 + hdr + rc, :])

    def up_proj(k):
        hk = hn_ref[k * rm:(k + 1) * rm, :]
        ua_ref[hdr + k * rm:hdr + (k + 1) * rm, :] = jnp.dot(hk, wa, preferred_element_type=F32)
        uv_ref[hdr + k * rm:hdr + (k + 1) * rm, :] = jnp.dot(hk, wv, preferred_element_type=F32)

    ngroups = tm // rm
    up_proj(0)
    for k in range(ngroups):
        if k + 1 < ngroups:
            up_proj(k + 1)
        rows = slice(k * rm, (k + 1) * rm)
        for r0 in range(k * rm, (k + 1) * rm, rc):
            ca = conv(ua_ref, cwa_ref, cba_ref, r0)
            cv = conv(uv_ref, cwv_ref, cbv_ref, r0)
            act_ref[r0:r0 + rc, :] = (jax.nn.silu(ca) * cv).astype(BF16)
        o_ref[rows, :] += jnp.dot(act_ref[rows, :], wd, preferred_element_type=F32)

    ta = ua_ref[tm:tm + hdr, :]
    tv = uv_ref[tm:tm + hdr, :]
    ha_ref[slot, c] = ta
    hv_ref[slot, c] = tv
    sa_ref[c] = ta[hdr - 2:hdr, :]
    sv_ref[c] = tv[hdr - 2:hdr, :]

    @pl.when(c == nc - 1)
    def _():
        o_ref[...] = x_ref[...] + _rms(o_ref[...], gpost_ref[...])


FFN_ROW_CHUNK = 64
FFN_ROW_GROUP = 256
FFN_VMEM_LIMIT = V7X_VMEM_BYTES - 4 * 1024 * 1024


def ffn_prompt(x, layer, g_pre, w_up, conv_w, conv_b, w_down, g_post, batch, tm, tf):
    M, D = x.shape
    nc = D_FF // tf
    bps = SEQ // tm
    conv_b3 = conv_b.reshape(DEPTH, 1, 2 * D_FF)
    y, sa, sv = pl.pallas_call(
        functools.partial(_ffn_prompt_kernel, nc=nc, blocks_per_seq=bps, rc=FFN_ROW_CHUNK,
                          rm=FFN_ROW_GROUP),
        grid=(M // tm, nc),
        in_specs=[
            pl.BlockSpec((tm, D), lambda i, c: (i, 0)),
            pl.BlockSpec((None, 1, D), lambda i, c: (layer, 0, 0)),
            pl.BlockSpec((None, D, tf), lambda i, c: (layer, 0, c)),
            pl.BlockSpec((None, D, tf), lambda i, c: (layer, 0, nc + c)),
            pl.BlockSpec((None, 3, tf), lambda i, c: (layer, 0, c)),
            pl.BlockSpec((None, 3, tf), lambda i, c: (layer, 0, nc + c)),
            pl.BlockSpec((None, 1, tf), lambda i, c: (layer, 0, c)),
            pl.BlockSpec((None, 1, tf), lambda i, c: (layer, 0, nc + c)),
            pl.BlockSpec((None, tf, D), lambda i, c: (layer, c, 0)),
            pl.BlockSpec((None, 1, D), lambda i, c: (layer, 0, 0)),
        ],
        out_specs=[
            _once((tm, D), lambda i, c: (i, 0)),
            pl.BlockSpec((None, nc, 2, tf), lambda i, c: (i // bps, 0, 0, 0)),
            pl.BlockSpec((None, nc, 2, tf), lambda i, c: (i // bps, 0, 0, 0)),
        ],
        out_shape=[
            jax.ShapeDtypeStruct((M, D), F32),
            jax.ShapeDtypeStruct((batch, nc, 2, tf), F32),
            jax.ShapeDtypeStruct((batch, nc, 2, tf), F32),
        ],
        scratch_shapes=[
            pltpu.VMEM((tm, D), BF16),
            pltpu.VMEM((tm + SUBLANES, tf), F32),
            pltpu.VMEM((tm + SUBLANES, tf), F32),
            pltpu.VMEM((tm, tf), BF16),
            pltpu.VMEM((2, nc, SUBLANES, tf), F32),
            pltpu.VMEM((2, nc, SUBLANES, tf), F32),
        ],
        compiler_params=_params("arbitrary", "arbitrary", vmem=FFN_VMEM_LIMIT),
        name="ffn_prompt",
    )(x, g_pre.reshape(DEPTH, 1, D), w_up, w_up, conv_w, conv_w, conv_b3, conv_b3,
      w_down, g_post.reshape(DEPTH, 1, D))
    sa = sa.transpose(0, 2, 1, 3).reshape(batch, 2, D_FF)
    sv = sv.transpose(0, 2, 1, 3).reshape(batch, 2, D_FF)
    return y, jnp.concatenate([sa, sv], axis=-1)


def _ffn_sample_kernel(x_ref, gpre_ref, wua_ref, wuv_ref, cwa_ref, cwv_ref, cba_ref, cbv_ref,
                       wd_ref, gpost_ref, sta_ref, stv_ref, o_ref, na_ref, nv_ref, hn_ref, *, nc):
    c = pl.program_id(0)
    L = DEC_SEQ

    @pl.when(c == 0)
    def _():
        hn_ref[...] = _rms(x_ref[...], gpre_ref[...]).astype(BF16)

    hn = hn_ref[...]
    M = hn.shape[0]
    nb = M // L
    ua = jnp.dot(hn, wua_ref[...], preferred_element_type=F32)
    uv = jnp.dot(hn, wuv_ref[...], preferred_element_type=F32)
    tf = ua.shape[1]
    t = lax.broadcasted_iota(jnp.int32, (nb, L, tf), 1)

    def conv(u, st_ref, w_ref, b_ref):
        u3 = u.reshape(nb, L, tf)
        st = st_ref[...]
        prev2, prev1 = st[:, 0:1, :], st[:, 1:2, :]
        p1 = jnp.where(t == 0, prev1, pltpu.roll(u3, 1, 1))
        p2 = jnp.where(t == 0, prev2, jnp.where(t == 1, prev1, pltpu.roll(u3, 2, 1)))
        return (b_ref[...] + w_ref[0:1, :] * p2 + w_ref[1:2, :] * p1 + w_ref[2:3, :] * u3,
                u3[:, L - 2:L, :])

    ca, na = conv(ua, sta_ref, cwa_ref, cba_ref)
    cv, nv = conv(uv, stv_ref, cwv_ref, cbv_ref)
    na_ref[...] = na
    nv_ref[...] = nv
    act = (jax.nn.silu(ca) * cv).astype(BF16).reshape(M, tf)
    part = jnp.dot(act, wd_ref[...], preferred_element_type=F32)

    @pl.when(c == 0)
    def _():
        o_ref[...] = part

    @pl.when(c > 0)
    def _():
        o_ref[...] += part

    @pl.when(c == nc - 1)
    def _():
        o_ref[...] = x_ref[...] + _rms(o_ref[...], gpost_ref[...])


def ffn_sample(x, layer, g_pre, w_up, conv_w, conv_b, w_down, g_post, conv_state, tf):
    M, D = x.shape
    nbatch = M // DEC_SEQ
    nc = D_FF // tf
    conv_b3 = conv_b.reshape(DEPTH, 1, 2 * D_FF)
    y, na, nv = pl.pallas_call(
        functools.partial(_ffn_sample_kernel, nc=nc),
        grid=(nc,),
        in_specs=[
            pl.BlockSpec((M, D), lambda c: (0, 0)),
            pl.BlockSpec((None, 1, D), lambda c: (layer, 0, 0)),
            pl.BlockSpec((None, D, tf), lambda c: (layer, 0, c)),
            pl.BlockSpec((None, D, tf), lambda c: (layer, 0, nc + c)),
            pl.BlockSpec((None, 3, tf), lambda c: (layer, 0, c)),
            pl.BlockSpec((None, 3, tf), lambda c: (layer, 0, nc + c)),
            pl.BlockSpec((None, 1, tf), lambda c: (layer, 0, c)),
            pl.BlockSpec((None, 1, tf), lambda c: (layer, 0, nc + c)),
            pl.BlockSpec((None, tf, D), lambda c: (layer, c, 0)),
            pl.BlockSpec((None, 1, D), lambda c: (layer, 0, 0)),
            pl.BlockSpec((None, nbatch, 2, tf), lambda c: (layer, 0, 0, c)),
            pl.BlockSpec((None, nbatch, 2, tf), lambda c: (layer, 0, 0, nc + c)),
        ],
        out_specs=[
            pl.BlockSpec((M, D), lambda c: (0, 0)),
            pl.BlockSpec((nbatch, 2, tf), lambda c: (0, 0, c)),
            pl.BlockSpec((nbatch, 2, tf), lambda c: (0, 0, c)),
        ],
        out_shape=[
            jax.ShapeDtypeStruct((M, D), F32),
            jax.ShapeDtypeStruct((nbatch, 2, D_FF), F32),
            jax.ShapeDtypeStruct((nbatch, 2, D_FF), F32),
        ],
        scratch_shapes=[pltpu.VMEM((M, D), BF16)],
        compiler_params=_params("arbitrary"),
        name="ffn_sample",
    )(x, g_pre.reshape(DEPTH, 1, D), w_up, w_up, conv_w, conv_w, conv_b3, conv_b3,
      w_down, g_post.reshape(DEPTH, 1, D), conv_state, conv_state)
    return y, jnp.concatenate([na, nv], axis=-1)


def kernel(x_prompt, x_sample, mem_prompt, cache_swa_k, cache_swa_v, state_ssm_re, state_ssm_im, state_ffn_conv, cache_mem_k, cache_mem_v, g_mix_pre, g_mix_post, w_qkv, w_attn_o, attn_sinks, w_ssm_in, ssm_lambda_re, ssm_lambda_im, ssm_log_step, ssm_b_re, ssm_b_im, ssm_c_re, ssm_c_im, ssm_d, w_ssm_glu, g_x_pre, g_x_post, g_mem, w_x_q, w_mem_k, w_mem_v, w_x_o, g_ffn_pre, g_ffn_post, w_ffn_up, ffn_conv_w, ffn_conv_b, w_ffn_down):
    B = x_prompt.shape[0]
    SB = x_sample.shape[0]
    xw = N_XHEADS * XHEAD_DIM
    nkc = N_KV_HEADS * HEAD_DIM
    xp = x_prompt.reshape(B * SEQ, D_MODEL)
    xs = x_sample.reshape(SB * DEC_SEQ, D_MODEL)
    mem = mem_prompt.reshape(B * N_MEM, D_MODEL)
    MS = SB * DEC_SEQ
    TMP = 1024
    cos_p, sin_p = _rope_tables(jnp.arange(SEQ))
    cos_s, sin_s = _rope_tables(PAST_LEN + jnp.arange(DEC_SEQ))
    ck_all = cache_swa_k.reshape(cache_swa_k.shape[0], SB, WINDOW, nkc)
    cv_all = cache_swa_v.reshape(cache_swa_v.shape[0], SB, WINDOW, nkc)
    cmk_all = cache_mem_k.reshape(DEPTH, SB, N_MEM, xw)
    cmv_all = cache_mem_v.reshape(DEPTH, SB, N_MEM, xw)
    w_up_bf = w_ffn_up.astype(BF16)
    w_down_bf = w_ffn_down.astype(BF16)
    w_qkv_bf = w_qkv.astype(BF16)
    w_attn_o_bf = w_attn_o.astype(BF16)
    w_ssm_in_bf = w_ssm_in.astype(BF16)
    w_glu_bf = w_ssm_glu.astype(BF16)
    w_x_q_bf = w_x_q.astype(BF16)
    w_x_o_bf = w_x_o.astype(BF16)
    w_mem_kv_bf = jnp.concatenate([w_mem_k, w_mem_v], axis=-1).astype(BF16)
    TD = 512

    swa_kp, swa_vp, swa_ks, swa_vs = [], [], [], []
    ssm_rp, ssm_ip, ssm_rs, ssm_is = [], [], [], []
    conv_p, conv_s, memk_p, memv_p = [], [], [], []
    for i in range(DEPTH):
        j = i // 2
        if i % 2 == 0:
            qkv_p = norm_matmul(xp, g_mix_pre, i, w_qkv_bf, j, TD)
            qkv_s = norm_matmul(xs, g_mix_pre, i, w_qkv_bf, j, MS)
            op, kp, vp = swa_prompt(qkv_p, attn_sinks, j, cos_p, sin_p, B)
            sink_rows = jnp.repeat(attn_sinks[j].reshape(N_KV_HEADS, GQA_GROUP), DEC_SEQ,
                                   axis=1)[..., None]
            os_, kn, vn = swa_sample(qkv_s, ck_all, cv_all, j, sink_rows, cos_s, sin_s)
            swa_kp.append(kp.reshape(B, WINDOW, N_KV_HEADS, HEAD_DIM))
            swa_vp.append(vp.reshape(B, WINDOW, N_KV_HEADS, HEAD_DIM))
            swa_ks.append(kn.reshape(SB, WINDOW, N_KV_HEADS, HEAD_DIM))
            swa_vs.append(vn.reshape(SB, WINDOW, N_KV_HEADS, HEAD_DIM))
            xp = matmul_post(op, w_attn_o_bf, j, g_mix_post, i, xp, TD)
            xs = matmul_post(os_, w_attn_o_bf, j, g_mix_post, i, xs, MS)
        else:
            a_re, a_im, b_blk, c_blk = _s5_weights(
                ssm_lambda_re[j], ssm_lambda_im[j], ssm_log_step[j], ssm_b_re[j], ssm_b_im[j],
                ssm_c_re[j], ssm_c_im[j])
            d_row = ssm_d[j].reshape(1, D_MODEL)
            up_ = norm_matmul(xp, g_mix_pre, i, w_ssm_in_bf, j, TD)
            us_ = norm_matmul(xs, g_mix_pre, i, w_ssm_in_bf, j, MS)
            a_tiles = jnp.stack([a_re.reshape(N_SUPER, SUBLANES, LANES),
                                 a_im.reshape(N_SUPER, SUBLANES, LANES)], axis=1
                                ).reshape(2 * N_SUPER, SUBLANES, LANES)
            zero_state = jnp.zeros((B, N_SUPER, SUBLANES, LANES), F32)
            zp, rp, ip = s5_prompt(up_, zero_state, zero_state, a_tiles, b_blk, c_blk, d_row,
                                   B, 256)
            nstate = N_SSM_GROUPS * SSM_STATE
            zs, rn, im_ = s5_sample(us_, state_ssm_re[j].reshape(SB, nstate),
                                    state_ssm_im[j].reshape(SB, nstate),
                                    a_re.reshape(1, nstate), a_im.reshape(1, nstate),
                                    b_blk, c_blk, d_row)
            ssm_rp.append(rp.reshape(B, N_SSM_GROUPS, SSM_STATE))
            ssm_ip.append(ip.reshape(B, N_SSM_GROUPS, SSM_STATE))
            ssm_rs.append(rn.reshape(SB, N_SSM_GROUPS, SSM_STATE))
            ssm_is.append(im_.reshape(SB, N_SSM_GROUPS, SSM_STATE))
            xp = glu_post(zp, w_glu_bf, j, g_mix_post, i, xp, TD)
            xs = glu_post(zs, w_glu_bf, j, g_mix_post, i, xs, MS)
        mkv = norm_matmul(mem, g_mem, i, w_mem_kv_bf, i, B * N_MEM)
        mk, mv = mkv[:, :xw], mkv[:, xw:]
        memk_p.append(mk.reshape(B, N_MEM, N_XHEADS, XHEAD_DIM))
        memv_p.append(mv.reshape(B, N_MEM, N_XHEADS, XHEAD_DIM))
        xp = xattn_prompt(xp, i, g_x_pre, w_x_q_bf, mk, mv, w_x_o_bf, g_x_post, B, TD)
        qs = norm_matmul(xs, g_x_pre, i, w_x_q_bf, i, MS)
        as_ = xattn_sample(qs, cmk_all, cmv_all, i)
        xs = matmul_post(as_, w_x_o_bf, i, g_x_post, i, xs, MS)
        xp, cp = ffn_prompt(xp, i, g_ffn_pre, w_up_bf, ffn_conv_w, ffn_conv_b, w_down_bf,
                            g_ffn_post, B, TMP, 512)
        xs, cs = ffn_sample(xs, i, g_ffn_pre, w_up_bf, ffn_conv_w, ffn_conv_b, w_down_bf,
                            g_ffn_post, state_ffn_conv, 1408)
        conv_p.append(cp)
        conv_s.append(cs)
    return (xp.reshape(B, SEQ, D_MODEL), xs.reshape(SB, DEC_SEQ, D_MODEL),
            jnp.stack(swa_kp), jnp.stack(swa_vp), jnp.stack(swa_ks), jnp.stack(swa_vs),
            jnp.stack(ssm_rp), jnp.stack(ssm_ip), jnp.stack(ssm_rs), jnp.stack(ssm_is),
            jnp.stack(conv_p), jnp.stack(conv_s), jnp.stack(memk_p), jnp.stack(memv_p))
```

```python
import functools
import math

import jax
import jax.numpy as jnp
from jax import lax
from jax.experimental import pallas as pl
from jax.experimental.pallas import tpu as pltpu

D_MODEL = 2048
SEQ = 4096
DEPTH = 4
DEC_SEQ = 8
PAST_LEN = 16384
HEAD_DIM = 64
N_HEADS = 32
N_KV_HEADS = 4
GQA_GROUP = 8
WINDOW = 128
ROPE_THETA = 10000.0
SSM_GROUP = 16
N_SSM_GROUPS = 128
SSM_STATE = 64
N_MEM = 256
N_XHEADS = 4
XHEAD_DIM = 128
D_FF = 5632
NORM_EPS = 1e-6
NEG_INF = -1e30

F32 = jnp.float32
BF16 = jnp.bfloat16

V7X_VMEM_BYTES = 64 * 1024 * 1024
VMEM_LIMIT = V7X_VMEM_BYTES - 8 * 1024 * 1024
LANES = 128
SUBLANES = 8
SSM_SUPER = 16
N_SUPER = N_SSM_GROUPS // SSM_SUPER
SUPER_IN = SSM_SUPER * SSM_GROUP
SUPER_STATE = SSM_SUPER * SSM_STATE
SLABS = SUPER_STATE // LANES


def _params(*sem, vmem=VMEM_LIMIT):
    return pltpu.CompilerParams(dimension_semantics=sem, vmem_limit_bytes=vmem)


def _rms(x, g):
    ms = jnp.mean(x * x, axis=-1, keepdims=True)
    return x * lax.rsqrt(ms + NORM_EPS) * g


def _bdot(a, b):
    return jnp.dot(a.astype(BF16), b.astype(BF16), preferred_element_type=F32)


def _once(shape, index_map):
    return pl.BlockSpec(shape, index_map, pipeline_mode=pl.Buffered(1))


ROW_GROUP = 256


def _resident(shape, index_map):
    return pl.BlockSpec(shape, index_map, pipeline_mode=pl.Buffered(1))


def _row_groups(tm):
    rg = min(ROW_GROUP, tm)
    return [slice(k * rg, (k + 1) * rg) for k in range(tm // rg)]


def _norm_matmul_kernel(x_ref, g_ref, w_ref, o_ref):
    g = g_ref[...]
    for rows in _row_groups(x_ref.shape[0]):
        hn = _rms(x_ref[rows, :], g).astype(BF16)
        o_ref[rows, :] = jnp.dot(hn, w_ref[...], preferred_element_type=F32).astype(o_ref.dtype)


def norm_matmul(x, g, g_layer, w, w_layer, tm, out_dtype=F32):
    M, D = x.shape
    N = w.shape[-1]
    return pl.pallas_call(
        _norm_matmul_kernel,
        grid=(M // tm,),
        in_specs=[
            pl.BlockSpec((tm, D), lambda i: (i, 0)),
            pl.BlockSpec((None, 1, D), lambda i: (g_layer, 0, 0)),
            _resident((None, D, N), lambda i: (w_layer, 0, 0)),
        ],
        out_specs=pl.BlockSpec((tm, N), lambda i: (i, 0)),
        out_shape=jax.ShapeDtypeStruct((M, N), out_dtype),
        compiler_params=_params("arbitrary"),
        name="norm_matmul",
    )(x, g.reshape(g.shape[0], 1, D), w)


def _matmul_post_kernel(a_ref, w_ref, g_ref, x_ref, o_ref):
    g = g_ref[...]
    for rows in _row_groups(x_ref.shape[0]):
        y = jnp.dot(a_ref[rows, :].astype(BF16), w_ref[...], preferred_element_type=F32)
        o_ref[rows, :] = x_ref[rows, :] + _rms(y, g)


def matmul_post(a, w, w_layer, g, g_layer, x, tm):
    M, K = a.shape
    D = w.shape[-1]
    return pl.pallas_call(
        _matmul_post_kernel,
        grid=(M // tm,),
        in_specs=[
            pl.BlockSpec((tm, K), lambda i: (i, 0)),
            _resident((None, K, D), lambda i: (w_layer, 0, 0)),
            pl.BlockSpec((None, 1, D), lambda i: (g_layer, 0, 0)),
            pl.BlockSpec((tm, D), lambda i: (i, 0)),
        ],
        out_specs=pl.BlockSpec((tm, D), lambda i: (i, 0)),
        out_shape=jax.ShapeDtypeStruct((M, D), F32),
        compiler_params=_params("arbitrary"),
        name="matmul_post",
    )(a, w, g.reshape(g.shape[0], 1, D), x)


GLU_COLS = 512


def _glu_post_kernel(z_ref, w_ref, g_ref, x_ref, o_ref, y_ref):
    D = o_ref.shape[1]
    g = g_ref[...]
    for rows in _row_groups(x_ref.shape[0]):
        z = z_ref[rows, :]
        for n in range(D // GLU_COLS):
            cols = slice(n * GLU_COLS, (n + 1) * GLU_COLS)
            gcols = slice(D + n * GLU_COLS, D + (n + 1) * GLU_COLS)
            val = jnp.dot(z, w_ref[:, cols], preferred_element_type=F32)
            gate = jnp.dot(z, w_ref[:, gcols], preferred_element_type=F32)
            y_ref[rows, cols] = val * jax.nn.sigmoid(gate)
        o_ref[rows, :] = x_ref[rows, :] + _rms(y_ref[rows, :], g)


def glu_post(z, w, w_layer, g, g_layer, x, tm):
    M, K = z.shape
    D = D_MODEL
    return pl.pallas_call(
        _glu_post_kernel,
        grid=(M // tm,),
        in_specs=[
            pl.BlockSpec((tm, K), lambda i: (i, 0)),
            _resident((None, K, 2 * D), lambda i: (w_layer, 0, 0)),
            pl.BlockSpec((None, 1, D), lambda i: (g_layer, 0, 0)),
            pl.BlockSpec((tm, D), lambda i: (i, 0)),
        ],
        out_specs=pl.BlockSpec((tm, D), lambda i: (i, 0)),
        out_shape=jax.ShapeDtypeStruct((M, D), F32),
        scratch_shapes=[pltpu.VMEM((tm, D), F32)],
        compiler_params=_params("arbitrary"),
        name="glu_post",
    )(z, w, g.reshape(g.shape[0], 1, D), x)


def _rope_chunk(xc, cos, sin_signed):
    lane = lax.broadcasted_iota(jnp.int32, xc.shape, 1)
    first_half = (lane & (HEAD_DIM - 1)) < (HEAD_DIM // 2)
    partner = jnp.where(first_half, pltpu.roll(xc, LANES - HEAD_DIM // 2, 1),
                        pltpu.roll(xc, HEAD_DIM // 2, 1))
    return xc * cos + partner * sin_signed


def _rope_tables(pos):
    half = HEAD_DIM // 2
    inv_freq = ROPE_THETA ** (-jnp.arange(half, dtype=F32) * 2.0 / HEAD_DIM)
    ang = pos.astype(F32)[:, None] * inv_freq[None, :]
    cos, sin = jnp.cos(ang), jnp.sin(ang)
    cos128 = jnp.tile(cos, (1, LANES // half))
    sin128 = jnp.tile(jnp.concatenate([-sin, sin], axis=1), (1, LANES // HEAD_DIM))
    return cos128, sin128


def _swa_prompt_kernel(sink_ref, q_ref, kv_ref, cos_ref, sin_ref,
                       o_ref, kc_ref, vc_ref, kk_ref, vv_ref, *, layer):
    i = pl.program_id(1)
    blk = WINDOW
    cos, sn = cos_ref[...], sin_ref[...]
    kv = kv_ref[...]
    nkc = N_KV_HEADS * HEAD_DIM
    k = jnp.concatenate([_rope_chunk(kv[:, c * LANES:(c + 1) * LANES], cos, sn)
                         for c in range(nkc // LANES)], axis=1)
    v = kv[:, nkc:2 * nkc]
    kc_ref[...] = k
    vc_ref[...] = v

    cur = i % 2
    prv = 1 - cur

    @pl.when(i == 0)
    def _():
        kk_ref[1] = jnp.zeros((blk, nkc), BF16)
        vv_ref[1] = jnp.zeros((blk, nkc), BF16)

    kb, vb = k.astype(BF16), v.astype(BF16)
    kk_ref[cur] = kb
    vv_ref[cur] = vb
    kk = jnp.concatenate([kk_ref[prv], kb], axis=0)
    vv = jnp.concatenate([vv_ref[prv], vb], axis=0)
    rows = GQA_GROUP * blk
    qi = lax.broadcasted_iota(jnp.int32, (rows, 2 * blk), 0) & (blk - 1)
    si = lax.broadcasted_iota(jnp.int32, (rows, 2 * blk), 1)
    d = blk + qi - si
    mask = (d >= 0) & (d < WINDOW) & ((si >= blk) | (i > 0))

    chunks_per_group = GQA_GROUP * HEAD_DIM // LANES

    def scores(kh):
        hs = slice(kh * HEAD_DIM, (kh + 1) * HEAD_DIM)
        pieces = []
        for c in range(kh * chunks_per_group, (kh + 1) * chunks_per_group):
            qc = (_rope_chunk(q_ref[:, c * LANES:(c + 1) * LANES], cos, sn)
                  * (HEAD_DIM ** -0.5)).astype(BF16)
            pieces += [qc[:, hh * HEAD_DIM:(hh + 1) * HEAD_DIM] for hh in range(LANES // HEAD_DIM)]
        qs = jnp.concatenate(pieces, axis=0)
        return lax.dot_general(qs, kk[:, hs], (((1,), (1,)), ((), ())),
                               preferred_element_type=F32)

    ahead = 2
    s_all = [scores(kh) for kh in range(min(ahead, N_KV_HEADS))]
    outs = []
    for kh in range(N_KV_HEADS):
        if kh + ahead < N_KV_HEADS:
            s_all.append(scores(kh + ahead))
        hs = slice(kh * HEAD_DIM, (kh + 1) * HEAD_DIM)
        sk = jnp.concatenate([jnp.full((blk, 1), sink_ref[layer, kh * GQA_GROUP + j], F32)
                              for j in range(GQA_GROUP)], axis=0)
        s = jnp.where(mask, s_all[kh], NEG_INF)
        mx = jnp.maximum(jnp.max(s, axis=1, keepdims=True), sk)
        p = jnp.exp(s - mx)
        den = jnp.sum(p, axis=1, keepdims=True) + jnp.exp(sk - mx)
        o = jnp.dot(p.astype(BF16), vv[:, hs], preferred_element_type=F32) / den
        outs += [o[j * blk:(j + 1) * blk, :] for j in range(GQA_GROUP)]
    o_ref[...] = jnp.concatenate(outs, axis=1).astype(o_ref.dtype)


def swa_prompt(qkv, sinks, layer, cos, sin, batch):
    blk = WINDOW
    nb = SEQ // blk
    nq = N_HEADS * HEAD_DIM
    nkv = 2 * N_KV_HEADS * HEAD_DIM
    o, kc, vc = pl.pallas_call(
        functools.partial(_swa_prompt_kernel, layer=layer),
        grid=(batch, nb),
        in_specs=[
            pl.BlockSpec(memory_space=pltpu.SMEM),
            pl.BlockSpec((blk, nq), lambda b, i: (b * nb + i, 0)),
            pl.BlockSpec((blk, nkv), lambda b, i: (b * nb + i, nq // nkv)),
            pl.BlockSpec((blk, LANES), lambda b, i: (i, 0)),
            pl.BlockSpec((blk, LANES), lambda b, i: (i, 0)),
        ],
        out_specs=[
            pl.BlockSpec((blk, nq), lambda b, i: (b * nb + i, 0)),
            pl.BlockSpec((None, blk, nkv // 2), lambda b, i: (b, 0, 0)),
            pl.BlockSpec((None, blk, nkv // 2), lambda b, i: (b, 0, 0)),
        ],
        out_shape=[
            jax.ShapeDtypeStruct((batch * SEQ, nq), BF16),
            jax.ShapeDtypeStruct((batch, blk, nkv // 2), F32),
            jax.ShapeDtypeStruct((batch, blk, nkv // 2), F32),
        ],
        scratch_shapes=[pltpu.VMEM((2, blk, nkv // 2), BF16),
                        pltpu.VMEM((2, blk, nkv // 2), BF16)],
        compiler_params=_params("arbitrary", "arbitrary"),
        name="swa_prompt",
    )(sinks, qkv, qkv, cos, sin)
    return o, kc, vc


def _swa_sample_kernel(sink_ref, qkv_ref, ck_ref, cv_ref, cos_ref, sin_ref,
                       o_ref, nk_ref, nv_ref, *, nb):
    L = DEC_SEQ
    wb = WINDOW
    nq = N_HEADS * HEAD_DIM
    nkc = N_KV_HEADS * HEAD_DIM
    cos, sn = cos_ref[...], sin_ref[...]
    rows = GQA_GROUP * L
    t_q = lax.broadcasted_iota(jnp.int32, (rows, wb), 0) % L
    c_k = lax.broadcasted_iota(jnp.int32, (rows, wb), 1)
    mask_c = c_k >= t_q + 1
    t_q2 = lax.broadcasted_iota(jnp.int32, (rows, L), 0) % L
    t_k2 = lax.broadcasted_iota(jnp.int32, (rows, L), 1)
    mask_n = t_k2 <= t_q2
    for b in range(nb):
        x = qkv_ref[b * L:(b + 1) * L, :]
        qr = jnp.concatenate([_rope_chunk(x[:, c * LANES:(c + 1) * LANES], cos, sn)
                              for c in range(nq // LANES)], axis=1) * (HEAD_DIM ** -0.5)
        kn = jnp.concatenate([_rope_chunk(x[:, nq + c * LANES:nq + (c + 1) * LANES], cos, sn)
                              for c in range(nkc // LANES)], axis=1)
        vn = x[:, nq + nkc:nq + 2 * nkc]
        ck = ck_ref[b]
        cv = cv_ref[b]
        nk_ref[b, 0:wb - L, :] = ck[L:wb, :]
        nk_ref[b, wb - L:wb, :] = kn
        nv_ref[b, 0:wb - L, :] = cv[L:wb, :]
        nv_ref[b, wb - L:wb, :] = vn
        ckb, cvb, knb, vnb = ck.astype(BF16), cv.astype(BF16), kn.astype(BF16), vn.astype(BF16)
        dn = (((1,), (1,)), ((), ()))
        scores = []
        for kh in range(N_KV_HEADS):
            hs = slice(kh * HEAD_DIM, (kh + 1) * HEAD_DIM)
            qs = jnp.concatenate(
                [qr[:, (kh * GQA_GROUP + j) * HEAD_DIM:(kh * GQA_GROUP + j + 1) * HEAD_DIM]
                 for j in range(GQA_GROUP)], axis=0).astype(BF16)
            scores.append((lax.dot_general(qs, ckb[:, hs], dn, preferred_element_type=F32),
                           lax.dot_general(qs, knb[:, hs], dn, preferred_element_type=F32)))
        outs = []
        for kh in range(N_KV_HEADS):
            hs = slice(kh * HEAD_DIM, (kh + 1) * HEAD_DIM)
            s_c = jnp.where(mask_c, scores[kh][0], NEG_INF)
            s_n = jnp.where(mask_n, scores[kh][1], NEG_INF)
            sk = sink_ref[kh]
            mx = jnp.maximum(jnp.maximum(jnp.max(s_c, axis=1, keepdims=True),
                                         jnp.max(s_n, axis=1, keepdims=True)), sk)
            p_c = jnp.exp(s_c - mx)
            p_n = jnp.exp(s_n - mx)
            den = (jnp.sum(p_c, axis=1, keepdims=True) + jnp.sum(p_n, axis=1, keepdims=True)
                   + jnp.exp(sk - mx))
            o = (jnp.dot(p_c.astype(BF16), cvb[:, hs], preferred_element_type=F32)
                 + jnp.dot(p_n.astype(BF16), vnb[:, hs], preferred_element_type=F32)) / den
            outs.extend(o[j * L:(j + 1) * L, :] for j in range(GQA_GROUP))
        o_ref[b * L:(b + 1) * L, :] = jnp.concatenate(outs, axis=1).astype(o_ref.dtype)


def swa_sample(qkv, cache_k, cache_v, layer, sink_rows, cos, sin, nb=8):
    nbatch = cache_k.shape[1]
    nq = N_HEADS * HEAD_DIM
    nkc = N_KV_HEADS * HEAD_DIM
    L = DEC_SEQ
    return pl.pallas_call(
        functools.partial(_swa_sample_kernel, nb=nb),
        grid=(nbatch // nb,),
        in_specs=[
            pl.BlockSpec((N_KV_HEADS, GQA_GROUP * L, 1), lambda g: (0, 0, 0)),
            pl.BlockSpec((nb * L, nq + 2 * nkc), lambda g: (g, 0)),
            pl.BlockSpec((None, nb, WINDOW, nkc), lambda g: (layer, g, 0, 0)),
            pl.BlockSpec((None, nb, WINDOW, nkc), lambda g: (layer, g, 0, 0)),
            pl.BlockSpec((L, LANES), lambda g: (0, 0)),
            pl.BlockSpec((L, LANES), lambda g: (0, 0)),
        ],
        out_specs=[
            pl.BlockSpec((nb * L, nq), lambda g: (g, 0)),
            pl.BlockSpec((nb, WINDOW, nkc), lambda g: (g, 0, 0)),
            pl.BlockSpec((nb, WINDOW, nkc), lambda g: (g, 0, 0)),
        ],
        out_shape=[
            jax.ShapeDtypeStruct((nbatch * L, nq), BF16),
            jax.ShapeDtypeStruct((nbatch, WINDOW, nkc), F32),
            jax.ShapeDtypeStruct((nbatch, WINDOW, nkc), F32),
        ],
        compiler_params=_params("arbitrary"),
        name="swa_sample",
    )(sink_rows, qkv, cache_k, cache_v, cos, sin)


def _xattn_heads(q, mk, mv):
    heads = [slice(h * XHEAD_DIM, (h + 1) * XHEAD_DIM) for h in range(N_XHEADS)]
    scores = [lax.dot_general((q[:, hs] * (XHEAD_DIM ** -0.5)).astype(BF16), mk[:, hs].astype(BF16),
                              (((1,), (1,)), ((), ())), preferred_element_type=F32)
              for hs in heads]
    outs = []
    for hs, s in zip(heads, scores):
        mx = jnp.max(s, axis=1, keepdims=True)
        p = jnp.exp(s - mx)
        den = jnp.sum(p, axis=1, keepdims=True)
        outs.append(jnp.dot(p.astype(BF16), mv[:, hs].astype(BF16),
                            preferred_element_type=F32) / den)
    return jnp.concatenate(outs, axis=1)


def _xattn_prompt_kernel(x_ref, gpre_ref, wq_ref, mk_ref, mv_ref, wo_ref, gpost_ref, o_ref):
    mk = mk_ref[...].astype(BF16)
    mv = mv_ref[...].astype(BF16)
    gpre, gpost = gpre_ref[...], gpost_ref[...]
    for rows in _row_groups(x_ref.shape[0]):
        x = x_ref[rows, :]
        q = jnp.dot(_rms(x, gpre).astype(BF16), wq_ref[...], preferred_element_type=F32)
        a = _xattn_heads(q, mk, mv).astype(BF16)
        y = jnp.dot(a, wo_ref[...], preferred_element_type=F32)
        o_ref[rows, :] = x + _rms(y, gpost)


def xattn_prompt(x, layer, g_pre, w_q, mk, mv, w_o, g_post, batch, tm):
    nq = SEQ // tm
    xw = N_XHEADS * XHEAD_DIM
    D = D_MODEL
    return pl.pallas_call(
        _xattn_prompt_kernel,
        grid=(batch, nq),
        in_specs=[
            pl.BlockSpec((tm, D), lambda b, i: (b * nq + i, 0)),
            pl.BlockSpec((None, 1, D), lambda b, i: (layer, 0, 0)),
            _resident((None, D, xw), lambda b, i: (layer, 0, 0)),
            pl.BlockSpec((N_MEM, xw), lambda b, i: (b, 0)),
            pl.BlockSpec((N_MEM, xw), lambda b, i: (b, 0)),
            _resident((None, xw, D), lambda b, i: (layer, 0, 0)),
            pl.BlockSpec((None, 1, D), lambda b, i: (layer, 0, 0)),
        ],
        out_specs=pl.BlockSpec((tm, D), lambda b, i: (b * nq + i, 0)),
        out_shape=jax.ShapeDtypeStruct((batch * SEQ, D), F32),
        compiler_params=_params("arbitrary", "arbitrary"),
        name="xattn_prompt",
    )(x, g_pre.reshape(DEPTH, 1, D), w_q, mk, mv, w_o, g_post.reshape(DEPTH, 1, D))


def _xattn_sample_kernel(q_ref, mk_ref, mv_ref, o_ref, *, nb):
    L = DEC_SEQ
    for b in range(nb):
        o_ref[b * L:(b + 1) * L, :] = _xattn_heads(
            q_ref[b * L:(b + 1) * L, :], mk_ref[b], mv_ref[b]).astype(o_ref.dtype)


def xattn_sample(q, cache_k, cache_v, layer, nb=8):
    nbatch = cache_k.shape[1]
    L = DEC_SEQ
    xw = N_XHEADS * XHEAD_DIM
    return pl.pallas_call(
        functools.partial(_xattn_sample_kernel, nb=nb),
        grid=(nbatch // nb,),
        in_specs=[
            pl.BlockSpec((nb * L, xw), lambda g: (g, 0)),
            pl.BlockSpec((None, nb, N_MEM, xw), lambda g: (layer, g, 0, 0)),
            pl.BlockSpec((None, nb, N_MEM, xw), lambda g: (layer, g, 0, 0)),
        ],
        out_specs=pl.BlockSpec((nb * L, xw), lambda g: (g, 0)),
        out_shape=jax.ShapeDtypeStruct((nbatch * L, xw), BF16),
        compiler_params=_params("arbitrary"),
        name="xattn_sample",
    )(q, cache_k, cache_v)


def _s5_prompt_kernel(u_ref, h0r_ref, h0i_ref, a_ref, b_ref, c_ref, d_ref,
                      z_ref, hr_ref, hi_ref, x_ref, hst_ref, *, T):
    c = pl.program_id(1)
    nslab = 2 * SLABS
    G = T // SUBLANES

    @pl.when(c == 0)
    def _():
        for sb in range(N_SUPER):
            hst_ref[2 * sb] = h0r_ref[sb]
            hst_ref[2 * sb + 1] = h0i_ref[sb]

    def slab_rows(sb, s):
        return pl.ds((sb * nslab + s) * SUBLANES, SUBLANES)

    for sb in range(N_SUPER):
        bu = _bdot(u_ref[:, sb * SUPER_IN:(sb + 1) * SUPER_IN], b_ref[sb])
        for s in range(nslab):
            x_ref[:, slab_rows(sb, s), :] = bu[:, s * LANES:(s + 1) * LANES].reshape(
                G, SUBLANES, LANES)

    a = [a_ref[k] for k in range(2 * N_SUPER)]
    h_init = tuple(hst_ref[k] for k in range(2 * N_SUPER))

    def step(t, h):
        g = lax.shift_right_logical(t, 3)
        r = lax.bitwise_and(t, SUBLANES - 1)
        new = []
        for sb in range(N_SUPER):
            ir = pl.ds((sb * nslab) * SUBLANES + r, SUBLANES, stride=SUBLANES)
            ii = pl.ds((sb * nslab + SLABS) * SUBLANES + r, SUBLANES, stride=SUBLANES)
            ar, ai = a[2 * sb], a[2 * sb + 1]
            hr, hi = h[2 * sb], h[2 * sb + 1]
            nr = ar * hr - ai * hi + x_ref[g, ir, :]
            ni = ar * hi + ai * hr + x_ref[g, ii, :]
            x_ref[g, ir, :] = nr
            x_ref[g, ii, :] = ni
            new += [nr, ni]
        return tuple(new)

    h_fin = lax.fori_loop(0, T, step, h_init)
    for sb in range(N_SUPER):
        hst_ref[2 * sb] = h_fin[2 * sb]
        hst_ref[2 * sb + 1] = h_fin[2 * sb + 1]
        hr_ref[sb] = h_fin[2 * sb]
        hi_ref[sb] = h_fin[2 * sb + 1]

    for sb in range(N_SUPER):
        hcat = jnp.concatenate([x_ref[:, slab_rows(sb, s), :].reshape(T, LANES)
                                for s in range(nslab)], axis=1)
        cols = slice(sb * SUPER_IN, (sb + 1) * SUPER_IN)
        y = _bdot(hcat, c_ref[sb]) + d_ref[:, cols] * u_ref[:, cols]
        z_ref[:, cols] = jax.nn.gelu(y).astype(z_ref.dtype)


def s5_prompt(u, h0r, h0i, a_tiles, b_blk, c_blk, d_row, batch, T):
    nc = SEQ // T
    st = (None, N_SUPER, SUBLANES, LANES)
    return pl.pallas_call(
        functools.partial(_s5_prompt_kernel, T=T),
        grid=(batch, nc),
        in_specs=[
            pl.BlockSpec((T, D_MODEL), lambda b, c: (b * nc + c, 0)),
            pl.BlockSpec(st, lambda b, c: (b, 0, 0, 0)),
            pl.BlockSpec(st, lambda b, c: (b, 0, 0, 0)),
            _once((2 * N_SUPER, SUBLANES, LANES), lambda b, c: (0, 0, 0)),
            _once((N_SUPER, SUPER_IN, 2 * SUPER_STATE), lambda b, c: (0, 0, 0)),
            _once((N_SUPER, 2 * SUPER_STATE, SUPER_IN), lambda b, c: (0, 0, 0)),
            _once((1, D_MODEL), lambda b, c: (0, 0)),
        ],
        out_specs=[
            pl.BlockSpec((T, D_MODEL), lambda b, c: (b * nc + c, 0)),
            pl.BlockSpec(st, lambda b, c: (b, 0, 0, 0)),
            pl.BlockSpec(st, lambda b, c: (b, 0, 0, 0)),
        ],
        out_shape=[
            jax.ShapeDtypeStruct((batch * SEQ, D_MODEL), BF16),
            jax.ShapeDtypeStruct((batch, N_SUPER, SUBLANES, LANES), F32),
            jax.ShapeDtypeStruct((batch, N_SUPER, SUBLANES, LANES), F32),
        ],
        scratch_shapes=[
            pltpu.VMEM((T // SUBLANES, N_SUPER * 2 * SLABS * SUBLANES, LANES), F32),
            pltpu.VMEM((2 * N_SUPER, SUBLANES, LANES), F32),
        ],
        compiler_params=_params("arbitrary", "arbitrary"),
        name="s5_prompt",
    )(u, h0r, h0i, a_tiles, b_blk, c_blk, d_row)


def _s5_sample_kernel(u_ref, h0r_ref, h0i_ref, ar_ref, ai_ref, b_ref, c_ref, d_ref,
                      z_ref, hr_ref, hi_ref, x_ref, *, nbatch):
    L = DEC_SEQ
    R = nbatch * L
    for sb in range(N_SUPER):
        cols = slice(sb * SUPER_IN, (sb + 1) * SUPER_IN)
        bu = _bdot(u_ref[:, cols], b_ref[sb])
        for s in range(2 * SLABS):
            x_ref[s * R:(s + 1) * R, :] = bu[:, s * LANES:(s + 1) * LANES]
        for s in range(SLABS):
            st = slice(sb * SUPER_STATE + s * LANES, sb * SUPER_STATE + (s + 1) * LANES)
            ar, ai = ar_ref[:, st], ai_ref[:, st]
            hr, hi = h0r_ref[:, st], h0i_ref[:, st]
            for t in range(L):
                rr = pl.ds(s * R + t, nbatch, stride=L)
                ri = pl.ds((SLABS + s) * R + t, nbatch, stride=L)
                nr = ar * hr - ai * hi + x_ref[rr, :]
                ni = ar * hi + ai * hr + x_ref[ri, :]
                x_ref[rr, :] = nr
                x_ref[ri, :] = ni
                hr, hi = nr, ni
            hr_ref[:, st] = hr
            hi_ref[:, st] = hi
        hcat = jnp.concatenate([x_ref[s * R:(s + 1) * R, :] for s in range(2 * SLABS)], axis=1)
        y = _bdot(hcat, c_ref[sb]) + d_ref[:, cols] * u_ref[:, cols]
        z_ref[:, cols] = jax.nn.gelu(y).astype(z_ref.dtype)


def s5_sample(u, h0r, h0i, a_re, a_im, b_blk, c_blk, d_row):
    rows = u.shape[0]
    nbatch = rows // DEC_SEQ
    nstate = N_SSM_GROUPS * SSM_STATE
    full = lambda shape: pl.BlockSpec(shape, lambda i: (0,) * len(shape))
    return pl.pallas_call(
        functools.partial(_s5_sample_kernel, nbatch=nbatch),
        grid=(1,),
        in_specs=[
            full((rows, D_MODEL)), full((nbatch, nstate)), full((nbatch, nstate)),
            full((1, nstate)), full((1, nstate)),
            full((N_SUPER, SUPER_IN, 2 * SUPER_STATE)),
            full((N_SUPER, 2 * SUPER_STATE, SUPER_IN)),
            full((1, D_MODEL)),
        ],
        out_specs=[full((rows, D_MODEL)), full((nbatch, nstate)), full((nbatch, nstate))],
        out_shape=[
            jax.ShapeDtypeStruct((rows, D_MODEL), BF16),
            jax.ShapeDtypeStruct((nbatch, nstate), F32),
            jax.ShapeDtypeStruct((nbatch, nstate), F32),
        ],
        scratch_shapes=[pltpu.VMEM((2 * SLABS * rows, LANES), F32)],
        compiler_params=_params("arbitrary"),
        name="s5_sample",
    )(u, h0r, h0i, a_re, a_im, b_blk, c_blk, d_row)


def _s5_weights(lam_re, lam_im, log_step, b_re, b_im, c_re, c_im):
    lam = lax.complex(lam_re.astype(F32), lam_im.astype(F32))
    dt = jnp.exp(log_step.astype(F32))[:, None]
    abar = jnp.exp(lam * dt)
    bbar = ((abar - 1.0) / lam)[..., None] * lax.complex(b_re.astype(F32), b_im.astype(F32))
    eye = jnp.eye(SSM_SUPER, dtype=F32)

    def b_layout(m):
        m = m.reshape(N_SUPER, SSM_SUPER, SSM_STATE, SSM_GROUP)
        return jnp.einsum('sgph,gk->sghkp', m, eye).reshape(N_SUPER, SUPER_IN, SUPER_STATE)

    def c_layout(m):
        m = m.reshape(N_SUPER, SSM_SUPER, SSM_GROUP, SSM_STATE)
        return jnp.einsum('sghp,gk->sgpkh', m, eye).reshape(N_SUPER, SUPER_STATE, SUPER_IN)

    b_blk = jnp.concatenate([b_layout(bbar.real), b_layout(bbar.imag)], axis=2).astype(BF16)
    c_blk = jnp.concatenate([c_layout(c_re.astype(F32)), -c_layout(c_im.astype(F32))],
                            axis=1).astype(BF16)
    return abar.real, abar.imag, b_blk, c_blk


def _ffn_prompt_kernel(x_ref, gpre_ref, wua_ref, wuv_ref, cwa_ref, cwv_ref, cba_ref, cbv_ref,
                       wd_ref, gpost_ref, o_ref, sa_ref, sv_ref,
                       hn_ref, ua_ref, uv_ref, act_ref, ha_ref, hv_ref,
                       *, nc, blocks_per_seq, rc, rm):
    i = pl.program_id(0)
    c = pl.program_id(1)
    tm, D = hn_ref.shape
    hdr = SUBLANES

    @pl.when(c == 0)
    def _():
        hn_ref[...] = _rms(x_ref[...], gpre_ref[...]).astype(BF16)
        o_ref[...] = jnp.zeros(o_ref.shape, F32)

    slot = i % 2

    @pl.when(i % blocks_per_seq == 0)
    def _():
        ha_ref[1 - slot, c] = jnp.zeros(ha_ref.shape[2:], F32)
        hv_ref[1 - slot, c] = jnp.zeros(hv_ref.shape[2:], F32)

    ua_ref[0:hdr, :] = ha_ref[1 - slot, c]
    uv_ref[0:hdr, :] = hv_ref[1 - slot, c]
    wa = wua_ref[...].astype(BF16)
    wv = wuv_ref[...].astype(BF16)
    wd = wd_ref[...].astype(BF16)

    def conv(buf_ref, w_ref, b_ref, r0):
        return (b_ref[...] + w_ref[0:1, :] * buf_ref[r0 + hdr - 2:r0 + hdr - 2 + rc, :]
                + w_ref[1:2, :] * buf_ref[r0 + hdr - 1:r0 + hdr - 1 + rc, :]
                + w_ref[2:3, :] * buf_ref[r0 + hdr:r0 + hdr + rc, :])

    def up_proj(k):
        hk = hn_ref[k * rm:(k + 1) * rm, :]
        ua_ref[hdr + k * rm:hdr + (k + 1) * rm, :] = jnp.dot(hk, wa, preferred_element_type=F32)
        uv_ref[hdr + k * rm:hdr + (k + 1) * rm, :] = jnp.dot(hk, wv, preferred_element_type=F32)

    ngroups = tm // rm
    up_proj(0)
    for k in range(ngroups):
        if k + 1 < ngroups:
            up_proj(k + 1)
        rows = slice(k * rm, (k + 1) * rm)
        for r0 in range(k * rm, (k + 1) * rm, rc):
            ca = conv(ua_ref, cwa_ref, cba_ref, r0)
            cv = conv(uv_ref, cwv_ref, cbv_ref, r0)
            act_ref[r0:r0 + rc, :] = (jax.nn.silu(ca) * cv).astype(BF16)
        o_ref[rows, :] += jnp.dot(act_ref[rows, :], wd, preferred_element_type=F32)

    ta = ua_ref[tm:tm + hdr, :]
    tv = uv_ref[tm:tm + hdr, :]
    ha_ref[slot, c] = ta
    hv_ref[slot, c] = tv
    sa_ref[c] = ta[hdr - 2:hdr, :]
    sv_ref[c] = tv[hdr - 2:hdr, :]

    @pl.when(c == nc - 1)
    def _():
        o_ref[...] = x_ref[...] + _rms(o_ref[...], gpost_ref[...])


FFN_ROW_CHUNK = 64
FFN_ROW_GROUP = 256
FFN_VMEM_LIMIT = V7X_VMEM_BYTES - 4 * 1024 * 1024


def ffn_prompt(x, layer, g_pre, w_up, conv_w, conv_b, w_down, g_post, batch, tm, tf):
    M, D = x.shape
    nc = D_FF // tf
    bps = SEQ // tm
    conv_b3 = conv_b.reshape(DEPTH, 1, 2 * D_FF)
    y, sa, sv = pl.pallas_call(
        functools.partial(_ffn_prompt_kernel, nc=nc, blocks_per_seq=bps, rc=FFN_ROW_CHUNK,
                          rm=FFN_ROW_GROUP),
        grid=(M // tm, nc),
        in_specs=[
            pl.BlockSpec((tm, D), lambda i, c: (i, 0)),
            pl.BlockSpec((None, 1, D), lambda i, c: (layer, 0, 0)),
            pl.BlockSpec((None, D, tf), lambda i, c: (layer, 0, c)),
            pl.BlockSpec((None, D, tf), lambda i, c: (layer, 0, nc + c)),
            pl.BlockSpec((None, 3, tf), lambda i, c: (layer, 0, c)),
            pl.BlockSpec((None, 3, tf), lambda i, c: (layer, 0, nc + c)),
            pl.BlockSpec((None, 1, tf), lambda i, c: (layer, 0, c)),
            pl.BlockSpec((None, 1, tf), lambda i, c: (layer, 0, nc + c)),
            pl.BlockSpec((None, tf, D), lambda i, c: (layer, c, 0)),
            pl.BlockSpec((None, 1, D), lambda i, c: (layer, 0, 0)),
        ],
        out_specs=[
            _once((tm, D), lambda i, c: (i, 0)),
            pl.BlockSpec((None, nc, 2, tf), lambda i, c: (i // bps, 0, 0, 0)),
            pl.BlockSpec((None, nc, 2, tf), lambda i, c: (i // bps, 0, 0, 0)),
        ],
        out_shape=[
            jax.ShapeDtypeStruct((M, D), F32),
            jax.ShapeDtypeStruct((batch, nc, 2, tf), F32),
            jax.ShapeDtypeStruct((batch, nc, 2, tf), F32),
        ],
        scratch_shapes=[
            pltpu.VMEM((tm, D), BF16),
            pltpu.VMEM((tm + SUBLANES, tf), F32),
            pltpu.VMEM((tm + SUBLANES, tf), F32),
            pltpu.VMEM((tm, tf), BF16),
            pltpu.VMEM((2, nc, SUBLANES, tf), F32),
            pltpu.VMEM((2, nc, SUBLANES, tf), F32),
        ],
        compiler_params=_params("arbitrary", "arbitrary", vmem=FFN_VMEM_LIMIT),
        name="ffn_prompt",
    )(x, g_pre.reshape(DEPTH, 1, D), w_up, w_up, conv_w, conv_w, conv_b3, conv_b3,
      w_down, g_post.reshape(DEPTH, 1, D))
    sa = sa.transpose(0, 2, 1, 3).reshape(batch, 2, D_FF)
    sv = sv.transpose(0, 2, 1, 3).reshape(batch, 2, D_FF)
    return y, jnp.concatenate([sa, sv], axis=-1)


def _ffn_sample_kernel(x_ref, gpre_ref, wua_ref, wuv_ref, cwa_ref, cwv_ref, cba_ref, cbv_ref,
                       wd_ref, gpost_ref, sta_ref, stv_ref, o_ref, na_ref, nv_ref, hn_ref, *, nc):
    c = pl.program_id(0)
    L = DEC_SEQ

    @pl.when(c == 0)
    def _():
        hn_ref[...] = _rms(x_ref[...], gpre_ref[...]).astype(BF16)

    hn = hn_ref[...]
    M = hn.shape[0]
    nb = M // L
    ua = jnp.dot(hn, wua_ref[...], preferred_element_type=F32)
    uv = jnp.dot(hn, wuv_ref[...], preferred_element_type=F32)
    tf = ua.shape[1]
    t = lax.broadcasted_iota(jnp.int32, (nb, L, tf), 1)

    def conv(u, st_ref, w_ref, b_ref):
        u3 = u.reshape(nb, L, tf)
        st = st_ref[...]
        prev2, prev1 = st[:, 0:1, :], st[:, 1:2, :]
        p1 = jnp.where(t == 0, prev1, pltpu.roll(u3, 1, 1))
        p2 = jnp.where(t == 0, prev2, jnp.where(t == 1, prev1, pltpu.roll(u3, 2, 1)))
        return (b_ref[...] + w_ref[0:1, :] * p2 + w_ref[1:2, :] * p1 + w_ref[2:3, :] * u3,
                u3[:, L - 2:L, :])

    ca, na = conv(ua, sta_ref, cwa_ref, cba_ref)
    cv, nv = conv(uv, stv_ref, cwv_ref, cbv_ref)
    na_ref[...] = na
    nv_ref[...] = nv
    act = (jax.nn.silu(ca) * cv).astype(BF16).reshape(M, tf)
    part = jnp.dot(act, wd_ref[...], preferred_element_type=F32)

    @pl.when(c == 0)
    def _():
        o_ref[...] = part

    @pl.when(c > 0)
    def _():
        o_ref[...] += part

    @pl.when(c == nc - 1)
    def _():
        o_ref[...] = x_ref[...] + _rms(o_ref[...], gpost_ref[...])


def ffn_sample(x, layer, g_pre, w_up, conv_w, conv_b, w_down, g_post, conv_state, tf):
    M, D = x.shape
    nbatch = M // DEC_SEQ
    nc = D_FF // tf
    conv_b3 = conv_b.reshape(DEPTH, 1, 2 * D_FF)
    y, na, nv = pl.pallas_call(
        functools.partial(_ffn_sample_kernel, nc=nc),
        grid=(nc,),
        in_specs=[
            pl.BlockSpec((M, D), lambda c: (0, 0)),
            pl.BlockSpec((None, 1, D), lambda c: (layer, 0, 0)),
            pl.BlockSpec((None, D, tf), lambda c: (layer, 0, c)),
            pl.BlockSpec((None, D, tf), lambda c: (layer, 0, nc + c)),
            pl.BlockSpec((None, 3, tf), lambda c: (layer, 0, c)),
            pl.BlockSpec((None, 3, tf), lambda c: (layer, 0, nc + c)),
            pl.BlockSpec((None, 1, tf), lambda c: (layer, 0, c)),
            pl.BlockSpec((None, 1, tf), lambda c: (layer, 0, nc + c)),
            pl.BlockSpec((None, tf, D), lambda c: (layer, c, 0)),
            pl.BlockSpec((None, 1, D), lambda c: (layer, 0, 0)),
            pl.BlockSpec((None, nbatch, 2, tf), lambda c: (layer, 0, 0, c)),
            pl.BlockSpec((None, nbatch, 2, tf), lambda c: (layer, 0, 0, nc + c)),
        ],
        out_specs=[
            pl.BlockSpec((M, D), lambda c: (0, 0)),
            pl.BlockSpec((nbatch, 2, tf), lambda c: (0, 0, c)),
            pl.BlockSpec((nbatch, 2, tf), lambda c: (0, 0, c)),
        ],
        out_shape=[
            jax.ShapeDtypeStruct((M, D), F32),
            jax.ShapeDtypeStruct((nbatch, 2, D_FF), F32),
            jax.ShapeDtypeStruct((nbatch, 2, D_FF), F32),
        ],
        scratch_shapes=[pltpu.VMEM((M, D), BF16)],
        compiler_params=_params("arbitrary"),
        name="ffn_sample",
    )(x, g_pre.reshape(DEPTH, 1, D), w_up, w_up, conv_w, conv_w, conv_b3, conv_b3,
      w_down, g_post.reshape(DEPTH, 1, D), conv_state, conv_state)
    return y, jnp.concatenate([na, nv], axis=-1)


def kernel(x_prompt, x_sample, mem_prompt, cache_swa_k, cache_swa_v, state_ssm_re, state_ssm_im, state_ffn_conv, cache_mem_k, cache_mem_v, g_mix_pre, g_mix_post, w_qkv, w_attn_o, attn_sinks, w_ssm_in, ssm_lambda_re, ssm_lambda_im, ssm_log_step, ssm_b_re, ssm_b_im, ssm_c_re, ssm_c_im, ssm_d, w_ssm_glu, g_x_pre, g_x_post, g_mem, w_x_q, w_mem_k, w_mem_v, w_x_o, g_ffn_pre, g_ffn_post, w_ffn_up, ffn_conv_w, ffn_conv_b, w_ffn_down):
    B = x_prompt.shape[0]
    SB = x_sample.shape[0]
    xw = N_XHEADS * XHEAD_DIM
    nkc = N_KV_HEADS * HEAD_DIM
    xp = x_prompt.reshape(B * SEQ, D_MODEL)
    xs = x_sample.reshape(SB * DEC_SEQ, D_MODEL)
    mem = mem_prompt.reshape(B * N_MEM, D_MODEL)
    MS = SB * DEC_SEQ
    TMP = 1024
    cos_p, sin_p = _rope_tables(jnp.arange(SEQ))
    cos_s, sin_s = _rope_tables(PAST_LEN + jnp.arange(DEC_SEQ))
    ck_all = cache_swa_k.reshape(cache_swa_k.shape[0], SB, WINDOW, nkc)
    cv_all = cache_swa_v.reshape(cache_swa_v.shape[0], SB, WINDOW, nkc)
    cmk_all = cache_mem_k.reshape(DEPTH, SB, N_MEM, xw)
    cmv_all = cache_mem_v.reshape(DEPTH, SB, N_MEM, xw)
    w_up_bf = w_ffn_up.astype(BF16)
    w_down_bf = w_ffn_down.astype(BF16)
    w_qkv_bf = w_qkv.astype(BF16)
    w_attn_o_bf = w_attn_o.astype(BF16)
    w_ssm_in_bf = w_ssm_in.astype(BF16)
    w_glu_bf = w_ssm_glu.astype(BF16)
    w_x_q_bf = w_x_q.astype(BF16)
    w_x_o_bf = w_x_o.astype(BF16)
    w_mem_kv_bf = jnp.concatenate([w_mem_k, w_mem_v], axis=-1).astype(BF16)
    TD = 512

    swa_kp, swa_vp, swa_ks, swa_vs = [], [], [], []
    ssm_rp, ssm_ip, ssm_rs, ssm_is = [], [], [], []
    conv_p, conv_s, memk_p, memv_p = [], [], [], []
    for i in range(DEPTH):
        j = i // 2
        if i % 2 == 0:
            qkv_p = norm_matmul(xp, g_mix_pre, i, w_qkv_bf, j, TD)
            qkv_s = norm_matmul(xs, g_mix_pre, i, w_qkv_bf, j, MS)
            op, kp, vp = swa_prompt(qkv_p, attn_sinks, j, cos_p, sin_p, B)
            sink_rows = jnp.repeat(attn_sinks[j].reshape(N_KV_HEADS, GQA_GROUP), DEC_SEQ,
                                   axis=1)[..., None]
            os_, kn, vn = swa_sample(qkv_s, ck_all, cv_all, j, sink_rows, cos_s, sin_s)
            swa_kp.append(kp.reshape(B, WINDOW, N_KV_HEADS, HEAD_DIM))
            swa_vp.append(vp.reshape(B, WINDOW, N_KV_HEADS, HEAD_DIM))
            swa_ks.append(kn.reshape(SB, WINDOW, N_KV_HEADS, HEAD_DIM))
            swa_vs.append(vn.reshape(SB, WINDOW, N_KV_HEADS, HEAD_DIM))
            xp = matmul_post(op, w_attn_o_bf, j, g_mix_post, i, xp, TD)
            xs = matmul_post(os_, w_attn_o_bf, j, g_mix_post, i, xs, MS)
        else:
            a_re, a_im, b_blk, c_blk = _s5_weights(
                ssm_lambda_re[j], ssm_lambda_im[j], ssm_log_step[j], ssm_b_re[j], ssm_b_im[j],
                ssm_c_re[j], ssm_c_im[j])
            d_row = ssm_d[j].reshape(1, D_MODEL)
            up_ = norm_matmul(xp, g_mix_pre, i, w_ssm_in_bf, j, TD)
            us_ = norm_matmul(xs, g_mix_pre, i, w_ssm_in_bf, j, MS)
            a_tiles = jnp.stack([a_re.reshape(N_SUPER, SUBLANES, LANES),
                                 a_im.reshape(N_SUPER, SUBLANES, LANES)], axis=1
                                ).reshape(2 * N_SUPER, SUBLANES, LANES)
            zero_state = jnp.zeros((B, N_SUPER, SUBLANES, LANES), F32)
            zp, rp, ip = s5_prompt(up_, zero_state, zero_state, a_tiles, b_blk, c_blk, d_row,
                                   B, 256)
            nstate = N_SSM_GROUPS * SSM_STATE
            zs, rn, im_ = s5_sample(us_, state_ssm_re[j].reshape(SB, nstate),
                                    state_ssm_im[j].reshape(SB, nstate),
                                    a_re.reshape(1, nstate), a_im.reshape(1, nstate),
                                    b_blk, c_blk, d_row)
            ssm_rp.append(rp.reshape(B, N_SSM_GROUPS, SSM_STATE))
            ssm_ip.append(ip.reshape(B, N_SSM_GROUPS, SSM_STATE))
            ssm_rs.append(rn.reshape(SB, N_SSM_GROUPS, SSM_STATE))
            ssm_is.append(im_.reshape(SB, N_SSM_GROUPS, SSM_STATE))
            xp = glu_post(zp, w_glu_bf, j, g_mix_post, i, xp, TD)
            xs = glu_post(zs, w_glu_bf, j, g_mix_post, i, xs, MS)
        mkv = norm_matmul(mem, g_mem, i, w_mem_kv_bf, i, B * N_MEM)
        mk, mv = mkv[:, :xw], mkv[:, xw:]
        memk_p.append(mk.reshape(B, N_MEM, N_XHEADS, XHEAD_DIM))
        memv_p.append(mv.reshape(B, N_MEM, N_XHEADS, XHEAD_DIM))
        xp = xattn_prompt(xp, i, g_x_pre, w_x_q_bf, mk, mv, w_x_o_bf, g_x_post, B, TD)
        qs = norm_matmul(xs, g_x_pre, i, w_x_q_bf, i, MS)
        as_ = xattn_sample(qs, cmk_all, cmv_all, i)
        xs = matmul_post(as_, w_x_o_bf, i, g_x_post, i, xs, MS)
        xp, cp = ffn_prompt(xp, i, g_ffn_pre, w_up_bf, ffn_conv_w, ffn_conv_b, w_down_bf,
                            g_ffn_post, B, TMP, 512)
        xs, cs = ffn_sample(xs, i, g_ffn_pre, w_up_bf, ffn_conv_w, ffn_conv_b, w_down_bf,
                            g_ffn_post, state_ffn_conv, 1408)
        conv_p.append(cp)
        conv_s.append(cs)
    return (xp.reshape(B, SEQ, D_MODEL), xs.reshape(SB, DEC_SEQ, D_MODEL),
            jnp.stack(swa_kp), jnp.stack(swa_vp), jnp.stack(swa_ks), jnp.stack(swa_vs),
            jnp.stack(ssm_rp), jnp.stack(ssm_ip), jnp.stack(ssm_rs), jnp.stack(ssm_is),
            jnp.stack(conv_p), jnp.stack(conv_s), jnp.stack(memk_p), jnp.stack(memv_p))
```

```python
import functools
import math

import jax
import jax.numpy as jnp
from jax import lax
from jax.experimental import pallas as pl
from jax.experimental.pallas import tpu as pltpu

D_MODEL = 2048
SEQ = 4096
DEPTH = 4
DEC_SEQ = 8
PAST_LEN = 16384
HEAD_DIM = 64
N_HEADS = 32
N_KV_HEADS = 4
GQA_GROUP = 8
WINDOW = 128
ROPE_THETA = 10000.0
SSM_GROUP = 16
N_SSM_GROUPS = 128
SSM_STATE = 64
N_MEM = 256
N_XHEADS = 4
XHEAD_DIM = 128
D_FF = 5632
NORM_EPS = 1e-6
NEG_INF = -1e30

F32 = jnp.float32
BF16 = jnp.bfloat16

V7X_VMEM_BYTES = 64 * 1024 * 1024
VMEM_LIMIT = V7X_VMEM_BYTES - 8 * 1024 * 1024
LANES = 128
SUBLANES = 8
SSM_SUPER = 16
N_SUPER = N_SSM_GROUPS // SSM_SUPER
SUPER_IN = SSM_SUPER * SSM_GROUP
SUPER_STATE = SSM_SUPER * SSM_STATE
SLABS = SUPER_STATE // LANES


def _params(*sem, vmem=VMEM_LIMIT):
    return pltpu.CompilerParams(dimension_semantics=sem, vmem_limit_bytes=vmem)


def _rms(x, g):
    ms = jnp.mean(x * x, axis=-1, keepdims=True)
    return x * lax.rsqrt(ms + NORM_EPS) * g


def _bdot(a, b):
    return jnp.dot(a.astype(BF16), b.astype(BF16), preferred_element_type=F32)


def _once(shape, index_map):
    return pl.BlockSpec(shape, index_map, pipeline_mode=pl.Buffered(1))


ROW_GROUP = 256


def _resident(shape, index_map):
    return pl.BlockSpec(shape, index_map, pipeline_mode=pl.Buffered(1))


def _row_groups(tm):
    rg = min(ROW_GROUP, tm)
    return [slice(k * rg, (k + 1) * rg) for k in range(tm // rg)]


def _norm_matmul_kernel(x_ref, g_ref, w_ref, o_ref):
    g = g_ref[...]
    for rows in _row_groups(x_ref.shape[0]):
        hn = _rms(x_ref[rows, :], g).astype(BF16)
        o_ref[rows, :] = jnp.dot(hn, w_ref[...], preferred_element_type=F32).astype(o_ref.dtype)


def norm_matmul(x, g, g_layer, w, w_layer, tm, out_dtype=F32):
    M, D = x.shape
    N = w.shape[-1]
    return pl.pallas_call(
        _norm_matmul_kernel,
        grid=(M // tm,),
        in_specs=[
            pl.BlockSpec((tm, D), lambda i: (i, 0)),
            pl.BlockSpec((None, 1, D), lambda i: (g_layer, 0, 0)),
            _resident((None, D, N), lambda i: (w_layer, 0, 0)),
        ],
        out_specs=pl.BlockSpec((tm, N), lambda i: (i, 0)),
        out_shape=jax.ShapeDtypeStruct((M, N), out_dtype),
        compiler_params=_params("arbitrary"),
        name="norm_matmul",
    )(x, g.reshape(g.shape[0], 1, D), w)


def _matmul_post_kernel(a_ref, w_ref, g_ref, x_ref, o_ref):
    g = g_ref[...]
    for rows in _row_groups(x_ref.shape[0]):
        y = jnp.dot(a_ref[rows, :].astype(BF16), w_ref[...], preferred_element_type=F32)
        o_ref[rows, :] = x_ref[rows, :] + _rms(y, g)


def matmul_post(a, w, w_layer, g, g_layer, x, tm):
    M, K = a.shape
    D = w.shape[-1]
    return pl.pallas_call(
        _matmul_post_kernel,
        grid=(M // tm,),
        in_specs=[
            pl.BlockSpec((tm, K), lambda i: (i, 0)),
            _resident((None, K, D), lambda i: (w_layer, 0, 0)),
            pl.BlockSpec((None, 1, D), lambda i: (g_layer, 0, 0)),
            pl.BlockSpec((tm, D), lambda i: (i, 0)),
        ],
        out_specs=pl.BlockSpec((tm, D), lambda i: (i, 0)),
        out_shape=jax.ShapeDtypeStruct((M, D), F32),
        compiler_params=_params("arbitrary"),
        name="matmul_post",
    )(a, w, g.reshape(g.shape[0], 1, D), x)


GLU_COLS = 512


def _glu_post_kernel(z_ref, w_ref, g_ref, x_ref, o_ref, y_ref):
    D = o_ref.shape[1]
    g = g_ref[...]
    for rows in _row_groups(x_ref.shape[0]):
        z = z_ref[rows, :]
        for n in range(D // GLU_COLS):
            cols = slice(n * GLU_COLS, (n + 1) * GLU_COLS)
            gcols = slice(D + n * GLU_COLS, D + (n + 1) * GLU_COLS)
            val = jnp.dot(z, w_ref[:, cols], preferred_element_type=F32)
            gate = jnp.dot(z, w_ref[:, gcols], preferred_element_type=F32)
            y_ref[rows, cols] = val * jax.nn.sigmoid(gate)
        o_ref[rows, :] = x_ref[rows, :] + _rms(y_ref[rows, :], g)


def glu_post(z, w, w_layer, g, g_layer, x, tm):
    M, K = z.shape
    D = D_MODEL
    return pl.pallas_call(
        _glu_post_kernel,
        grid=(M // tm,),
        in_specs=[
            pl.BlockSpec((tm, K), lambda i: (i, 0)),
            _resident((None, K, 2 * D), lambda i: (w_layer, 0, 0)),
            pl.BlockSpec((None, 1, D), lambda i: (g_layer, 0, 0)),
            pl.BlockSpec((tm, D), lambda i: (i, 0)),
        ],
        out_specs=pl.BlockSpec((tm, D), lambda i: (i, 0)),
        out_shape=jax.ShapeDtypeStruct((M, D), F32),
        scratch_shapes=[pltpu.VMEM((tm, D), F32)],
        compiler_params=_params("arbitrary"),
        name="glu_post",
    )(z, w, g.reshape(g.shape[0], 1, D), x)


def _rope_chunk(xc, cos, sin_signed):
    lane = lax.broadcasted_iota(jnp.int32, xc.shape, 1)
    first_half = (lane & (HEAD_DIM - 1)) < (HEAD_DIM // 2)
    partner = jnp.where(first_half, pltpu.roll(xc, LANES - HEAD_DIM // 2, 1),
                        pltpu.roll(xc, HEAD_DIM // 2, 1))
    return xc * cos + partner * sin_signed


def _rope_tables(pos):
    half = HEAD_DIM // 2
    inv_freq = ROPE_THETA ** (-jnp.arange(half, dtype=F32) * 2.0 / HEAD_DIM)
    ang = pos.astype(F32)[:, None] * inv_freq[None, :]
    cos, sin = jnp.cos(ang), jnp.sin(ang)
    cos128 = jnp.tile(cos, (1, LANES // half))
    sin128 = jnp.tile(jnp.concatenate([-sin, sin], axis=1), (1, LANES // HEAD_DIM))
    return cos128, sin128


def _swa_prompt_kernel(sink_ref, q_ref, kv_ref, cos_ref, sin_ref,
                       o_ref, kc_ref, vc_ref, kk_ref, vv_ref, *, layer):
    i = pl.program_id(1)
    blk = WINDOW
    cos, sn = cos_ref[...], sin_ref[...]
    kv = kv_ref[...]
    nkc = N_KV_HEADS * HEAD_DIM
    k = jnp.concatenate([_rope_chunk(kv[:, c * LANES:(c + 1) * LANES], cos, sn)
                         for c in range(nkc // LANES)], axis=1)
    v = kv[:, nkc:2 * nkc]
    kc_ref[...] = k
    vc_ref[...] = v

    cur = i % 2
    prv = 1 - cur

    @pl.when(i == 0)
    def _():
        kk_ref[1] = jnp.zeros((blk, nkc), BF16)
        vv_ref[1] = jnp.zeros((blk, nkc), BF16)

    kb, vb = k.astype(BF16), v.astype(BF16)
    kk_ref[cur] = kb
    vv_ref[cur] = vb
    kk = jnp.concatenate([kk_ref[prv], kb], axis=0)
    vv = jnp.concatenate([vv_ref[prv], vb], axis=0)
    rows = GQA_GROUP * blk
    qi = lax.broadcasted_iota(jnp.int32, (rows, 2 * blk), 0) & (blk - 1)
    si = lax.broadcasted_iota(jnp.int32, (rows, 2 * blk), 1)
    d = blk + qi - si
    mask = (d >= 0) & (d < WINDOW) & ((si >= blk) | (i > 0))

    chunks_per_group = GQA_GROUP * HEAD_DIM // LANES

    def scores(kh):
        hs = slice(kh * HEAD_DIM, (kh + 1) * HEAD_DIM)
        pieces = []
        for c in range(kh * chunks_per_group, (kh + 1) * chunks_per_group):
            qc = (_rope_chunk(q_ref[:, c * LANES:(c + 1) * LANES], cos, sn)
                  * (HEAD_DIM ** -0.5)).astype(BF16)
            pieces += [qc[:, hh * HEAD_DIM:(hh + 1) * HEAD_DIM] for hh in range(LANES // HEAD_DIM)]
        qs = jnp.concatenate(pieces, axis=0)
        return lax.dot_general(qs, kk[:, hs], (((1,), (1,)), ((), ())),
                               preferred_element_type=F32)

    ahead = 2
    s_all = [scores(kh) for kh in range(min(ahead, N_KV_HEADS))]
    outs = []
    for kh in range(N_KV_HEADS):
        if kh + ahead < N_KV_HEADS:
            s_all.append(scores(kh + ahead))
        hs = slice(kh * HEAD_DIM, (kh + 1) * HEAD_DIM)
        sk = jnp.concatenate([jnp.full((blk, 1), sink_ref[layer, kh * GQA_GROUP + j], F32)
                              for j in range(GQA_GROUP)], axis=0)
        s = jnp.where(mask, s_all[kh], NEG_INF)
        mx = jnp.maximum(jnp.max(s, axis=1, keepdims=True), sk)
        p = jnp.exp(s - mx)
        den = jnp.sum(p, axis=1, keepdims=True) + jnp.exp(sk - mx)
        o = jnp.dot(p.astype(BF16), vv[:, hs], preferred_element_type=F32) / den
        outs += [o[j * blk:(j + 1) * blk, :] for j in range(GQA_GROUP)]
    o_ref[...] = jnp.concatenate(outs, axis=1).astype(o_ref.dtype)


def swa_prompt(qkv, sinks, layer, cos, sin, batch):
    blk = WINDOW
    nb = SEQ // blk
    nq = N_HEADS * HEAD_DIM
    nkv = 2 * N_KV_HEADS * HEAD_DIM
    o, kc, vc = pl.pallas_call(
        functools.partial(_swa_prompt_kernel, layer=layer),
        grid=(batch, nb),
        in_specs=[
            pl.BlockSpec(memory_space=pltpu.SMEM),
            pl.BlockSpec((blk, nq), lambda b, i: (b * nb + i, 0)),
            pl.BlockSpec((blk, nkv), lambda b, i: (b * nb + i, nq // nkv)),
            pl.BlockSpec((blk, LANES), lambda b, i: (i, 0)),
            pl.BlockSpec((blk, LANES), lambda b, i: (i, 0)),
        ],
        out_specs=[
            pl.BlockSpec((blk, nq), lambda b, i: (b * nb + i, 0)),
            pl.BlockSpec((None, blk, nkv // 2), lambda b, i: (b, 0, 0)),
            pl.BlockSpec((None, blk, nkv // 2), lambda b, i: (b, 0, 0)),
        ],
        out_shape=[
            jax.ShapeDtypeStruct((batch * SEQ, nq), BF16),
            jax.ShapeDtypeStruct((batch, blk, nkv // 2), F32),
            jax.ShapeDtypeStruct((batch, blk, nkv // 2), F32),
        ],
        scratch_shapes=[pltpu.VMEM((2, blk, nkv // 2), BF16),
                        pltpu.VMEM((2, blk, nkv // 2), BF16)],
        compiler_params=_params("arbitrary", "arbitrary"),
        name="swa_prompt",
    )(sinks, qkv, qkv, cos, sin)
    return o, kc, vc


def _swa_sample_kernel(sink_ref, qkv_ref, ck_ref, cv_ref, cos_ref, sin_ref,
                       o_ref, nk_ref, nv_ref, *, nb):
    L = DEC_SEQ
    wb = WINDOW
    nq = N_HEADS * HEAD_DIM
    nkc = N_KV_HEADS * HEAD_DIM
    cos, sn = cos_ref[...], sin_ref[...]
    rows = GQA_GROUP * L
    t_q = lax.broadcasted_iota(jnp.int32, (rows, wb), 0) % L
    c_k = lax.broadcasted_iota(jnp.int32, (rows, wb), 1)
    mask_c = c_k >= t_q + 1
    t_q2 = lax.broadcasted_iota(jnp.int32, (rows, L), 0) % L
    t_k2 = lax.broadcasted_iota(jnp.int32, (rows, L), 1)
    mask_n = t_k2 <= t_q2
    for b in range(nb):
        x = qkv_ref[b * L:(b + 1) * L, :]
        qr = jnp.concatenate([_rope_chunk(x[:, c * LANES:(c + 1) * LANES], cos, sn)
                              for c in range(nq // LANES)], axis=1) * (HEAD_DIM ** -0.5)
        kn = jnp.concatenate([_rope_chunk(x[:, nq + c * LANES:nq + (c + 1) * LANES], cos, sn)
                              for c in range(nkc // LANES)], axis=1)
        vn = x[:, nq + nkc:nq + 2 * nkc]
        ck = ck_ref[b]
        cv = cv_ref[b]
        nk_ref[b, 0:wb - L, :] = ck[L:wb, :]
        nk_ref[b, wb - L:wb, :] = kn
        nv_ref[b, 0:wb - L, :] = cv[L:wb, :]
        nv_ref[b, wb - L:wb, :] = vn
        ckb, cvb, knb, vnb = ck.astype(BF16), cv.astype(BF16), kn.astype(BF16), vn.astype(BF16)
        dn = (((1,), (1,)), ((), ()))
        scores = []
        for kh in range(N_KV_HEADS):
            hs = slice(kh * HEAD_DIM, (kh + 1) * HEAD_DIM)
            qs = jnp.concatenate(
                [qr[:, (kh * GQA_GROUP + j) * HEAD_DIM:(kh * GQA_GROUP + j + 1) * HEAD_DIM]
                 for j in range(GQA_GROUP)], axis=0).astype(BF16)
            scores.append((lax.dot_general(qs, ckb[:, hs], dn, preferred_element_type=F32),
                           lax.dot_general(qs, knb[:, hs], dn, preferred_element_type=F32)))
        outs = []
        for kh in range(N_KV_HEADS):
            hs = slice(kh * HEAD_DIM, (kh + 1) * HEAD_DIM)
            s_c = jnp.where(mask_c, scores[kh][0], NEG_INF)
            s_n = jnp.where(mask_n, scores[kh][1], NEG_INF)
            sk = sink_ref[kh]
            mx = jnp.maximum(jnp.maximum(jnp.max(s_c, axis=1, keepdims=True),
                                         jnp.max(s_n, axis=1, keepdims=True)), sk)
            p_c = jnp.exp(s_c - mx)
            p_n = jnp.exp(s_n - mx)
            den = (jnp.sum(p_c, axis=1, keepdims=True) + jnp.sum(p_n, axis=1, keepdims=True)
                   + jnp.exp(sk - mx))
            o = (jnp.dot(p_c.astype(BF16), cvb[:, hs], preferred_element_type=F32)
                 + jnp.dot(p_n.astype(BF16), vnb[:, hs], preferred_element_type=F32)) / den
            outs.extend(o[j * L:(j + 1) * L, :] for j in range(GQA_GROUP))
        o_ref[b * L:(b + 1) * L, :] = jnp.concatenate(outs, axis=1).astype(o_ref.dtype)


def swa_sample(qkv, cache_k, cache_v, layer, sink_rows, cos, sin, nb=8):
    nbatch = cache_k.shape[1]
    nq = N_HEADS * HEAD_DIM
    nkc = N_KV_HEADS * HEAD_DIM
    L = DEC_SEQ
    return pl.pallas_call(
        functools.partial(_swa_sample_kernel, nb=nb),
        grid=(nbatch // nb,),
        in_specs=[
            pl.BlockSpec((N_KV_HEADS, GQA_GROUP * L, 1), lambda g: (0, 0, 0)),
            pl.BlockSpec((nb * L, nq + 2 * nkc), lambda g: (g, 0)),
            pl.BlockSpec((None, nb, WINDOW, nkc), lambda g: (layer, g, 0, 0)),
            pl.BlockSpec((None, nb, WINDOW, nkc), lambda g: (layer, g, 0, 0)),
            pl.BlockSpec((L, LANES), lambda g: (0, 0)),
            pl.BlockSpec((L, LANES), lambda g: (0, 0)),
        ],
        out_specs=[
            pl.BlockSpec((nb * L, nq), lambda g: (g, 0)),
            pl.BlockSpec((nb, WINDOW, nkc), lambda g: (g, 0, 0)),
            pl.BlockSpec((nb, WINDOW, nkc), lambda g: (g, 0, 0)),
        ],
        out_shape=[
            jax.ShapeDtypeStruct((nbatch * L, nq), BF16),
            jax.ShapeDtypeStruct((nbatch, WINDOW, nkc), F32),
            jax.ShapeDtypeStruct((nbatch, WINDOW, nkc), F32),
        ],
        compiler_params=_params("arbitrary"),
        name="swa_sample",
    )(sink_rows, qkv, cache_k, cache_v, cos, sin)


def _xattn_heads(q, mk, mv):
    heads = [slice(h * XHEAD_DIM, (h + 1) * XHEAD_DIM) for h in range(N_XHEADS)]
    scores = [lax.dot_general((q[:, hs] * (XHEAD_DIM ** -0.5)).astype(BF16), mk[:, hs].astype(BF16),
                              (((1,), (1,)), ((), ())), preferred_element_type=F32)
              for hs in heads]
    outs = []
    for hs, s in zip(heads, scores):
        mx = jnp.max(s, axis=1, keepdims=True)
        p = jnp.exp(s - mx)
        den = jnp.sum(p, axis=1, keepdims=True)
        outs.append(jnp.dot(p.astype(BF16), mv[:, hs].astype(BF16),
                            preferred_element_type=F32) / den)
    return jnp.concatenate(outs, axis=1)


def _xattn_prompt_kernel(x_ref, gpre_ref, wq_ref, mk_ref, mv_ref, wo_ref, gpost_ref, o_ref):
    mk = mk_ref[...].astype(BF16)
    mv = mv_ref[...].astype(BF16)
    gpre, gpost = gpre_ref[...], gpost_ref[...]
    for rows in _row_groups(x_ref.shape[0]):
        x = x_ref[rows, :]
        q = jnp.dot(_rms(x, gpre).astype(BF16), wq_ref[...], preferred_element_type=F32)
        a = _xattn_heads(q, mk, mv).astype(BF16)
        y = jnp.dot(a, wo_ref[...], preferred_element_type=F32)
        o_ref[rows, :] = x + _rms(y, gpost)


def xattn_prompt(x, layer, g_pre, w_q, mk, mv, w_o, g_post, batch, tm):
    nq = SEQ // tm
    xw = N_XHEADS * XHEAD_DIM
    D = D_MODEL
    return pl.pallas_call(
        _xattn_prompt_kernel,
        grid=(batch, nq),
        in_specs=[
            pl.BlockSpec((tm, D), lambda b, i: (b * nq + i, 0)),
            pl.BlockSpec((None, 1, D), lambda b, i: (layer, 0, 0)),
            _resident((None, D, xw), lambda b, i: (layer, 0, 0)),
            pl.BlockSpec((N_MEM, xw), lambda b, i: (b, 0)),
            pl.BlockSpec((N_MEM, xw), lambda b, i: (b, 0)),
            _resident((None, xw, D), lambda b, i: (layer, 0, 0)),
            pl.BlockSpec((None, 1, D), lambda b, i: (layer, 0, 0)),
        ],
        out_specs=pl.BlockSpec((tm, D), lambda b, i: (b * nq + i, 0)),
        out_shape=jax.ShapeDtypeStruct((batch * SEQ, D), F32),
        compiler_params=_params("arbitrary", "arbitrary"),
        name="xattn_prompt",
    )(x, g_pre.reshape(DEPTH, 1, D), w_q, mk, mv, w_o, g_post.reshape(DEPTH, 1, D))


def _xattn_sample_kernel(q_ref, mk_ref, mv_ref, o_ref, *, nb):
    L = DEC_SEQ
    for b in range(nb):
        o_ref[b * L:(b + 1) * L, :] = _xattn_heads(
            q_ref[b * L:(b + 1) * L, :], mk_ref[b], mv_ref[b]).astype(o_ref.dtype)


def xattn_sample(q, cache_k, cache_v, layer, nb=8):
    nbatch = cache_k.shape[1]
    L = DEC_SEQ
    xw = N_XHEADS * XHEAD_DIM
    return pl.pallas_call(
        functools.partial(_xattn_sample_kernel, nb=nb),
        grid=(nbatch // nb,),
        in_specs=[
            pl.BlockSpec((nb * L, xw), lambda g: (g, 0)),
            pl.BlockSpec((None, nb, N_MEM, xw), lambda g: (layer, g, 0, 0)),
            pl.BlockSpec((None, nb, N_MEM, xw), lambda g: (layer, g, 0, 0)),
        ],
        out_specs=pl.BlockSpec((nb * L, xw), lambda g: (g, 0)),
        out_shape=jax.ShapeDtypeStruct((nbatch * L, xw), BF16),
        compiler_params=_params("arbitrary"),
        name="xattn_sample",
    )(q, cache_k, cache_v)


def _s5_prompt_kernel(u_ref, h0r_ref, h0i_ref, a_ref, b_ref, c_ref, d_ref,
                      z_ref, hr_ref, hi_ref, x_ref, hst_ref, *, T):
    c = pl.program_id(1)
    nslab = 2 * SLABS
    G = T // SUBLANES

    @pl.when(c == 0)
    def _():
        for sb in range(N_SUPER):
            hst_ref[2 * sb] = h0r_ref[sb]
            hst_ref[2 * sb + 1] = h0i_ref[sb]

    def slab_rows(sb, s):
        return pl.ds((sb * nslab + s) * S5_SLAB_PITCH, SUBLANES)

    for sb in range(N_SUPER):
        bu = _bdot(u_ref[:, sb * SUPER_IN:(sb + 1) * SUPER_IN], b_ref[sb])
        for s in range(nslab):
            x_ref[:, slab_rows(sb, s), :] = bu[:, s * LANES:(s + 1) * LANES].reshape(
                G, SUBLANES, LANES)

    a = [a_ref[k] for k in range(2 * N_SUPER)]
    h_init = tuple(hst_ref[k] for k in range(2 * N_SUPER))

    def step(t, h):
        g = lax.shift_right_logical(t, 3)
        r = lax.bitwise_and(t, SUBLANES - 1)
        new = []
        for sb in range(N_SUPER):
            ir = pl.ds((sb * nslab) * S5_SLAB_PITCH + r, SUBLANES, stride=S5_SLAB_PITCH)
            ii = pl.ds((sb * nslab + SLABS) * S5_SLAB_PITCH + r, SUBLANES, stride=S5_SLAB_PITCH)
            ar, ai = a[2 * sb], a[2 * sb + 1]
            hr, hi = h[2 * sb], h[2 * sb + 1]
            nr = ar * hr - ai * hi + x_ref[g, ir, :]
            ni = ar * hi + ai * hr + x_ref[g, ii, :]
            x_ref[g, ir, :] = nr
            x_ref[g, ii, :] = ni
            new += [nr, ni]
        return tuple(new)

    h_fin = lax.fori_loop(0, T, step, h_init)
    for sb in range(N_SUPER):
        hst_ref[2 * sb] = h_fin[2 * sb]
        hst_ref[2 * sb + 1] = h_fin[2 * sb + 1]
        hr_ref[sb] = h_fin[2 * sb]
        hi_ref[sb] = h_fin[2 * sb + 1]

    for sb in range(N_SUPER):
        hcat = jnp.concatenate([x_ref[:, slab_rows(sb, s), :].reshape(T, LANES)
                                for s in range(nslab)], axis=1)
        cols = slice(sb * SUPER_IN, (sb + 1) * SUPER_IN)
        y = _bdot(hcat, c_ref[sb]) + d_ref[:, cols] * u_ref[:, cols]
        z_ref[:, cols] = jax.nn.gelu(y).astype(z_ref.dtype)


S5_SLAB_PITCH = 12


def s5_prompt(u, h0r, h0i, a_tiles, b_blk, c_blk, d_row, batch, T):
    nc = SEQ // T
    st = (None, N_SUPER, SUBLANES, LANES)
    return pl.pallas_call(
        functools.partial(_s5_prompt_kernel, T=T),
        grid=(batch, nc),
        in_specs=[
            pl.BlockSpec((T, D_MODEL), lambda b, c: (b * nc + c, 0)),
            pl.BlockSpec(st, lambda b, c: (b, 0, 0, 0)),
            pl.BlockSpec(st, lambda b, c: (b, 0, 0, 0)),
            _once((2 * N_SUPER, SUBLANES, LANES), lambda b, c: (0, 0, 0)),
            _once((N_SUPER, SUPER_IN, 2 * SUPER_STATE), lambda b, c: (0, 0, 0)),
            _once((N_SUPER, 2 * SUPER_STATE, SUPER_IN), lambda b, c: (0, 0, 0)),
            _once((1, D_MODEL), lambda b, c: (0, 0)),
        ],
        out_specs=[
            pl.BlockSpec((T, D_MODEL), lambda b, c: (b * nc + c, 0)),
            pl.BlockSpec(st, lambda b, c: (b, 0, 0, 0)),
            pl.BlockSpec(st, lambda b, c: (b, 0, 0, 0)),
        ],
        out_shape=[
            jax.ShapeDtypeStruct((batch * SEQ, D_MODEL), BF16),
            jax.ShapeDtypeStruct((batch, N_SUPER, SUBLANES, LANES), F32),
            jax.ShapeDtypeStruct((batch, N_SUPER, SUBLANES, LANES), F32),
        ],
        scratch_shapes=[
            pltpu.VMEM((T // SUBLANES, N_SUPER * 2 * SLABS * S5_SLAB_PITCH, LANES), F32),
            pltpu.VMEM((2 * N_SUPER, SUBLANES, LANES), F32),
        ],
        compiler_params=_params("arbitrary", "arbitrary"),
        name="s5_prompt",
    )(u, h0r, h0i, a_tiles, b_blk, c_blk, d_row)


def _s5_sample_kernel(u_ref, h0r_ref, h0i_ref, ar_ref, ai_ref, b_ref, c_ref, d_ref,
                      z_ref, hr_ref, hi_ref, x_ref, *, nbatch):
    L = DEC_SEQ
    R = nbatch * L
    for sb in range(N_SUPER):
        cols = slice(sb * SUPER_IN, (sb + 1) * SUPER_IN)
        bu = _bdot(u_ref[:, cols], b_ref[sb])
        for s in range(2 * SLABS):
            x_ref[s * R:(s + 1) * R, :] = bu[:, s * LANES:(s + 1) * LANES]
        for s in range(SLABS):
            st = slice(sb * SUPER_STATE + s * LANES, sb * SUPER_STATE + (s + 1) * LANES)
            ar, ai = ar_ref[:, st], ai_ref[:, st]
            hr, hi = h0r_ref[:, st], h0i_ref[:, st]
            for t in range(L):
                rr = pl.ds(s * R + t, nbatch, stride=L)
                ri = pl.ds((SLABS + s) * R + t, nbatch, stride=L)
                nr = ar * hr - ai * hi + x_ref[rr, :]
                ni = ar * hi + ai * hr + x_ref[ri, :]
                x_ref[rr, :] = nr
                x_ref[ri, :] = ni
                hr, hi = nr, ni
            hr_ref[:, st] = hr
            hi_ref[:, st] = hi
        hcat = jnp.concatenate([x_ref[s * R:(s + 1) * R, :] for s in range(2 * SLABS)], axis=1)
        y = _bdot(hcat, c_ref[sb]) + d_ref[:, cols] * u_ref[:, cols]
        z_ref[:, cols] = jax.nn.gelu(y).astype(z_ref.dtype)


def s5_sample(u, h0r, h0i, a_re, a_im, b_blk, c_blk, d_row):
    rows = u.shape[0]
    nbatch = rows // DEC_SEQ
    nstate = N_SSM_GROUPS * SSM_STATE
    full = lambda shape: pl.BlockSpec(shape, lambda i: (0,) * len(shape))
    return pl.pallas_call(
        functools.partial(_s5_sample_kernel, nbatch=nbatch),
        grid=(1,),
        in_specs=[
            full((rows, D_MODEL)), full((nbatch, nstate)), full((nbatch, nstate)),
            full((1, nstate)), full((1, nstate)),
            full((N_SUPER, SUPER_IN, 2 * SUPER_STATE)),
            full((N_SUPER, 2 * SUPER_STATE, SUPER_IN)),
            full((1, D_MODEL)),
        ],
        out_specs=[full((rows, D_MODEL)), full((nbatch, nstate)), full((nbatch, nstate))],
        out_shape=[
            jax.ShapeDtypeStruct((rows, D_MODEL), BF16),
            jax.ShapeDtypeStruct((nbatch, nstate), F32),
            jax.ShapeDtypeStruct((nbatch, nstate), F32),
        ],
        scratch_shapes=[pltpu.VMEM((2 * SLABS * rows, LANES), F32)],
        compiler_params=_params("arbitrary"),
        name="s5_sample",
    )(u, h0r, h0i, a_re, a_im, b_blk, c_blk, d_row)


def _s5_weights(lam_re, lam_im, log_step, b_re, b_im, c_re, c_im):
    lam = lax.complex(lam_re.astype(F32), lam_im.astype(F32))
    dt = jnp.exp(log_step.astype(F32))[:, None]
    abar = jnp.exp(lam * dt)
    bbar = ((abar - 1.0) / lam)[..., None] * lax.complex(b_re.astype(F32), b_im.astype(F32))
    eye = jnp.eye(SSM_SUPER, dtype=F32)

    def b_layout(m):
        m = m.reshape(N_SUPER, SSM_SUPER, SSM_STATE, SSM_GROUP)
        return jnp.einsum('sgph,gk->sghkp', m, eye).reshape(N_SUPER, SUPER_IN, SUPER_STATE)

    def c_layout(m):
        m = m.reshape(N_SUPER, SSM_SUPER, SSM_GROUP, SSM_STATE)
        return jnp.einsum('sghp,gk->sgpkh', m, eye).reshape(N_SUPER, SUPER_STATE, SUPER_IN)

    b_blk = jnp.concatenate([b_layout(bbar.real), b_layout(bbar.imag)], axis=2).astype(BF16)
    c_blk = jnp.concatenate([c_layout(c_re.astype(F32)), -c_layout(c_im.astype(F32))],
                            axis=1).astype(BF16)
    return abar.real, abar.imag, b_blk, c_blk


def _ffn_prompt_kernel(x_ref, gpre_ref, wua_ref, wuv_ref, cwa_ref, cwv_ref, cba_ref, cbv_ref,
                       wd_ref, gpost_ref, o_ref, sa_ref, sv_ref,
                       hn_ref, ua_ref, uv_ref, act_ref, ha_ref, hv_ref,
                       *, nc, blocks_per_seq, rc, rm):
    i = pl.program_id(0)
    c = pl.program_id(1)
    tm, D = hn_ref.shape
    hdr = SUBLANES

    @pl.when(c == 0)
    def _():
        hn_ref[...] = _rms(x_ref[...], gpre_ref[...]).astype(BF16)
        o_ref[...] = jnp.zeros(o_ref.shape, F32)

    slot = i % 2

    @pl.when(i % blocks_per_seq == 0)
    def _():
        ha_ref[1 - slot, c] = jnp.zeros(ha_ref.shape[2:], F32)
        hv_ref[1 - slot, c] = jnp.zeros(hv_ref.shape[2:], F32)

    ua_ref[0:hdr, :] = ha_ref[1 - slot, c]
    uv_ref[0:hdr, :] = hv_ref[1 - slot, c]
    wa = wua_ref[...].astype(BF16)
    wv = wuv_ref[...].astype(BF16)
    wd = wd_ref[...].astype(BF16)

    def conv(buf_ref, w_ref, b_ref, r0):
        return (b_ref[...] + w_ref[0:1, :] * buf_ref[r0 + hdr - 2:r0 + hdr - 2 + rc, :]
                + w_ref[1:2, :] * buf_ref[r0 + hdr - 1:r0 + hdr - 1 + rc, :]
                + w_ref[2:3, :] * buf_ref[r0 + hdr:r0 + hdr + rc, :])

    def up_proj(k):
        hk = hn_ref[k * rm:(k + 1) * rm, :]
        ua_ref[hdr + k * rm:hdr + (k + 1) * rm, :] = jnp.dot(hk, wa, preferred_element_type=F32)
        uv_ref[hdr + k * rm:hdr + (k + 1) * rm, :] = jnp.dot(hk, wv, preferred_element_type=F32)

    ngroups = tm // rm
    up_proj(0)
    for k in range(ngroups):
        if k + 1 < ngroups:
            up_proj(k + 1)
        rows = slice(k * rm, (k + 1) * rm)
        for r0 in range(k * rm, (k + 1) * rm, rc):
            ca = conv(ua_ref, cwa_ref, cba_ref, r0)
            cv = conv(uv_ref, cwv_ref, cbv_ref, r0)
            act_ref[r0:r0 + rc, :] = (jax.nn.silu(ca) * cv).astype(BF16)
        o_ref[rows, :] += jnp.dot(act_ref[rows, :], wd, preferred_element_type=F32)

    ta = ua_ref[tm:tm + hdr, :]
    tv = uv_ref[tm:tm + hdr, :]
    ha_ref[slot, c] = ta
    hv_ref[slot, c] = tv
    sa_ref[c] = ta[hdr - 2:hdr, :]
    sv_ref[c] = tv[hdr - 2:hdr, :]

    @pl.when(c == nc - 1)
    def _():
        o_ref[...] = x_ref[...] + _rms(o_ref[...], gpost_ref[...])


FFN_ROW_CHUNK = 64
FFN_ROW_GROUP = 256
FFN_VMEM_LIMIT = V7X_VMEM_BYTES - 4 * 1024 * 1024


def ffn_prompt(x, layer, g_pre, w_up, conv_w, conv_b, w_down, g_post, batch, tm, tf):
    M, D = x.shape
    nc = D_FF // tf
    bps = SEQ // tm
    conv_b3 = conv_b.reshape(DEPTH, 1, 2 * D_FF)
    y, sa, sv = pl.pallas_call(
        functools.partial(_ffn_prompt_kernel, nc=nc, blocks_per_seq=bps, rc=FFN_ROW_CHUNK,
                          rm=FFN_ROW_GROUP),
        grid=(M // tm, nc),
        in_specs=[
            pl.BlockSpec((tm, D), lambda i, c: (i, 0)),
            pl.BlockSpec((None, 1, D), lambda i, c: (layer, 0, 0)),
            pl.BlockSpec((None, D, tf), lambda i, c: (layer, 0, c)),
            pl.BlockSpec((None, D, tf), lambda i, c: (layer, 0, nc + c)),
            pl.BlockSpec((None, 3, tf), lambda i, c: (layer, 0, c)),
            pl.BlockSpec((None, 3, tf), lambda i, c: (layer, 0, nc + c)),
            pl.BlockSpec((None, 1, tf), lambda i, c: (layer, 0, c)),
            pl.BlockSpec((None, 1, tf), lambda i, c: (layer, 0, nc + c)),
            pl.BlockSpec((None, tf, D), lambda i, c: (layer, c, 0)),
            pl.BlockSpec((None, 1, D), lambda i, c: (layer, 0, 0)),
        ],
        out_specs=[
            _once((tm, D), lambda i, c: (i, 0)),
            pl.BlockSpec((None, nc, 2, tf), lambda i, c: (i // bps, 0, 0, 0)),
            pl.BlockSpec((None, nc, 2, tf), lambda i, c: (i // bps, 0, 0, 0)),
        ],
        out_shape=[
            jax.ShapeDtypeStruct((M, D), F32),
            jax.ShapeDtypeStruct((batch, nc, 2, tf), F32),
            jax.ShapeDtypeStruct((batch, nc, 2, tf), F32),
        ],
        scratch_shapes=[
            pltpu.VMEM((tm, D), BF16),
            pltpu.VMEM((tm + SUBLANES, tf), F32),
            pltpu.VMEM((tm + SUBLANES, tf), F32),
            pltpu.VMEM((tm, tf), BF16),
            pltpu.VMEM((2, nc, SUBLANES, tf), F32),
            pltpu.VMEM((2, nc, SUBLANES, tf), F32),
        ],
        compiler_params=_params("arbitrary", "arbitrary", vmem=FFN_VMEM_LIMIT),
        name="ffn_prompt",
    )(x, g_pre.reshape(DEPTH, 1, D), w_up, w_up, conv_w, conv_w, conv_b3, conv_b3,
      w_down, g_post.reshape(DEPTH, 1, D))
    sa = sa.transpose(0, 2, 1, 3).reshape(batch, 2, D_FF)
    sv = sv.transpose(0, 2, 1, 3).reshape(batch, 2, D_FF)
    return y, jnp.concatenate([sa, sv], axis=-1)


def _ffn_sample_kernel(x_ref, gpre_ref, wua_ref, wuv_ref, cwa_ref, cwv_ref, cba_ref, cbv_ref,
                       wd_ref, gpost_ref, sta_ref, stv_ref, o_ref, na_ref, nv_ref, hn_ref, *, nc):
    c = pl.program_id(0)
    L = DEC_SEQ

    @pl.when(c == 0)
    def _():
        hn_ref[...] = _rms(x_ref[...], gpre_ref[...]).astype(BF16)

    hn = hn_ref[...]
    M = hn.shape[0]
    nb = M // L
    ua = jnp.dot(hn, wua_ref[...], preferred_element_type=F32)
    uv = jnp.dot(hn, wuv_ref[...], preferred_element_type=F32)
    tf = ua.shape[1]
    t = lax.broadcasted_iota(jnp.int32, (nb, L, tf), 1)

    def conv(u, st_ref, w_ref, b_ref):
        u3 = u.reshape(nb, L, tf)
        st = st_ref[...]
        prev2, prev1 = st[:, 0:1, :], st[:, 1:2, :]
        p1 = jnp.where(t == 0, prev1, pltpu.roll(u3, 1, 1))
        p2 = jnp.where(t == 0, prev2, jnp.where(t == 1, prev1, pltpu.roll(u3, 2, 1)))
        return (b_ref[...] + w_ref[0:1, :] * p2 + w_ref[1:2, :] * p1 + w_ref[2:3, :] * u3,
                u3[:, L - 2:L, :])

    ca, na = conv(ua, sta_ref, cwa_ref, cba_ref)
    cv, nv = conv(uv, stv_ref, cwv_ref, cbv_ref)
    na_ref[...] = na
    nv_ref[...] = nv
    act = (jax.nn.silu(ca) * cv).astype(BF16).reshape(M, tf)
    part = jnp.dot(act, wd_ref[...], preferred_element_type=F32)

    @pl.when(c == 0)
    def _():
        o_ref[...] = part

    @pl.when(c > 0)
    def _():
        o_ref[...] += part

    @pl.when(c == nc - 1)
    def _():
        o_ref[...] = x_ref[...] + _rms(o_ref[...], gpost_ref[...])


def ffn_sample(x, layer, g_pre, w_up, conv_w, conv_b, w_down, g_post, conv_state, tf):
    M, D = x.shape
    nbatch = M // DEC_SEQ
    nc = D_FF // tf
    conv_b3 = conv_b.reshape(DEPTH, 1, 2 * D_FF)
    y, na, nv = pl.pallas_call(
        functools.partial(_ffn_sample_kernel, nc=nc),
        grid=(nc,),
        in_specs=[
            pl.BlockSpec((M, D), lambda c: (0, 0)),
            pl.BlockSpec((None, 1, D), lambda c: (layer, 0, 0)),
            pl.BlockSpec((None, D, tf), lambda c: (layer, 0, c)),
            pl.BlockSpec((None, D, tf), lambda c: (layer, 0, nc + c)),
            pl.BlockSpec((None, 3, tf), lambda c: (layer, 0, c)),
            pl.BlockSpec((None, 3, tf), lambda c: (layer, 0, nc + c)),
            pl.BlockSpec((None, 1, tf), lambda c: (layer, 0, c)),
            pl.BlockSpec((None, 1, tf), lambda c: (layer, 0, nc + c)),
            pl.BlockSpec((None, tf, D), lambda c: (layer, c, 0)),
            pl.BlockSpec((None, 1, D), lambda c: (layer, 0, 0)),
            pl.BlockSpec((None, nbatch, 2, tf), lambda c: (layer, 0, 0, c)),
            pl.BlockSpec((None, nbatch, 2, tf), lambda c: (layer, 0, 0, nc + c)),
        ],
        out_specs=[
            pl.BlockSpec((M, D), lambda c: (0, 0)),
            pl.BlockSpec((nbatch, 2, tf), lambda c: (0, 0, c)),
            pl.BlockSpec((nbatch, 2, tf), lambda c: (0, 0, c)),
        ],
        out_shape=[
            jax.ShapeDtypeStruct((M, D), F32),
            jax.ShapeDtypeStruct((nbatch, 2, D_FF), F32),
            jax.ShapeDtypeStruct((nbatch, 2, D_FF), F32),
        ],
        scratch_shapes=[pltpu.VMEM((M, D), BF16)],
        compiler_params=_params("arbitrary"),
        name="ffn_sample",
    )(x, g_pre.reshape(DEPTH, 1, D), w_up, w_up, conv_w, conv_w, conv_b3, conv_b3,
      w_down, g_post.reshape(DEPTH, 1, D), conv_state, conv_state)
    return y, jnp.concatenate([na, nv], axis=-1)


def kernel(x_prompt, x_sample, mem_prompt, cache_swa_k, cache_swa_v, state_ssm_re, state_ssm_im, state_ffn_conv, cache_mem_k, cache_mem_v, g_mix_pre, g_mix_post, w_qkv, w_attn_o, attn_sinks, w_ssm_in, ssm_lambda_re, ssm_lambda_im, ssm_log_step, ssm_b_re, ssm_b_im, ssm_c_re, ssm_c_im, ssm_d, w_ssm_glu, g_x_pre, g_x_post, g_mem, w_x_q, w_mem_k, w_mem_v, w_x_o, g_ffn_pre, g_ffn_post, w_ffn_up, ffn_conv_w, ffn_conv_b, w_ffn_down):
    B = x_prompt.shape[0]
    SB = x_sample.shape[0]
    xw = N_XHEADS * XHEAD_DIM
    nkc = N_KV_HEADS * HEAD_DIM
    xp = x_prompt.reshape(B * SEQ, D_MODEL)
    xs = x_sample.reshape(SB * DEC_SEQ, D_MODEL)
    mem = mem_prompt.reshape(B * N_MEM, D_MODEL)
    MS = SB * DEC_SEQ
    TMP = 1024
    cos_p, sin_p = _rope_tables(jnp.arange(SEQ))
    cos_s, sin_s = _rope_tables(PAST_LEN + jnp.arange(DEC_SEQ))
    ck_all = cache_swa_k.reshape(cache_swa_k.shape[0], SB, WINDOW, nkc)
    cv_all = cache_swa_v.reshape(cache_swa_v.shape[0], SB, WINDOW, nkc)
    cmk_all = cache_mem_k.reshape(DEPTH, SB, N_MEM, xw)
    cmv_all = cache_mem_v.reshape(DEPTH, SB, N_MEM, xw)
    w_up_bf = w_ffn_up.astype(BF16)
    w_down_bf = w_ffn_down.astype(BF16)
    w_qkv_bf = w_qkv.astype(BF16)
    w_attn_o_bf = w_attn_o.astype(BF16)
    w_ssm_in_bf = w_ssm_in.astype(BF16)
    w_glu_bf = w_ssm_glu.astype(BF16)
    w_x_q_bf = w_x_q.astype(BF16)
    w_x_o_bf = w_x_o.astype(BF16)
    w_mem_kv_bf = jnp.concatenate([w_mem_k, w_mem_v], axis=-1).astype(BF16)
    TD = 512

    swa_kp, swa_vp, swa_ks, swa_vs = [], [], [], []
    ssm_rp, ssm_ip, ssm_rs, ssm_is = [], [], [], []
    conv_p, conv_s, memk_p, memv_p = [], [], [], []
    for i in range(DEPTH):
        j = i // 2
        if i % 2 == 0:
            qkv_p = norm_matmul(xp, g_mix_pre, i, w_qkv_bf, j, TD)
            qkv_s = norm_matmul(xs, g_mix_pre, i, w_qkv_bf, j, MS)
            op, kp, vp = swa_prompt(qkv_p, attn_sinks, j, cos_p, sin_p, B)
            sink_rows = jnp.repeat(attn_sinks[j].reshape(N_KV_HEADS, GQA_GROUP), DEC_SEQ,
                                   axis=1)[..., None]
            os_, kn, vn = swa_sample(qkv_s, ck_all, cv_all, j, sink_rows, cos_s, sin_s)
            swa_kp.append(kp.reshape(B, WINDOW, N_KV_HEADS, HEAD_DIM))
            swa_vp.append(vp.reshape(B, WINDOW, N_KV_HEADS, HEAD_DIM))
            swa_ks.append(kn.reshape(SB, WINDOW, N_KV_HEADS, HEAD_DIM))
            swa_vs.append(vn.reshape(SB, WINDOW, N_KV_HEADS, HEAD_DIM))
            xp = matmul_post(op, w_attn_o_bf, j, g_mix_post, i, xp, TD)
            xs = matmul_post(os_, w_attn_o_bf, j, g_mix_post, i, xs, MS)
        else:
            a_re, a_im, b_blk, c_blk = _s5_weights(
                ssm_lambda_re[j], ssm_lambda_im[j], ssm_log_step[j], ssm_b_re[j], ssm_b_im[j],
                ssm_c_re[j], ssm_c_im[j])
            d_row = ssm_d[j].reshape(1, D_MODEL)
            up_ = norm_matmul(xp, g_mix_pre, i, w_ssm_in_bf, j, TD)
            us_ = norm_matmul(xs, g_mix_pre, i, w_ssm_in_bf, j, MS)
            a_tiles = jnp.stack([a_re.reshape(N_SUPER, SUBLANES, LANES),
                                 a_im.reshape(N_SUPER, SUBLANES, LANES)], axis=1
                                ).reshape(2 * N_SUPER, SUBLANES, LANES)
            zero_state = jnp.zeros((B, N_SUPER, SUBLANES, LANES), F32)
            zp, rp, ip = s5_prompt(up_, zero_state, zero_state, a_tiles, b_blk, c_blk, d_row,
                                   B, 256)
            nstate = N_SSM_GROUPS * SSM_STATE
            zs, rn, im_ = s5_sample(us_, state_ssm_re[j].reshape(SB, nstate),
                                    state_ssm_im[j].reshape(SB, nstate),
                                    a_re.reshape(1, nstate), a_im.reshape(1, nstate),
                                    b_blk, c_blk, d_row)
            ssm_rp.append(rp.reshape(B, N_SSM_GROUPS, SSM_STATE))
            ssm_ip.append(ip.reshape(B, N_SSM_GROUPS, SSM_STATE))
            ssm_rs.append(rn.reshape(SB, N_SSM_GROUPS, SSM_STATE))
            ssm_is.append(im_.reshape(SB, N_SSM_GROUPS, SSM_STATE))
            xp = glu_post(zp, w_glu_bf, j, g_mix_post, i, xp, TD)
            xs = glu_post(zs, w_glu_bf, j, g_mix_post, i, xs, MS)
        mkv = norm_matmul(mem, g_mem, i, w_mem_kv_bf, i, B * N_MEM)
        mk, mv = mkv[:, :xw], mkv[:, xw:]
        memk_p.append(mk.reshape(B, N_MEM, N_XHEADS, XHEAD_DIM))
        memv_p.append(mv.reshape(B, N_MEM, N_XHEADS, XHEAD_DIM))
        xp = xattn_prompt(xp, i, g_x_pre, w_x_q_bf, mk, mv, w_x_o_bf, g_x_post, B, TD)
        qs = norm_matmul(xs, g_x_pre, i, w_x_q_bf, i, MS)
        as_ = xattn_sample(qs, cmk_all, cmv_all, i)
        xs = matmul_post(as_, w_x_o_bf, i, g_x_post, i, xs, MS)
        xp, cp = ffn_prompt(xp, i, g_ffn_pre, w_up_bf, ffn_conv_w, ffn_conv_b, w_down_bf,
                            g_ffn_post, B, TMP, 512)
        xs, cs = ffn_sample(xs, i, g_ffn_pre, w_up_bf, ffn_conv_w, ffn_conv_b, w_down_bf,
                            g_ffn_post, state_ffn_conv, 1408)
        conv_p.append(cp)
        conv_s.append(cs)
    return (xp.reshape(B, SEQ, D_MODEL), xs.reshape(SB, DEC_SEQ, D_MODEL),
            jnp.stack(swa_kp), jnp.stack(swa_vp), jnp.stack(swa_ks), jnp.stack(swa_vs),
            jnp.stack(ssm_rp), jnp.stack(ssm_ip), jnp.stack(ssm_rs), jnp.stack(ssm_is),
            jnp.stack(conv_p), jnp.stack(conv_s), jnp.stack(memk_p), jnp.stack(memv_p))
```

```python
import functools
import math

import jax
import jax.numpy as jnp
from jax import lax
from jax.experimental import pallas as pl
from jax.experimental.pallas import tpu as pltpu

D_MODEL = 2048
SEQ = 4096
DEPTH = 4
DEC_SEQ = 8
PAST_LEN = 16384
HEAD_DIM = 64
N_HEADS = 32
N_KV_HEADS = 4
GQA_GROUP = 8
WINDOW = 128
ROPE_THETA = 10000.0
SSM_GROUP = 16
N_SSM_GROUPS = 128
SSM_STATE = 64
N_MEM = 256
N_XHEADS = 4
XHEAD_DIM = 128
D_FF = 5632
NORM_EPS = 1e-6
NEG_INF = -1e30

F32 = jnp.float32
BF16 = jnp.bfloat16

V7X_VMEM_BYTES = 64 * 1024 * 1024
VMEM_LIMIT = V7X_VMEM_BYTES - 8 * 1024 * 1024
LANES = 128
SUBLANES = 8
SSM_SUPER = 16
N_SUPER = N_SSM_GROUPS // SSM_SUPER
SUPER_IN = SSM_SUPER * SSM_GROUP
SUPER_STATE = SSM_SUPER * SSM_STATE
SLABS = SUPER_STATE // LANES


def _params(*sem, vmem=VMEM_LIMIT):
    return pltpu.CompilerParams(dimension_semantics=sem, vmem_limit_bytes=vmem)


def _rms(x, g):
    ms = jnp.mean(x * x, axis=-1, keepdims=True)
    return x * lax.rsqrt(ms + NORM_EPS) * g


def _bdot(a, b):
    return jnp.dot(a.astype(BF16), b.astype(BF16), preferred_element_type=F32)


def _once(shape, index_map):
    return pl.BlockSpec(shape, index_map, pipeline_mode=pl.Buffered(1))


ROW_GROUP = 256


def _resident(shape, index_map):
    return pl.BlockSpec(shape, index_map, pipeline_mode=pl.Buffered(1))


def _row_groups(tm):
    rg = min(ROW_GROUP, tm)
    return [slice(k * rg, (k + 1) * rg) for k in range(tm // rg)]


def _norm_matmul_kernel(x_ref, g_ref, w_ref, o_ref):
    g = g_ref[...]
    for rows in _row_groups(x_ref.shape[0]):
        hn = _rms(x_ref[rows, :], g).astype(BF16)
        o_ref[rows, :] = jnp.dot(hn, w_ref[...], preferred_element_type=F32).astype(o_ref.dtype)


def norm_matmul(x, g, g_layer, w, w_layer, tm, out_dtype=F32):
    M, D = x.shape
    N = w.shape[-1]
    return pl.pallas_call(
        _norm_matmul_kernel,
        grid=(M // tm,),
        in_specs=[
            pl.BlockSpec((tm, D), lambda i: (i, 0)),
            pl.BlockSpec((None, 1, D), lambda i: (g_layer, 0, 0)),
            _resident((None, D, N), lambda i: (w_layer, 0, 0)),
        ],
        out_specs=pl.BlockSpec((tm, N), lambda i: (i, 0)),
        out_shape=jax.ShapeDtypeStruct((M, N), out_dtype),
        compiler_params=_params("arbitrary"),
        name="norm_matmul",
    )(x, g.reshape(g.shape[0], 1, D), w)


def _matmul_post_kernel(a_ref, w_ref, g_ref, x_ref, o_ref):
    g = g_ref[...]
    for rows in _row_groups(x_ref.shape[0]):
        y = jnp.dot(a_ref[rows, :].astype(BF16), w_ref[...], preferred_element_type=F32)
        o_ref[rows, :] = x_ref[rows, :] + _rms(y, g)


def matmul_post(a, w, w_layer, g, g_layer, x, tm):
    M, K = a.shape
    D = w.shape[-1]
    return pl.pallas_call(
        _matmul_post_kernel,
        grid=(M // tm,),
        in_specs=[
            pl.BlockSpec((tm, K), lambda i: (i, 0)),
            _resident((None, K, D), lambda i: (w_layer, 0, 0)),
            pl.BlockSpec((None, 1, D), lambda i: (g_layer, 0, 0)),
            pl.BlockSpec((tm, D), lambda i: (i, 0)),
        ],
        out_specs=pl.BlockSpec((tm, D), lambda i: (i, 0)),
        out_shape=jax.ShapeDtypeStruct((M, D), F32),
        compiler_params=_params("arbitrary"),
        name="matmul_post",
    )(a, w, g.reshape(g.shape[0], 1, D), x)


GLU_COLS = 512


def _glu_post_kernel(z_ref, w_ref, g_ref, x_ref, o_ref, y_ref):
    D = o_ref.shape[1]
    g = g_ref[...]
    for rows in _row_groups(x_ref.shape[0]):
        z = z_ref[rows, :]
        for n in range(D // GLU_COLS):
            cols = slice(n * GLU_COLS, (n + 1) * GLU_COLS)
            gcols = slice(D + n * GLU_COLS, D + (n + 1) * GLU_COLS)
            val = jnp.dot(z, w_ref[:, cols], preferred_element_type=F32)
            gate = jnp.dot(z, w_ref[:, gcols], preferred_element_type=F32)
            y_ref[rows, cols] = val * jax.nn.sigmoid(gate)
        o_ref[rows, :] = x_ref[rows, :] + _rms(y_ref[rows, :], g)


def glu_post(z, w, w_layer, g, g_layer, x, tm):
    M, K = z.shape
    D = D_MODEL
    return pl.pallas_call(
        _glu_post_kernel,
        grid=(M // tm,),
        in_specs=[
            pl.BlockSpec((tm, K), lambda i: (i, 0)),
            _resident((None, K, 2 * D), lambda i: (w_layer, 0, 0)),
            pl.BlockSpec((None, 1, D), lambda i: (g_layer, 0, 0)),
            pl.BlockSpec((tm, D), lambda i: (i, 0)),
        ],
        out_specs=pl.BlockSpec((tm, D), lambda i: (i, 0)),
        out_shape=jax.ShapeDtypeStruct((M, D), F32),
        scratch_shapes=[pltpu.VMEM((tm, D), F32)],
        compiler_params=_params("arbitrary"),
        name="glu_post",
    )(z, w, g.reshape(g.shape[0], 1, D), x)


def _rope_chunk(xc, cos, sin_signed):
    lane = lax.broadcasted_iota(jnp.int32, xc.shape, 1)
    first_half = (lane & (HEAD_DIM - 1)) < (HEAD_DIM // 2)
    partner = jnp.where(first_half, pltpu.roll(xc, LANES - HEAD_DIM // 2, 1),
                        pltpu.roll(xc, HEAD_DIM // 2, 1))
    return xc * cos + partner * sin_signed


def _rope_tables(pos):
    half = HEAD_DIM // 2
    inv_freq = ROPE_THETA ** (-jnp.arange(half, dtype=F32) * 2.0 / HEAD_DIM)
    ang = pos.astype(F32)[:, None] * inv_freq[None, :]
    cos, sin = jnp.cos(ang), jnp.sin(ang)
    cos128 = jnp.tile(cos, (1, LANES // half))
    sin128 = jnp.tile(jnp.concatenate([-sin, sin], axis=1), (1, LANES // HEAD_DIM))
    return cos128, sin128


def _swa_prompt_kernel(sink_ref, q_ref, kv_ref, cos_ref, sin_ref,
                       o_ref, kc_ref, vc_ref, kk_ref, vv_ref, *, layer):
    i = pl.program_id(1)
    blk = WINDOW
    cos, sn = cos_ref[...], sin_ref[...]
    kv = kv_ref[...]
    nkc = N_KV_HEADS * HEAD_DIM
    k = jnp.concatenate([_rope_chunk(kv[:, c * LANES:(c + 1) * LANES], cos, sn)
                         for c in range(nkc // LANES)], axis=1)
    v = kv[:, nkc:2 * nkc]
    kc_ref[...] = k
    vc_ref[...] = v

    cur = i % 2
    prv = 1 - cur

    @pl.when(i == 0)
    def _():
        kk_ref[1] = jnp.zeros((blk, nkc), BF16)
        vv_ref[1] = jnp.zeros((blk, nkc), BF16)

    kb, vb = k.astype(BF16), v.astype(BF16)
    kk_ref[cur] = kb
    vv_ref[cur] = vb
    kk = jnp.concatenate([kk_ref[prv], kb], axis=0)
    vv = jnp.concatenate([vv_ref[prv], vb], axis=0)
    rows = GQA_GROUP * blk
    qi = lax.broadcasted_iota(jnp.int32, (rows, 2 * blk), 0) & (blk - 1)
    si = lax.broadcasted_iota(jnp.int32, (rows, 2 * blk), 1)
    d = blk + qi - si
    mask = (d >= 0) & (d < WINDOW) & ((si >= blk) | (i > 0))

    chunks_per_group = GQA_GROUP * HEAD_DIM // LANES

    def scores(kh):
        hs = slice(kh * HEAD_DIM, (kh + 1) * HEAD_DIM)
        pieces = []
        for c in range(kh * chunks_per_group, (kh + 1) * chunks_per_group):
            qc = (_rope_chunk(q_ref[:, c * LANES:(c + 1) * LANES], cos, sn)
                  * (HEAD_DIM ** -0.5)).astype(BF16)
            pieces += [qc[:, hh * HEAD_DIM:(hh + 1) * HEAD_DIM] for hh in range(LANES // HEAD_DIM)]
        qs = jnp.concatenate(pieces, axis=0)
        return lax.dot_general(qs, kk[:, hs], (((1,), (1,)), ((), ())),
                               preferred_element_type=F32)

    ahead = 2
    s_all = [scores(kh) for kh in range(min(ahead, N_KV_HEADS))]
    outs = []
    for kh in range(N_KV_HEADS):
        if kh + ahead < N_KV_HEADS:
            s_all.append(scores(kh + ahead))
        hs = slice(kh * HEAD_DIM, (kh + 1) * HEAD_DIM)
        sk = jnp.concatenate([jnp.full((blk, 1), sink_ref[layer, kh * GQA_GROUP + j], F32)
                              for j in range(GQA_GROUP)], axis=0)
        s = jnp.where(mask, s_all[kh], NEG_INF)
        mx = jnp.maximum(jnp.max(s, axis=1, keepdims=True), sk)
        p = jnp.exp(s - mx)
        den = jnp.sum(p, axis=1, keepdims=True) + jnp.exp(sk - mx)
        o = jnp.dot(p.astype(BF16), vv[:, hs], preferred_element_type=F32) / den
        outs += [o[j * blk:(j + 1) * blk, :] for j in range(GQA_GROUP)]
    o_ref[...] = jnp.concatenate(outs, axis=1).astype(o_ref.dtype)


def swa_prompt(qkv, sinks, layer, cos, sin, batch):
    blk = WINDOW
    nb = SEQ // blk
    nq = N_HEADS * HEAD_DIM
    nkv = 2 * N_KV_HEADS * HEAD_DIM
    o, kc, vc = pl.pallas_call(
        functools.partial(_swa_prompt_kernel, layer=layer),
        grid=(batch, nb),
        in_specs=[
            pl.BlockSpec(memory_space=pltpu.SMEM),
            pl.BlockSpec((blk, nq), lambda b, i: (b * nb + i, 0)),
            pl.BlockSpec((blk, nkv), lambda b, i: (b * nb + i, nq // nkv)),
            pl.BlockSpec((blk, LANES), lambda b, i: (i, 0)),
            pl.BlockSpec((blk, LANES), lambda b, i: (i, 0)),
        ],
        out_specs=[
            pl.BlockSpec((blk, nq), lambda b, i: (b * nb + i, 0)),
            pl.BlockSpec((None, blk, nkv // 2), lambda b, i: (b, 0, 0)),
            pl.BlockSpec((None, blk, nkv // 2), lambda b, i: (b, 0, 0)),
        ],
        out_shape=[
            jax.ShapeDtypeStruct((batch * SEQ, nq), BF16),
            jax.ShapeDtypeStruct((batch, blk, nkv // 2), F32),
            jax.ShapeDtypeStruct((batch, blk, nkv // 2), F32),
        ],
        scratch_shapes=[pltpu.VMEM((2, blk, nkv // 2), BF16),
                        pltpu.VMEM((2, blk, nkv // 2), BF16)],
        compiler_params=_params("arbitrary", "arbitrary"),
        name="swa_prompt",
    )(sinks, qkv, qkv, cos, sin)
    return o, kc, vc


def _swa_sample_kernel(sink_ref, qkv_ref, ck_ref, cv_ref, cos_ref, sin_ref,
                       o_ref, nk_ref, nv_ref, *, nb):
    L = DEC_SEQ
    wb = WINDOW
    nq = N_HEADS * HEAD_DIM
    nkc = N_KV_HEADS * HEAD_DIM
    cos, sn = cos_ref[...], sin_ref[...]
    rows = GQA_GROUP * L
    t_q = lax.broadcasted_iota(jnp.int32, (rows, wb), 0) % L
    c_k = lax.broadcasted_iota(jnp.int32, (rows, wb), 1)
    mask_c = c_k >= t_q + 1
    t_q2 = lax.broadcasted_iota(jnp.int32, (rows, L), 0) % L
    t_k2 = lax.broadcasted_iota(jnp.int32, (rows, L), 1)
    mask_n = t_k2 <= t_q2
    for b in range(nb):
        x = qkv_ref[b * L:(b + 1) * L, :]
        qr = jnp.concatenate([_rope_chunk(x[:, c * LANES:(c + 1) * LANES], cos, sn)
                              for c in range(nq // LANES)], axis=1) * (HEAD_DIM ** -0.5)
        kn = jnp.concatenate([_rope_chunk(x[:, nq + c * LANES:nq + (c + 1) * LANES], cos, sn)
                              for c in range(nkc // LANES)], axis=1)
        vn = x[:, nq + nkc:nq + 2 * nkc]
        ck = ck_ref[b]
        cv = cv_ref[b]
        nk_ref[b, 0:wb - L, :] = ck[L:wb, :]
        nk_ref[b, wb - L:wb, :] = kn
        nv_ref[b, 0:wb - L, :] = cv[L:wb, :]
        nv_ref[b, wb - L:wb, :] = vn
        ckb, cvb, knb, vnb = ck.astype(BF16), cv.astype(BF16), kn.astype(BF16), vn.astype(BF16)
        dn = (((1,), (1,)), ((), ()))
        scores = []
        for kh in range(N_KV_HEADS):
            hs = slice(kh * HEAD_DIM, (kh + 1) * HEAD_DIM)
            qs = jnp.concatenate(
                [qr[:, (kh * GQA_GROUP + j) * HEAD_DIM:(kh * GQA_GROUP + j + 1) * HEAD_DIM]
                 for j in range(GQA_GROUP)], axis=0).astype(BF16)
            scores.append((lax.dot_general(qs, ckb[:, hs], dn, preferred_element_type=F32),
                           lax.dot_general(qs, knb[:, hs], dn, preferred_element_type=F32)))
        outs = []
        for kh in range(N_KV_HEADS):
            hs = slice(kh * HEAD_DIM, (kh + 1) * HEAD_DIM)
            s_c = jnp.where(mask_c, scores[kh][0], NEG_INF)
            s_n = jnp.where(mask_n, scores[kh][1], NEG_INF)
            sk = sink_ref[kh]
            mx = jnp.maximum(jnp.maximum(jnp.max(s_c, axis=1, keepdims=True),
                                         jnp.max(s_n, axis=1, keepdims=True)), sk)
            p_c = jnp.exp(s_c - mx)
            p_n = jnp.exp(s_n - mx)
            den = (jnp.sum(p_c, axis=1, keepdims=True) + jnp.sum(p_n, axis=1, keepdims=True)
                   + jnp.exp(sk - mx))
            o = (jnp.dot(p_c.astype(BF16), cvb[:, hs], preferred_element_type=F32)
                 + jnp.dot(p_n.astype(BF16), vnb[:, hs], preferred_element_type=F32)) / den
            outs.extend(o[j * L:(j + 1) * L, :] for j in range(GQA_GROUP))
        o_ref[b * L:(b + 1) * L, :] = jnp.concatenate(outs, axis=1).astype(o_ref.dtype)


def swa_sample(qkv, cache_k, cache_v, layer, sink_rows, cos, sin, nb=8):
    nbatch = cache_k.shape[1]
    nq = N_HEADS * HEAD_DIM
    nkc = N_KV_HEADS * HEAD_DIM
    L = DEC_SEQ
    return pl.pallas_call(
        functools.partial(_swa_sample_kernel, nb=nb),
        grid=(nbatch // nb,),
        in_specs=[
            pl.BlockSpec((N_KV_HEADS, GQA_GROUP * L, 1), lambda g: (0, 0, 0)),
            pl.BlockSpec((nb * L, nq + 2 * nkc), lambda g: (g, 0)),
            pl.BlockSpec((None, nb, WINDOW, nkc), lambda g: (layer, g, 0, 0)),
            pl.BlockSpec((None, nb, WINDOW, nkc), lambda g: (layer, g, 0, 0)),
            pl.BlockSpec((L, LANES), lambda g: (0, 0)),
            pl.BlockSpec((L, LANES), lambda g: (0, 0)),
        ],
        out_specs=[
            pl.BlockSpec((nb * L, nq), lambda g: (g, 0)),
            pl.BlockSpec((nb, WINDOW, nkc), lambda g: (g, 0, 0)),
            pl.BlockSpec((nb, WINDOW, nkc), lambda g: (g, 0, 0)),
        ],
        out_shape=[
            jax.ShapeDtypeStruct((nbatch * L, nq), BF16),
            jax.ShapeDtypeStruct((nbatch, WINDOW, nkc), F32),
            jax.ShapeDtypeStruct((nbatch, WINDOW, nkc), F32),
        ],
        compiler_params=_params("arbitrary"),
        name="swa_sample",
    )(sink_rows, qkv, cache_k, cache_v, cos, sin)


_XHEADS = [slice(h * XHEAD_DIM, (h + 1) * XHEAD_DIM) for h in range(N_XHEADS)]


def _xattn_scores(q, mk):
    return [lax.dot_general((q[:, hs] * (XHEAD_DIM ** -0.5)).astype(BF16), mk[:, hs].astype(BF16),
                            (((1,), (1,)), ((), ())), preferred_element_type=F32)
            for hs in _XHEADS]


def _xattn_attend(scores, mv):
    outs = []
    for hs, s in zip(_XHEADS, scores):
        mx = jnp.max(s, axis=1, keepdims=True)
        p = jnp.exp(s - mx)
        den = jnp.sum(p, axis=1, keepdims=True)
        outs.append(jnp.dot(p.astype(BF16), mv[:, hs].astype(BF16),
                            preferred_element_type=F32) / den)
    return jnp.concatenate(outs, axis=1)


def _xattn_heads(q, mk, mv):
    return _xattn_attend(_xattn_scores(q, mk), mv)


def _xattn_prompt_kernel(x_ref, gpre_ref, wq_ref, mk_ref, mv_ref, wo_ref, gpost_ref, o_ref):
    mk = mk_ref[...].astype(BF16)
    mv = mv_ref[...].astype(BF16)
    gpre, gpost = gpre_ref[...], gpost_ref[...]
    groups = _row_groups(x_ref.shape[0])

    def project(rows):
        q = jnp.dot(_rms(x_ref[rows, :], gpre).astype(BF16), wq_ref[...],
                    preferred_element_type=F32)
        return _xattn_scores(q, mk)

    scores = project(groups[0])
    for k, rows in enumerate(groups):
        nxt = project(groups[k + 1]) if k + 1 < len(groups) else None
        a = _xattn_attend(scores, mv).astype(BF16)
        y = jnp.dot(a, wo_ref[...], preferred_element_type=F32)
        o_ref[rows, :] = x_ref[rows, :] + _rms(y, gpost)
        scores = nxt


def xattn_prompt(x, layer, g_pre, w_q, mk, mv, w_o, g_post, batch, tm):
    nq = SEQ // tm
    xw = N_XHEADS * XHEAD_DIM
    D = D_MODEL
    return pl.pallas_call(
        _xattn_prompt_kernel,
        grid=(batch, nq),
        in_specs=[
            pl.BlockSpec((tm, D), lambda b, i: (b * nq + i, 0)),
            pl.BlockSpec((None, 1, D), lambda b, i: (layer, 0, 0)),
            _resident((None, D, xw), lambda b, i: (layer, 0, 0)),
            pl.BlockSpec((N_MEM, xw), lambda b, i: (b, 0)),
            pl.BlockSpec((N_MEM, xw), lambda b, i: (b, 0)),
            _resident((None, xw, D), lambda b, i: (layer, 0, 0)),
            pl.BlockSpec((None, 1, D), lambda b, i: (layer, 0, 0)),
        ],
        out_specs=pl.BlockSpec((tm, D), lambda b, i: (b * nq + i, 0)),
        out_shape=jax.ShapeDtypeStruct((batch * SEQ, D), F32),
        compiler_params=_params("arbitrary", "arbitrary"),
        name="xattn_prompt",
    )(x, g_pre.reshape(DEPTH, 1, D), w_q, mk, mv, w_o, g_post.reshape(DEPTH, 1, D))


def _xattn_sample_kernel(q_ref, mk_ref, mv_ref, o_ref, *, nb):
    L = DEC_SEQ
    for b in range(nb):
        o_ref[b * L:(b + 1) * L, :] = _xattn_heads(
            q_ref[b * L:(b + 1) * L, :], mk_ref[b], mv_ref[b]).astype(o_ref.dtype)


def xattn_sample(q, cache_k, cache_v, layer, nb=8):
    nbatch = cache_k.shape[1]
    L = DEC_SEQ
    xw = N_XHEADS * XHEAD_DIM
    return pl.pallas_call(
        functools.partial(_xattn_sample_kernel, nb=nb),
        grid=(nbatch // nb,),
        in_specs=[
            pl.BlockSpec((nb * L, xw), lambda g: (g, 0)),
            pl.BlockSpec((None, nb, N_MEM, xw), lambda g: (layer, g, 0, 0)),
            pl.BlockSpec((None, nb, N_MEM, xw), lambda g: (layer, g, 0, 0)),
        ],
        out_specs=pl.BlockSpec((nb * L, xw), lambda g: (g, 0)),
        out_shape=jax.ShapeDtypeStruct((nbatch * L, xw), BF16),
        compiler_params=_params("arbitrary"),
        name="xattn_sample",
    )(q, cache_k, cache_v)


def _s5_prompt_kernel(u_ref, h0r_ref, h0i_ref, a_ref, b_ref, c_ref, d_ref,
                      z_ref, hr_ref, hi_ref, x_ref, hst_ref, *, T):
    c = pl.program_id(1)
    nslab = 2 * SLABS
    G = T // SUBLANES

    @pl.when(c == 0)
    def _():
        for sb in range(N_SUPER):
            hst_ref[2 * sb] = h0r_ref[sb]
            hst_ref[2 * sb + 1] = h0i_ref[sb]

    def slab_rows(sb, s):
        return pl.ds((sb * nslab + s) * S5_SLAB_PITCH, SUBLANES)

    for sb in range(N_SUPER):
        bu = _bdot(u_ref[:, sb * SUPER_IN:(sb + 1) * SUPER_IN], b_ref[sb])
        for s in range(nslab):
            x_ref[:, slab_rows(sb, s), :] = bu[:, s * LANES:(s + 1) * LANES].reshape(
                G, SUBLANES, LANES)

    a = [a_ref[k] for k in range(2 * N_SUPER)]
    h_init = tuple(hst_ref[k] for k in range(2 * N_SUPER))

    def step(t, h):
        g = lax.shift_right_logical(t, 3)
        r = lax.bitwise_and(t, SUBLANES - 1)
        new = []
        for sb in range(N_SUPER):
            ir = pl.ds((sb * nslab) * S5_SLAB_PITCH + r, SUBLANES, stride=S5_SLAB_PITCH)
            ii = pl.ds((sb * nslab + SLABS) * S5_SLAB_PITCH + r, SUBLANES, stride=S5_SLAB_PITCH)
            ar, ai = a[2 * sb], a[2 * sb + 1]
            hr, hi = h[2 * sb], h[2 * sb + 1]
            nr = ar * hr - ai * hi + x_ref[g, ir, :]
            ni = ar * hi + ai * hr + x_ref[g, ii, :]
            x_ref[g, ir, :] = nr
            x_ref[g, ii, :] = ni
            new += [nr, ni]
        return tuple(new)

    h_fin = lax.fori_loop(0, T, step, h_init)
    for sb in range(N_SUPER):
        hst_ref[2 * sb] = h_fin[2 * sb]
        hst_ref[2 * sb + 1] = h_fin[2 * sb + 1]
        hr_ref[sb] = h_fin[2 * sb]
        hi_ref[sb] = h_fin[2 * sb + 1]

    for sb in range(N_SUPER):
        hcat = jnp.concatenate([x_ref[:, slab_rows(sb, s), :].reshape(T, LANES)
                                for s in range(nslab)], axis=1)
        cols = slice(sb * SUPER_IN, (sb + 1) * SUPER_IN)
        y = _bdot(hcat, c_ref[sb]) + d_ref[:, cols] * u_ref[:, cols]
        z_ref[:, cols] = jax.nn.gelu(y).astype(z_ref.dtype)


S5_SLAB_PITCH = 12


def s5_prompt(u, h0r, h0i, a_tiles, b_blk, c_blk, d_row, batch, T):
    nc = SEQ // T
    st = (None, N_SUPER, SUBLANES, LANES)
    return pl.pallas_call(
        functools.partial(_s5_prompt_kernel, T=T),
        grid=(batch, nc),
        in_specs=[
            pl.BlockSpec((T, D_MODEL), lambda b, c: (b * nc + c, 0)),
            pl.BlockSpec(st, lambda b, c: (b, 0, 0, 0)),
            pl.BlockSpec(st, lambda b, c: (b, 0, 0, 0)),
            _once((2 * N_SUPER, SUBLANES, LANES), lambda b, c: (0, 0, 0)),
            _once((N_SUPER, SUPER_IN, 2 * SUPER_STATE), lambda b, c: (0, 0, 0)),
            _once((N_SUPER, 2 * SUPER_STATE, SUPER_IN), lambda b, c: (0, 0, 0)),
            _once((1, D_MODEL), lambda b, c: (0, 0)),
        ],
        out_specs=[
            pl.BlockSpec((T, D_MODEL), lambda b, c: (b * nc + c, 0)),
            pl.BlockSpec(st, lambda b, c: (b, 0, 0, 0)),
            pl.BlockSpec(st, lambda b, c: (b, 0, 0, 0)),
        ],
        out_shape=[
            jax.ShapeDtypeStruct((batch * SEQ, D_MODEL), BF16),
            jax.ShapeDtypeStruct((batch, N_SUPER, SUBLANES, LANES), F32),
            jax.ShapeDtypeStruct((batch, N_SUPER, SUBLANES, LANES), F32),
        ],
        scratch_shapes=[
            pltpu.VMEM((T // SUBLANES, N_SUPER * 2 * SLABS * S5_SLAB_PITCH, LANES), F32),
            pltpu.VMEM((2 * N_SUPER, SUBLANES, LANES), F32),
        ],
        compiler_params=_params("arbitrary", "arbitrary"),
        name="s5_prompt",
    )(u, h0r, h0i, a_tiles, b_blk, c_blk, d_row)


def _s5_sample_kernel(u_ref, h0r_ref, h0i_ref, ar_ref, ai_ref, b_ref, c_ref, d_ref,
                      z_ref, hr_ref, hi_ref, x_ref, *, nbatch):
    L = DEC_SEQ
    R = nbatch * L
    for sb in range(N_SUPER):
        cols = slice(sb * SUPER_IN, (sb + 1) * SUPER_IN)
        bu = _bdot(u_ref[:, cols], b_ref[sb])
        for s in range(2 * SLABS):
            x_ref[s * R:(s + 1) * R, :] = bu[:, s * LANES:(s + 1) * LANES]
        for s in range(SLABS):
            st = slice(sb * SUPER_STATE + s * LANES, sb * SUPER_STATE + (s + 1) * LANES)
            ar, ai = ar_ref[:, st], ai_ref[:, st]
            hr, hi = h0r_ref[:, st], h0i_ref[:, st]
            for t in range(L):
                rr = pl.ds(s * R + t, nbatch, stride=L)
                ri = pl.ds((SLABS + s) * R + t, nbatch, stride=L)
                nr = ar * hr - ai * hi + x_ref[rr, :]
                ni = ar * hi + ai * hr + x_ref[ri, :]
                x_ref[rr, :] = nr
                x_ref[ri, :] = ni
                hr, hi = nr, ni
            hr_ref[:, st] = hr
            hi_ref[:, st] = hi
        hcat = jnp.concatenate([x_ref[s * R:(s + 1) * R, :] for s in range(2 * SLABS)], axis=1)
        y = _bdot(hcat, c_ref[sb]) + d_ref[:, cols] * u_ref[:, cols]
        z_ref[:, cols] = jax.nn.gelu(y).astype(z_ref.dtype)


def s5_sample(u, h0r, h0i, a_re, a_im, b_blk, c_blk, d_row):
    rows = u.shape[0]
    nbatch = rows // DEC_SEQ
    nstate = N_SSM_GROUPS * SSM_STATE
    full = lambda shape: pl.BlockSpec(shape, lambda i: (0,) * len(shape))
    return pl.pallas_call(
        functools.partial(_s5_sample_kernel, nbatch=nbatch),
        grid=(1,),
        in_specs=[
            full((rows, D_MODEL)), full((nbatch, nstate)), full((nbatch, nstate)),
            full((1, nstate)), full((1, nstate)),
            full((N_SUPER, SUPER_IN, 2 * SUPER_STATE)),
            full((N_SUPER, 2 * SUPER_STATE, SUPER_IN)),
            full((1, D_MODEL)),
        ],
        out_specs=[full((rows, D_MODEL)), full((nbatch, nstate)), full((nbatch, nstate))],
        out_shape=[
            jax.ShapeDtypeStruct((rows, D_MODEL), BF16),
            jax.ShapeDtypeStruct((nbatch, nstate), F32),
            jax.ShapeDtypeStruct((nbatch, nstate), F32),
        ],
        scratch_shapes=[pltpu.VMEM((2 * SLABS * rows, LANES), F32)],
        compiler_params=_params("arbitrary"),
        name="s5_sample",
    )(u, h0r, h0i, a_re, a_im, b_blk, c_blk, d_row)


def _s5_weights(lam_re, lam_im, log_step, b_re, b_im, c_re, c_im):
    lam = lax.complex(lam_re.astype(F32), lam_im.astype(F32))
    dt = jnp.exp(log_step.astype(F32))[:, None]
    abar = jnp.exp(lam * dt)
    bbar = ((abar - 1.0) / lam)[..., None] * lax.complex(b_re.astype(F32), b_im.astype(F32))
    eye = jnp.eye(SSM_SUPER, dtype=F32)

    def b_layout(m):
        m = m.reshape(N_SUPER, SSM_SUPER, SSM_STATE, SSM_GROUP)
        return jnp.einsum('sgph,gk->sghkp', m, eye).reshape(N_SUPER, SUPER_IN, SUPER_STATE)

    def c_layout(m):
        m = m.reshape(N_SUPER, SSM_SUPER, SSM_GROUP, SSM_STATE)
        return jnp.einsum('sghp,gk->sgpkh', m, eye).reshape(N_SUPER, SUPER_STATE, SUPER_IN)

    b_blk = jnp.concatenate([b_layout(bbar.real), b_layout(bbar.imag)], axis=2).astype(BF16)
    c_blk = jnp.concatenate([c_layout(c_re.astype(F32)), -c_layout(c_im.astype(F32))],
                            axis=1).astype(BF16)
    return abar.real, abar.imag, b_blk, c_blk


def _ffn_prompt_kernel(x_ref, gpre_ref, wua_ref, wuv_ref, cwa_ref, cwv_ref, cba_ref, cbv_ref,
                       wd_ref, gpost_ref, o_ref, sa_ref, sv_ref,
                       hn_ref, ua_ref, uv_ref, act_ref, ha_ref, hv_ref,
                       *, nc, blocks_per_seq, rc, rm):
    i = pl.program_id(0)
    c = pl.program_id(1)
    tm, D = hn_ref.shape
    hdr = SUBLANES

    @pl.when(c == 0)
    def _():
        hn_ref[...] = _rms(x_ref[...], gpre_ref[...]).astype(BF16)
        o_ref[...] = jnp.zeros(o_ref.shape, F32)

    slot = i % 2

    @pl.when(i % blocks_per_seq == 0)
    def _():
        ha_ref[1 - slot, c] = jnp.zeros(ha_ref.shape[2:], F32)
        hv_ref[1 - slot, c] = jnp.zeros(hv_ref.shape[2:], F32)

    ua_ref[0:hdr, :] = ha_ref[1 - slot, c]
    uv_ref[0:hdr, :] = hv_ref[1 - slot, c]
    wa = wua_ref[...].astype(BF16)
    wv = wuv_ref[...].astype(BF16)
    wd = wd_ref[...].astype(BF16)

    def conv(buf_ref, w_ref, b_ref, r0):
        return (b_ref[...] + w_ref[0:1, :] * buf_ref[r0 + hdr - 2:r0 + hdr - 2 + rc, :]
                + w_ref[1:2, :] * buf_ref[r0 + hdr - 1:r0 + hdr - 1 + rc, :]
                + w_ref[2:3, :] * buf_ref[r0 + hdr:r0 + hdr + rc, :])

    def up_proj(k):
        hk = hn_ref[k * rm:(k + 1) * rm, :]
        ua_ref[hdr + k * rm:hdr + (k + 1) * rm, :] = jnp.dot(hk, wa, preferred_element_type=F32)
        uv_ref[hdr + k * rm:hdr + (k + 1) * rm, :] = jnp.dot(hk, wv, preferred_element_type=F32)

    ngroups = tm // rm
    up_proj(0)
    for k in range(ngroups):
        if k + 1 < ngroups:
            up_proj(k + 1)
        rows = slice(k * rm, (k + 1) * rm)
        for r0 in range(k * rm, (k + 1) * rm, rc):
            ca = conv(ua_ref, cwa_ref, cba_ref, r0)
            cv = conv(uv_ref, cwv_ref, cbv_ref, r0)
            act_ref[r0:r0 + rc, :] = (jax.nn.silu(ca) * cv).astype(BF16)
        o_ref[rows, :] += jnp.dot(act_ref[rows, :], wd, preferred_element_type=F32)

    ta = ua_ref[tm:tm + hdr, :]
    tv = uv_ref[tm:tm + hdr, :]
    ha_ref[slot, c] = ta
    hv_ref[slot, c] = tv
    sa_ref[c] = ta[hdr - 2:hdr, :]
    sv_ref[c] = tv[hdr - 2:hdr, :]

    @pl.when(c == nc - 1)
    def _():
        o_ref[...] = x_ref[...] + _rms(o_ref[...], gpost_ref[...])


FFN_ROW_CHUNK = 64
FFN_ROW_GROUP = 512
FFN_VMEM_LIMIT = V7X_VMEM_BYTES - 4 * 1024 * 1024


def ffn_prompt(x, layer, g_pre, w_up, conv_w, conv_b, w_down, g_post, batch, tm, tf):
    M, D = x.shape
    nc = D_FF // tf
    bps = SEQ // tm
    conv_b3 = conv_b.reshape(DEPTH, 1, 2 * D_FF)
    y, sa, sv = pl.pallas_call(
        functools.partial(_ffn_prompt_kernel, nc=nc, blocks_per_seq=bps, rc=FFN_ROW_CHUNK,
                          rm=FFN_ROW_GROUP),
        grid=(M // tm, nc),
        in_specs=[
            pl.BlockSpec((tm, D), lambda i, c: (i, 0)),
            pl.BlockSpec((None, 1, D), lambda i, c: (layer, 0, 0)),
            pl.BlockSpec((None, D, tf), lambda i, c: (layer, 0, c)),
            pl.BlockSpec((None, D, tf), lambda i, c: (layer, 0, nc + c)),
            pl.BlockSpec((None, 3, tf), lambda i, c: (layer, 0, c)),
            pl.BlockSpec((None, 3, tf), lambda i, c: (layer, 0, nc + c)),
            pl.BlockSpec((None, 1, tf), lambda i, c: (layer, 0, c)),
            pl.BlockSpec((None, 1, tf), lambda i, c: (layer, 0, nc + c)),
            pl.BlockSpec((None, tf, D), lambda i, c: (layer, c, 0)),
            pl.BlockSpec((None, 1, D), lambda i, c: (layer, 0, 0)),
        ],
        out_specs=[
            _once((tm, D), lambda i, c: (i, 0)),
            pl.BlockSpec((None, nc, 2, tf), lambda i, c: (i // bps, 0, 0, 0)),
            pl.BlockSpec((None, nc, 2, tf), lambda i, c: (i // bps, 0, 0, 0)),
        ],
        out_shape=[
            jax.ShapeDtypeStruct((M, D), F32),
            jax.ShapeDtypeStruct((batch, nc, 2, tf), F32),
            jax.ShapeDtypeStruct((batch, nc, 2, tf), F32),
        ],
        scratch_shapes=[
            pltpu.VMEM((tm, D), BF16),
            pltpu.VMEM((tm + SUBLANES, tf), F32),
            pltpu.VMEM((tm + SUBLANES, tf), F32),
            pltpu.VMEM((tm, tf), BF16),
            pltpu.VMEM((2, nc, SUBLANES, tf), F32),
            pltpu.VMEM((2, nc, SUBLANES, tf), F32),
        ],
        compiler_params=_params("arbitrary", "arbitrary", vmem=FFN_VMEM_LIMIT),
        name="ffn_prompt",
    )(x, g_pre.reshape(DEPTH, 1, D), w_up, w_up, conv_w, conv_w, conv_b3, conv_b3,
      w_down, g_post.reshape(DEPTH, 1, D))
    sa = sa.transpose(0, 2, 1, 3).reshape(batch, 2, D_FF)
    sv = sv.transpose(0, 2, 1, 3).reshape(batch, 2, D_FF)
    return y, jnp.concatenate([sa, sv], axis=-1)


def _ffn_sample_kernel(x_ref, gpre_ref, wua_ref, wuv_ref, cwa_ref, cwv_ref, cba_ref, cbv_ref,
                       wd_ref, gpost_ref, sta_ref, stv_ref, o_ref, na_ref, nv_ref, hn_ref, *, nc):
    c = pl.program_id(0)
    L = DEC_SEQ

    @pl.when(c == 0)
    def _():
        hn_ref[...] = _rms(x_ref[...], gpre_ref[...]).astype(BF16)

    hn = hn_ref[...]
    M = hn.shape[0]
    nb = M // L
    ua = jnp.dot(hn, wua_ref[...], preferred_element_type=F32)
    uv = jnp.dot(hn, wuv_ref[...], preferred_element_type=F32)
    tf = ua.shape[1]
    t = lax.broadcasted_iota(jnp.int32, (nb, L, tf), 1)

    def conv(u, st_ref, w_ref, b_ref):
        u3 = u.reshape(nb, L, tf)
        st = st_ref[...]
        prev2, prev1 = st[:, 0:1, :], st[:, 1:2, :]
        p1 = jnp.where(t == 0, prev1, pltpu.roll(u3, 1, 1))
        p2 = jnp.where(t == 0, prev2, jnp.where(t == 1, prev1, pltpu.roll(u3, 2, 1)))
        return (b_ref[...] + w_ref[0:1, :] * p2 + w_ref[1:2, :] * p1 + w_ref[2:3, :] * u3,
                u3[:, L - 2:L, :])

    ca, na = conv(ua, sta_ref, cwa_ref, cba_ref)
    cv, nv = conv(uv, stv_ref, cwv_ref, cbv_ref)
    na_ref[...] = na
    nv_ref[...] = nv
    act = (jax.nn.silu(ca) * cv).astype(BF16).reshape(M, tf)
    part = jnp.dot(act, wd_ref[...], preferred_element_type=F32)

    @pl.when(c == 0)
    def _():
        o_ref[...] = part

    @pl.when(c > 0)
    def _():
        o_ref[...] += part

    @pl.when(c == nc - 1)
    def _():
        o_ref[...] = x_ref[...] + _rms(o_ref[...], gpost_ref[...])


def ffn_sample(x, layer, g_pre, w_up, conv_w, conv_b, w_down, g_post, conv_state, tf):
    M, D = x.shape
    nbatch = M // DEC_SEQ
    nc = D_FF // tf
    conv_b3 = conv_b.reshape(DEPTH, 1, 2 * D_FF)
    y, na, nv = pl.pallas_call(
        functools.partial(_ffn_sample_kernel, nc=nc),
        grid=(nc,),
        in_specs=[
            pl.BlockSpec((M, D), lambda c: (0, 0)),
            pl.BlockSpec((None, 1, D), lambda c: (layer, 0, 0)),
            pl.BlockSpec((None, D, tf), lambda c: (layer, 0, c)),
            pl.BlockSpec((None, D, tf), lambda c: (layer, 0, nc + c)),
            pl.BlockSpec((None, 3, tf), lambda c: (layer, 0, c)),
            pl.BlockSpec((None, 3, tf), lambda c: (layer, 0, nc + c)),
            pl.BlockSpec((None, 1, tf), lambda c: (layer, 0, c)),
            pl.BlockSpec((None, 1, tf), lambda c: (layer, 0, nc + c)),
            pl.BlockSpec((None, tf, D), lambda c: (layer, c, 0)),
            pl.BlockSpec((None, 1, D), lambda c: (layer, 0, 0)),
            pl.BlockSpec((None, nbatch, 2, tf), lambda c: (layer, 0, 0, c)),
            pl.BlockSpec((None, nbatch, 2, tf), lambda c: (layer, 0, 0, nc + c)),
        ],
        out_specs=[
            pl.BlockSpec((M, D), lambda c: (0, 0)),
            pl.BlockSpec((nbatch, 2, tf), lambda c: (0, 0, c)),
            pl.BlockSpec((nbatch, 2, tf), lambda c: (0, 0, c)),
        ],
        out_shape=[
            jax.ShapeDtypeStruct((M, D), F32),
            jax.ShapeDtypeStruct((nbatch, 2, D_FF), F32),
            jax.ShapeDtypeStruct((nbatch, 2, D_FF), F32),
        ],
        scratch_shapes=[pltpu.VMEM((M, D), BF16)],
        compiler_params=_params("arbitrary"),
        name="ffn_sample",
    )(x, g_pre.reshape(DEPTH, 1, D), w_up, w_up, conv_w, conv_w, conv_b3, conv_b3,
      w_down, g_post.reshape(DEPTH, 1, D), conv_state, conv_state)
    return y, jnp.concatenate([na, nv], axis=-1)


def kernel(x_prompt, x_sample, mem_prompt, cache_swa_k, cache_swa_v, state_ssm_re, state_ssm_im, state_ffn_conv, cache_mem_k, cache_mem_v, g_mix_pre, g_mix_post, w_qkv, w_attn_o, attn_sinks, w_ssm_in, ssm_lambda_re, ssm_lambda_im, ssm_log_step, ssm_b_re, ssm_b_im, ssm_c_re, ssm_c_im, ssm_d, w_ssm_glu, g_x_pre, g_x_post, g_mem, w_x_q, w_mem_k, w_mem_v, w_x_o, g_ffn_pre, g_ffn_post, w_ffn_up, ffn_conv_w, ffn_conv_b, w_ffn_down):
    B = x_prompt.shape[0]
    SB = x_sample.shape[0]
    xw = N_XHEADS * XHEAD_DIM
    nkc = N_KV_HEADS * HEAD_DIM
    xp = x_prompt.reshape(B * SEQ, D_MODEL)
    xs = x_sample.reshape(SB * DEC_SEQ, D_MODEL)
    mem = mem_prompt.reshape(B * N_MEM, D_MODEL)
    MS = SB * DEC_SEQ
    TMP = 1024
    cos_p, sin_p = _rope_tables(jnp.arange(SEQ))
    cos_s, sin_s = _rope_tables(PAST_LEN + jnp.arange(DEC_SEQ))
    ck_all = cache_swa_k.reshape(cache_swa_k.shape[0], SB, WINDOW, nkc)
    cv_all = cache_swa_v.reshape(cache_swa_v.shape[0], SB, WINDOW, nkc)
    cmk_all = cache_mem_k.reshape(DEPTH, SB, N_MEM, xw)
    cmv_all = cache_mem_v.reshape(DEPTH, SB, N_MEM, xw)
    w_up_bf = w_ffn_up.astype(BF16)
    w_down_bf = w_ffn_down.astype(BF16)
    w_qkv_bf = w_qkv.astype(BF16)
    w_attn_o_bf = w_attn_o.astype(BF16)
    w_ssm_in_bf = w_ssm_in.astype(BF16)
    w_glu_bf = w_ssm_glu.astype(BF16)
    w_x_q_bf = w_x_q.astype(BF16)
    w_x_o_bf = w_x_o.astype(BF16)
    w_mem_kv_bf = jnp.concatenate([w_mem_k, w_mem_v], axis=-1).astype(BF16)
    TD = 512

    swa_kp, swa_vp, swa_ks, swa_vs = [], [], [], []
    ssm_rp, ssm_ip, ssm_rs, ssm_is = [], [], [], []
    conv_p, conv_s, memk_p, memv_p = [], [], [], []
    for i in range(DEPTH):
        j = i // 2
        if i % 2 == 0:
            qkv_p = norm_matmul(xp, g_mix_pre, i, w_qkv_bf, j, TD)
            qkv_s = norm_matmul(xs, g_mix_pre, i, w_qkv_bf, j, MS)
            op, kp, vp = swa_prompt(qkv_p, attn_sinks, j, cos_p, sin_p, B)
            sink_rows = jnp.repeat(attn_sinks[j].reshape(N_KV_HEADS, GQA_GROUP), DEC_SEQ,
                                   axis=1)[..., None]
            os_, kn, vn = swa_sample(qkv_s, ck_all, cv_all, j, sink_rows, cos_s, sin_s)
            swa_kp.append(kp.reshape(B, WINDOW, N_KV_HEADS, HEAD_DIM))
            swa_vp.append(vp.reshape(B, WINDOW, N_KV_HEADS, HEAD_DIM))
            swa_ks.append(kn.reshape(SB, WINDOW, N_KV_HEADS, HEAD_DIM))
            swa_vs.append(vn.reshape(SB, WINDOW, N_KV_HEADS, HEAD_DIM))
            xp = matmul_post(op, w_attn_o_bf, j, g_mix_post, i, xp, TD)
            xs = matmul_post(os_, w_attn_o_bf, j, g_mix_post, i, xs, MS)
        else:
            a_re, a_im, b_blk, c_blk = _s5_weights(
                ssm_lambda_re[j], ssm_lambda_im[j], ssm_log_step[j], ssm_b_re[j], ssm_b_im[j],
                ssm_c_re[j], ssm_c_im[j])
            d_row = ssm_d[j].reshape(1, D_MODEL)
            up_ = norm_matmul(xp, g_mix_pre, i, w_ssm_in_bf, j, TD)
            us_ = norm_matmul(xs, g_mix_pre, i, w_ssm_in_bf, j, MS)
            a_tiles = jnp.stack([a_re.reshape(N_SUPER, SUBLANES, LANES),
                                 a_im.reshape(N_SUPER, SUBLANES, LANES)], axis=1
                                ).reshape(2 * N_SUPER, SUBLANES, LANES)
            zero_state = jnp.zeros((B, N_SUPER, SUBLANES, LANES), F32)
            zp, rp, ip = s5_prompt(up_, zero_state, zero_state, a_tiles, b_blk, c_blk, d_row,
                                   B, 256)
            nstate = N_SSM_GROUPS * SSM_STATE
            zs, rn, im_ = s5_sample(us_, state_ssm_re[j].reshape(SB, nstate),
                                    state_ssm_im[j].reshape(SB, nstate),
                                    a_re.reshape(1, nstate), a_im.reshape(1, nstate),
                                    b_blk, c_blk, d_row)
            ssm_rp.append(rp.reshape(B, N_SSM_GROUPS, SSM_STATE))
            ssm_ip.append(ip.reshape(B, N_SSM_GROUPS, SSM_STATE))
            ssm_rs.append(rn.reshape(SB, N_SSM_GROUPS, SSM_STATE))
            ssm_is.append(im_.reshape(SB, N_SSM_GROUPS, SSM_STATE))
            xp = glu_post(zp, w_glu_bf, j, g_mix_post, i, xp, TD)
            xs = glu_post(zs, w_glu_bf, j, g_mix_post, i, xs, MS)
        mkv = norm_matmul(mem, g_mem, i, w_mem_kv_bf, i, B * N_MEM)
        mk, mv = mkv[:, :xw], mkv[:, xw:]
        memk_p.append(mk.reshape(B, N_MEM, N_XHEADS, XHEAD_DIM))
        memv_p.append(mv.reshape(B, N_MEM, N_XHEADS, XHEAD_DIM))
        xp = xattn_prompt(xp, i, g_x_pre, w_x_q_bf, mk, mv, w_x_o_bf, g_x_post, B, 2 * TD)
        qs = norm_matmul(xs, g_x_pre, i, w_x_q_bf, i, MS)
        as_ = xattn_sample(qs, cmk_all, cmv_all, i)
        xs = matmul_post(as_, w_x_o_bf, i, g_x_post, i, xs, MS)
        xp, cp = ffn_prompt(xp, i, g_ffn_pre, w_up_bf, ffn_conv_w, ffn_conv_b, w_down_bf,
                            g_ffn_post, B, TMP, 512)
        xs, cs = ffn_sample(xs, i, g_ffn_pre, w_up_bf, ffn_conv_w, ffn_conv_b, w_down_bf,
                            g_ffn_post, state_ffn_conv, 1408)
        conv_p.append(cp)
        conv_s.append(cs)
    return (xp.reshape(B, SEQ, D_MODEL), xs.reshape(SB, DEC_SEQ, D_MODEL),
            jnp.stack(swa_kp), jnp.stack(swa_vp), jnp.stack(swa_ks), jnp.stack(swa_vs),
            jnp.stack(ssm_rp), jnp.stack(ssm_ip), jnp.stack(ssm_rs), jnp.stack(ssm_is),
            jnp.stack(conv_p), jnp.stack(conv_s), jnp.stack(memk_p), jnp.stack(memv_p))
```

```python
import functools
import math

import jax
import jax.numpy as jnp
from jax import lax
from jax.experimental import pallas as pl
from jax.experimental.pallas import tpu as pltpu

D_MODEL = 2048
SEQ = 4096
DEPTH = 4
DEC_SEQ = 8
PAST_LEN = 16384
HEAD_DIM = 64
N_HEADS = 32
N_KV_HEADS = 4
GQA_GROUP = 8
WINDOW = 128
ROPE_THETA = 10000.0
SSM_GROUP = 16
N_SSM_GROUPS = 128
SSM_STATE = 64
N_MEM = 256
N_XHEADS = 4
XHEAD_DIM = 128
D_FF = 5632
NORM_EPS = 1e-6
NEG_INF = -1e30

F32 = jnp.float32
BF16 = jnp.bfloat16

V7X_VMEM_BYTES = 64 * 1024 * 1024
VMEM_LIMIT = V7X_VMEM_BYTES - 8 * 1024 * 1024
LANES = 128
SUBLANES = 8
SSM_SUPER = 16
N_SUPER = N_SSM_GROUPS // SSM_SUPER
SUPER_IN = SSM_SUPER * SSM_GROUP
SUPER_STATE = SSM_SUPER * SSM_STATE
SLABS = SUPER_STATE // LANES


def _params(*sem, vmem=VMEM_LIMIT):
    return pltpu.CompilerParams(dimension_semantics=sem, vmem_limit_bytes=vmem)


def _rms(x, g):
    ms = jnp.mean(x * x, axis=-1, keepdims=True)
    return x * lax.rsqrt(ms + NORM_EPS) * g


def _bdot(a, b):
    return jnp.dot(a.astype(BF16), b.astype(BF16), preferred_element_type=F32)


def _once(shape, index_map):
    return pl.BlockSpec(shape, index_map, pipeline_mode=pl.Buffered(1))


ROW_GROUP = 256


def _resident(shape, index_map):
    return pl.BlockSpec(shape, index_map, pipeline_mode=pl.Buffered(1))


def _row_groups(tm):
    rg = min(ROW_GROUP, tm)
    return [slice(k * rg, (k + 1) * rg) for k in range(tm // rg)]


def _norm_matmul_kernel(x_ref, g_ref, w_ref, o_ref):
    g = g_ref[...]
    for rows in _row_groups(x_ref.shape[0]):
        hn = _rms(x_ref[rows, :], g).astype(BF16)
        o_ref[rows, :] = jnp.dot(hn, w_ref[...], preferred_element_type=F32).astype(o_ref.dtype)


def norm_matmul(x, g, g_layer, w, w_layer, tm, out_dtype=F32):
    M, D = x.shape
    N = w.shape[-1]
    return pl.pallas_call(
        _norm_matmul_kernel,
        grid=(M // tm,),
        in_specs=[
            pl.BlockSpec((tm, D), lambda i: (i, 0)),
            pl.BlockSpec((None, 1, D), lambda i: (g_layer, 0, 0)),
            _resident((None, D, N), lambda i: (w_layer, 0, 0)),
        ],
        out_specs=pl.BlockSpec((tm, N), lambda i: (i, 0)),
        out_shape=jax.ShapeDtypeStruct((M, N), out_dtype),
        compiler_params=_params("arbitrary"),
        name="norm_matmul",
    )(x, g.reshape(g.shape[0], 1, D), w)


def _matmul_post_kernel(a_ref, w_ref, g_ref, x_ref, o_ref):
    g = g_ref[...]
    for rows in _row_groups(x_ref.shape[0]):
        y = jnp.dot(a_ref[rows, :].astype(BF16), w_ref[...], preferred_element_type=F32)
        o_ref[rows, :] = x_ref[rows, :] + _rms(y, g)


def matmul_post(a, w, w_layer, g, g_layer, x, tm):
    M, K = a.shape
    D = w.shape[-1]
    return pl.pallas_call(
        _matmul_post_kernel,
        grid=(M // tm,),
        in_specs=[
            pl.BlockSpec((tm, K), lambda i: (i, 0)),
            _resident((None, K, D), lambda i: (w_layer, 0, 0)),
            pl.BlockSpec((None, 1, D), lambda i: (g_layer, 0, 0)),
            pl.BlockSpec((tm, D), lambda i: (i, 0)),
        ],
        out_specs=pl.BlockSpec((tm, D), lambda i: (i, 0)),
        out_shape=jax.ShapeDtypeStruct((M, D), F32),
        compiler_params=_params("arbitrary"),
        name="matmul_post",
    )(a, w, g.reshape(g.shape[0], 1, D), x)


GLU_COLS = 512


def _glu_post_kernel(z_ref, w_ref, g_ref, x_ref, o_ref, y_ref):
    D = o_ref.shape[1]
    g = g_ref[...]
    for rows in _row_groups(x_ref.shape[0]):
        z = z_ref[rows, :]
        for n in range(D // GLU_COLS):
            cols = slice(n * GLU_COLS, (n + 1) * GLU_COLS)
            gcols = slice(D + n * GLU_COLS, D + (n + 1) * GLU_COLS)
            val = jnp.dot(z, w_ref[:, cols], preferred_element_type=F32)
            gate = jnp.dot(z, w_ref[:, gcols], preferred_element_type=F32)
            y_ref[rows, cols] = val * jax.nn.sigmoid(gate)
        o_ref[rows, :] = x_ref[rows, :] + _rms(y_ref[rows, :], g)


def glu_post(z, w, w_layer, g, g_layer, x, tm):
    M, K = z.shape
    D = D_MODEL
    return pl.pallas_call(
        _glu_post_kernel,
        grid=(M // tm,),
        in_specs=[
            pl.BlockSpec((tm, K), lambda i: (i, 0)),
            _resident((None, K, 2 * D), lambda i: (w_layer, 0, 0)),
            pl.BlockSpec((None, 1, D), lambda i: (g_layer, 0, 0)),
            pl.BlockSpec((tm, D), lambda i: (i, 0)),
        ],
        out_specs=pl.BlockSpec((tm, D), lambda i: (i, 0)),
        out_shape=jax.ShapeDtypeStruct((M, D), F32),
        scratch_shapes=[pltpu.VMEM((tm, D), F32)],
        compiler_params=_params("arbitrary"),
        name="glu_post",
    )(z, w, g.reshape(g.shape[0], 1, D), x)


def _rope_chunk(xc, cos, sin_signed):
    lane = lax.broadcasted_iota(jnp.int32, xc.shape, 1)
    first_half = (lane & (HEAD_DIM - 1)) < (HEAD_DIM // 2)
    partner = jnp.where(first_half, pltpu.roll(xc, LANES - HEAD_DIM // 2, 1),
                        pltpu.roll(xc, HEAD_DIM // 2, 1))
    return xc * cos + partner * sin_signed


def _rope_tables(pos):
    half = HEAD_DIM // 2
    inv_freq = ROPE_THETA ** (-jnp.arange(half, dtype=F32) * 2.0 / HEAD_DIM)
    ang = pos.astype(F32)[:, None] * inv_freq[None, :]
    cos, sin = jnp.cos(ang), jnp.sin(ang)
    cos128 = jnp.tile(cos, (1, LANES // half))
    sin128 = jnp.tile(jnp.concatenate([-sin, sin], axis=1), (1, LANES // HEAD_DIM))
    return cos128, sin128


def _swa_prompt_kernel(sink_ref, q_ref, kv_ref, cos_ref, sin_ref, wu_ref, wd_ref,
                       o_ref, kc_ref, vc_ref, wub_ref, wdb_ref, kk_ref, vv_ref, *, layer):
    i = pl.program_id(1)
    blk = WINDOW
    wub_ref[...] = wu_ref[...].astype(BF16)
    wdb_ref[...] = wd_ref[...].astype(BF16)
    cos, sn = cos_ref[...], sin_ref[...]
    kv = kv_ref[...]
    nkc = N_KV_HEADS * HEAD_DIM
    k = jnp.concatenate([_rope_chunk(kv[:, c * LANES:(c + 1) * LANES], cos, sn)
                         for c in range(nkc // LANES)], axis=1)
    v = kv[:, nkc:2 * nkc]
    kc_ref[...] = k
    vc_ref[...] = v

    cur = i % 2
    prv = 1 - cur

    @pl.when(i == 0)
    def _():
        kk_ref[1] = jnp.zeros((blk, nkc), BF16)
        vv_ref[1] = jnp.zeros((blk, nkc), BF16)

    kb, vb = k.astype(BF16), v.astype(BF16)
    kk_ref[cur] = kb
    vv_ref[cur] = vb
    kk = jnp.concatenate([kk_ref[prv], kb], axis=0)
    vv = jnp.concatenate([vv_ref[prv], vb], axis=0)
    rows = GQA_GROUP * blk
    qi = lax.broadcasted_iota(jnp.int32, (rows, 2 * blk), 0) & (blk - 1)
    si = lax.broadcasted_iota(jnp.int32, (rows, 2 * blk), 1)
    d = blk + qi - si
    mask = (d >= 0) & (d < WINDOW) & ((si >= blk) | (i > 0))

    chunks_per_group = GQA_GROUP * HEAD_DIM // LANES

    def scores(kh):
        hs = slice(kh * HEAD_DIM, (kh + 1) * HEAD_DIM)
        pieces = []
        for c in range(kh * chunks_per_group, (kh + 1) * chunks_per_group):
            qc = (_rope_chunk(q_ref[:, c * LANES:(c + 1) * LANES], cos, sn)
                  * (HEAD_DIM ** -0.5)).astype(BF16)
            pieces += [qc[:, hh * HEAD_DIM:(hh + 1) * HEAD_DIM] for hh in range(LANES // HEAD_DIM)]
        qs = jnp.concatenate(pieces, axis=0)
        return lax.dot_general(qs, kk[:, hs], (((1,), (1,)), ((), ())),
                               preferred_element_type=F32)

    ahead = 2
    s_all = [scores(kh) for kh in range(min(ahead, N_KV_HEADS))]
    outs = []
    for kh in range(N_KV_HEADS):
        if kh + ahead < N_KV_HEADS:
            s_all.append(scores(kh + ahead))
        hs = slice(kh * HEAD_DIM, (kh + 1) * HEAD_DIM)
        sk = jnp.concatenate([jnp.full((blk, 1), sink_ref[layer, kh * GQA_GROUP + j], F32)
                              for j in range(GQA_GROUP)], axis=0)
        s = jnp.where(mask, s_all[kh], NEG_INF)
        mx = jnp.maximum(jnp.max(s, axis=1, keepdims=True), sk)
        p = jnp.exp(s - mx)
        den = jnp.sum(p, axis=1, keepdims=True) + jnp.exp(sk - mx)
        o = jnp.dot(p.astype(BF16), vv[:, hs], preferred_element_type=F32) / den
        outs += [o[j * blk:(j + 1) * blk, :] for j in range(GQA_GROUP)]
    o_ref[...] = jnp.concatenate(outs, axis=1).astype(o_ref.dtype)


def swa_prompt(qkv, sinks, layer, cos, sin, batch, w_up, w_down):
    blk = WINDOW
    nb = SEQ // blk
    nq = N_HEADS * HEAD_DIM
    nkv = 2 * N_KV_HEADS * HEAD_DIM
    per_layer = batch * nb // 2
    ur, dr = D_MODEL // per_layer, D_FF // per_layer

    def w_in(b, i):
        n = b * nb + i
        return (2 * layer + n // per_layer, n % per_layer, 0)

    def w_out(b, i):
        n = b * nb + i
        return (n // per_layer, n % per_layer, 0)

    o, kc, vc, w_up_bf, w_down_bf = pl.pallas_call(
        functools.partial(_swa_prompt_kernel, layer=layer),
        grid=(batch, nb),
        in_specs=[
            pl.BlockSpec(memory_space=pltpu.SMEM),
            pl.BlockSpec((blk, nq), lambda b, i: (b * nb + i, 0)),
            pl.BlockSpec((blk, nkv), lambda b, i: (b * nb + i, nq // nkv)),
            pl.BlockSpec((blk, LANES), lambda b, i: (i, 0)),
            pl.BlockSpec((blk, LANES), lambda b, i: (i, 0)),
            pl.BlockSpec((None, ur, 2 * D_FF), w_in),
            pl.BlockSpec((None, dr, D_MODEL), w_in),
        ],
        out_specs=[
            pl.BlockSpec((blk, nq), lambda b, i: (b * nb + i, 0)),
            pl.BlockSpec((None, blk, nkv // 2), lambda b, i: (b, 0, 0)),
            pl.BlockSpec((None, blk, nkv // 2), lambda b, i: (b, 0, 0)),
            pl.BlockSpec((None, ur, 2 * D_FF), w_out),
            pl.BlockSpec((None, dr, D_MODEL), w_out),
        ],
        out_shape=[
            jax.ShapeDtypeStruct((batch * SEQ, nq), BF16),
            jax.ShapeDtypeStruct((batch, blk, nkv // 2), F32),
            jax.ShapeDtypeStruct((batch, blk, nkv // 2), F32),
            jax.ShapeDtypeStruct((2, D_MODEL, 2 * D_FF), BF16),
            jax.ShapeDtypeStruct((2, D_FF, D_MODEL), BF16),
        ],
        scratch_shapes=[pltpu.VMEM((2, blk, nkv // 2), BF16),
                        pltpu.VMEM((2, blk, nkv // 2), BF16)],
        compiler_params=_params("arbitrary", "arbitrary"),
        name="swa_prompt",
    )(sinks, qkv, qkv, cos, sin, w_up, w_down)
    return o, kc, vc, w_up_bf, w_down_bf


def _swa_sample_kernel(sink_ref, qkv_ref, ck_ref, cv_ref, cos_ref, sin_ref,
                       o_ref, nk_ref, nv_ref, *, nb):
    L = DEC_SEQ
    wb = WINDOW
    nq = N_HEADS * HEAD_DIM
    nkc = N_KV_HEADS * HEAD_DIM
    cos, sn = cos_ref[...], sin_ref[...]
    rows = GQA_GROUP * L
    t_q = lax.broadcasted_iota(jnp.int32, (rows, wb), 0) % L
    c_k = lax.broadcasted_iota(jnp.int32, (rows, wb), 1)
    mask_c = c_k >= t_q + 1
    t_q2 = lax.broadcasted_iota(jnp.int32, (rows, L), 0) % L
    t_k2 = lax.broadcasted_iota(jnp.int32, (rows, L), 1)
    mask_n = t_k2 <= t_q2
    for b in range(nb):
        x = qkv_ref[b * L:(b + 1) * L, :]
        qr = jnp.concatenate([_rope_chunk(x[:, c * LANES:(c + 1) * LANES], cos, sn)
                              for c in range(nq // LANES)], axis=1) * (HEAD_DIM ** -0.5)
        kn = jnp.concatenate([_rope_chunk(x[:, nq + c * LANES:nq + (c + 1) * LANES], cos, sn)
                              for c in range(nkc // LANES)], axis=1)
        vn = x[:, nq + nkc:nq + 2 * nkc]
        ck = ck_ref[b]
        cv = cv_ref[b]
        nk_ref[b, 0:wb - L, :] = ck[L:wb, :]
        nk_ref[b, wb - L:wb, :] = kn
        nv_ref[b, 0:wb - L, :] = cv[L:wb, :]
        nv_ref[b, wb - L:wb, :] = vn
        ckb, cvb, knb, vnb = ck.astype(BF16), cv.astype(BF16), kn.astype(BF16), vn.astype(BF16)
        dn = (((1,), (1,)), ((), ()))
        scores = []
        for kh in range(N_KV_HEADS):
            hs = slice(kh * HEAD_DIM, (kh + 1) * HEAD_DIM)
            qs = jnp.concatenate(
                [qr[:, (kh * GQA_GROUP + j) * HEAD_DIM:(kh * GQA_GROUP + j + 1) * HEAD_DIM]
                 for j in range(GQA_GROUP)], axis=0).astype(BF16)
            scores.append((lax.dot_general(qs, ckb[:, hs], dn, preferred_element_type=F32),
                           lax.dot_general(qs, knb[:, hs], dn, preferred_element_type=F32)))
        outs = []
        for kh in range(N_KV_HEADS):
            hs = slice(kh * HEAD_DIM, (kh + 1) * HEAD_DIM)
            s_c = jnp.where(mask_c, scores[kh][0], NEG_INF)
            s_n = jnp.where(mask_n, scores[kh][1], NEG_INF)
            sk = sink_ref[kh]
            mx = jnp.maximum(jnp.maximum(jnp.max(s_c, axis=1, keepdims=True),
                                         jnp.max(s_n, axis=1, keepdims=True)), sk)
            p_c = jnp.exp(s_c - mx)
            p_n = jnp.exp(s_n - mx)
            den = (jnp.sum(p_c, axis=1, keepdims=True) + jnp.sum(p_n, axis=1, keepdims=True)
                   + jnp.exp(sk - mx))
            o = (jnp.dot(p_c.astype(BF16), cvb[:, hs], preferred_element_type=F32)
                 + jnp.dot(p_n.astype(BF16), vnb[:, hs], preferred_element_type=F32)) / den
            outs.extend(o[j * L:(j + 1) * L, :] for j in range(GQA_GROUP))
        o_ref[b * L:(b + 1) * L, :] = jnp.concatenate(outs, axis=1).astype(o_ref.dtype)


def swa_sample(qkv, cache_k, cache_v, layer, sink_rows, cos, sin, nb=8):
    nbatch = cache_k.shape[1]
    nq = N_HEADS * HEAD_DIM
    nkc = N_KV_HEADS * HEAD_DIM
    L = DEC_SEQ
    return pl.pallas_call(
        functools.partial(_swa_sample_kernel, nb=nb),
        grid=(nbatch // nb,),
        in_specs=[
            pl.BlockSpec((N_KV_HEADS, GQA_GROUP * L, 1), lambda g: (0, 0, 0)),
            pl.BlockSpec((nb * L, nq + 2 * nkc), lambda g: (g, 0)),
            pl.BlockSpec((None, nb, WINDOW, nkc), lambda g: (layer, g, 0, 0)),
            pl.BlockSpec((None, nb, WINDOW, nkc), lambda g: (layer, g, 0, 0)),
            pl.BlockSpec((L, LANES), lambda g: (0, 0)),
            pl.BlockSpec((L, LANES), lambda g: (0, 0)),
        ],
        out_specs=[
            pl.BlockSpec((nb * L, nq), lambda g: (g, 0)),
            pl.BlockSpec((nb, WINDOW, nkc), lambda g: (g, 0, 0)),
            pl.BlockSpec((nb, WINDOW, nkc), lambda g: (g, 0, 0)),
        ],
        out_shape=[
            jax.ShapeDtypeStruct((nbatch * L, nq), BF16),
            jax.ShapeDtypeStruct((nbatch, WINDOW, nkc), F32),
            jax.ShapeDtypeStruct((nbatch, WINDOW, nkc), F32),
        ],
        compiler_params=_params("arbitrary"),
        name="swa_sample",
    )(sink_rows, qkv, cache_k, cache_v, cos, sin)


_XHEADS = [slice(h * XHEAD_DIM, (h + 1) * XHEAD_DIM) for h in range(N_XHEADS)]


def _xattn_scores(q, mk):
    return [lax.dot_general((q[:, hs] * (XHEAD_DIM ** -0.5)).astype(BF16), mk[:, hs].astype(BF16),
                            (((1,), (1,)), ((), ())), preferred_element_type=F32)
            for hs in _XHEADS]


def _xattn_attend(scores, mv):
    outs = []
    for hs, s in zip(_XHEADS, scores):
        mx = jnp.max(s, axis=1, keepdims=True)
        p = jnp.exp(s - mx)
        den = jnp.sum(p, axis=1, keepdims=True)
        outs.append(jnp.dot(p.astype(BF16), mv[:, hs].astype(BF16),
                            preferred_element_type=F32) / den)
    return jnp.concatenate(outs, axis=1)


def _xattn_heads(q, mk, mv):
    return _xattn_attend(_xattn_scores(q, mk), mv)


def _xattn_prompt_kernel(x_ref, gpre_ref, wq_ref, mk_ref, mv_ref, wo_ref, gpost_ref, o_ref):
    mk = mk_ref[...].astype(BF16)
    mv = mv_ref[...].astype(BF16)
    gpre, gpost = gpre_ref[...], gpost_ref[...]
    groups = _row_groups(x_ref.shape[0])

    def project(rows):
        q = jnp.dot(_rms(x_ref[rows, :], gpre).astype(BF16), wq_ref[...],
                    preferred_element_type=F32)
        return _xattn_scores(q, mk)

    scores = project(groups[0])
    for k, rows in enumerate(groups):
        nxt = project(groups[k + 1]) if k + 1 < len(groups) else None
        a = _xattn_attend(scores, mv).astype(BF16)
        y = jnp.dot(a, wo_ref[...], preferred_element_type=F32)
        o_ref[rows, :] = x_ref[rows, :] + _rms(y, gpost)
        scores = nxt


def xattn_prompt(x, layer, g_pre, w_q, mk, mv, w_o, g_post, batch, tm):
    nq = SEQ // tm
    xw = N_XHEADS * XHEAD_DIM
    D = D_MODEL
    return pl.pallas_call(
        _xattn_prompt_kernel,
        grid=(batch, nq),
        in_specs=[
            pl.BlockSpec((tm, D), lambda b, i: (b * nq + i, 0)),
            pl.BlockSpec((None, 1, D), lambda b, i: (layer, 0, 0)),
            _resident((None, D, xw), lambda b, i: (layer, 0, 0)),
            pl.BlockSpec((N_MEM, xw), lambda b, i: (b, 0)),
            pl.BlockSpec((N_MEM, xw), lambda b, i: (b, 0)),
            _resident((None, xw, D), lambda b, i: (layer, 0, 0)),
            pl.BlockSpec((None, 1, D), lambda b, i: (layer, 0, 0)),
        ],
        out_specs=pl.BlockSpec((tm, D), lambda b, i: (b * nq + i, 0)),
        out_shape=jax.ShapeDtypeStruct((batch * SEQ, D), F32),
        compiler_params=_params("arbitrary", "arbitrary"),
        name="xattn_prompt",
    )(x, g_pre.reshape(DEPTH, 1, D), w_q, mk, mv, w_o, g_post.reshape(DEPTH, 1, D))


def _xattn_sample_kernel(q_ref, mk_ref, mv_ref, o_ref, *, nb):
    L = DEC_SEQ
    for b in range(nb):
        o_ref[b * L:(b + 1) * L, :] = _xattn_heads(
            q_ref[b * L:(b + 1) * L, :], mk_ref[b], mv_ref[b]).astype(o_ref.dtype)


def xattn_sample(q, cache_k, cache_v, layer, nb=8):
    nbatch = cache_k.shape[1]
    L = DEC_SEQ
    xw = N_XHEADS * XHEAD_DIM
    return pl.pallas_call(
        functools.partial(_xattn_sample_kernel, nb=nb),
        grid=(nbatch // nb,),
        in_specs=[
            pl.BlockSpec((nb * L, xw), lambda g: (g, 0)),
            pl.BlockSpec((None, nb, N_MEM, xw), lambda g: (layer, g, 0, 0)),
            pl.BlockSpec((None, nb, N_MEM, xw), lambda g: (layer, g, 0, 0)),
        ],
        out_specs=pl.BlockSpec((nb * L, xw), lambda g: (g, 0)),
        out_shape=jax.ShapeDtypeStruct((nbatch * L, xw), BF16),
        compiler_params=_params("arbitrary"),
        name="xattn_sample",
    )(q, cache_k, cache_v)


def _s5_prompt_kernel(u_ref, h0r_ref, h0i_ref, a_ref, b_ref, c_ref, d_ref,
                      z_ref, hr_ref, hi_ref, x_ref, hst_ref, *, T):
    c = pl.program_id(1)
    nslab = 2 * SLABS
    G = T // SUBLANES

    @pl.when(c == 0)
    def _():
        for sb in range(N_SUPER):
            hst_ref[2 * sb] = h0r_ref[sb]
            hst_ref[2 * sb + 1] = h0i_ref[sb]

    def slab_rows(sb, s):
        return pl.ds((sb * nslab + s) * S5_SLAB_PITCH, SUBLANES)

    for sb in range(N_SUPER):
        bu = _bdot(u_ref[:, sb * SUPER_IN:(sb + 1) * SUPER_IN], b_ref[sb])
        for s in range(nslab):
            x_ref[:, slab_rows(sb, s), :] = bu[:, s * LANES:(s + 1) * LANES].reshape(
                G, SUBLANES, LANES)

    a = [a_ref[k] for k in range(2 * N_SUPER)]
    h_init = tuple(hst_ref[k] for k in range(2 * N_SUPER))

    def step(t, h):
        g = lax.shift_right_logical(t, 3)
        r = lax.bitwise_and(t, SUBLANES - 1)
        new = []
        for sb in range(N_SUPER):
            ir = pl.ds((sb * nslab) * S5_SLAB_PITCH + r, SUBLANES, stride=S5_SLAB_PITCH)
            ii = pl.ds((sb * nslab + SLABS) * S5_SLAB_PITCH + r, SUBLANES, stride=S5_SLAB_PITCH)
            ar, ai = a[2 * sb], a[2 * sb + 1]
            hr, hi = h[2 * sb], h[2 * sb + 1]
            nr = ar * hr - ai * hi + x_ref[g, ir, :]
            ni = ar * hi + ai * hr + x_ref[g, ii, :]
            x_ref[g, ir, :] = nr
            x_ref[g, ii, :] = ni
            new += [nr, ni]
        return tuple(new)

    h_fin = lax.fori_loop(0, T, step, h_init)
    for sb in range(N_SUPER):
        hst_ref[2 * sb] = h_fin[2 * sb]
        hst_ref[2 * sb + 1] = h_fin[2 * sb + 1]
        hr_ref[sb] = h_fin[2 * sb]
        hi_ref[sb] = h_fin[2 * sb + 1]

    for sb in range(N_SUPER):
        hcat = jnp.concatenate([x_ref[:, slab_rows(sb, s), :].reshape(T, LANES)
                                for s in range(nslab)], axis=1)
        cols = slice(sb * SUPER_IN, (sb + 1) * SUPER_IN)
        y = _bdot(hcat, c_ref[sb]) + d_ref[:, cols] * u_ref[:, cols]
        z_ref[:, cols] = jax.nn.gelu(y).astype(z_ref.dtype)


S5_SLAB_PITCH = 12


def s5_prompt(u, h0r, h0i, a_tiles, b_blk, c_blk, d_row, batch, T):
    nc = SEQ // T
    st = (None, N_SUPER, SUBLANES, LANES)
    return pl.pallas_call(
        functools.partial(_s5_prompt_kernel, T=T),
        grid=(batch, nc),
        in_specs=[
            pl.BlockSpec((T, D_MODEL), lambda b, c: (b * nc + c, 0)),
            pl.BlockSpec(st, lambda b, c: (b, 0, 0, 0)),
            pl.BlockSpec(st, lambda b, c: (b, 0, 0, 0)),
            _once((2 * N_SUPER, SUBLANES, LANES), lambda b, c: (0, 0, 0)),
            _once((N_SUPER, SUPER_IN, 2 * SUPER_STATE), lambda b, c: (0, 0, 0)),
            _once((N_SUPER, 2 * SUPER_STATE, SUPER_IN), lambda b, c: (0, 0, 0)),
            _once((1, D_MODEL), lambda b, c: (0, 0)),
        ],
        out_specs=[
            pl.BlockSpec((T, D_MODEL), lambda b, c: (b * nc + c, 0)),
            pl.BlockSpec(st, lambda b, c: (b, 0, 0, 0)),
            pl.BlockSpec(st, lambda b, c: (b, 0, 0, 0)),
        ],
        out_shape=[
            jax.ShapeDtypeStruct((batch * SEQ, D_MODEL), BF16),
            jax.ShapeDtypeStruct((batch, N_SUPER, SUBLANES, LANES), F32),
            jax.ShapeDtypeStruct((batch, N_SUPER, SUBLANES, LANES), F32),
        ],
        scratch_shapes=[
            pltpu.VMEM((T // SUBLANES, N_SUPER * 2 * SLABS * S5_SLAB_PITCH, LANES), F32),
            pltpu.VMEM((2 * N_SUPER, SUBLANES, LANES), F32),
        ],
        compiler_params=_params("arbitrary", "arbitrary"),
        name="s5_prompt",
    )(u, h0r, h0i, a_tiles, b_blk, c_blk, d_row)


def _s5_sample_kernel(u_ref, h0r_ref, h0i_ref, ar_ref, ai_ref, b_ref, c_ref, d_ref,
                      z_ref, hr_ref, hi_ref, x_ref, *, nbatch):
    L = DEC_SEQ
    R = nbatch * L
    for sb in range(N_SUPER):
        cols = slice(sb * SUPER_IN, (sb + 1) * SUPER_IN)
        bu = _bdot(u_ref[:, cols], b_ref[sb])
        for s in range(2 * SLABS):
            x_ref[s * R:(s + 1) * R, :] = bu[:, s * LANES:(s + 1) * LANES]
        for s in range(SLABS):
            st = slice(sb * SUPER_STATE + s * LANES, sb * SUPER_STATE + (s + 1) * LANES)
            ar, ai = ar_ref[:, st], ai_ref[:, st]
            hr, hi = h0r_ref[:, st], h0i_ref[:, st]
            for t in range(L):
                rr = pl.ds(s * R + t, nbatch, stride=L)
                ri = pl.ds((SLABS + s) * R + t, nbatch, stride=L)
                nr = ar * hr - ai * hi + x_ref[rr, :]
                ni = ar * hi + ai * hr + x_ref[ri, :]
                x_ref[rr, :] = nr
                x_ref[ri, :] = ni
                hr, hi = nr, ni
            hr_ref[:, st] = hr
            hi_ref[:, st] = hi
        hcat = jnp.concatenate([x_ref[s * R:(s + 1) * R, :] for s in range(2 * SLABS)], axis=1)
        y = _bdot(hcat, c_ref[sb]) + d_ref[:, cols] * u_ref[:, cols]
        z_ref[:, cols] = jax.nn.gelu(y).astype(z_ref.dtype)


def s5_sample(u, h0r, h0i, a_re, a_im, b_blk, c_blk, d_row):
    rows = u.shape[0]
    nbatch = rows // DEC_SEQ
    nstate = N_SSM_GROUPS * SSM_STATE
    full = lambda shape: pl.BlockSpec(shape, lambda i: (0,) * len(shape))
    return pl.pallas_call(
        functools.partial(_s5_sample_kernel, nbatch=nbatch),
        grid=(1,),
        in_specs=[
            full((rows, D_MODEL)), full((nbatch, nstate)), full((nbatch, nstate)),
            full((1, nstate)), full((1, nstate)),
            full((N_SUPER, SUPER_IN, 2 * SUPER_STATE)),
            full((N_SUPER, 2 * SUPER_STATE, SUPER_IN)),
            full((1, D_MODEL)),
        ],
        out_specs=[full((rows, D_MODEL)), full((nbatch, nstate)), full((nbatch, nstate))],
        out_shape=[
            jax.ShapeDtypeStruct((rows, D_MODEL), BF16),
            jax.ShapeDtypeStruct((nbatch, nstate), F32),
            jax.ShapeDtypeStruct((nbatch, nstate), F32),
        ],
        scratch_shapes=[pltpu.VMEM((2 * SLABS * rows, LANES), F32)],
        compiler_params=_params("arbitrary"),
        name="s5_sample",
    )(u, h0r, h0i, a_re, a_im, b_blk, c_blk, d_row)


def _s5_weights(lam_re, lam_im, log_step, b_re, b_im, c_re, c_im):
    lam = lax.complex(lam_re.astype(F32), lam_im.astype(F32))
    dt = jnp.exp(log_step.astype(F32))[:, None]
    abar = jnp.exp(lam * dt)
    bbar = ((abar - 1.0) / lam)[..., None] * lax.complex(b_re.astype(F32), b_im.astype(F32))
    eye = jnp.eye(SSM_SUPER, dtype=F32)

    def b_layout(m):
        m = m.reshape(N_SUPER, SSM_SUPER, SSM_STATE, SSM_GROUP)
        return jnp.einsum('sgph,gk->sghkp', m, eye).reshape(N_SUPER, SUPER_IN, SUPER_STATE)

    def c_layout(m):
        m = m.reshape(N_SUPER, SSM_SUPER, SSM_GROUP, SSM_STATE)
        return jnp.einsum('sghp,gk->sgpkh', m, eye).reshape(N_SUPER, SUPER_STATE, SUPER_IN)

    b_blk = jnp.concatenate([b_layout(bbar.real), b_layout(bbar.imag)], axis=2).astype(BF16)
    c_blk = jnp.concatenate([c_layout(c_re.astype(F32)), -c_layout(c_im.astype(F32))],
                            axis=1).astype(BF16)
    return abar.real, abar.imag, b_blk, c_blk


def _ffn_prompt_kernel(x_ref, gpre_ref, wua_ref, wuv_ref, cwa_ref, cwv_ref, cba_ref, cbv_ref,
                       wd_ref, gpost_ref, o_ref, sa_ref, sv_ref,
                       hn_ref, ua_ref, uv_ref, act_ref, ha_ref, hv_ref,
                       *, nc, blocks_per_seq, rc, rm):
    i = pl.program_id(0)
    c = pl.program_id(1)
    tm, D = hn_ref.shape
    hdr = SUBLANES

    @pl.when(c == 0)
    def _():
        hn_ref[...] = _rms(x_ref[...], gpre_ref[...]).astype(BF16)
        o_ref[...] = jnp.zeros(o_ref.shape, F32)

    slot = i % 2

    @pl.when(i % blocks_per_seq == 0)
    def _():
        ha_ref[1 - slot, c] = jnp.zeros(ha_ref.shape[2:], F32)
        hv_ref[1 - slot, c] = jnp.zeros(hv_ref.shape[2:], F32)

    ua_ref[0:hdr, :] = ha_ref[1 - slot, c]
    uv_ref[0:hdr, :] = hv_ref[1 - slot, c]
    wa = wua_ref[...].astype(BF16)
    wv = wuv_ref[...].astype(BF16)
    wd = wd_ref[...].astype(BF16)

    def conv(buf_ref, w_ref, b_ref, r0):
        return (b_ref[...] + w_ref[0:1, :] * buf_ref[r0 + hdr - 2:r0 + hdr - 2 + rc, :]
                + w_ref[1:2, :] * buf_ref[r0 + hdr - 1:r0 + hdr - 1 + rc, :]
                + w_ref[2:3, :] * buf_ref[r0 + hdr:r0 + hdr + rc, :])

    def up_proj(k):
        hk = hn_ref[k * rm:(k + 1) * rm, :]
        ua_ref[hdr + k * rm:hdr + (k + 1) * rm, :] = jnp.dot(hk, wa, preferred_element_type=F32)
        uv_ref[hdr + k * rm:hdr + (k + 1) * rm, :] = jnp.dot(hk, wv, preferred_element_type=F32)

    ngroups = tm // rm
    up_proj(0)
    for k in range(ngroups):
        if k + 1 < ngroups:
            up_proj(k + 1)
        rows = slice(k * rm, (k + 1) * rm)
        for r0 in range(k * rm, (k + 1) * rm, rc):
            ca = conv(ua_ref, cwa_ref, cba_ref, r0)
            cv = conv(uv_ref, cwv_ref, cbv_ref, r0)
            act_ref[r0:r0 + rc, :] = (jax.nn.silu(ca) * cv).astype(BF16)
        o_ref[rows, :] += jnp.dot(act_ref[rows, :], wd, preferred_element_type=F32)

    ta = ua_ref[tm:tm + hdr, :]
    tv = uv_ref[tm:tm + hdr, :]
    ha_ref[slot, c] = ta
    hv_ref[slot, c] = tv
    sa_ref[c] = ta[hdr - 2:hdr, :]
    sv_ref[c] = tv[hdr - 2:hdr, :]

    @pl.when(c == nc - 1)
    def _():
        o_ref[...] = x_ref[...] + _rms(o_ref[...], gpost_ref[...])


FFN_ROW_CHUNK = 64
FFN_ROW_GROUP = 512
FFN_VMEM_LIMIT = V7X_VMEM_BYTES - 4 * 1024 * 1024


def ffn_prompt(x, layer, g_pre, w_up, w_layer, conv_w, conv_b, w_down, g_post, batch, tm, tf):
    M, D = x.shape
    nc = D_FF // tf
    bps = SEQ // tm
    conv_b3 = conv_b.reshape(DEPTH, 1, 2 * D_FF)
    y, sa, sv = pl.pallas_call(
        functools.partial(_ffn_prompt_kernel, nc=nc, blocks_per_seq=bps, rc=FFN_ROW_CHUNK,
                          rm=FFN_ROW_GROUP),
        grid=(M // tm, nc),
        in_specs=[
            pl.BlockSpec((tm, D), lambda i, c: (i, 0)),
            pl.BlockSpec((None, 1, D), lambda i, c: (layer, 0, 0)),
            pl.BlockSpec((None, D, tf), lambda i, c: (w_layer, 0, c)),
            pl.BlockSpec((None, D, tf), lambda i, c: (w_layer, 0, nc + c)),
            pl.BlockSpec((None, 3, tf), lambda i, c: (layer, 0, c)),
            pl.BlockSpec((None, 3, tf), lambda i, c: (layer, 0, nc + c)),
            pl.BlockSpec((None, 1, tf), lambda i, c: (layer, 0, c)),
            pl.BlockSpec((None, 1, tf), lambda i, c: (layer, 0, nc + c)),
            pl.BlockSpec((None, tf, D), lambda i, c: (w_layer, c, 0)),
            pl.BlockSpec((None, 1, D), lambda i, c: (layer, 0, 0)),
        ],
        out_specs=[
            _once((tm, D), lambda i, c: (i, 0)),
            pl.BlockSpec((None, nc, 2, tf), lambda i, c: (i // bps, 0, 0, 0)),
            pl.BlockSpec((None, nc, 2, tf), lambda i, c: (i // bps, 0, 0, 0)),
        ],
        out_shape=[
            jax.ShapeDtypeStruct((M, D), F32),
            jax.ShapeDtypeStruct((batch, nc, 2, tf), F32),
            jax.ShapeDtypeStruct((batch, nc, 2, tf), F32),
        ],
        scratch_shapes=[
            pltpu.VMEM((tm, D), BF16),
            pltpu.VMEM((tm + SUBLANES, tf), F32),
            pltpu.VMEM((tm + SUBLANES, tf), F32),
            pltpu.VMEM((tm, tf), BF16),
            pltpu.VMEM((2, nc, SUBLANES, tf), F32),
            pltpu.VMEM((2, nc, SUBLANES, tf), F32),
        ],
        compiler_params=_params("arbitrary", "arbitrary", vmem=FFN_VMEM_LIMIT),
        name="ffn_prompt",
    )(x, g_pre.reshape(DEPTH, 1, D), w_up, w_up, conv_w, conv_w, conv_b3, conv_b3,
      w_down, g_post.reshape(DEPTH, 1, D))
    sa = sa.transpose(0, 2, 1, 3).reshape(batch, 2, D_FF)
    sv = sv.transpose(0, 2, 1, 3).reshape(batch, 2, D_FF)
    return y, jnp.concatenate([sa, sv], axis=-1)


def _ffn_sample_kernel(x_ref, gpre_ref, wua_ref, wuv_ref, cwa_ref, cwv_ref, cba_ref, cbv_ref,
                       wd_ref, gpost_ref, sta_ref, stv_ref, o_ref, na_ref, nv_ref, hn_ref, *, nc):
    c = pl.program_id(0)
    L = DEC_SEQ

    @pl.when(c == 0)
    def _():
        hn_ref[...] = _rms(x_ref[...], gpre_ref[...]).astype(BF16)

    hn = hn_ref[...]
    M = hn.shape[0]
    nb = M // L
    ua = jnp.dot(hn, wua_ref[...], preferred_element_type=F32)
    uv = jnp.dot(hn, wuv_ref[...], preferred_element_type=F32)
    tf = ua.shape[1]
    t = lax.broadcasted_iota(jnp.int32, (nb, L, tf), 1)

    def conv(u, st_ref, w_ref, b_ref):
        u3 = u.reshape(nb, L, tf)
        st = st_ref[...]
        prev2, prev1 = st[:, 0:1, :], st[:, 1:2, :]
        p1 = jnp.where(t == 0, prev1, pltpu.roll(u3, 1, 1))
        p2 = jnp.where(t == 0, prev2, jnp.where(t == 1, prev1, pltpu.roll(u3, 2, 1)))
        return (b_ref[...] + w_ref[0:1, :] * p2 + w_ref[1:2, :] * p1 + w_ref[2:3, :] * u3,
                u3[:, L - 2:L, :])

    ca, na = conv(ua, sta_ref, cwa_ref, cba_ref)
    cv, nv = conv(uv, stv_ref, cwv_ref, cbv_ref)
    na_ref[...] = na
    nv_ref[...] = nv
    act = (jax.nn.silu(ca) * cv).astype(BF16).reshape(M, tf)
    part = jnp.dot(act, wd_ref[...], preferred_element_type=F32)

    @pl.when(c == 0)
    def _():
        o_ref[...] = part

    @pl.when(c > 0)
    def _():
        o_ref[...] += part

    @pl.when(c == nc - 1)
    def _():
        o_ref[...] = x_ref[...] + _rms(o_ref[...], gpost_ref[...])


def ffn_sample(x, layer, g_pre, w_up, w_layer, conv_w, conv_b, w_down, g_post, conv_state, tf):
    M, D = x.shape
    nbatch = M // DEC_SEQ
    nc = D_FF // tf
    conv_b3 = conv_b.reshape(DEPTH, 1, 2 * D_FF)
    y, na, nv = pl.pallas_call(
        functools.partial(_ffn_sample_kernel, nc=nc),
        grid=(nc,),
        in_specs=[
            pl.BlockSpec((M, D), lambda c: (0, 0)),
            pl.BlockSpec((None, 1, D), lambda c: (layer, 0, 0)),
            pl.BlockSpec((None, D, tf), lambda c: (w_layer, 0, c)),
            pl.BlockSpec((None, D, tf), lambda c: (w_layer, 0, nc + c)),
            pl.BlockSpec((None, 3, tf), lambda c: (layer, 0, c)),
            pl.BlockSpec((None, 3, tf), lambda c: (layer, 0, nc + c)),
            pl.BlockSpec((None, 1, tf), lambda c: (layer, 0, c)),
            pl.BlockSpec((None, 1, tf), lambda c: (layer, 0, nc + c)),
            pl.BlockSpec((None, tf, D), lambda c: (w_layer, c, 0)),
            pl.BlockSpec((None, 1, D), lambda c: (layer, 0, 0)),
            pl.BlockSpec((None, nbatch, 2, tf), lambda c: (layer, 0, 0, c)),
            pl.BlockSpec((None, nbatch, 2, tf), lambda c: (layer, 0, 0, nc + c)),
        ],
        out_specs=[
            pl.BlockSpec((M, D), lambda c: (0, 0)),
            pl.BlockSpec((nbatch, 2, tf), lambda c: (0, 0, c)),
            pl.BlockSpec((nbatch, 2, tf), lambda c: (0, 0, c)),
        ],
        out_shape=[
            jax.ShapeDtypeStruct((M, D), F32),
            jax.ShapeDtypeStruct((nbatch, 2, D_FF), F32),
            jax.ShapeDtypeStruct((nbatch, 2, D_FF), F32),
        ],
        scratch_shapes=[pltpu.VMEM((M, D), BF16)],
        compiler_params=_params("arbitrary"),
        name="ffn_sample",
    )(x, g_pre.reshape(DEPTH, 1, D), w_up, w_up, conv_w, conv_w, conv_b3, conv_b3,
      w_down, g_post.reshape(DEPTH, 1, D), conv_state, conv_state)
    return y, jnp.concatenate([na, nv], axis=-1)


def kernel(x_prompt, x_sample, mem_prompt, cache_swa_k, cache_swa_v, state_ssm_re, state_ssm_im, state_ffn_conv, cache_mem_k, cache_mem_v, g_mix_pre, g_mix_post, w_qkv, w_attn_o, attn_sinks, w_ssm_in, ssm_lambda_re, ssm_lambda_im, ssm_log_step, ssm_b_re, ssm_b_im, ssm_c_re, ssm_c_im, ssm_d, w_ssm_glu, g_x_pre, g_x_post, g_mem, w_x_q, w_mem_k, w_mem_v, w_x_o, g_ffn_pre, g_ffn_post, w_ffn_up, ffn_conv_w, ffn_conv_b, w_ffn_down):
    B = x_prompt.shape[0]
    SB = x_sample.shape[0]
    xw = N_XHEADS * XHEAD_DIM
    nkc = N_KV_HEADS * HEAD_DIM
    xp = x_prompt.reshape(B * SEQ, D_MODEL)
    xs = x_sample.reshape(SB * DEC_SEQ, D_MODEL)
    mem = mem_prompt.reshape(B * N_MEM, D_MODEL)
    MS = SB * DEC_SEQ
    TMP = 1024
    cos_p, sin_p = _rope_tables(jnp.arange(SEQ))
    cos_s, sin_s = _rope_tables(PAST_LEN + jnp.arange(DEC_SEQ))
    ck_all = cache_swa_k.reshape(cache_swa_k.shape[0], SB, WINDOW, nkc)
    cv_all = cache_swa_v.reshape(cache_swa_v.shape[0], SB, WINDOW, nkc)
    cmk_all = cache_mem_k.reshape(DEPTH, SB, N_MEM, xw)
    cmv_all = cache_mem_v.reshape(DEPTH, SB, N_MEM, xw)
    w_qkv_bf = w_qkv.astype(BF16)
    w_attn_o_bf = w_attn_o.astype(BF16)
    w_ssm_in_bf = w_ssm_in.astype(BF16)
    w_glu_bf = w_ssm_glu.astype(BF16)
    w_x_q_bf = w_x_q.astype(BF16)
    w_x_o_bf = w_x_o.astype(BF16)
    w_mem_kv_bf = jnp.concatenate([w_mem_k, w_mem_v], axis=-1).astype(BF16)
    TD = 512

    swa_kp, swa_vp, swa_ks, swa_vs = [], [], [], []
    ssm_rp, ssm_ip, ssm_rs, ssm_is = [], [], [], []
    conv_p, conv_s, memk_p, memv_p = [], [], [], []
    for i in range(DEPTH):
        j = i // 2
        if i % 2 == 0:
            qkv_p = norm_matmul(xp, g_mix_pre, i, w_qkv_bf, j, TD)
            qkv_s = norm_matmul(xs, g_mix_pre, i, w_qkv_bf, j, MS)
            op, kp, vp, w_up_bf, w_down_bf = swa_prompt(qkv_p, attn_sinks, j, cos_p, sin_p, B,
                                                        w_ffn_up, w_ffn_down)
            sink_rows = jnp.repeat(attn_sinks[j].reshape(N_KV_HEADS, GQA_GROUP), DEC_SEQ,
                                   axis=1)[..., None]
            os_, kn, vn = swa_sample(qkv_s, ck_all, cv_all, j, sink_rows, cos_s, sin_s)
            swa_kp.append(kp.reshape(B, WINDOW, N_KV_HEADS, HEAD_DIM))
            swa_vp.append(vp.reshape(B, WINDOW, N_KV_HEADS, HEAD_DIM))
            swa_ks.append(kn.reshape(SB, WINDOW, N_KV_HEADS, HEAD_DIM))
            swa_vs.append(vn.reshape(SB, WINDOW, N_KV_HEADS, HEAD_DIM))
            xp = matmul_post(op, w_attn_o_bf, j, g_mix_post, i, xp, TD)
            xs = matmul_post(os_, w_attn_o_bf, j, g_mix_post, i, xs, MS)
        else:
            a_re, a_im, b_blk, c_blk = _s5_weights(
                ssm_lambda_re[j], ssm_lambda_im[j], ssm_log_step[j], ssm_b_re[j], ssm_b_im[j],
                ssm_c_re[j], ssm_c_im[j])
            d_row = ssm_d[j].reshape(1, D_MODEL)
            up_ = norm_matmul(xp, g_mix_pre, i, w_ssm_in_bf, j, TD)
            us_ = norm_matmul(xs, g_mix_pre, i, w_ssm_in_bf, j, MS)
            a_tiles = jnp.stack([a_re.reshape(N_SUPER, SUBLANES, LANES),
                                 a_im.reshape(N_SUPER, SUBLANES, LANES)], axis=1
                                ).reshape(2 * N_SUPER, SUBLANES, LANES)
            zero_state = jnp.zeros((B, N_SUPER, SUBLANES, LANES), F32)
            zp, rp, ip = s5_prompt(up_, zero_state, zero_state, a_tiles, b_blk, c_blk, d_row,
                                   B, 256)
            nstate = N_SSM_GROUPS * SSM_STATE
            zs, rn, im_ = s5_sample(us_, state_ssm_re[j].reshape(SB, nstate),
                                    state_ssm_im[j].reshape(SB, nstate),
                                    a_re.reshape(1, nstate), a_im.reshape(1, nstate),
                                    b_blk, c_blk, d_row)
            ssm_rp.append(rp.reshape(B, N_SSM_GROUPS, SSM_STATE))
            ssm_ip.append(ip.reshape(B, N_SSM_GROUPS, SSM_STATE))
            ssm_rs.append(rn.reshape(SB, N_SSM_GROUPS, SSM_STATE))
            ssm_is.append(im_.reshape(SB, N_SSM_GROUPS, SSM_STATE))
            xp = glu_post(zp, w_glu_bf, j, g_mix_post, i, xp, TD)
            xs = glu_post(zs, w_glu_bf, j, g_mix_post, i, xs, MS)
        mkv = norm_matmul(mem, g_mem, i, w_mem_kv_bf, i, B * N_MEM)
        mk, mv = mkv[:, :xw], mkv[:, xw:]
        memk_p.append(mk.reshape(B, N_MEM, N_XHEADS, XHEAD_DIM))
        memv_p.append(mv.reshape(B, N_MEM, N_XHEADS, XHEAD_DIM))
        xp = xattn_prompt(xp, i, g_x_pre, w_x_q_bf, mk, mv, w_x_o_bf, g_x_post, B, 2 * TD)
        qs = norm_matmul(xs, g_x_pre, i, w_x_q_bf, i, MS)
        as_ = xattn_sample(qs, cmk_all, cmv_all, i)
        xs = matmul_post(as_, w_x_o_bf, i, g_x_post, i, xs, MS)
        xp, cp = ffn_prompt(xp, i, g_ffn_pre, w_up_bf, i % 2, ffn_conv_w, ffn_conv_b, w_down_bf,
                            g_ffn_post, B, TMP, 512)
        xs, cs = ffn_sample(xs, i, g_ffn_pre, w_up_bf, i % 2, ffn_conv_w, ffn_conv_b, w_down_bf,
                            g_ffn_post, state_ffn_conv, 1408)
        conv_p.append(cp)
        conv_s.append(cs)
    return (xp.reshape(B, SEQ, D_MODEL), xs.reshape(SB, DEC_SEQ, D_MODEL),
            jnp.stack(swa_kp), jnp.stack(swa_vp), jnp.stack(swa_ks), jnp.stack(swa_vs),
            jnp.stack(ssm_rp), jnp.stack(ssm_ip), jnp.stack(ssm_rs), jnp.stack(ssm_is),
            jnp.stack(conv_p), jnp.stack(conv_s), jnp.stack(memk_p), jnp.stack(memv_p))
```

```python
import functools

import jax
import jax.numpy as jnp
from jax import lax
from jax.experimental import pallas as pl
from jax.experimental.pallas import tpu as pltpu

D_MODEL = 2048
SEQ = 4096
DEPTH = 4
DEC_SEQ = 8
PAST_LEN = 16384
HEAD_DIM = 64
N_HEADS = 32
N_KV_HEADS = 4
GQA_GROUP = 8
WINDOW = 128
ROPE_THETA = 10000.0
SSM_GROUP = 16
N_SSM_GROUPS = 128
SSM_STATE = 64
N_MEM = 256
N_XHEADS = 4
XHEAD_DIM = 128
D_FF = 5632
NORM_EPS = 1e-6
NEG_INF = -1e30

F32 = jnp.float32
BF16 = jnp.bfloat16

V7X_VMEM_BYTES = 64 * 1024 * 1024
VMEM_LIMIT = V7X_VMEM_BYTES - 8 * 1024 * 1024
LANES = 128
SUBLANES = 8
SSM_SUPER = 16
N_SUPER = N_SSM_GROUPS // SSM_SUPER
SUPER_IN = SSM_SUPER * SSM_GROUP
SUPER_STATE = SSM_SUPER * SSM_STATE
SLABS = SUPER_STATE // LANES


def _params(*sem, vmem=VMEM_LIMIT):
    return pltpu.CompilerParams(dimension_semantics=sem, vmem_limit_bytes=vmem)


def _rms(x, g):
    ms = jnp.mean(x * x, axis=-1, keepdims=True)
    return x * lax.rsqrt(ms + NORM_EPS) * g


def _bdot(a, b):
    return jnp.dot(a.astype(BF16), b.astype(BF16), preferred_element_type=F32)


def _resident(shape, index_map):
    return pl.BlockSpec(shape, index_map, pipeline_mode=pl.Buffered(1))


ROW_GROUP = 256


def _row_groups(tm):
    rg = min(ROW_GROUP, tm)
    return [slice(k * rg, (k + 1) * rg) for k in range(tm // rg)]


def _norm_matmul_kernel(x_ref, g_ref, w_ref, o_ref):
    g = g_ref[...]
    for rows in _row_groups(x_ref.shape[0]):
        hn = _rms(x_ref[rows, :], g).astype(BF16)
        o_ref[rows, :] = jnp.dot(hn, w_ref[...], preferred_element_type=F32).astype(o_ref.dtype)


def norm_matmul(x, g, g_layer, w, w_layer, tm, out_dtype=F32):
    M, D = x.shape
    N = w.shape[-1]
    return pl.pallas_call(
        _norm_matmul_kernel,
        grid=(M // tm,),
        in_specs=[
            pl.BlockSpec((tm, D), lambda i: (i, 0)),
            pl.BlockSpec((None, 1, D), lambda i: (g_layer, 0, 0)),
            _resident((None, D, N), lambda i: (w_layer, 0, 0)),
        ],
        out_specs=pl.BlockSpec((tm, N), lambda i: (i, 0)),
        out_shape=jax.ShapeDtypeStruct((M, N), out_dtype),
        compiler_params=_params("arbitrary"),
        name="norm_matmul",
    )(x, g.reshape(g.shape[0], 1, D), w)


def _matmul_post_kernel(a_ref, w_ref, g_ref, x_ref, o_ref):
    g = g_ref[...]
    for rows in _row_groups(x_ref.shape[0]):
        y = jnp.dot(a_ref[rows, :].astype(BF16), w_ref[...], preferred_element_type=F32)
        o_ref[rows, :] = x_ref[rows, :] + _rms(y, g)


def matmul_post(a, w, w_layer, g, g_layer, x, tm):
    M, K = a.shape
    D = w.shape[-1]
    return pl.pallas_call(
        _matmul_post_kernel,
        grid=(M // tm,),
        in_specs=[
            pl.BlockSpec((tm, K), lambda i: (i, 0)),
            _resident((None, K, D), lambda i: (w_layer, 0, 0)),
            pl.BlockSpec((None, 1, D), lambda i: (g_layer, 0, 0)),
            pl.BlockSpec((tm, D), lambda i: (i, 0)),
        ],
        out_specs=pl.BlockSpec((tm, D), lambda i: (i, 0)),
        out_shape=jax.ShapeDtypeStruct((M, D), F32),
        compiler_params=_params("arbitrary"),
        name="matmul_post",
    )(a, w, g.reshape(g.shape[0], 1, D), x)


GLU_COLS = 512


def _glu_post_kernel(z_ref, w_ref, g_ref, x_ref, o_ref, y_ref):
    D = o_ref.shape[1]
    g = g_ref[...]
    for rows in _row_groups(x_ref.shape[0]):
        z = z_ref[rows, :]
        for n in range(D // GLU_COLS):
            cols = slice(n * GLU_COLS, (n + 1) * GLU_COLS)
            gcols = slice(D + n * GLU_COLS, D + (n + 1) * GLU_COLS)
            val = jnp.dot(z, w_ref[:, cols], preferred_element_type=F32)
            gate = jnp.dot(z, w_ref[:, gcols], preferred_element_type=F32)
            y_ref[rows, cols] = val * jax.nn.sigmoid(gate)
        o_ref[rows, :] = x_ref[rows, :] + _rms(y_ref[rows, :], g)


def glu_post(z, w, w_layer, g, g_layer, x, tm):
    M, K = z.shape
    D = D_MODEL
    return pl.pallas_call(
        _glu_post_kernel,
        grid=(M // tm,),
        in_specs=[
            pl.BlockSpec((tm, K), lambda i: (i, 0)),
            _resident((None, K, 2 * D), lambda i: (w_layer, 0, 0)),
            pl.BlockSpec((None, 1, D), lambda i: (g_layer, 0, 0)),
            pl.BlockSpec((tm, D), lambda i: (i, 0)),
        ],
        out_specs=pl.BlockSpec((tm, D), lambda i: (i, 0)),
        out_shape=jax.ShapeDtypeStruct((M, D), F32),
        scratch_shapes=[pltpu.VMEM((tm, D), F32)],
        compiler_params=_params("arbitrary"),
        name="glu_post",
    )(z, w, g.reshape(g.shape[0], 1, D), x)


def _rope_chunk(xc, cos, sin_signed):
    lane = lax.broadcasted_iota(jnp.int32, xc.shape, 1)
    first_half = (lane & (HEAD_DIM - 1)) < (HEAD_DIM // 2)
    partner = jnp.where(first_half, pltpu.roll(xc, LANES - HEAD_DIM // 2, 1),
                        pltpu.roll(xc, HEAD_DIM // 2, 1))
    return xc * cos + partner * sin_signed


def _rope_tables(pos):
    half = HEAD_DIM // 2
    inv_freq = ROPE_THETA ** (-jnp.arange(half, dtype=F32) * 2.0 / HEAD_DIM)
    ang = pos.astype(F32)[:, None] * inv_freq[None, :]
    cos, sin = jnp.cos(ang), jnp.sin(ang)
    cos128 = jnp.tile(cos, (1, LANES // half))
    sin128 = jnp.tile(jnp.concatenate([-sin, sin], axis=1), (1, LANES // HEAD_DIM))
    return cos128, sin128


def _swa_prompt_kernel(sink_ref, q_ref, kv_ref, cos_ref, sin_ref, wu_ref, wd_ref,
                       o_ref, kc_ref, vc_ref, wub_ref, wdb_ref, kk_ref, vv_ref, *, layer):
    i = pl.program_id(1)
    blk = WINDOW
    wub_ref[...] = wu_ref[...].astype(BF16)
    wdb_ref[...] = wd_ref[...].astype(BF16)
    cos, sn = cos_ref[...], sin_ref[...]
    kv = kv_ref[...]
    nkc = N_KV_HEADS * HEAD_DIM
    k = jnp.concatenate([_rope_chunk(kv[:, c * LANES:(c + 1) * LANES], cos, sn)
                         for c in range(nkc // LANES)], axis=1)
    v = kv[:, nkc:2 * nkc]
    kc_ref[...] = k
    vc_ref[...] = v

    cur = i % 2
    prv = 1 - cur

    @pl.when(i == 0)
    def _():
        kk_ref[1] = jnp.zeros((blk, nkc), BF16)
        vv_ref[1] = jnp.zeros((blk, nkc), BF16)

    kb, vb = k.astype(BF16), v.astype(BF16)
    kk_ref[cur] = kb
    vv_ref[cur] = vb
    kk = jnp.concatenate([kk_ref[prv], kb], axis=0)
    vv = jnp.concatenate([vv_ref[prv], vb], axis=0)
    rows = GQA_GROUP * blk
    qi = lax.broadcasted_iota(jnp.int32, (rows, 2 * blk), 0) & (blk - 1)
    si = lax.broadcasted_iota(jnp.int32, (rows, 2 * blk), 1)
    d = blk + qi - si
    mask = (d >= 0) & (d < WINDOW) & ((si >= blk) | (i > 0))

    chunks_per_group = GQA_GROUP * HEAD_DIM // LANES

    def scores(kh):
        hs = slice(kh * HEAD_DIM, (kh + 1) * HEAD_DIM)
        pieces = []
        for c in range(kh * chunks_per_group, (kh + 1) * chunks_per_group):
            qc = (_rope_chunk(q_ref[:, c * LANES:(c + 1) * LANES], cos, sn)
                  * (HEAD_DIM ** -0.5)).astype(BF16)
            pieces += [qc[:, hh * HEAD_DIM:(hh + 1) * HEAD_DIM] for hh in range(LANES // HEAD_DIM)]
        qs = jnp.concatenate(pieces, axis=0)
        return lax.dot_general(qs, kk[:, hs], (((1,), (1,)), ((), ())),
                               preferred_element_type=F32)

    ahead = 2
    s_all = [scores(kh) for kh in range(min(ahead, N_KV_HEADS))]
    outs = []
    for kh in range(N_KV_HEADS):
        if kh + ahead < N_KV_HEADS:
            s_all.append(scores(kh + ahead))
        hs = slice(kh * HEAD_DIM, (kh + 1) * HEAD_DIM)
        sk = jnp.concatenate([jnp.full((blk, 1), sink_ref[layer, kh * GQA_GROUP + j], F32)
                              for j in range(GQA_GROUP)], axis=0)
        s = jnp.where(mask, s_all[kh], NEG_INF)
        mx = jnp.maximum(jnp.max(s, axis=1, keepdims=True), sk)
        p = jnp.exp(s - mx)
        den = jnp.sum(p, axis=1, keepdims=True) + jnp.exp(sk - mx)
        o = jnp.dot(p.astype(BF16), vv[:, hs], preferred_element_type=F32) / den
        outs += [o[j * blk:(j + 1) * blk, :] for j in range(GQA_GROUP)]
    o_ref[...] = jnp.concatenate(outs, axis=1).astype(o_ref.dtype)


def swa_prompt(qkv, sinks, layer, cos, sin, batch, w_up, w_down):
    blk = WINDOW
    nb = SEQ // blk
    nq = N_HEADS * HEAD_DIM
    nkv = 2 * N_KV_HEADS * HEAD_DIM
    per_layer = batch * nb // 2
    ur, dr = D_MODEL // per_layer, D_FF // per_layer

    def w_in(b, i):
        n = b * nb + i
        return (2 * layer + n // per_layer, n % per_layer, 0)

    def w_out(b, i):
        n = b * nb + i
        return (n // per_layer, n % per_layer, 0)

    o, kc, vc, w_up_bf, w_down_bf = pl.pallas_call(
        functools.partial(_swa_prompt_kernel, layer=layer),
        grid=(batch, nb),
        in_specs=[
            pl.BlockSpec(memory_space=pltpu.SMEM),
            pl.BlockSpec((blk, nq), lambda b, i: (b * nb + i, 0)),
            pl.BlockSpec((blk, nkv), lambda b, i: (b * nb + i, nq // nkv)),
            pl.BlockSpec((blk, LANES), lambda b, i: (i, 0)),
            pl.BlockSpec((blk, LANES), lambda b, i: (i, 0)),
            pl.BlockSpec((None, ur, 2 * D_FF), w_in),
            pl.BlockSpec((None, dr, D_MODEL), w_in),
        ],
        out_specs=[
            pl.BlockSpec((blk, nq), lambda b, i: (b * nb + i, 0)),
            pl.BlockSpec((None, blk, nkv // 2), lambda b, i: (b, 0, 0)),
            pl.BlockSpec((None, blk, nkv // 2), lambda b, i: (b, 0, 0)),
            pl.BlockSpec((None, ur, 2 * D_FF), w_out),
            pl.BlockSpec((None, dr, D_MODEL), w_out),
        ],
        out_shape=[
            jax.ShapeDtypeStruct((batch * SEQ, nq), BF16),
            jax.ShapeDtypeStruct((batch, blk, nkv // 2), F32),
            jax.ShapeDtypeStruct((batch, blk, nkv // 2), F32),
            jax.ShapeDtypeStruct((2, D_MODEL, 2 * D_FF), BF16),
            jax.ShapeDtypeStruct((2, D_FF, D_MODEL), BF16),
        ],
        scratch_shapes=[pltpu.VMEM((2, blk, nkv // 2), BF16),
                        pltpu.VMEM((2, blk, nkv // 2), BF16)],
        compiler_params=_params("arbitrary", "arbitrary"),
        name="swa_prompt",
    )(sinks, qkv, qkv, cos, sin, w_up, w_down)
    return o, kc, vc, w_up_bf, w_down_bf


def _swa_sample_kernel(sink_ref, qkv_ref, ck_ref, cv_ref, cos_ref, sin_ref,
                       o_ref, nk_ref, nv_ref, *, nb):
    L = DEC_SEQ
    wb = WINDOW
    nq = N_HEADS * HEAD_DIM
    nkc = N_KV_HEADS * HEAD_DIM
    cos, sn = cos_ref[...], sin_ref[...]
    rows = GQA_GROUP * L
    t_q = lax.broadcasted_iota(jnp.int32, (rows, wb), 0) % L
    c_k = lax.broadcasted_iota(jnp.int32, (rows, wb), 1)
    mask_c = c_k >= t_q + 1
    t_q2 = lax.broadcasted_iota(jnp.int32, (rows, L), 0) % L
    t_k2 = lax.broadcasted_iota(jnp.int32, (rows, L), 1)
    mask_n = t_k2 <= t_q2
    for b in range(nb):
        x = qkv_ref[b * L:(b + 1) * L, :]
        qr = jnp.concatenate([_rope_chunk(x[:, c * LANES:(c + 1) * LANES], cos, sn)
                              for c in range(nq // LANES)], axis=1) * (HEAD_DIM ** -0.5)
        kn = jnp.concatenate([_rope_chunk(x[:, nq + c * LANES:nq + (c + 1) * LANES], cos, sn)
                              for c in range(nkc // LANES)], axis=1)
        vn = x[:, nq + nkc:nq + 2 * nkc]
        ck = ck_ref[b]
        cv = cv_ref[b]
        nk_ref[b, 0:wb - L, :] = ck[L:wb, :]
        nk_ref[b, wb - L:wb, :] = kn
        nv_ref[b, 0:wb - L, :] = cv[L:wb, :]
        nv_ref[b, wb - L:wb, :] = vn
        ckb, cvb, knb, vnb = ck.astype(BF16), cv.astype(BF16), kn.astype(BF16), vn.astype(BF16)
        dn = (((1,), (1,)), ((), ()))
        scores = []
        for kh in range(N_KV_HEADS):
            hs = slice(kh * HEAD_DIM, (kh + 1) * HEAD_DIM)
            qs = jnp.concatenate(
                [qr[:, (kh * GQA_GROUP + j) * HEAD_DIM:(kh * GQA_GROUP + j + 1) * HEAD_DIM]
                 for j in range(GQA_GROUP)], axis=0).astype(BF16)
            scores.append((lax.dot_general(qs, ckb[:, hs], dn, preferred_element_type=F32),
                           lax.dot_general(qs, knb[:, hs], dn, preferred_element_type=F32)))
        outs = []
        for kh in range(N_KV_HEADS):
            hs = slice(kh * HEAD_DIM, (kh + 1) * HEAD_DIM)
            s_c = jnp.where(mask_c, scores[kh][0], NEG_INF)
            s_n = jnp.where(mask_n, scores[kh][1], NEG_INF)
            sk = sink_ref[kh]
            mx = jnp.maximum(jnp.maximum(jnp.max(s_c, axis=1, keepdims=True),
                                         jnp.max(s_n, axis=1, keepdims=True)), sk)
            p_c = jnp.exp(s_c - mx)
            p_n = jnp.exp(s_n - mx)
            den = (jnp.sum(p_c, axis=1, keepdims=True) + jnp.sum(p_n, axis=1, keepdims=True)
                   + jnp.exp(sk - mx))
            o = (jnp.dot(p_c.astype(BF16), cvb[:, hs], preferred_element_type=F32)
                 + jnp.dot(p_n.astype(BF16), vnb[:, hs], preferred_element_type=F32)) / den
            outs.extend(o[j * L:(j + 1) * L, :] for j in range(GQA_GROUP))
        o_ref[b * L:(b + 1) * L, :] = jnp.concatenate(outs, axis=1).astype(o_ref.dtype)


def swa_sample(qkv, cache_k, cache_v, layer, sink_rows, cos, sin, nb=8):
    nbatch = cache_k.shape[1]
    nq = N_HEADS * HEAD_DIM
    nkc = N_KV_HEADS * HEAD_DIM
    L = DEC_SEQ
    return pl.pallas_call(
        functools.partial(_swa_sample_kernel, nb=nb),
        grid=(nbatch // nb,),
        in_specs=[
            pl.BlockSpec((N_KV_HEADS, GQA_GROUP * L, 1), lambda g: (0, 0, 0)),
            pl.BlockSpec((nb * L, nq + 2 * nkc), lambda g: (g, 0)),
            pl.BlockSpec((None, nb, WINDOW, nkc), lambda g: (layer, g, 0, 0)),
            pl.BlockSpec((None, nb, WINDOW, nkc), lambda g: (layer, g, 0, 0)),
            pl.BlockSpec((L, LANES), lambda g: (0, 0)),
            pl.BlockSpec((L, LANES), lambda g: (0, 0)),
        ],
        out_specs=[
            pl.BlockSpec((nb * L, nq), lambda g: (g, 0)),
            pl.BlockSpec((nb, WINDOW, nkc), lambda g: (g, 0, 0)),
            pl.BlockSpec((nb, WINDOW, nkc), lambda g: (g, 0, 0)),
        ],
        out_shape=[
            jax.ShapeDtypeStruct((nbatch * L, nq), BF16),
            jax.ShapeDtypeStruct((nbatch, WINDOW, nkc), F32),
            jax.ShapeDtypeStruct((nbatch, WINDOW, nkc), F32),
        ],
        compiler_params=_params("arbitrary"),
        name="swa_sample",
    )(sink_rows, qkv, cache_k, cache_v, cos, sin)


_XHEADS = [slice(h * XHEAD_DIM, (h + 1) * XHEAD_DIM) for h in range(N_XHEADS)]


def _xattn_scores(q, mk):
    return [lax.dot_general((q[:, hs] * (XHEAD_DIM ** -0.5)).astype(BF16), mk[:, hs].astype(BF16),
                            (((1,), (1,)), ((), ())), preferred_element_type=F32)
            for hs in _XHEADS]


def _xattn_attend(scores, mv):
    outs = []
    for hs, s in zip(_XHEADS, scores):
        mx = jnp.max(s, axis=1, keepdims=True)
        p = jnp.exp(s - mx)
        den = jnp.sum(p, axis=1, keepdims=True)
        outs.append(jnp.dot(p.astype(BF16), mv[:, hs].astype(BF16),
                            preferred_element_type=F32) / den)
    return jnp.concatenate(outs, axis=1)


def _xattn_heads(q, mk, mv):
    return _xattn_attend(_xattn_scores(q, mk), mv)


def _xattn_prompt_kernel(x_ref, gpre_ref, wq_ref, mk_ref, mv_ref, wo_ref, gpost_ref, o_ref):
    mk = mk_ref[...].astype(BF16)
    mv = mv_ref[...].astype(BF16)
    gpre, gpost = gpre_ref[...], gpost_ref[...]
    groups = _row_groups(x_ref.shape[0])

    def project(rows):
        q = jnp.dot(_rms(x_ref[rows, :], gpre).astype(BF16), wq_ref[...],
                    preferred_element_type=F32)
        return _xattn_scores(q, mk)

    scores = project(groups[0])
    for k, rows in enumerate(groups):
        nxt = project(groups[k + 1]) if k + 1 < len(groups) else None
        a = _xattn_attend(scores, mv).astype(BF16)
        y = jnp.dot(a, wo_ref[...], preferred_element_type=F32)
        o_ref[rows, :] = x_ref[rows, :] + _rms(y, gpost)
        scores = nxt


def xattn_prompt(x, layer, g_pre, w_q, mk, mv, w_o, g_post, batch, tm):
    nq = SEQ // tm
    xw = N_XHEADS * XHEAD_DIM
    D = D_MODEL
    return pl.pallas_call(
        _xattn_prompt_kernel,
        grid=(batch, nq),
        in_specs=[
            pl.BlockSpec((tm, D), lambda b, i: (b * nq + i, 0)),
            pl.BlockSpec((None, 1, D), lambda b, i: (layer, 0, 0)),
            _resident((None, D, xw), lambda b, i: (layer, 0, 0)),
            pl.BlockSpec((N_MEM, xw), lambda b, i: (b, 0)),
            pl.BlockSpec((N_MEM, xw), lambda b, i: (b, 0)),
            _resident((None, xw, D), lambda b, i: (layer, 0, 0)),
            pl.BlockSpec((None, 1, D), lambda b, i: (layer, 0, 0)),
        ],
        out_specs=pl.BlockSpec((tm, D), lambda b, i: (b * nq + i, 0)),
        out_shape=jax.ShapeDtypeStruct((batch * SEQ, D), F32),
        compiler_params=_params("arbitrary", "arbitrary"),
        name="xattn_prompt",
    )(x, g_pre.reshape(DEPTH, 1, D), w_q, mk, mv, w_o, g_post.reshape(DEPTH, 1, D))


def _xattn_sample_kernel(q_ref, mk_ref, mv_ref, o_ref, *, nb):
    L = DEC_SEQ
    for b in range(nb):
        o_ref[b * L:(b + 1) * L, :] = _xattn_heads(
            q_ref[b * L:(b + 1) * L, :], mk_ref[b], mv_ref[b]).astype(o_ref.dtype)


def xattn_sample(q, cache_k, cache_v, layer, nb=8):
    nbatch = cache_k.shape[1]
    L = DEC_SEQ
    xw = N_XHEADS * XHEAD_DIM
    return pl.pallas_call(
        functools.partial(_xattn_sample_kernel, nb=nb),
        grid=(nbatch // nb,),
        in_specs=[
            pl.BlockSpec((nb * L, xw), lambda g: (g, 0)),
            pl.BlockSpec((None, nb, N_MEM, xw), lambda g: (layer, g, 0, 0)),
            pl.BlockSpec((None, nb, N_MEM, xw), lambda g: (layer, g, 0, 0)),
        ],
        out_specs=pl.BlockSpec((nb * L, xw), lambda g: (g, 0)),
        out_shape=jax.ShapeDtypeStruct((nbatch * L, xw), BF16),
        compiler_params=_params("arbitrary"),
        name="xattn_sample",
    )(q, cache_k, cache_v)


def _s5_prompt_kernel(u_ref, h0r_ref, h0i_ref, a_ref, b_ref, c_ref, d_ref,
                      z_ref, hr_ref, hi_ref, x_ref, hst_ref, *, T):
    c = pl.program_id(1)
    nslab = 2 * SLABS
    G = T // SUBLANES

    @pl.when(c == 0)
    def _():
        for sb in range(N_SUPER):
            hst_ref[2 * sb] = h0r_ref[sb]
            hst_ref[2 * sb + 1] = h0i_ref[sb]

    def slab_rows(sb, s):
        return pl.ds((sb * nslab + s) * S5_SLAB_PITCH, SUBLANES)

    for sb in range(N_SUPER):
        bu = _bdot(u_ref[:, sb * SUPER_IN:(sb + 1) * SUPER_IN], b_ref[sb])
        for s in range(nslab):
            x_ref[:, slab_rows(sb, s), :] = bu[:, s * LANES:(s + 1) * LANES].reshape(
                G, SUBLANES, LANES)

    a = [a_ref[k] for k in range(2 * N_SUPER)]
    h_init = tuple(hst_ref[k] for k in range(2 * N_SUPER))

    def step(t, h):
        g = lax.shift_right_logical(t, 3)
        r = lax.bitwise_and(t, SUBLANES - 1)
        new = []
        for sb in range(N_SUPER):
            ir = pl.ds((sb * nslab) * S5_SLAB_PITCH + r, SUBLANES, stride=S5_SLAB_PITCH)
            ii = pl.ds((sb * nslab + SLABS) * S5_SLAB_PITCH + r, SUBLANES, stride=S5_SLAB_PITCH)
            ar, ai = a[2 * sb], a[2 * sb + 1]
            hr, hi = h[2 * sb], h[2 * sb + 1]
            nr = ar * hr - ai * hi + x_ref[g, ir, :]
            ni = ar * hi + ai * hr + x_ref[g, ii, :]
            x_ref[g, ir, :] = nr
            x_ref[g, ii, :] = ni
            new += [nr, ni]
        return tuple(new)

    h_fin = lax.fori_loop(0, T, step, h_init)
    for sb in range(N_SUPER):
        hst_ref[2 * sb] = h_fin[2 * sb]
        hst_ref[2 * sb + 1] = h_fin[2 * sb + 1]
        hr_ref[sb] = h_fin[2 * sb]
        hi_ref[sb] = h_fin[2 * sb + 1]

    for sb in range(N_SUPER):
        hcat = jnp.concatenate([x_ref[:, slab_rows(sb, s), :].reshape(T, LANES)
                                for s in range(nslab)], axis=1)
        cols = slice(sb * SUPER_IN, (sb + 1) * SUPER_IN)
        y = _bdot(hcat, c_ref[sb]) + d_ref[:, cols] * u_ref[:, cols]
        z_ref[:, cols] = jax.nn.gelu(y).astype(z_ref.dtype)


S5_SLAB_PITCH = 12


def s5_prompt(u, h0r, h0i, a_tiles, b_blk, c_blk, d_row, batch, T):
    nc = SEQ // T
    st = (None, N_SUPER, SUBLANES, LANES)
    return pl.pallas_call(
        functools.partial(_s5_prompt_kernel, T=T),
        grid=(batch, nc),
        in_specs=[
            pl.BlockSpec((T, D_MODEL), lambda b, c: (b * nc + c, 0)),
            pl.BlockSpec(st, lambda b, c: (b, 0, 0, 0)),
            pl.BlockSpec(st, lambda b, c: (b, 0, 0, 0)),
            _resident((2 * N_SUPER, SUBLANES, LANES), lambda b, c: (0, 0, 0)),
            _resident((N_SUPER, SUPER_IN, 2 * SUPER_STATE), lambda b, c: (0, 0, 0)),
            _resident((N_SUPER, 2 * SUPER_STATE, SUPER_IN), lambda b, c: (0, 0, 0)),
            _resident((1, D_MODEL), lambda b, c: (0, 0)),
        ],
        out_specs=[
            pl.BlockSpec((T, D_MODEL), lambda b, c: (b * nc + c, 0)),
            pl.BlockSpec(st, lambda b, c: (b, 0, 0, 0)),
            pl.BlockSpec(st, lambda b, c: (b, 0, 0, 0)),
        ],
        out_shape=[
            jax.ShapeDtypeStruct((batch * SEQ, D_MODEL), BF16),
            jax.ShapeDtypeStruct((batch, N_SUPER, SUBLANES, LANES), F32),
            jax.ShapeDtypeStruct((batch, N_SUPER, SUBLANES, LANES), F32),
        ],
        scratch_shapes=[
            pltpu.VMEM((T // SUBLANES, N_SUPER * 2 * SLABS * S5_SLAB_PITCH, LANES), F32),
            pltpu.VMEM((2 * N_SUPER, SUBLANES, LANES), F32),
        ],
        compiler_params=_params("arbitrary", "arbitrary"),
        name="s5_prompt",
    )(u, h0r, h0i, a_tiles, b_blk, c_blk, d_row)


def _s5_sample_kernel(u_ref, h0r_ref, h0i_ref, ar_ref, ai_ref, b_ref, c_ref, d_ref,
                      z_ref, hr_ref, hi_ref, x_ref, *, nbatch):
    L = DEC_SEQ
    R = nbatch * L
    for sb in range(N_SUPER):
        cols = slice(sb * SUPER_IN, (sb + 1) * SUPER_IN)
        bu = _bdot(u_ref[:, cols], b_ref[sb])
        for s in range(2 * SLABS):
            x_ref[s * R:(s + 1) * R, :] = bu[:, s * LANES:(s + 1) * LANES]
        for s in range(SLABS):
            st = slice(sb * SUPER_STATE + s * LANES, sb * SUPER_STATE + (s + 1) * LANES)
            ar, ai = ar_ref[:, st], ai_ref[:, st]
            hr, hi = h0r_ref[:, st], h0i_ref[:, st]
            for t in range(L):
                rr = pl.ds(s * R + t, nbatch, stride=L)
                ri = pl.ds((SLABS + s) * R + t, nbatch, stride=L)
                nr = ar * hr - ai * hi + x_ref[rr, :]
                ni = ar * hi + ai * hr + x_ref[ri, :]
                x_ref[rr, :] = nr
                x_ref[ri, :] = ni
                hr, hi = nr, ni
            hr_ref[:, st] = hr
            hi_ref[:, st] = hi
        hcat = jnp.concatenate([x_ref[s * R:(s + 1) * R, :] for s in range(2 * SLABS)], axis=1)
        y = _bdot(hcat, c_ref[sb]) + d_ref[:, cols] * u_ref[:, cols]
        z_ref[:, cols] = jax.nn.gelu(y).astype(z_ref.dtype)


def s5_sample(u, h0r, h0i, a_re, a_im, b_blk, c_blk, d_row):
    rows = u.shape[0]
    nbatch = rows // DEC_SEQ
    nstate = N_SSM_GROUPS * SSM_STATE
    full = lambda shape: pl.BlockSpec(shape, lambda i: (0,) * len(shape))
    return pl.pallas_call(
        functools.partial(_s5_sample_kernel, nbatch=nbatch),
        grid=(1,),
        in_specs=[
            full((rows, D_MODEL)), full((nbatch, nstate)), full((nbatch, nstate)),
            full((1, nstate)), full((1, nstate)),
            full((N_SUPER, SUPER_IN, 2 * SUPER_STATE)),
            full((N_SUPER, 2 * SUPER_STATE, SUPER_IN)),
            full((1, D_MODEL)),
        ],
        out_specs=[full((rows, D_MODEL)), full((nbatch, nstate)), full((nbatch, nstate))],
        out_shape=[
            jax.ShapeDtypeStruct((rows, D_MODEL), BF16),
            jax.ShapeDtypeStruct((nbatch, nstate), F32),
            jax.ShapeDtypeStruct((nbatch, nstate), F32),
        ],
        scratch_shapes=[pltpu.VMEM((2 * SLABS * rows, LANES), F32)],
        compiler_params=_params("arbitrary"),
        name="s5_sample",
    )(u, h0r, h0i, a_re, a_im, b_blk, c_blk, d_row)


def _s5_weights(lam_re, lam_im, log_step, b_re, b_im, c_re, c_im):
    lam = lax.complex(lam_re.astype(F32), lam_im.astype(F32))
    dt = jnp.exp(log_step.astype(F32))[:, None]
    abar = jnp.exp(lam * dt)
    bbar = ((abar - 1.0) / lam)[..., None] * lax.complex(b_re.astype(F32), b_im.astype(F32))
    eye = jnp.eye(SSM_SUPER, dtype=F32)

    def b_layout(m):
        m = m.reshape(N_SUPER, SSM_SUPER, SSM_STATE, SSM_GROUP)
        return jnp.einsum('sgph,gk->sghkp', m, eye).reshape(N_SUPER, SUPER_IN, SUPER_STATE)

    def c_layout(m):
        m = m.reshape(N_SUPER, SSM_SUPER, SSM_GROUP, SSM_STATE)
        return jnp.einsum('sghp,gk->sgpkh', m, eye).reshape(N_SUPER, SUPER_STATE, SUPER_IN)

    b_blk = jnp.concatenate([b_layout(bbar.real), b_layout(bbar.imag)], axis=2).astype(BF16)
    c_blk = jnp.concatenate([c_layout(c_re.astype(F32)), -c_layout(c_im.astype(F32))],
                            axis=1).astype(BF16)
    return abar.real, abar.imag, b_blk, c_blk


def _ffn_prompt_kernel(x_ref, gpre_ref, wua_ref, wuv_ref, cwa_ref, cwv_ref, cba_ref, cbv_ref,
                       wd_ref, gpost_ref, o_ref, sa_ref, sv_ref,
                       hn_ref, ua_ref, uv_ref, act_ref, ha_ref, hv_ref,
                       *, nc, blocks_per_seq, rc, rm):
    i = pl.program_id(0)
    c = pl.program_id(1)
    tm, D = hn_ref.shape
    hdr = SUBLANES

    @pl.when(c == 0)
    def _():
        hn_ref[...] = _rms(x_ref[...], gpre_ref[...]).astype(BF16)
        o_ref[...] = jnp.zeros(o_ref.shape, F32)

    slot = i % 2

    @pl.when(i % blocks_per_seq == 0)
    def _():
        ha_ref[1 - slot, c] = jnp.zeros(ha_ref.shape[2:], F32)
        hv_ref[1 - slot, c] = jnp.zeros(hv_ref.shape[2:], F32)

    ua_ref[0:hdr, :] = ha_ref[1 - slot, c]
    uv_ref[0:hdr, :] = hv_ref[1 - slot, c]
    wa = wua_ref[...].astype(BF16)
    wv = wuv_ref[...].astype(BF16)
    wd = wd_ref[...].astype(BF16)

    def conv(buf_ref, w_ref, b_ref, r0):
        return (b_ref[...] + w_ref[0:1, :] * buf_ref[r0 + hdr - 2:r0 + hdr - 2 + rc, :]
                + w_ref[1:2, :] * buf_ref[r0 + hdr - 1:r0 + hdr - 1 + rc, :]
                + w_ref[2:3, :] * buf_ref[r0 + hdr:r0 + hdr + rc, :])

    def up_proj(k):
        hk = hn_ref[k * rm:(k + 1) * rm, :]
        ua_ref[hdr + k * rm:hdr + (k + 1) * rm, :] = jnp.dot(hk, wa, preferred_element_type=F32)
        uv_ref[hdr + k * rm:hdr + (k + 1) * rm, :] = jnp.dot(hk, wv, preferred_element_type=F32)

    ngroups = tm // rm
    up_proj(0)
    for k in range(ngroups):
        if k + 1 < ngroups:
            up_proj(k + 1)
        rows = slice(k * rm, (k + 1) * rm)
        for r0 in range(k * rm, (k + 1) * rm, rc):
            ca = conv(ua_ref, cwa_ref, cba_ref, r0)
            cv = conv(uv_ref, cwv_ref, cbv_ref, r0)
            act_ref[r0:r0 + rc, :] = (jax.nn.silu(ca) * cv).astype(BF16)
        o_ref[rows, :] += jnp.dot(act_ref[rows, :], wd, preferred_element_type=F32)

    ta = ua_ref[tm:tm + hdr, :]
    tv = uv_ref[tm:tm + hdr, :]
    ha_ref[slot, c] = ta
    hv_ref[slot, c] = tv
    sa_ref[c] = ta[hdr - 2:hdr, :]
    sv_ref[c] = tv[hdr - 2:hdr, :]

    @pl.when(c == nc - 1)
    def _():
        o_ref[...] = x_ref[...] + _rms(o_ref[...], gpost_ref[...])


FFN_ROW_CHUNK = 64
FFN_ROW_GROUP = 512
FFN_VMEM_LIMIT = V7X_VMEM_BYTES - 4 * 1024 * 1024


def ffn_prompt(x, layer, g_pre, w_up, w_layer, conv_w, conv_b, w_down, g_post, batch, tm, tf):
    M, D = x.shape
    nc = D_FF // tf
    bps = SEQ // tm
    conv_b3 = conv_b.reshape(DEPTH, 1, 2 * D_FF)
    y, sa, sv = pl.pallas_call(
        functools.partial(_ffn_prompt_kernel, nc=nc, blocks_per_seq=bps, rc=FFN_ROW_CHUNK,
                          rm=FFN_ROW_GROUP),
        grid=(M // tm, nc),
        in_specs=[
            pl.BlockSpec((tm, D), lambda i, c: (i, 0)),
            pl.BlockSpec((None, 1, D), lambda i, c: (layer, 0, 0)),
            pl.BlockSpec((None, D, tf), lambda i, c: (w_layer, 0, c)),
            pl.BlockSpec((None, D, tf), lambda i, c: (w_layer, 0, nc + c)),
            pl.BlockSpec((None, 3, tf), lambda i, c: (layer, 0, c)),
            pl.BlockSpec((None, 3, tf), lambda i, c: (layer, 0, nc + c)),
            pl.BlockSpec((None, 1, tf), lambda i, c: (layer, 0, c)),
            pl.BlockSpec((None, 1, tf), lambda i, c: (layer, 0, nc + c)),
            pl.BlockSpec((None, tf, D), lambda i, c: (w_layer, c, 0)),
            pl.BlockSpec((None, 1, D), lambda i, c: (layer, 0, 0)),
        ],
        out_specs=[
            _resident((tm, D), lambda i, c: (i, 0)),
            pl.BlockSpec((None, nc, 2, tf), lambda i, c: (i // bps, 0, 0, 0)),
            pl.BlockSpec((None, nc, 2, tf), lambda i, c: (i // bps, 0, 0, 0)),
        ],
        out_shape=[
            jax.ShapeDtypeStruct((M, D), F32),
            jax.ShapeDtypeStruct((batch, nc, 2, tf), F32),
            jax.ShapeDtypeStruct((batch, nc, 2, tf), F32),
        ],
        scratch_shapes=[
            pltpu.VMEM((tm, D), BF16),
            pltpu.VMEM((tm + SUBLANES, tf), F32),
            pltpu.VMEM((tm + SUBLANES, tf), F32),
            pltpu.VMEM((tm, tf), BF16),
            pltpu.VMEM((2, nc, SUBLANES, tf), F32),
            pltpu.VMEM((2, nc, SUBLANES, tf), F32),
        ],
        compiler_params=_params("arbitrary", "arbitrary", vmem=FFN_VMEM_LIMIT),
        name="ffn_prompt",
    )(x, g_pre.reshape(DEPTH, 1, D), w_up, w_up, conv_w, conv_w, conv_b3, conv_b3,
      w_down, g_post.reshape(DEPTH, 1, D))
    sa = sa.transpose(0, 2, 1, 3).reshape(batch, 2, D_FF)
    sv = sv.transpose(0, 2, 1, 3).reshape(batch, 2, D_FF)
    return y, jnp.concatenate([sa, sv], axis=-1)


def _ffn_sample_kernel(x_ref, gpre_ref, wua_ref, wuv_ref, cwa_ref, cwv_ref, cba_ref, cbv_ref,
                       wd_ref, gpost_ref, sta_ref, stv_ref, o_ref, na_ref, nv_ref, hn_ref, *, nc):
    c = pl.program_id(0)
    L = DEC_SEQ

    @pl.when(c == 0)
    def _():
        hn_ref[...] = _rms(x_ref[...], gpre_ref[...]).astype(BF16)

    hn = hn_ref[...]
    M = hn.shape[0]
    nb = M // L
    ua = jnp.dot(hn, wua_ref[...], preferred_element_type=F32)
    uv = jnp.dot(hn, wuv_ref[...], preferred_element_type=F32)
    tf = ua.shape[1]
    t = lax.broadcasted_iota(jnp.int32, (nb, L, tf), 1)

    def conv(u, st_ref, w_ref, b_ref):
        u3 = u.reshape(nb, L, tf)
        st = st_ref[...]
        prev2, prev1 = st[:, 0:1, :], st[:, 1:2, :]
        p1 = jnp.where(t == 0, prev1, pltpu.roll(u3, 1, 1))
        p2 = jnp.where(t == 0, prev2, jnp.where(t == 1, prev1, pltpu.roll(u3, 2, 1)))
        return (b_ref[...] + w_ref[0:1, :] * p2 + w_ref[1:2, :] * p1 + w_ref[2:3, :] * u3,
                u3[:, L - 2:L, :])

    ca, na = conv(ua, sta_ref, cwa_ref, cba_ref)
    cv, nv = conv(uv, stv_ref, cwv_ref, cbv_ref)
    na_ref[...] = na
    nv_ref[...] = nv
    act = (jax.nn.silu(ca) * cv).astype(BF16).reshape(M, tf)
    part = jnp.dot(act, wd_ref[...], preferred_element_type=F32)

    @pl.when(c == 0)
    def _():
        o_ref[...] = part

    @pl.when(c > 0)
    def _():
        o_ref[...] += part

    @pl.when(c == nc - 1)
    def _():
        o_ref[...] = x_ref[...] + _rms(o_ref[...], gpost_ref[...])


def ffn_sample(x, layer, g_pre, w_up, w_layer, conv_w, conv_b, w_down, g_post, conv_state, tf):
    M, D = x.shape
    nbatch = M // DEC_SEQ
    nc = D_FF // tf
    conv_b3 = conv_b.reshape(DEPTH, 1, 2 * D_FF)
    y, na, nv = pl.pallas_call(
        functools.partial(_ffn_sample_kernel, nc=nc),
        grid=(nc,),
        in_specs=[
            pl.BlockSpec((M, D), lambda c: (0, 0)),
            pl.BlockSpec((None, 1, D), lambda c: (layer, 0, 0)),
            pl.BlockSpec((None, D, tf), lambda c: (w_layer, 0, c)),
            pl.BlockSpec((None, D, tf), lambda c: (w_layer, 0, nc + c)),
            pl.BlockSpec((None, 3, tf), lambda c: (layer, 0, c)),
            pl.BlockSpec((None, 3, tf), lambda c: (layer, 0, nc + c)),
            pl.BlockSpec((None, 1, tf), lambda c: (layer, 0, c)),
            pl.BlockSpec((None, 1, tf), lambda c: (layer, 0, nc + c)),
            pl.BlockSpec((None, tf, D), lambda c: (w_layer, c, 0)),
            pl.BlockSpec((None, 1, D), lambda c: (layer, 0, 0)),
            pl.BlockSpec((None, nbatch, 2, tf), lambda c: (layer, 0, 0, c)),
            pl.BlockSpec((None, nbatch, 2, tf), lambda c: (layer, 0, 0, nc + c)),
        ],
        out_specs=[
            pl.BlockSpec((M, D), lambda c: (0, 0)),
            pl.BlockSpec((nbatch, 2, tf), lambda c: (0, 0, c)),
            pl.BlockSpec((nbatch, 2, tf), lambda c: (0, 0, c)),
        ],
        out_shape=[
            jax.ShapeDtypeStruct((M, D), F32),
            jax.ShapeDtypeStruct((nbatch, 2, D_FF), F32),
            jax.ShapeDtypeStruct((nbatch, 2, D_FF), F32),
        ],
        scratch_shapes=[pltpu.VMEM((M, D), BF16)],
        compiler_params=_params("arbitrary"),
        name="ffn_sample",
    )(x, g_pre.reshape(DEPTH, 1, D), w_up, w_up, conv_w, conv_w, conv_b3, conv_b3,
      w_down, g_post.reshape(DEPTH, 1, D), conv_state, conv_state)
    return y, jnp.concatenate([na, nv], axis=-1)


def kernel(x_prompt, x_sample, mem_prompt, cache_swa_k, cache_swa_v, state_ssm_re, state_ssm_im, state_ffn_conv, cache_mem_k, cache_mem_v, g_mix_pre, g_mix_post, w_qkv, w_attn_o, attn_sinks, w_ssm_in, ssm_lambda_re, ssm_lambda_im, ssm_log_step, ssm_b_re, ssm_b_im, ssm_c_re, ssm_c_im, ssm_d, w_ssm_glu, g_x_pre, g_x_post, g_mem, w_x_q, w_mem_k, w_mem_v, w_x_o, g_ffn_pre, g_ffn_post, w_ffn_up, ffn_conv_w, ffn_conv_b, w_ffn_down):
    B = x_prompt.shape[0]
    SB = x_sample.shape[0]
    xw = N_XHEADS * XHEAD_DIM
    nkc = N_KV_HEADS * HEAD_DIM
    xp = x_prompt.reshape(B * SEQ, D_MODEL)
    xs = x_sample.reshape(SB * DEC_SEQ, D_MODEL)
    mem = mem_prompt.reshape(B * N_MEM, D_MODEL)
    MS = SB * DEC_SEQ
    TMP = 1024
    cos_p, sin_p = _rope_tables(jnp.arange(SEQ))
    cos_s, sin_s = _rope_tables(PAST_LEN + jnp.arange(DEC_SEQ))
    ck_all = cache_swa_k.reshape(cache_swa_k.shape[0], SB, WINDOW, nkc)
    cv_all = cache_swa_v.reshape(cache_swa_v.shape[0], SB, WINDOW, nkc)
    cmk_all = cache_mem_k.reshape(DEPTH, SB, N_MEM, xw)
    cmv_all = cache_mem_v.reshape(DEPTH, SB, N_MEM, xw)
    w_qkv_bf = w_qkv.astype(BF16)
    w_attn_o_bf = w_attn_o.astype(BF16)
    w_ssm_in_bf = w_ssm_in.astype(BF16)
    w_glu_bf = w_ssm_glu.astype(BF16)
    w_x_q_bf = w_x_q.astype(BF16)
    w_x_o_bf = w_x_o.astype(BF16)
    w_mem_kv_bf = jnp.concatenate([w_mem_k, w_mem_v], axis=-1).astype(BF16)
    TD = 512

    swa_kp, swa_vp, swa_ks, swa_vs = [], [], [], []
    ssm_rp, ssm_ip, ssm_rs, ssm_is = [], [], [], []
    conv_p, conv_s, memk_p, memv_p = [], [], [], []
    for i in range(DEPTH):
        j = i // 2
        if i % 2 == 0:
            qkv_p = norm_matmul(xp, g_mix_pre, i, w_qkv_bf, j, TD)
            qkv_s = norm_matmul(xs, g_mix_pre, i, w_qkv_bf, j, MS)
            op, kp, vp, w_up_bf, w_down_bf = swa_prompt(qkv_p, attn_sinks, j, cos_p, sin_p, B,
                                                        w_ffn_up, w_ffn_down)
            sink_rows = jnp.repeat(attn_sinks[j].reshape(N_KV_HEADS, GQA_GROUP), DEC_SEQ,
                                   axis=1)[..., None]
            os_, kn, vn = swa_sample(qkv_s, ck_all, cv_all, j, sink_rows, cos_s, sin_s)
            swa_kp.append(kp.reshape(B, WINDOW, N_KV_HEADS, HEAD_DIM))
            swa_vp.append(vp.reshape(B, WINDOW, N_KV_HEADS, HEAD_DIM))
            swa_ks.append(kn.reshape(SB, WINDOW, N_KV_HEADS, HEAD_DIM))
            swa_vs.append(vn.reshape(SB, WINDOW, N_KV_HEADS, HEAD_DIM))
            xp = matmul_post(op, w_attn_o_bf, j, g_mix_post, i, xp, TD)
            xs = matmul_post(os_, w_attn_o_bf, j, g_mix_post, i, xs, MS)
        else:
            a_re, a_im, b_blk, c_blk = _s5_weights(
                ssm_lambda_re[j], ssm_lambda_im[j], ssm_log_step[j], ssm_b_re[j], ssm_b_im[j],
                ssm_c_re[j], ssm_c_im[j])
            d_row = ssm_d[j].reshape(1, D_MODEL)
            up_ = norm_matmul(xp, g_mix_pre, i, w_ssm_in_bf, j, TD)
            us_ = norm_matmul(xs, g_mix_pre, i, w_ssm_in_bf, j, MS)
            a_tiles = jnp.stack([a_re.reshape(N_SUPER, SUBLANES, LANES),
                                 a_im.reshape(N_SUPER, SUBLANES, LANES)], axis=1
                                ).reshape(2 * N_SUPER, SUBLANES, LANES)
            zero_state = jnp.zeros((B, N_SUPER, SUBLANES, LANES), F32)
            zp, rp, ip = s5_prompt(up_, zero_state, zero_state, a_tiles, b_blk, c_blk, d_row,
                                   B, 256)
            nstate = N_SSM_GROUPS * SSM_STATE
            zs, rn, im_ = s5_sample(us_, state_ssm_re[j].reshape(SB, nstate),
                                    state_ssm_im[j].reshape(SB, nstate),
                                    a_re.reshape(1, nstate), a_im.reshape(1, nstate),
                                    b_blk, c_blk, d_row)
            ssm_rp.append(rp.reshape(B, N_SSM_GROUPS, SSM_STATE))
            ssm_ip.append(ip.reshape(B, N_SSM_GROUPS, SSM_STATE))
            ssm_rs.append(rn.reshape(SB, N_SSM_GROUPS, SSM_STATE))
            ssm_is.append(im_.reshape(SB, N_SSM_GROUPS, SSM_STATE))
            xp = glu_post(zp, w_glu_bf, j, g_mix_post, i, xp, TD)
            xs = glu_post(zs, w_glu_bf, j, g_mix_post, i, xs, MS)
        mkv = norm_matmul(mem, g_mem, i, w_mem_kv_bf, i, B * N_MEM)
        mk, mv = mkv[:, :xw], mkv[:, xw:]
        memk_p.append(mk.reshape(B, N_MEM, N_XHEADS, XHEAD_DIM))
        memv_p.append(mv.reshape(B, N_MEM, N_XHEADS, XHEAD_DIM))
        xp = xattn_prompt(xp, i, g_x_pre, w_x_q_bf, mk, mv, w_x_o_bf, g_x_post, B, 2 * TD)
        qs = norm_matmul(xs, g_x_pre, i, w_x_q_bf, i, MS)
        as_ = xattn_sample(qs, cmk_all, cmv_all, i)
        xs = matmul_post(as_, w_x_o_bf, i, g_x_post, i, xs, MS)
        xp, cp = ffn_prompt(xp, i, g_ffn_pre, w_up_bf, i % 2, ffn_conv_w, ffn_conv_b, w_down_bf,
                            g_ffn_post, B, TMP, 512)
        xs, cs = ffn_sample(xs, i, g_ffn_pre, w_up_bf, i % 2, ffn_conv_w, ffn_conv_b, w_down_bf,
                            g_ffn_post, state_ffn_conv, 1408)
        conv_p.append(cp)
        conv_s.append(cs)
    return (xp.reshape(B, SEQ, D_MODEL), xs.reshape(SB, DEC_SEQ, D_MODEL),
            jnp.stack(swa_kp), jnp.stack(swa_vp), jnp.stack(swa_ks), jnp.stack(swa_vs),
            jnp.stack(ssm_rp), jnp.stack(ssm_ip), jnp.stack(ssm_rs), jnp.stack(ssm_is),
            jnp.stack(conv_p), jnp.stack(conv_s), jnp.stack(memk_p), jnp.stack(memv_p))
```

```python
import functools

import jax
import jax.numpy as jnp
from jax import lax
from jax.experimental import pallas as pl
from jax.experimental.pallas import tpu as pltpu

D_MODEL = 2048
SEQ = 4096
DEPTH = 4
DEC_SEQ = 8
PAST_LEN = 16384
HEAD_DIM = 64
N_HEADS = 32
N_KV_HEADS = 4
GQA_GROUP = 8
WINDOW = 128
ROPE_THETA = 10000.0
SSM_GROUP = 16
N_SSM_GROUPS = 128
SSM_STATE = 64
N_MEM = 256
N_XHEADS = 4
XHEAD_DIM = 128
D_FF = 5632
NORM_EPS = 1e-6
NEG_INF = -1e30

F32 = jnp.float32
BF16 = jnp.bfloat16

V7X_VMEM_BYTES = 64 * 1024 * 1024
VMEM_LIMIT = V7X_VMEM_BYTES - 8 * 1024 * 1024
LANES = 128
SUBLANES = 8
SSM_SUPER = 16
N_SUPER = N_SSM_GROUPS // SSM_SUPER
SUPER_IN = SSM_SUPER * SSM_GROUP
SUPER_STATE = SSM_SUPER * SSM_STATE
SLABS = SUPER_STATE // LANES


def _params(*sem, vmem=VMEM_LIMIT):
    return pltpu.CompilerParams(dimension_semantics=sem, vmem_limit_bytes=vmem)


def _rms(x, g):
    ms = jnp.mean(x * x, axis=-1, keepdims=True)
    return x * lax.rsqrt(ms + NORM_EPS) * g


def _bdot(a, b):
    return jnp.dot(a.astype(BF16), b.astype(BF16), preferred_element_type=F32)


def _resident(shape, index_map):
    return pl.BlockSpec(shape, index_map, pipeline_mode=pl.Buffered(1))


ROW_GROUP = 256


def _row_groups(tm):
    rg = min(ROW_GROUP, tm)
    return [slice(k * rg, (k + 1) * rg) for k in range(tm // rg)]


def _norm_matmul_kernel(x_ref, g_ref, w_ref, o_ref):
    g = g_ref[...]
    for rows in _row_groups(x_ref.shape[0]):
        hn = _rms(x_ref[rows, :], g).astype(BF16)
        o_ref[rows, :] = jnp.dot(hn, w_ref[...], preferred_element_type=F32).astype(o_ref.dtype)


def norm_matmul(x, g, g_layer, w, w_layer, tm, out_dtype=F32):
    M, D = x.shape
    N = w.shape[-1]
    return pl.pallas_call(
        _norm_matmul_kernel,
        grid=(M // tm,),
        in_specs=[
            pl.BlockSpec((tm, D), lambda i: (i, 0)),
            pl.BlockSpec((None, 1, D), lambda i: (g_layer, 0, 0)),
            _resident((None, D, N), lambda i: (w_layer, 0, 0)),
        ],
        out_specs=pl.BlockSpec((tm, N), lambda i: (i, 0)),
        out_shape=jax.ShapeDtypeStruct((M, N), out_dtype),
        compiler_params=_params("arbitrary"),
        name="norm_matmul",
    )(x, g.reshape(g.shape[0], 1, D), w)


def _matmul_post_kernel(a_ref, w_ref, g_ref, x_ref, o_ref):
    g = g_ref[...]
    for rows in _row_groups(x_ref.shape[0]):
        y = jnp.dot(a_ref[rows, :].astype(BF16), w_ref[...], preferred_element_type=F32)
        o_ref[rows, :] = x_ref[rows, :] + _rms(y, g)


def matmul_post(a, w, w_layer, g, g_layer, x, tm):
    M, K = a.shape
    D = w.shape[-1]
    return pl.pallas_call(
        _matmul_post_kernel,
        grid=(M // tm,),
        in_specs=[
            pl.BlockSpec((tm, K), lambda i: (i, 0)),
            _resident((None, K, D), lambda i: (w_layer, 0, 0)),
            pl.BlockSpec((None, 1, D), lambda i: (g_layer, 0, 0)),
            pl.BlockSpec((tm, D), lambda i: (i, 0)),
        ],
        out_specs=pl.BlockSpec((tm, D), lambda i: (i, 0)),
        out_shape=jax.ShapeDtypeStruct((M, D), F32),
        compiler_params=_params("arbitrary"),
        name="matmul_post",
    )(a, w, g.reshape(g.shape[0], 1, D), x)


GLU_COLS = 512


def _glu_post_kernel(z_ref, w_ref, g_ref, x_ref, o_ref, y_ref):
    D = o_ref.shape[1]
    g = g_ref[...]
    for rows in _row_groups(x_ref.shape[0]):
        z = z_ref[rows, :]
        for n in range(D // GLU_COLS):
            cols = slice(n * GLU_COLS, (n + 1) * GLU_COLS)
            gcols = slice(D + n * GLU_COLS, D + (n + 1) * GLU_COLS)
            val = jnp.dot(z, w_ref[:, cols], preferred_element_type=F32)
            gate = jnp.dot(z, w_ref[:, gcols], preferred_element_type=F32)
            y_ref[rows, cols] = val * jax.nn.sigmoid(gate)
        o_ref[rows, :] = x_ref[rows, :] + _rms(y_ref[rows, :], g)


def glu_post(z, w, w_layer, g, g_layer, x, tm):
    M, K = z.shape
    D = D_MODEL
    return pl.pallas_call(
        _glu_post_kernel,
        grid=(M // tm,),
        in_specs=[
            pl.BlockSpec((tm, K), lambda i: (i, 0)),
            _resident((None, K, 2 * D), lambda i: (w_layer, 0, 0)),
            pl.BlockSpec((None, 1, D), lambda i: (g_layer, 0, 0)),
            pl.BlockSpec((tm, D), lambda i: (i, 0)),
        ],
        out_specs=pl.BlockSpec((tm, D), lambda i: (i, 0)),
        out_shape=jax.ShapeDtypeStruct((M, D), F32),
        scratch_shapes=[pltpu.VMEM((tm, D), F32)],
        compiler_params=_params("arbitrary"),
        name="glu_post",
    )(z, w, g.reshape(g.shape[0], 1, D), x)


def _rope_chunk(xc, cos, sin_signed):
    lane = lax.broadcasted_iota(jnp.int32, xc.shape, 1)
    first_half = (lane & (HEAD_DIM - 1)) < (HEAD_DIM // 2)
    partner = jnp.where(first_half, pltpu.roll(xc, LANES - HEAD_DIM // 2, 1),
                        pltpu.roll(xc, HEAD_DIM // 2, 1))
    return xc * cos + partner * sin_signed


def _rope_tables(pos):
    half = HEAD_DIM // 2
    inv_freq = ROPE_THETA ** (-jnp.arange(half, dtype=F32) * 2.0 / HEAD_DIM)
    ang = pos.astype(F32)[:, None] * inv_freq[None, :]
    cos, sin = jnp.cos(ang), jnp.sin(ang)
    cos128 = jnp.tile(cos, (1, LANES // half))
    sin128 = jnp.tile(jnp.concatenate([-sin, sin], axis=1), (1, LANES // HEAD_DIM))
    return cos128, sin128


def _swa_prompt_kernel(sink_ref, q_ref, kv_ref, cos_ref, sin_ref, wu_ref, wd_ref,
                       o_ref, kc_ref, vc_ref, wub_ref, wdb_ref, kk_ref, vv_ref, *, layer):
    i = pl.program_id(1)
    blk = WINDOW
    wub_ref[...] = wu_ref[...].astype(BF16)
    wdb_ref[...] = wd_ref[...].astype(BF16)
    cos, sn = cos_ref[...], sin_ref[...]
    kv = kv_ref[...]
    nkc = N_KV_HEADS * HEAD_DIM
    k = jnp.concatenate([_rope_chunk(kv[:, c * LANES:(c + 1) * LANES], cos, sn)
                         for c in range(nkc // LANES)], axis=1)
    v = kv[:, nkc:2 * nkc]
    kc_ref[...] = k
    vc_ref[...] = v

    cur = i % 2
    prv = 1 - cur

    @pl.when(i == 0)
    def _():
        kk_ref[1] = jnp.zeros((blk, nkc), BF16)
        vv_ref[1] = jnp.zeros((blk, nkc), BF16)

    kb, vb = k.astype(BF16), v.astype(BF16)
    kk_ref[cur] = kb
    vv_ref[cur] = vb
    kk = jnp.concatenate([kk_ref[prv], kb], axis=0)
    vv = jnp.concatenate([vv_ref[prv], vb], axis=0)
    rows = GQA_GROUP * blk
    qi = lax.broadcasted_iota(jnp.int32, (rows, 2 * blk), 0) & (blk - 1)
    si = lax.broadcasted_iota(jnp.int32, (rows, 2 * blk), 1)
    d = blk + qi - si
    mask = (d >= 0) & (d < WINDOW) & ((si >= blk) | (i > 0))

    chunks_per_group = GQA_GROUP * HEAD_DIM // LANES

    def scores(kh):
        hs = slice(kh * HEAD_DIM, (kh + 1) * HEAD_DIM)
        pieces = []
        for c in range(kh * chunks_per_group, (kh + 1) * chunks_per_group):
            qc = (_rope_chunk(q_ref[:, c * LANES:(c + 1) * LANES], cos, sn)
                  * (HEAD_DIM ** -0.5)).astype(BF16)
            pieces += [qc[:, hh * HEAD_DIM:(hh + 1) * HEAD_DIM] for hh in range(LANES // HEAD_DIM)]
        qs = jnp.concatenate(pieces, axis=0)
        return lax.dot_general(qs, kk[:, hs], (((1,), (1,)), ((), ())),
                               preferred_element_type=F32)

    ahead = 2
    s_all = [scores(kh) for kh in range(min(ahead, N_KV_HEADS))]
    outs = []
    for kh in range(N_KV_HEADS):
        if kh + ahead < N_KV_HEADS:
            s_all.append(scores(kh + ahead))
        hs = slice(kh * HEAD_DIM, (kh + 1) * HEAD_DIM)
        sk = jnp.concatenate([jnp.full((blk, 1), sink_ref[layer, kh * GQA_GROUP + j], F32)
                              for j in range(GQA_GROUP)], axis=0)
        s = jnp.where(mask, s_all[kh], NEG_INF)
        mx = jnp.maximum(jnp.max(s, axis=1, keepdims=True), sk)
        p = jnp.exp(s - mx)
        den = jnp.sum(p, axis=1, keepdims=True) + jnp.exp(sk - mx)
        o = jnp.dot(p.astype(BF16), vv[:, hs], preferred_element_type=F32) / den
        outs += [o[j * blk:(j + 1) * blk, :] for j in range(GQA_GROUP)]
    o_ref[...] = jnp.concatenate(outs, axis=1).astype(o_ref.dtype)


def swa_prompt(qkv, sinks, layer, cos, sin, batch, w_up, w_down):
    blk = WINDOW
    nb = SEQ // blk
    nq = N_HEADS * HEAD_DIM
    nkv = 2 * N_KV_HEADS * HEAD_DIM
    per_layer = batch * nb // 2
    ur, dr = D_MODEL // per_layer, D_FF // per_layer

    def w_in(b, i):
        n = b * nb + i
        return (2 * layer + n // per_layer, n % per_layer, 0)

    def w_out(b, i):
        n = b * nb + i
        return (n // per_layer, n % per_layer, 0)

    o, kc, vc, w_up_bf, w_down_bf = pl.pallas_call(
        functools.partial(_swa_prompt_kernel, layer=layer),
        grid=(batch, nb),
        in_specs=[
            pl.BlockSpec(memory_space=pltpu.SMEM),
            pl.BlockSpec((blk, nq), lambda b, i: (b * nb + i, 0)),
            pl.BlockSpec((blk, nkv), lambda b, i: (b * nb + i, nq // nkv)),
            pl.BlockSpec((blk, LANES), lambda b, i: (i, 0)),
            pl.BlockSpec((blk, LANES), lambda b, i: (i, 0)),
            pl.BlockSpec((None, ur, 2 * D_FF), w_in),
            pl.BlockSpec((None, dr, D_MODEL), w_in),
        ],
        out_specs=[
            pl.BlockSpec((blk, nq), lambda b, i: (b * nb + i, 0)),
            pl.BlockSpec((None, blk, nkv // 2), lambda b, i: (b, 0, 0)),
            pl.BlockSpec((None, blk, nkv // 2), lambda b, i: (b, 0, 0)),
            pl.BlockSpec((None, ur, 2 * D_FF), w_out),
            pl.BlockSpec((None, dr, D_MODEL), w_out),
        ],
        out_shape=[
            jax.ShapeDtypeStruct((batch * SEQ, nq), BF16),
            jax.ShapeDtypeStruct((batch, blk, nkv // 2), F32),
            jax.ShapeDtypeStruct((batch, blk, nkv // 2), F32),
            jax.ShapeDtypeStruct((2, D_MODEL, 2 * D_FF), BF16),
            jax.ShapeDtypeStruct((2, D_FF, D_MODEL), BF16),
        ],
        scratch_shapes=[pltpu.VMEM((2, blk, nkv // 2), BF16),
                        pltpu.VMEM((2, blk, nkv // 2), BF16)],
        compiler_params=_params("arbitrary", "arbitrary"),
        name="swa_prompt",
    )(sinks, qkv, qkv, cos, sin, w_up, w_down)
    return o, kc, vc, w_up_bf, w_down_bf


def _swa_sample_kernel(sink_ref, qkv_ref, ck_ref, cv_ref, cos_ref, sin_ref,
                       o_ref, nk_ref, nv_ref, *, nb):
    L = DEC_SEQ
    wb = WINDOW
    nq = N_HEADS * HEAD_DIM
    nkc = N_KV_HEADS * HEAD_DIM
    cos, sn = cos_ref[...], sin_ref[...]
    rows = GQA_GROUP * L
    t_q = lax.broadcasted_iota(jnp.int32, (rows, wb), 0) % L
    c_k = lax.broadcasted_iota(jnp.int32, (rows, wb), 1)
    mask_c = c_k >= t_q + 1
    t_q2 = lax.broadcasted_iota(jnp.int32, (rows, L), 0) % L
    t_k2 = lax.broadcasted_iota(jnp.int32, (rows, L), 1)
    mask_n = t_k2 <= t_q2
    for b in range(nb):
        x = qkv_ref[b * L:(b + 1) * L, :]
        qr = jnp.concatenate([_rope_chunk(x[:, c * LANES:(c + 1) * LANES], cos, sn)
                              for c in range(nq // LANES)], axis=1) * (HEAD_DIM ** -0.5)
        kn = jnp.concatenate([_rope_chunk(x[:, nq + c * LANES:nq + (c + 1) * LANES], cos, sn)
                              for c in range(nkc // LANES)], axis=1)
        vn = x[:, nq + nkc:nq + 2 * nkc]
        knb, vnb = kn.astype(BF16), vn.astype(BF16)
        ckb, cvb = [], []
        for kh in range(N_KV_HEADS):
            hs = slice(kh * HEAD_DIM, (kh + 1) * HEAD_DIM)
            ck_h = ck_ref[b, :, kh, :]
            cv_h = cv_ref[b, :, kh, :]
            nk_ref[b, 0:wb - L, kh, :] = ck_h[L:wb, :]
            nk_ref[b, wb - L:wb, kh, :] = kn[:, hs]
            nv_ref[b, 0:wb - L, kh, :] = cv_h[L:wb, :]
            nv_ref[b, wb - L:wb, kh, :] = vn[:, hs]
            ckb.append(ck_h.astype(BF16))
            cvb.append(cv_h.astype(BF16))
        dn = (((1,), (1,)), ((), ()))
        scores = []
        for kh in range(N_KV_HEADS):
            hs = slice(kh * HEAD_DIM, (kh + 1) * HEAD_DIM)
            qs = jnp.concatenate(
                [qr[:, (kh * GQA_GROUP + j) * HEAD_DIM:(kh * GQA_GROUP + j + 1) * HEAD_DIM]
                 for j in range(GQA_GROUP)], axis=0).astype(BF16)
            scores.append((lax.dot_general(qs, ckb[kh], dn, preferred_element_type=F32),
                           lax.dot_general(qs, knb[:, hs], dn, preferred_element_type=F32)))
        outs = []
        for kh in range(N_KV_HEADS):
            hs = slice(kh * HEAD_DIM, (kh + 1) * HEAD_DIM)
            s_c = jnp.where(mask_c, scores[kh][0], NEG_INF)
            s_n = jnp.where(mask_n, scores[kh][1], NEG_INF)
            sk = sink_ref[kh]
            mx = jnp.maximum(jnp.maximum(jnp.max(s_c, axis=1, keepdims=True),
                                         jnp.max(s_n, axis=1, keepdims=True)), sk)
            p_c = jnp.exp(s_c - mx)
            p_n = jnp.exp(s_n - mx)
            den = (jnp.sum(p_c, axis=1, keepdims=True) + jnp.sum(p_n, axis=1, keepdims=True)
                   + jnp.exp(sk - mx))
            o = (jnp.dot(p_c.astype(BF16), cvb[kh], preferred_element_type=F32)
                 + jnp.dot(p_n.astype(BF16), vnb[:, hs], preferred_element_type=F32)) / den
            outs.extend(o[j * L:(j + 1) * L, :] for j in range(GQA_GROUP))
        o_ref[b * L:(b + 1) * L, :] = jnp.concatenate(outs, axis=1).astype(o_ref.dtype)


def swa_sample(qkv, cache_k, cache_v, layer, sink_rows, cos, sin, nb=8):
    nbatch = cache_k.shape[1]
    nq = N_HEADS * HEAD_DIM
    nkc = N_KV_HEADS * HEAD_DIM
    L = DEC_SEQ
    return pl.pallas_call(
        functools.partial(_swa_sample_kernel, nb=nb),
        grid=(nbatch // nb,),
        in_specs=[
            pl.BlockSpec((N_KV_HEADS, GQA_GROUP * L, 1), lambda g: (0, 0, 0)),
            pl.BlockSpec((nb * L, nq + 2 * nkc), lambda g: (g, 0)),
            pl.BlockSpec((None, nb, WINDOW, N_KV_HEADS, HEAD_DIM), lambda g: (layer, g, 0, 0, 0)),
            pl.BlockSpec((None, nb, WINDOW, N_KV_HEADS, HEAD_DIM), lambda g: (layer, g, 0, 0, 0)),
            pl.BlockSpec((L, LANES), lambda g: (0, 0)),
            pl.BlockSpec((L, LANES), lambda g: (0, 0)),
        ],
        out_specs=[
            pl.BlockSpec((nb * L, nq), lambda g: (g, 0)),
            pl.BlockSpec((nb, WINDOW, N_KV_HEADS, HEAD_DIM), lambda g: (g, 0, 0, 0)),
            pl.BlockSpec((nb, WINDOW, N_KV_HEADS, HEAD_DIM), lambda g: (g, 0, 0, 0)),
        ],
        out_shape=[
            jax.ShapeDtypeStruct((nbatch * L, nq), BF16),
            jax.ShapeDtypeStruct((nbatch, WINDOW, N_KV_HEADS, HEAD_DIM), F32),
            jax.ShapeDtypeStruct((nbatch, WINDOW, N_KV_HEADS, HEAD_DIM), F32),
        ],
        compiler_params=_params("arbitrary"),
        name="swa_sample",
    )(sink_rows, qkv, cache_k, cache_v, cos, sin)


_XHEADS = [slice(h * XHEAD_DIM, (h + 1) * XHEAD_DIM) for h in range(N_XHEADS)]


def _xattn_scores(q, mk):
    return [lax.dot_general((q[:, hs] * (XHEAD_DIM ** -0.5)).astype(BF16), mk[:, hs].astype(BF16),
                            (((1,), (1,)), ((), ())), preferred_element_type=F32)
            for hs in _XHEADS]


def _xattn_attend(scores, mv):
    outs = []
    for hs, s in zip(_XHEADS, scores):
        mx = jnp.max(s, axis=1, keepdims=True)
        p = jnp.exp(s - mx)
        den = jnp.sum(p, axis=1, keepdims=True)
        outs.append(jnp.dot(p.astype(BF16), mv[:, hs].astype(BF16),
                            preferred_element_type=F32) / den)
    return jnp.concatenate(outs, axis=1)


def _xattn_heads(q, mk, mv):
    return _xattn_attend(_xattn_scores(q, mk), mv)


def _xattn_prompt_kernel(x_ref, gpre_ref, wq_ref, mk_ref, mv_ref, wo_ref, gpost_ref, o_ref):
    mk = mk_ref[...].astype(BF16)
    mv = mv_ref[...].astype(BF16)
    gpre, gpost = gpre_ref[...], gpost_ref[...]
    groups = _row_groups(x_ref.shape[0])

    def project(rows):
        q = jnp.dot(_rms(x_ref[rows, :], gpre).astype(BF16), wq_ref[...],
                    preferred_element_type=F32)
        return _xattn_scores(q, mk)

    scores = project(groups[0])
    for k, rows in enumerate(groups):
        nxt = project(groups[k + 1]) if k + 1 < len(groups) else None
        a = _xattn_attend(scores, mv).astype(BF16)
        y = jnp.dot(a, wo_ref[...], preferred_element_type=F32)
        o_ref[rows, :] = x_ref[rows, :] + _rms(y, gpost)
        scores = nxt


def xattn_prompt(x, layer, g_pre, w_q, mk, mv, w_o, g_post, batch, tm):
    nq = SEQ // tm
    xw = N_XHEADS * XHEAD_DIM
    D = D_MODEL
    return pl.pallas_call(
        _xattn_prompt_kernel,
        grid=(batch, nq),
        in_specs=[
            pl.BlockSpec((tm, D), lambda b, i: (b * nq + i, 0)),
            pl.BlockSpec((None, 1, D), lambda b, i: (layer, 0, 0)),
            _resident((None, D, xw), lambda b, i: (layer, 0, 0)),
            pl.BlockSpec((N_MEM, xw), lambda b, i: (b, 0)),
            pl.BlockSpec((N_MEM, xw), lambda b, i: (b, 0)),
            _resident((None, xw, D), lambda b, i: (layer, 0, 0)),
            pl.BlockSpec((None, 1, D), lambda b, i: (layer, 0, 0)),
        ],
        out_specs=pl.BlockSpec((tm, D), lambda b, i: (b * nq + i, 0)),
        out_shape=jax.ShapeDtypeStruct((batch * SEQ, D), F32),
        compiler_params=_params("arbitrary", "arbitrary"),
        name="xattn_prompt",
    )(x, g_pre.reshape(DEPTH, 1, D), w_q, mk, mv, w_o, g_post.reshape(DEPTH, 1, D))


def _xattn_sample_kernel(q_ref, mk_ref, mv_ref, o_ref, *, nb):
    L = DEC_SEQ
    for b in range(nb):
        o_ref[b * L:(b + 1) * L, :] = _xattn_heads(
            q_ref[b * L:(b + 1) * L, :], mk_ref[b], mv_ref[b]).astype(o_ref.dtype)


def xattn_sample(q, cache_k, cache_v, layer, nb=8):
    nbatch = cache_k.shape[1]
    L = DEC_SEQ
    xw = N_XHEADS * XHEAD_DIM
    return pl.pallas_call(
        functools.partial(_xattn_sample_kernel, nb=nb),
        grid=(nbatch // nb,),
        in_specs=[
            pl.BlockSpec((nb * L, xw), lambda g: (g, 0)),
            pl.BlockSpec((None, nb, N_MEM, xw), lambda g: (layer, g, 0, 0)),
            pl.BlockSpec((None, nb, N_MEM, xw), lambda g: (layer, g, 0, 0)),
        ],
        out_specs=pl.BlockSpec((nb * L, xw), lambda g: (g, 0)),
        out_shape=jax.ShapeDtypeStruct((nbatch * L, xw), BF16),
        compiler_params=_params("arbitrary"),
        name="xattn_sample",
    )(q, cache_k, cache_v)


def _s5_prompt_kernel(u_ref, h0r_ref, h0i_ref, a_ref, b_ref, c_ref, d_ref,
                      z_ref, hr_ref, hi_ref, x_ref, hst_ref, *, T):
    c = pl.program_id(1)
    nslab = 2 * SLABS
    G = T // SUBLANES

    @pl.when(c == 0)
    def _():
        for sb in range(N_SUPER):
            hst_ref[2 * sb] = h0r_ref[sb]
            hst_ref[2 * sb + 1] = h0i_ref[sb]

    def slab_rows(sb, s):
        return pl.ds((sb * nslab + s) * S5_SLAB_PITCH, SUBLANES)

    for sb in range(N_SUPER):
        bu = _bdot(u_ref[:, sb * SUPER_IN:(sb + 1) * SUPER_IN], b_ref[sb])
        for s in range(nslab):
            x_ref[:, slab_rows(sb, s), :] = bu[:, s * LANES:(s + 1) * LANES].reshape(
                G, SUBLANES, LANES)

    a = [a_ref[k] for k in range(2 * N_SUPER)]
    h_init = tuple(hst_ref[k] for k in range(2 * N_SUPER))

    def step(t, h):
        g = lax.shift_right_logical(t, 3)
        r = lax.bitwise_and(t, SUBLANES - 1)
        new = []
        for sb in range(N_SUPER):
            ir = pl.ds((sb * nslab) * S5_SLAB_PITCH + r, SUBLANES, stride=S5_SLAB_PITCH)
            ii = pl.ds((sb * nslab + SLABS) * S5_SLAB_PITCH + r, SUBLANES, stride=S5_SLAB_PITCH)
            ar, ai = a[2 * sb], a[2 * sb + 1]
            hr, hi = h[2 * sb], h[2 * sb + 1]
            nr = ar * hr - ai * hi + x_ref[g, ir, :]
            ni = ar * hi + ai * hr + x_ref[g, ii, :]
            x_ref[g, ir, :] = nr
            x_ref[g, ii, :] = ni
            new += [nr, ni]
        return tuple(new)

    h_fin = lax.fori_loop(0, T, step, h_init)
    for sb in range(N_SUPER):
        hst_ref[2 * sb] = h_fin[2 * sb]
        hst_ref[2 * sb + 1] = h_fin[2 * sb + 1]
        hr_ref[sb] = h_fin[2 * sb]
        hi_ref[sb] = h_fin[2 * sb + 1]

    for sb in range(N_SUPER):
        hcat = jnp.concatenate([x_ref[:, slab_rows(sb, s), :].reshape(T, LANES)
                                for s in range(nslab)], axis=1)
        cols = slice(sb * SUPER_IN, (sb + 1) * SUPER_IN)
        y = _bdot(hcat, c_ref[sb]) + d_ref[:, cols] * u_ref[:, cols]
        z_ref[:, cols] = jax.nn.gelu(y).astype(z_ref.dtype)


S5_SLAB_PITCH = 12


def s5_prompt(u, h0r, h0i, a_tiles, b_blk, c_blk, d_row, batch, T):
    nc = SEQ // T
    st = (None, N_SUPER, SUBLANES, LANES)
    return pl.pallas_call(
        functools.partial(_s5_prompt_kernel, T=T),
        grid=(batch, nc),
        in_specs=[
            pl.BlockSpec((T, D_MODEL), lambda b, c: (b * nc + c, 0)),
            pl.BlockSpec(st, lambda b, c: (b, 0, 0, 0)),
            pl.BlockSpec(st, lambda b, c: (b, 0, 0, 0)),
            _resident((2 * N_SUPER, SUBLANES, LANES), lambda b, c: (0, 0, 0)),
            _resident((N_SUPER, SUPER_IN, 2 * SUPER_STATE), lambda b, c: (0, 0, 0)),
            _resident((N_SUPER, 2 * SUPER_STATE, SUPER_IN), lambda b, c: (0, 0, 0)),
            _resident((1, D_MODEL), lambda b, c: (0, 0)),
        ],
        out_specs=[
            pl.BlockSpec((T, D_MODEL), lambda b, c: (b * nc + c, 0)),
            pl.BlockSpec(st, lambda b, c: (b, 0, 0, 0)),
            pl.BlockSpec(st, lambda b, c: (b, 0, 0, 0)),
        ],
        out_shape=[
            jax.ShapeDtypeStruct((batch * SEQ, D_MODEL), BF16),
            jax.ShapeDtypeStruct((batch, N_SUPER, SUBLANES, LANES), F32),
            jax.ShapeDtypeStruct((batch, N_SUPER, SUBLANES, LANES), F32),
        ],
        scratch_shapes=[
            pltpu.VMEM((T // SUBLANES, N_SUPER * 2 * SLABS * S5_SLAB_PITCH, LANES), F32),
            pltpu.VMEM((2 * N_SUPER, SUBLANES, LANES), F32),
        ],
        compiler_params=_params("arbitrary", "arbitrary"),
        name="s5_prompt",
    )(u, h0r, h0i, a_tiles, b_blk, c_blk, d_row)


def _s5_sample_kernel(u_ref, h0r_ref, h0i_ref, ar_ref, ai_ref, b_ref, c_ref, d_ref,
                      z_ref, hr_ref, hi_ref, x_ref, *, nbatch):
    L = DEC_SEQ
    R = nbatch * L
    for sb in range(N_SUPER):
        cols = slice(sb * SUPER_IN, (sb + 1) * SUPER_IN)
        bu = _bdot(u_ref[:, cols], b_ref[sb])
        for s in range(2 * SLABS):
            x_ref[s * R:(s + 1) * R, :] = bu[:, s * LANES:(s + 1) * LANES]
        for s in range(SLABS):
            st = slice(sb * SUPER_STATE + s * LANES, sb * SUPER_STATE + (s + 1) * LANES)
            ar, ai = ar_ref[:, st], ai_ref[:, st]
            hr, hi = h0r_ref[:, st], h0i_ref[:, st]
            for t in range(L):
                rr = pl.ds(s * R + t, nbatch, stride=L)
                ri = pl.ds((SLABS + s) * R + t, nbatch, stride=L)
                nr = ar * hr - ai * hi + x_ref[rr, :]
                ni = ar * hi + ai * hr + x_ref[ri, :]
                x_ref[rr, :] = nr
                x_ref[ri, :] = ni
                hr, hi = nr, ni
            hr_ref[:, st] = hr
            hi_ref[:, st] = hi
        hcat = jnp.concatenate([x_ref[s * R:(s + 1) * R, :] for s in range(2 * SLABS)], axis=1)
        y = _bdot(hcat, c_ref[sb]) + d_ref[:, cols] * u_ref[:, cols]
        z_ref[:, cols] = jax.nn.gelu(y).astype(z_ref.dtype)


def s5_sample(u, h0r, h0i, a_re, a_im, b_blk, c_blk, d_row):
    rows = u.shape[0]
    nbatch = rows // DEC_SEQ
    nstate = N_SSM_GROUPS * SSM_STATE
    full = lambda shape: pl.BlockSpec(shape, lambda i: (0,) * len(shape))
    return pl.pallas_call(
        functools.partial(_s5_sample_kernel, nbatch=nbatch),
        grid=(1,),
        in_specs=[
            full((rows, D_MODEL)), full((nbatch, nstate)), full((nbatch, nstate)),
            full((1, nstate)), full((1, nstate)),
            full((N_SUPER, SUPER_IN, 2 * SUPER_STATE)),
            full((N_SUPER, 2 * SUPER_STATE, SUPER_IN)),
            full((1, D_MODEL)),
        ],
        out_specs=[full((rows, D_MODEL)), full((nbatch, nstate)), full((nbatch, nstate))],
        out_shape=[
            jax.ShapeDtypeStruct((rows, D_MODEL), BF16),
            jax.ShapeDtypeStruct((nbatch, nstate), F32),
            jax.ShapeDtypeStruct((nbatch, nstate), F32),
        ],
        scratch_shapes=[pltpu.VMEM((2 * SLABS * rows, LANES), F32)],
        compiler_params=_params("arbitrary"),
        name="s5_sample",
    )(u, h0r, h0i, a_re, a_im, b_blk, c_blk, d_row)


def _s5_weights(lam_re, lam_im, log_step, b_re, b_im, c_re, c_im):
    lam = lax.complex(lam_re.astype(F32), lam_im.astype(F32))
    dt = jnp.exp(log_step.astype(F32))[:, None]
    abar = jnp.exp(lam * dt)
    bbar = ((abar - 1.0) / lam)[..., None] * lax.complex(b_re.astype(F32), b_im.astype(F32))
    eye = jnp.eye(SSM_SUPER, dtype=F32)

    def b_layout(m):
        m = m.reshape(N_SUPER, SSM_SUPER, SSM_STATE, SSM_GROUP)
        return jnp.einsum('sgph,gk->sghkp', m, eye).reshape(N_SUPER, SUPER_IN, SUPER_STATE)

    def c_layout(m):
        m = m.reshape(N_SUPER, SSM_SUPER, SSM_GROUP, SSM_STATE)
        return jnp.einsum('sghp,gk->sgpkh', m, eye).reshape(N_SUPER, SUPER_STATE, SUPER_IN)

    b_blk = jnp.concatenate([b_layout(bbar.real), b_layout(bbar.imag)], axis=2).astype(BF16)
    c_blk = jnp.concatenate([c_layout(c_re.astype(F32)), -c_layout(c_im.astype(F32))],
                            axis=1).astype(BF16)
    return abar.real, abar.imag, b_blk, c_blk


def _ffn_prompt_kernel(x_ref, gpre_ref, wua_ref, wuv_ref, cwa_ref, cwv_ref, cba_ref, cbv_ref,
                       wd_ref, gpost_ref, o_ref, sa_ref, sv_ref,
                       hn_ref, ua_ref, uv_ref, act_ref, ha_ref, hv_ref,
                       *, nc, blocks_per_seq, rc, rm):
    i = pl.program_id(0)
    c = pl.program_id(1)
    tm, D = hn_ref.shape
    hdr = SUBLANES

    @pl.when(c == 0)
    def _():
        hn_ref[...] = _rms(x_ref[...], gpre_ref[...]).astype(BF16)
        o_ref[...] = jnp.zeros(o_ref.shape, F32)

    slot = i % 2

    @pl.when(i % blocks_per_seq == 0)
    def _():
        ha_ref[1 - slot, c] = jnp.zeros(ha_ref.shape[2:], F32)
        hv_ref[1 - slot, c] = jnp.zeros(hv_ref.shape[2:], F32)

    ua_ref[0:hdr, :] = ha_ref[1 - slot, c]
    uv_ref[0:hdr, :] = hv_ref[1 - slot, c]
    wa = wua_ref[...].astype(BF16)
    wv = wuv_ref[...].astype(BF16)
    wd = wd_ref[...].astype(BF16)

    def conv(buf_ref, w_ref, b_ref, r0):
        return (b_ref[...] + w_ref[0:1, :] * buf_ref[r0 + hdr - 2:r0 + hdr - 2 + rc, :]
                + w_ref[1:2, :] * buf_ref[r0 + hdr - 1:r0 + hdr - 1 + rc, :]
                + w_ref[2:3, :] * buf_ref[r0 + hdr:r0 + hdr + rc, :])

    def up_proj(k):
        hk = hn_ref[k * rm:(k + 1) * rm, :]
        ua_ref[hdr + k * rm:hdr + (k + 1) * rm, :] = jnp.dot(hk, wa, preferred_element_type=F32)
        uv_ref[hdr + k * rm:hdr + (k + 1) * rm, :] = jnp.dot(hk, wv, preferred_element_type=F32)

    ngroups = tm // rm
    up_proj(0)
    for k in range(ngroups):
        if k + 1 < ngroups:
            up_proj(k + 1)
        rows = slice(k * rm, (k + 1) * rm)
        for r0 in range(k * rm, (k + 1) * rm, rc):
            ca = conv(ua_ref, cwa_ref, cba_ref, r0)
            cv = conv(uv_ref, cwv_ref, cbv_ref, r0)
            act_ref[r0:r0 + rc, :] = (jax.nn.silu(ca) * cv).astype(BF16)
        o_ref[rows, :] += jnp.dot(act_ref[rows, :], wd, preferred_element_type=F32)

    ta = ua_ref[tm:tm + hdr, :]
    tv = uv_ref[tm:tm + hdr, :]
    ha_ref[slot, c] = ta
    hv_ref[slot, c] = tv
    sa_ref[c] = ta[hdr - 2:hdr, :]
    sv_ref[c] = tv[hdr - 2:hdr, :]

    @pl.when(c == nc - 1)
    def _():
        o_ref[...] = x_ref[...] + _rms(o_ref[...], gpost_ref[...])


FFN_ROW_CHUNK = 64
FFN_ROW_GROUP = 512
FFN_VMEM_LIMIT = V7X_VMEM_BYTES - 4 * 1024 * 1024


def ffn_prompt(x, layer, g_pre, w_up, w_layer, conv_w, conv_b, w_down, g_post, batch, tm, tf):
    M, D = x.shape
    nc = D_FF // tf
    bps = SEQ // tm
    conv_b3 = conv_b.reshape(DEPTH, 1, 2 * D_FF)
    y, sa, sv = pl.pallas_call(
        functools.partial(_ffn_prompt_kernel, nc=nc, blocks_per_seq=bps, rc=FFN_ROW_CHUNK,
                          rm=FFN_ROW_GROUP),
        grid=(M // tm, nc),
        in_specs=[
            pl.BlockSpec((tm, D), lambda i, c: (i, 0)),
            pl.BlockSpec((None, 1, D), lambda i, c: (layer, 0, 0)),
            pl.BlockSpec((None, D, tf), lambda i, c: (w_layer, 0, c)),
            pl.BlockSpec((None, D, tf), lambda i, c: (w_layer, 0, nc + c)),
            pl.BlockSpec((None, 3, tf), lambda i, c: (layer, 0, c)),
            pl.BlockSpec((None, 3, tf), lambda i, c: (layer, 0, nc + c)),
            pl.BlockSpec((None, 1, tf), lambda i, c: (layer, 0, c)),
            pl.BlockSpec((None, 1, tf), lambda i, c: (layer, 0, nc + c)),
            pl.BlockSpec((None, tf, D), lambda i, c: (w_layer, c, 0)),
            pl.BlockSpec((None, 1, D), lambda i, c: (layer, 0, 0)),
        ],
        out_specs=[
            _resident((tm, D), lambda i, c: (i, 0)),
            pl.BlockSpec((None, nc, 2, tf), lambda i, c: (i // bps, 0, 0, 0)),
            pl.BlockSpec((None, nc, 2, tf), lambda i, c: (i // bps, 0, 0, 0)),
        ],
        out_shape=[
            jax.ShapeDtypeStruct((M, D), F32),
            jax.ShapeDtypeStruct((batch, nc, 2, tf), F32),
            jax.ShapeDtypeStruct((batch, nc, 2, tf), F32),
        ],
        scratch_shapes=[
            pltpu.VMEM((tm, D), BF16),
            pltpu.VMEM((tm + SUBLANES, tf), F32),
            pltpu.VMEM((tm + SUBLANES, tf), F32),
            pltpu.VMEM((tm, tf), BF16),
            pltpu.VMEM((2, nc, SUBLANES, tf), F32),
            pltpu.VMEM((2, nc, SUBLANES, tf), F32),
        ],
        compiler_params=_params("arbitrary", "arbitrary", vmem=FFN_VMEM_LIMIT),
        name="ffn_prompt",
    )(x, g_pre.reshape(DEPTH, 1, D), w_up, w_up, conv_w, conv_w, conv_b3, conv_b3,
      w_down, g_post.reshape(DEPTH, 1, D))
    sa = sa.transpose(0, 2, 1, 3).reshape(batch, 2, D_FF)
    sv = sv.transpose(0, 2, 1, 3).reshape(batch, 2, D_FF)
    return y, jnp.concatenate([sa, sv], axis=-1)


def _ffn_sample_kernel(x_ref, gpre_ref, wua_ref, wuv_ref, cwa_ref, cwv_ref, cba_ref, cbv_ref,
                       wd_ref, gpost_ref, sta_ref, stv_ref, o_ref, na_ref, nv_ref, hn_ref, *, nc):
    c = pl.program_id(0)
    L = DEC_SEQ

    @pl.when(c == 0)
    def _():
        hn_ref[...] = _rms(x_ref[...], gpre_ref[...]).astype(BF16)

    hn = hn_ref[...]
    M = hn.shape[0]
    nb = M // L
    ua = jnp.dot(hn, wua_ref[...], preferred_element_type=F32)
    uv = jnp.dot(hn, wuv_ref[...], preferred_element_type=F32)
    tf = ua.shape[1]
    t = lax.broadcasted_iota(jnp.int32, (nb, L, tf), 1)

    def conv(u, st_ref, w_ref, b_ref):
        u3 = u.reshape(nb, L, tf)
        st = st_ref[...]
        prev2, prev1 = st[:, 0:1, :], st[:, 1:2, :]
        p1 = jnp.where(t == 0, prev1, pltpu.roll(u3, 1, 1))
        p2 = jnp.where(t == 0, prev2, jnp.where(t == 1, prev1, pltpu.roll(u3, 2, 1)))
        return (b_ref[...] + w_ref[0:1, :] * p2 + w_ref[1:2, :] * p1 + w_ref[2:3, :] * u3,
                u3[:, L - 2:L, :])

    ca, na = conv(ua, sta_ref, cwa_ref, cba_ref)
    cv, nv = conv(uv, stv_ref, cwv_ref, cbv_ref)
    na_ref[...] = na
    nv_ref[...] = nv
    act = (jax.nn.silu(ca) * cv).astype(BF16).reshape(M, tf)
    part = jnp.dot(act, wd_ref[...], preferred_element_type=F32)

    @pl.when(c == 0)
    def _():
        o_ref[...] = part

    @pl.when(c > 0)
    def _():
        o_ref[...] += part

    @pl.when(c == nc - 1)
    def _():
        o_ref[...] = x_ref[...] + _rms(o_ref[...], gpost_ref[...])


def ffn_sample(x, layer, g_pre, w_up, w_layer, conv_w, conv_b, w_down, g_post, conv_state, tf):
    M, D = x.shape
    nbatch = M // DEC_SEQ
    nc = D_FF // tf
    conv_b3 = conv_b.reshape(DEPTH, 1, 2 * D_FF)
    y, na, nv = pl.pallas_call(
        functools.partial(_ffn_sample_kernel, nc=nc),
        grid=(nc,),
        in_specs=[
            pl.BlockSpec((M, D), lambda c: (0, 0)),
            pl.BlockSpec((None, 1, D), lambda c: (layer, 0, 0)),
            pl.BlockSpec((None, D, tf), lambda c: (w_layer, 0, c)),
            pl.BlockSpec((None, D, tf), lambda c: (w_layer, 0, nc + c)),
            pl.BlockSpec((None, 3, tf), lambda c: (layer, 0, c)),
            pl.BlockSpec((None, 3, tf), lambda c: (layer, 0, nc + c)),
            pl.BlockSpec((None, 1, tf), lambda c: (layer, 0, c)),
            pl.BlockSpec((None, 1, tf), lambda c: (layer, 0, nc + c)),
            pl.BlockSpec((None, tf, D), lambda c: (w_layer, c, 0)),
            pl.BlockSpec((None, 1, D), lambda c: (layer, 0, 0)),
            pl.BlockSpec((None, nbatch, 2, tf), lambda c: (layer, 0, 0, c)),
            pl.BlockSpec((None, nbatch, 2, tf), lambda c: (layer, 0, 0, nc + c)),
        ],
        out_specs=[
            pl.BlockSpec((M, D), lambda c: (0, 0)),
            pl.BlockSpec((nbatch, 2, tf), lambda c: (0, 0, c)),
            pl.BlockSpec((nbatch, 2, tf), lambda c: (0, 0, c)),
        ],
        out_shape=[
            jax.ShapeDtypeStruct((M, D), F32),
            jax.ShapeDtypeStruct((nbatch, 2, D_FF), F32),
            jax.ShapeDtypeStruct((nbatch, 2, D_FF), F32),
        ],
        scratch_shapes=[pltpu.VMEM((M, D), BF16)],
        compiler_params=_params("arbitrary"),
        name="ffn_sample",
    )(x, g_pre.reshape(DEPTH, 1, D), w_up, w_up, conv_w, conv_w, conv_b3, conv_b3,
      w_down, g_post.reshape(DEPTH, 1, D), conv_state, conv_state)
    return y, jnp.concatenate([na, nv], axis=-1)


def kernel(x_prompt, x_sample, mem_prompt, cache_swa_k, cache_swa_v, state_ssm_re, state_ssm_im, state_ffn_conv, cache_mem_k, cache_mem_v, g_mix_pre, g_mix_post, w_qkv, w_attn_o, attn_sinks, w_ssm_in, ssm_lambda_re, ssm_lambda_im, ssm_log_step, ssm_b_re, ssm_b_im, ssm_c_re, ssm_c_im, ssm_d, w_ssm_glu, g_x_pre, g_x_post, g_mem, w_x_q, w_mem_k, w_mem_v, w_x_o, g_ffn_pre, g_ffn_post, w_ffn_up, ffn_conv_w, ffn_conv_b, w_ffn_down):
    B = x_prompt.shape[0]
    SB = x_sample.shape[0]
    xw = N_XHEADS * XHEAD_DIM
    nkc = N_KV_HEADS * HEAD_DIM
    xp = x_prompt.reshape(B * SEQ, D_MODEL)
    xs = x_sample.reshape(SB * DEC_SEQ, D_MODEL)
    mem = mem_prompt.reshape(B * N_MEM, D_MODEL)
    MS = SB * DEC_SEQ
    TMP = 1024
    cos_p, sin_p = _rope_tables(jnp.arange(SEQ))
    cos_s, sin_s = _rope_tables(PAST_LEN + jnp.arange(DEC_SEQ))
    cmk_all = cache_mem_k.reshape(DEPTH, SB, N_MEM, xw)
    cmv_all = cache_mem_v.reshape(DEPTH, SB, N_MEM, xw)
    w_qkv_bf = w_qkv.astype(BF16)
    w_attn_o_bf = w_attn_o.astype(BF16)
    w_ssm_in_bf = w_ssm_in.astype(BF16)
    w_glu_bf = w_ssm_glu.astype(BF16)
    w_x_q_bf = w_x_q.astype(BF16)
    w_x_o_bf = w_x_o.astype(BF16)
    w_mem_kv_bf = jnp.concatenate([w_mem_k, w_mem_v], axis=-1).astype(BF16)
    TD = 512

    swa_kp, swa_vp, swa_ks, swa_vs = [], [], [], []
    ssm_rp, ssm_ip, ssm_rs, ssm_is = [], [], [], []
    conv_p, conv_s, memk_p, memv_p = [], [], [], []
    for i in range(DEPTH):
        j = i // 2
        if i % 2 == 0:
            qkv_p = norm_matmul(xp, g_mix_pre, i, w_qkv_bf, j, TD)
            qkv_s = norm_matmul(xs, g_mix_pre, i, w_qkv_bf, j, MS)
            op, kp, vp, w_up_bf, w_down_bf = swa_prompt(qkv_p, attn_sinks, j, cos_p, sin_p, B,
                                                        w_ffn_up, w_ffn_down)
            sink_rows = jnp.repeat(attn_sinks[j].reshape(N_KV_HEADS, GQA_GROUP), DEC_SEQ,
                                   axis=1)[..., None]
            os_, kn, vn = swa_sample(qkv_s, cache_swa_k, cache_swa_v, j, sink_rows, cos_s, sin_s)
            swa_kp.append(kp.reshape(B, WINDOW, N_KV_HEADS, HEAD_DIM))
            swa_vp.append(vp.reshape(B, WINDOW, N_KV_HEADS, HEAD_DIM))
            swa_ks.append(kn)
            swa_vs.append(vn)
            xp = matmul_post(op, w_attn_o_bf, j, g_mix_post, i, xp, TD)
            xs = matmul_post(os_, w_attn_o_bf, j, g_mix_post, i, xs, MS)
        else:
            a_re, a_im, b_blk, c_blk = _s5_weights(
                ssm_lambda_re[j], ssm_lambda_im[j], ssm_log_step[j], ssm_b_re[j], ssm_b_im[j],
                ssm_c_re[j], ssm_c_im[j])
            d_row = ssm_d[j].reshape(1, D_MODEL)
            up_ = norm_matmul(xp, g_mix_pre, i, w_ssm_in_bf, j, TD)
            us_ = norm_matmul(xs, g_mix_pre, i, w_ssm_in_bf, j, MS)
            a_tiles = jnp.stack([a_re.reshape(N_SUPER, SUBLANES, LANES),
                                 a_im.reshape(N_SUPER, SUBLANES, LANES)], axis=1
                                ).reshape(2 * N_SUPER, SUBLANES, LANES)
            zero_state = jnp.zeros((B, N_SUPER, SUBLANES, LANES), F32)
            zp, rp, ip = s5_prompt(up_, zero_state, zero_state, a_tiles, b_blk, c_blk, d_row,
                                   B, 256)
            nstate = N_SSM_GROUPS * SSM_STATE
            zs, rn, im_ = s5_sample(us_, state_ssm_re[j].reshape(SB, nstate),
                                    state_ssm_im[j].reshape(SB, nstate),
                                    a_re.reshape(1, nstate), a_im.reshape(1, nstate),
                                    b_blk, c_blk, d_row)
            ssm_rp.append(rp.reshape(B, N_SSM_GROUPS, SSM_STATE))
            ssm_ip.append(ip.reshape(B, N_SSM_GROUPS, SSM_STATE))
            ssm_rs.append(rn.reshape(SB, N_SSM_GROUPS, SSM_STATE))
            ssm_is.append(im_.reshape(SB, N_SSM_GROUPS, SSM_STATE))
            xp = glu_post(zp, w_glu_bf, j, g_mix_post, i, xp, TD)
            xs = glu_post(zs, w_glu_bf, j, g_mix_post, i, xs, MS)
        mkv = norm_matmul(mem, g_mem, i, w_mem_kv_bf, i, B * N_MEM)
        mk, mv = mkv[:, :xw], mkv[:, xw:]
        memk_p.append(mk.reshape(B, N_MEM, N_XHEADS, XHEAD_DIM))
        memv_p.append(mv.reshape(B, N_MEM, N_XHEADS, XHEAD_DIM))
        xp = xattn_prompt(xp, i, g_x_pre, w_x_q_bf, mk, mv, w_x_o_bf, g_x_post, B, 2 * TD)
        qs = norm_matmul(xs, g_x_pre, i, w_x_q_bf, i, MS)
        as_ = xattn_sample(qs, cmk_all, cmv_all, i)
        xs = matmul_post(as_, w_x_o_bf, i, g_x_post, i, xs, MS)
        xp, cp = ffn_prompt(xp, i, g_ffn_pre, w_up_bf, i % 2, ffn_conv_w, ffn_conv_b, w_down_bf,
                            g_ffn_post, B, TMP, 512)
        xs, cs = ffn_sample(xs, i, g_ffn_pre, w_up_bf, i % 2, ffn_conv_w, ffn_conv_b, w_down_bf,
                            g_ffn_post, state_ffn_conv, 1408)
        conv_p.append(cp)
        conv_s.append(cs)
    return (xp.reshape(B, SEQ, D_MODEL), xs.reshape(SB, DEC_SEQ, D_MODEL),
            jnp.stack(swa_kp), jnp.stack(swa_vp), jnp.stack(swa_ks), jnp.stack(swa_vs),
            jnp.stack(ssm_rp), jnp.stack(ssm_ip), jnp.stack(ssm_rs), jnp.stack(ssm_is),
            jnp.stack(conv_p), jnp.stack(conv_s), jnp.stack(memk_p), jnp.stack(memv_p))
```

```python
import functools

import jax
import jax.numpy as jnp
from jax import lax
from jax.experimental import pallas as pl
from jax.experimental.pallas import tpu as pltpu

D_MODEL = 2048
SEQ = 4096
DEPTH = 4
DEC_SEQ = 8
PAST_LEN = 16384
HEAD_DIM = 64
N_HEADS = 32
N_KV_HEADS = 4
GQA_GROUP = 8
WINDOW = 128
ROPE_THETA = 10000.0
SSM_GROUP = 16
N_SSM_GROUPS = 128
SSM_STATE = 64
N_MEM = 256
N_XHEADS = 4
XHEAD_DIM = 128
D_FF = 5632
NORM_EPS = 1e-6
NEG_INF = -1e30

F32 = jnp.float32
BF16 = jnp.bfloat16

V7X_VMEM_BYTES = 64 * 1024 * 1024
VMEM_LIMIT = V7X_VMEM_BYTES - 8 * 1024 * 1024
LANES = 128
SUBLANES = 8
SSM_SUPER = 16
N_SUPER = N_SSM_GROUPS // SSM_SUPER
SUPER_IN = SSM_SUPER * SSM_GROUP
SUPER_STATE = SSM_SUPER * SSM_STATE
SLABS = SUPER_STATE // LANES


def _params(*sem, vmem=VMEM_LIMIT):
    return pltpu.CompilerParams(dimension_semantics=sem, vmem_limit_bytes=vmem)


def _rms(x, g):
    ms = jnp.mean(x * x, axis=-1, keepdims=True)
    return x * lax.rsqrt(ms + NORM_EPS) * g


def _bdot(a, b):
    return jnp.dot(a.astype(BF16), b.astype(BF16), preferred_element_type=F32)


def _resident(shape, index_map):
    return pl.BlockSpec(shape, index_map, pipeline_mode=pl.Buffered(1))


ROW_GROUP = 256


def _row_groups(tm):
    rg = min(ROW_GROUP, tm)
    return [slice(k * rg, (k + 1) * rg) for k in range(tm // rg)]


def _norm_matmul_kernel(x_ref, g_ref, w_ref, o_ref):
    g = g_ref[...]
    for rows in _row_groups(x_ref.shape[0]):
        hn = _rms(x_ref[rows, :], g).astype(BF16)
        o_ref[rows, :] = jnp.dot(hn, w_ref[...], preferred_element_type=F32).astype(o_ref.dtype)


def norm_matmul(x, g, g_layer, w, w_layer, tm, out_dtype=F32):
    M, D = x.shape
    N = w.shape[-1]
    return pl.pallas_call(
        _norm_matmul_kernel,
        grid=(M // tm,),
        in_specs=[
            pl.BlockSpec((tm, D), lambda i: (i, 0)),
            pl.BlockSpec((None, 1, D), lambda i: (g_layer, 0, 0)),
            _resident((None, D, N), lambda i: (w_layer, 0, 0)),
        ],
        out_specs=pl.BlockSpec((tm, N), lambda i: (i, 0)),
        out_shape=jax.ShapeDtypeStruct((M, N), out_dtype),
        compiler_params=_params("arbitrary"),
        name="norm_matmul",
    )(x, g.reshape(g.shape[0], 1, D), w)


def _mem_kv_kernel(x_ref, g_ref, w_ref, k_ref, v_ref, kh_ref, vh_ref):
    xw = N_XHEADS * XHEAD_DIM
    hn = _rms(x_ref[...], g_ref[...]).astype(BF16)
    y = jnp.dot(hn, w_ref[...], preferred_element_type=F32)
    k_ref[...] = y[:, :xw]
    v_ref[...] = y[:, xw:]
    for h in range(N_XHEADS):
        kh_ref[:, h, :] = y[:, h * XHEAD_DIM:(h + 1) * XHEAD_DIM]
        vh_ref[:, h, :] = y[:, xw + h * XHEAD_DIM:xw + (h + 1) * XHEAD_DIM]


def mem_kv(mem, g, w, layer):
    M, D = mem.shape
    xw = N_XHEADS * XHEAD_DIM
    flat = jax.ShapeDtypeStruct((M, xw), F32)
    heads = jax.ShapeDtypeStruct((M, N_XHEADS, XHEAD_DIM), F32)
    return pl.pallas_call(
        _mem_kv_kernel,
        grid=(1,),
        in_specs=[
            pl.BlockSpec((M, D), lambda i: (0, 0)),
            pl.BlockSpec((None, 1, D), lambda i: (layer, 0, 0)),
            pl.BlockSpec((None, D, 2 * xw), lambda i: (layer, 0, 0)),
        ],
        out_specs=[
            pl.BlockSpec((M, xw), lambda i: (0, 0)),
            pl.BlockSpec((M, xw), lambda i: (0, 0)),
            pl.BlockSpec((M, N_XHEADS, XHEAD_DIM), lambda i: (0, 0, 0)),
            pl.BlockSpec((M, N_XHEADS, XHEAD_DIM), lambda i: (0, 0, 0)),
        ],
        out_shape=[flat, flat, heads, heads],
        compiler_params=_params("arbitrary"),
        name="mem_kv",
    )(mem, g.reshape(g.shape[0], 1, D), w)


def _matmul_post_kernel(a_ref, w_ref, g_ref, x_ref, o_ref):
    g = g_ref[...]
    for rows in _row_groups(x_ref.shape[0]):
        y = jnp.dot(a_ref[rows, :].astype(BF16), w_ref[...], preferred_element_type=F32)
        o_ref[rows, :] = x_ref[rows, :] + _rms(y, g)


def matmul_post(a, w, w_layer, g, g_layer, x, tm):
    M, K = a.shape
    D = w.shape[-1]
    return pl.pallas_call(
        _matmul_post_kernel,
        grid=(M // tm,),
        in_specs=[
            pl.BlockSpec((tm, K), lambda i: (i, 0)),
            _resident((None, K, D), lambda i: (w_layer, 0, 0)),
            pl.BlockSpec((None, 1, D), lambda i: (g_layer, 0, 0)),
            pl.BlockSpec((tm, D), lambda i: (i, 0)),
        ],
        out_specs=pl.BlockSpec((tm, D), lambda i: (i, 0)),
        out_shape=jax.ShapeDtypeStruct((M, D), F32),
        compiler_params=_params("arbitrary"),
        name="matmul_post",
    )(a, w, g.reshape(g.shape[0], 1, D), x)


GLU_COLS = 512


def _glu_post_kernel(z_ref, w_ref, g_ref, x_ref, o_ref, y_ref):
    D = o_ref.shape[1]
    g = g_ref[...]
    for rows in _row_groups(x_ref.shape[0]):
        z = z_ref[rows, :]
        for n in range(D // GLU_COLS):
            cols = slice(n * GLU_COLS, (n + 1) * GLU_COLS)
            gcols = slice(D + n * GLU_COLS, D + (n + 1) * GLU_COLS)
            val = jnp.dot(z, w_ref[:, cols], preferred_element_type=F32)
            gate = jnp.dot(z, w_ref[:, gcols], preferred_element_type=F32)
            y_ref[rows, cols] = val * jax.nn.sigmoid(gate)
        o_ref[rows, :] = x_ref[rows, :] + _rms(y_ref[rows, :], g)


def glu_post(z, w, w_layer, g, g_layer, x, tm):
    M, K = z.shape
    D = D_MODEL
    return pl.pallas_call(
        _glu_post_kernel,
        grid=(M // tm,),
        in_specs=[
            pl.BlockSpec((tm, K), lambda i: (i, 0)),
            _resident((None, K, 2 * D), lambda i: (w_layer, 0, 0)),
            pl.BlockSpec((None, 1, D), lambda i: (g_layer, 0, 0)),
            pl.BlockSpec((tm, D), lambda i: (i, 0)),
        ],
        out_specs=pl.BlockSpec((tm, D), lambda i: (i, 0)),
        out_shape=jax.ShapeDtypeStruct((M, D), F32),
        scratch_shapes=[pltpu.VMEM((tm, D), F32)],
        compiler_params=_params("arbitrary"),
        name="glu_post",
    )(z, w, g.reshape(g.shape[0], 1, D), x)


def _rope_chunk(xc, cos, sin_signed):
    lane = lax.broadcasted_iota(jnp.int32, xc.shape, 1)
    first_half = (lane & (HEAD_DIM - 1)) < (HEAD_DIM // 2)
    partner = jnp.where(first_half, pltpu.roll(xc, LANES - HEAD_DIM // 2, 1),
                        pltpu.roll(xc, HEAD_DIM // 2, 1))
    return xc * cos + partner * sin_signed


def _rope_tables(pos):
    half = HEAD_DIM // 2
    inv_freq = ROPE_THETA ** (-jnp.arange(half, dtype=F32) * 2.0 / HEAD_DIM)
    ang = pos.astype(F32)[:, None] * inv_freq[None, :]
    cos, sin = jnp.cos(ang), jnp.sin(ang)
    cos128 = jnp.tile(cos, (1, LANES // half))
    sin128 = jnp.tile(jnp.concatenate([-sin, sin], axis=1), (1, LANES // HEAD_DIM))
    return cos128, sin128


def _swa_prompt_kernel(sink_ref, q_ref, kv_ref, cos_ref, sin_ref, wu_ref, wd_ref,
                       o_ref, kc_ref, vc_ref, wub_ref, wdb_ref, kk_ref, vv_ref, *, layer):
    i = pl.program_id(1)
    blk = WINDOW
    wub_ref[...] = wu_ref[...].astype(BF16)
    wdb_ref[...] = wd_ref[...].astype(BF16)
    cos, sn = cos_ref[...], sin_ref[...]
    kv = kv_ref[...]
    nkc = N_KV_HEADS * HEAD_DIM
    k = jnp.concatenate([_rope_chunk(kv[:, c * LANES:(c + 1) * LANES], cos, sn)
                         for c in range(nkc // LANES)], axis=1)
    v = kv[:, nkc:2 * nkc]
    kc_ref[...] = k
    vc_ref[...] = v

    cur = i % 2
    prv = 1 - cur

    @pl.when(i == 0)
    def _():
        kk_ref[1] = jnp.zeros((blk, nkc), BF16)
        vv_ref[1] = jnp.zeros((blk, nkc), BF16)

    kb, vb = k.astype(BF16), v.astype(BF16)
    kk_ref[cur] = kb
    vv_ref[cur] = vb
    kk = jnp.concatenate([kk_ref[prv], kb], axis=0)
    vv = jnp.concatenate([vv_ref[prv], vb], axis=0)
    rows = GQA_GROUP * blk
    qi = lax.broadcasted_iota(jnp.int32, (rows, 2 * blk), 0) & (blk - 1)
    si = lax.broadcasted_iota(jnp.int32, (rows, 2 * blk), 1)
    d = blk + qi - si
    mask = (d >= 0) & (d < WINDOW) & ((si >= blk) | (i > 0))

    chunks_per_group = GQA_GROUP * HEAD_DIM // LANES

    def scores(kh):
        hs = slice(kh * HEAD_DIM, (kh + 1) * HEAD_DIM)
        pieces = []
        for c in range(kh * chunks_per_group, (kh + 1) * chunks_per_group):
            qc = (_rope_chunk(q_ref[:, c * LANES:(c + 1) * LANES], cos, sn)
                  * (HEAD_DIM ** -0.5)).astype(BF16)
            pieces += [qc[:, hh * HEAD_DIM:(hh + 1) * HEAD_DIM] for hh in range(LANES // HEAD_DIM)]
        qs = jnp.concatenate(pieces, axis=0)
        return lax.dot_general(qs, kk[:, hs], (((1,), (1,)), ((), ())),
                               preferred_element_type=F32)

    ahead = 2
    s_all = [scores(kh) for kh in range(min(ahead, N_KV_HEADS))]
    outs = []
    for kh in range(N_KV_HEADS):
        if kh + ahead < N_KV_HEADS:
            s_all.append(scores(kh + ahead))
        hs = slice(kh * HEAD_DIM, (kh + 1) * HEAD_DIM)
        sk = jnp.concatenate([jnp.full((blk, 1), sink_ref[layer, kh * GQA_GROUP + j], F32)
                              for j in range(GQA_GROUP)], axis=0)
        s = jnp.where(mask, s_all[kh], NEG_INF)
        mx = jnp.maximum(jnp.max(s, axis=1, keepdims=True), sk)
        p = jnp.exp(s - mx)
        den = jnp.sum(p, axis=1, keepdims=True) + jnp.exp(sk - mx)
        o = jnp.dot(p.astype(BF16), vv[:, hs], preferred_element_type=F32) / den
        outs += [o[j * blk:(j + 1) * blk, :] for j in range(GQA_GROUP)]
    o_ref[...] = jnp.concatenate(outs, axis=1).astype(o_ref.dtype)


def swa_prompt(qkv, sinks, layer, cos, sin, batch, w_up, w_down):
    blk = WINDOW
    nb = SEQ // blk
    nq = N_HEADS * HEAD_DIM
    nkv = 2 * N_KV_HEADS * HEAD_DIM
    per_layer = batch * nb // 2
    ur, dr = D_MODEL // per_layer, D_FF // per_layer

    def w_in(b, i):
        n = b * nb + i
        return (2 * layer + n // per_layer, n % per_layer, 0)

    def w_out(b, i):
        n = b * nb + i
        return (n // per_layer, n % per_layer, 0)

    o, kc, vc, w_up_bf, w_down_bf = pl.pallas_call(
        functools.partial(_swa_prompt_kernel, layer=layer),
        grid=(batch, nb),
        in_specs=[
            pl.BlockSpec(memory_space=pltpu.SMEM),
            pl.BlockSpec((blk, nq), lambda b, i: (b * nb + i, 0)),
            pl.BlockSpec((blk, nkv), lambda b, i: (b * nb + i, nq // nkv)),
            pl.BlockSpec((blk, LANES), lambda b, i: (i, 0)),
            pl.BlockSpec((blk, LANES), lambda b, i: (i, 0)),
            pl.BlockSpec((None, ur, 2 * D_FF), w_in),
            pl.BlockSpec((None, dr, D_MODEL), w_in),
        ],
        out_specs=[
            pl.BlockSpec((blk, nq), lambda b, i: (b * nb + i, 0)),
            pl.BlockSpec((None, blk, nkv // 2), lambda b, i: (b, 0, 0)),
            pl.BlockSpec((None, blk, nkv // 2), lambda b, i: (b, 0, 0)),
            pl.BlockSpec((None, ur, 2 * D_FF), w_out),
            pl.BlockSpec((None, dr, D_MODEL), w_out),
        ],
        out_shape=[
            jax.ShapeDtypeStruct((batch * SEQ, nq), BF16),
            jax.ShapeDtypeStruct((batch, blk, nkv // 2), F32),
            jax.ShapeDtypeStruct((batch, blk, nkv // 2), F32),
            jax.ShapeDtypeStruct((2, D_MODEL, 2 * D_FF), BF16),
            jax.ShapeDtypeStruct((2, D_FF, D_MODEL), BF16),
        ],
        scratch_shapes=[pltpu.VMEM((2, blk, nkv // 2), BF16),
                        pltpu.VMEM((2, blk, nkv // 2), BF16)],
        compiler_params=_params("arbitrary", "arbitrary"),
        name="swa_prompt",
    )(sinks, qkv, qkv, cos, sin, w_up, w_down)
    return o, kc, vc, w_up_bf, w_down_bf


def _swa_sample_kernel(sink_ref, qkv_ref, ck_ref, cv_ref, cos_ref, sin_ref,
                       o_ref, nk_ref, nv_ref, *, nb):
    L = DEC_SEQ
    wb = WINDOW
    nq = N_HEADS * HEAD_DIM
    nkc = N_KV_HEADS * HEAD_DIM
    cos, sn = cos_ref[...], sin_ref[...]
    rows = GQA_GROUP * L
    t_q = lax.broadcasted_iota(jnp.int32, (rows, wb), 0) % L
    c_k = lax.broadcasted_iota(jnp.int32, (rows, wb), 1)
    mask_c = c_k >= t_q + 1
    t_q2 = lax.broadcasted_iota(jnp.int32, (rows, L), 0) % L
    t_k2 = lax.broadcasted_iota(jnp.int32, (rows, L), 1)
    mask_n = t_k2 <= t_q2
    for b in range(nb):
        x = qkv_ref[b * L:(b + 1) * L, :]
        qr = jnp.concatenate([_rope_chunk(x[:, c * LANES:(c + 1) * LANES], cos, sn)
                              for c in range(nq // LANES)], axis=1) * (HEAD_DIM ** -0.5)
        kn = jnp.concatenate([_rope_chunk(x[:, nq + c * LANES:nq + (c + 1) * LANES], cos, sn)
                              for c in range(nkc // LANES)], axis=1)
        vn = x[:, nq + nkc:nq + 2 * nkc]
        ck = ck_ref[b]
        cv = cv_ref[b]
        nk_ref[b, 0:wb - L, :] = ck[L:wb, :]
        nk_ref[b, wb - L:wb, :] = kn
        nv_ref[b, 0:wb - L, :] = cv[L:wb, :]
        nv_ref[b, wb - L:wb, :] = vn
        ckb, cvb, knb, vnb = ck.astype(BF16), cv.astype(BF16), kn.astype(BF16), vn.astype(BF16)
        dn = (((1,), (1,)), ((), ()))
        scores = []
        for kh in range(N_KV_HEADS):
            hs = slice(kh * HEAD_DIM, (kh + 1) * HEAD_DIM)
            qs = jnp.concatenate(
                [qr[:, (kh * GQA_GROUP + j) * HEAD_DIM:(kh * GQA_GROUP + j + 1) * HEAD_DIM]
                 for j in range(GQA_GROUP)], axis=0).astype(BF16)
            scores.append((lax.dot_general(qs, ckb[:, hs], dn, preferred_element_type=F32),
                           lax.dot_general(qs, knb[:, hs], dn, preferred_element_type=F32)))
        outs = []
        for kh in range(N_KV_HEADS):
            hs = slice(kh * HEAD_DIM, (kh + 1) * HEAD_DIM)
            s_c = jnp.where(mask_c, scores[kh][0], NEG_INF)
            s_n = jnp.where(mask_n, scores[kh][1], NEG_INF)
            sk = sink_ref[kh]
            mx = jnp.maximum(jnp.maximum(jnp.max(s_c, axis=1, keepdims=True),
                                         jnp.max(s_n, axis=1, keepdims=True)), sk)
            p_c = jnp.exp(s_c - mx)
            p_n = jnp.exp(s_n - mx)
            den = (jnp.sum(p_c, axis=1, keepdims=True) + jnp.sum(p_n, axis=1, keepdims=True)
                   + jnp.exp(sk - mx))
            o = (jnp.dot(p_c.astype(BF16), cvb[:, hs], preferred_element_type=F32)
                 + jnp.dot(p_n.astype(BF16), vnb[:, hs], preferred_element_type=F32)) / den
            outs.extend(o[j * L:(j + 1) * L, :] for j in range(GQA_GROUP))
        o_ref[b * L:(b + 1) * L, :] = jnp.concatenate(outs, axis=1).astype(o_ref.dtype)


def swa_sample(qkv, cache_k, cache_v, layer, sink_rows, cos, sin, nb=8):
    nbatch = cache_k.shape[1]
    nq = N_HEADS * HEAD_DIM
    nkc = N_KV_HEADS * HEAD_DIM
    L = DEC_SEQ
    return pl.pallas_call(
        functools.partial(_swa_sample_kernel, nb=nb),
        grid=(nbatch // nb,),
        in_specs=[
            pl.BlockSpec((N_KV_HEADS, GQA_GROUP * L, 1), lambda g: (0, 0, 0)),
            pl.BlockSpec((nb * L, nq + 2 * nkc), lambda g: (g, 0)),
            pl.BlockSpec((None, nb, WINDOW, nkc), lambda g: (layer, g, 0, 0)),
            pl.BlockSpec((None, nb, WINDOW, nkc), lambda g: (layer, g, 0, 0)),
            pl.BlockSpec((L, LANES), lambda g: (0, 0)),
            pl.BlockSpec((L, LANES), lambda g: (0, 0)),
        ],
        out_specs=[
            pl.BlockSpec((nb * L, nq), lambda g: (g, 0)),
            pl.BlockSpec((nb, WINDOW, nkc), lambda g: (g, 0, 0)),
            pl.BlockSpec((nb, WINDOW, nkc), lambda g: (g, 0, 0)),
        ],
        out_shape=[
            jax.ShapeDtypeStruct((nbatch * L, nq), BF16),
            jax.ShapeDtypeStruct((nbatch, WINDOW, nkc), F32),
            jax.ShapeDtypeStruct((nbatch, WINDOW, nkc), F32),
        ],
        compiler_params=_params("arbitrary"),
        name="swa_sample",
    )(sink_rows, qkv, cache_k, cache_v, cos, sin)


_XHEADS = [slice(h * XHEAD_DIM, (h + 1) * XHEAD_DIM) for h in range(N_XHEADS)]


def _xattn_scores(q, mk):
    return [lax.dot_general((q[:, hs] * (XHEAD_DIM ** -0.5)).astype(BF16), mk[:, hs].astype(BF16),
                            (((1,), (1,)), ((), ())), preferred_element_type=F32)
            for hs in _XHEADS]


def _xattn_attend(scores, mv):
    outs = []
    for hs, s in zip(_XHEADS, scores):
        mx = jnp.max(s, axis=1, keepdims=True)
        p = jnp.exp(s - mx)
        den = jnp.sum(p, axis=1, keepdims=True)
        outs.append(jnp.dot(p.astype(BF16), mv[:, hs].astype(BF16),
                            preferred_element_type=F32) / den)
    return jnp.concatenate(outs, axis=1)


def _xattn_heads(q, mk, mv):
    return _xattn_attend(_xattn_scores(q, mk), mv)


def _xattn_prompt_kernel(x_ref, gpre_ref, wq_ref, mk_ref, mv_ref, wo_ref, gpost_ref, o_ref):
    mk = mk_ref[...].astype(BF16)
    mv = mv_ref[...].astype(BF16)
    gpre, gpost = gpre_ref[...], gpost_ref[...]
    groups = _row_groups(x_ref.shape[0])

    def project(rows):
        q = jnp.dot(_rms(x_ref[rows, :], gpre).astype(BF16), wq_ref[...],
                    preferred_element_type=F32)
        return _xattn_scores(q, mk)

    scores = project(groups[0])
    for k, rows in enumerate(groups):
        nxt = project(groups[k + 1]) if k + 1 < len(groups) else None
        a = _xattn_attend(scores, mv).astype(BF16)
        y = jnp.dot(a, wo_ref[...], preferred_element_type=F32)
        o_ref[rows, :] = x_ref[rows, :] + _rms(y, gpost)
        scores = nxt


def xattn_prompt(x, layer, g_pre, w_q, mk, mv, w_o, g_post, batch, tm):
    nq = SEQ // tm
    xw = N_XHEADS * XHEAD_DIM
    D = D_MODEL
    return pl.pallas_call(
        _xattn_prompt_kernel,
        grid=(batch, nq),
        in_specs=[
            pl.BlockSpec((tm, D), lambda b, i: (b * nq + i, 0)),
            pl.BlockSpec((None, 1, D), lambda b, i: (layer, 0, 0)),
            _resident((None, D, xw), lambda b, i: (layer, 0, 0)),
            pl.BlockSpec((N_MEM, xw), lambda b, i: (b, 0)),
            pl.BlockSpec((N_MEM, xw), lambda b, i: (b, 0)),
            _resident((None, xw, D), lambda b, i: (layer, 0, 0)),
            pl.BlockSpec((None, 1, D), lambda b, i: (layer, 0, 0)),
        ],
        out_specs=pl.BlockSpec((tm, D), lambda b, i: (b * nq + i, 0)),
        out_shape=jax.ShapeDtypeStruct((batch * SEQ, D), F32),
        compiler_params=_params("arbitrary", "arbitrary"),
        name="xattn_prompt",
    )(x, g_pre.reshape(DEPTH, 1, D), w_q, mk, mv, w_o, g_post.reshape(DEPTH, 1, D))


def _xattn_sample_kernel(q_ref, mk_ref, mv_ref, o_ref, *, nb):
    L = DEC_SEQ
    for b in range(nb):
        o_ref[b * L:(b + 1) * L, :] = _xattn_heads(
            q_ref[b * L:(b + 1) * L, :], mk_ref[b], mv_ref[b]).astype(o_ref.dtype)


def xattn_sample(q, cache_k, cache_v, layer, nb=8):
    nbatch = cache_k.shape[1]
    L = DEC_SEQ
    xw = N_XHEADS * XHEAD_DIM
    return pl.pallas_call(
        functools.partial(_xattn_sample_kernel, nb=nb),
        grid=(nbatch // nb,),
        in_specs=[
            pl.BlockSpec((nb * L, xw), lambda g: (g, 0)),
            pl.BlockSpec((None, nb, N_MEM, xw), lambda g: (layer, g, 0, 0)),
            pl.BlockSpec((None, nb, N_MEM, xw), lambda g: (layer, g, 0, 0)),
        ],
        out_specs=pl.BlockSpec((nb * L, xw), lambda g: (g, 0)),
        out_shape=jax.ShapeDtypeStruct((nbatch * L, xw), BF16),
        compiler_params=_params("arbitrary"),
        name="xattn_sample",
    )(q, cache_k, cache_v)


def _s5_prompt_kernel(u_ref, h0r_ref, h0i_ref, a_ref, b_ref, c_ref, d_ref,
                      z_ref, hr_ref, hi_ref, x_ref, hst_ref, *, T):
    c = pl.program_id(1)
    nslab = 2 * SLABS
    G = T // SUBLANES

    @pl.when(c == 0)
    def _():
        for sb in range(N_SUPER):
            hst_ref[2 * sb] = h0r_ref[sb]
            hst_ref[2 * sb + 1] = h0i_ref[sb]

    def slab_rows(sb, s):
        return pl.ds((sb * nslab + s) * S5_SLAB_PITCH, SUBLANES)

    for sb in range(N_SUPER):
        bu = _bdot(u_ref[:, sb * SUPER_IN:(sb + 1) * SUPER_IN], b_ref[sb])
        for s in range(nslab):
            x_ref[:, slab_rows(sb, s), :] = bu[:, s * LANES:(s + 1) * LANES].reshape(
                G, SUBLANES, LANES)

    a = [a_ref[k] for k in range(2 * N_SUPER)]
    h_init = tuple(hst_ref[k] for k in range(2 * N_SUPER))

    def step(t, h):
        g = lax.shift_right_logical(t, 3)
        r = lax.bitwise_and(t, SUBLANES - 1)
        new = []
        for sb in range(N_SUPER):
            ir = pl.ds((sb * nslab) * S5_SLAB_PITCH + r, SUBLANES, stride=S5_SLAB_PITCH)
            ii = pl.ds((sb * nslab + SLABS) * S5_SLAB_PITCH + r, SUBLANES, stride=S5_SLAB_PITCH)
            ar, ai = a[2 * sb], a[2 * sb + 1]
            hr, hi = h[2 * sb], h[2 * sb + 1]
            nr = ar * hr - ai * hi + x_ref[g, ir, :]
            ni = ar * hi + ai * hr + x_ref[g, ii, :]
            x_ref[g, ir, :] = nr
            x_ref[g, ii, :] = ni
            new += [nr, ni]
        return tuple(new)

    h_fin = lax.fori_loop(0, T, step, h_init)
    for sb in range(N_SUPER):
        hst_ref[2 * sb] = h_fin[2 * sb]
        hst_ref[2 * sb + 1] = h_fin[2 * sb + 1]
        hr_ref[sb] = h_fin[2 * sb]
        hi_ref[sb] = h_fin[2 * sb + 1]

    for sb in range(N_SUPER):
        hcat = jnp.concatenate([x_ref[:, slab_rows(sb, s), :].reshape(T, LANES)
                                for s in range(nslab)], axis=1)
        cols = slice(sb * SUPER_IN, (sb + 1) * SUPER_IN)
        y = _bdot(hcat, c_ref[sb]) + d_ref[:, cols] * u_ref[:, cols]
        z_ref[:, cols] = jax.nn.gelu(y).astype(z_ref.dtype)


S5_SLAB_PITCH = 12


def s5_prompt(u, h0r, h0i, a_tiles, b_blk, c_blk, d_row, batch, T):
    nc = SEQ // T
    st = (None, N_SUPER, SUBLANES, LANES)
    return pl.pallas_call(
        functools.partial(_s5_prompt_kernel, T=T),
        grid=(batch, nc),
        in_specs=[
            pl.BlockSpec((T, D_MODEL), lambda b, c: (b * nc + c, 0)),
            pl.BlockSpec(st, lambda b, c: (b, 0, 0, 0)),
            pl.BlockSpec(st, lambda b, c: (b, 0, 0, 0)),
            _resident((2 * N_SUPER, SUBLANES, LANES), lambda b, c: (0, 0, 0)),
            _resident((N_SUPER, SUPER_IN, 2 * SUPER_STATE), lambda b, c: (0, 0, 0)),
            _resident((N_SUPER, 2 * SUPER_STATE, SUPER_IN), lambda b, c: (0, 0, 0)),
            _resident((1, D_MODEL), lambda b, c: (0, 0)),
        ],
        out_specs=[
            pl.BlockSpec((T, D_MODEL), lambda b, c: (b * nc + c, 0)),
            pl.BlockSpec(st, lambda b, c: (b, 0, 0, 0)),
            pl.BlockSpec(st, lambda b, c: (b, 0, 0, 0)),
        ],
        out_shape=[
            jax.ShapeDtypeStruct((batch * SEQ, D_MODEL), BF16),
            jax.ShapeDtypeStruct((batch, N_SUPER, SUBLANES, LANES), F32),
            jax.ShapeDtypeStruct((batch, N_SUPER, SUBLANES, LANES), F32),
        ],
        scratch_shapes=[
            pltpu.VMEM((T // SUBLANES, N_SUPER * 2 * SLABS * S5_SLAB_PITCH, LANES), F32),
            pltpu.VMEM((2 * N_SUPER, SUBLANES, LANES), F32),
        ],
        compiler_params=_params("arbitrary", "arbitrary"),
        name="s5_prompt",
    )(u, h0r, h0i, a_tiles, b_blk, c_blk, d_row)


def _s5_sample_kernel(u_ref, h0r_ref, h0i_ref, ar_ref, ai_ref, b_ref, c_ref, d_ref,
                      z_ref, hr_ref, hi_ref, x_ref, *, nbatch):
    L = DEC_SEQ
    R = nbatch * L
    for sb in range(N_SUPER):
        cols = slice(sb * SUPER_IN, (sb + 1) * SUPER_IN)
        bu = _bdot(u_ref[:, cols], b_ref[sb])
        for s in range(2 * SLABS):
            x_ref[s * R:(s + 1) * R, :] = bu[:, s * LANES:(s + 1) * LANES]
        for s in range(SLABS):
            st = slice(sb * SUPER_STATE + s * LANES, sb * SUPER_STATE + (s + 1) * LANES)
            ar, ai = ar_ref[:, st], ai_ref[:, st]
            hr, hi = h0r_ref[:, st], h0i_ref[:, st]
            for t in range(L):
                rr = pl.ds(s * R + t, nbatch, stride=L)
                ri = pl.ds((SLABS + s) * R + t, nbatch, stride=L)
                nr = ar * hr - ai * hi + x_ref[rr, :]
                ni = ar * hi + ai * hr + x_ref[ri, :]
                x_ref[rr, :] = nr
                x_ref[ri, :] = ni
                hr, hi = nr, ni
            hr_ref[:, st] = hr
            hi_ref[:, st] = hi
        hcat = jnp.concatenate([x_ref[s * R:(s + 1) * R, :] for s in range(2 * SLABS)], axis=1)
        y = _bdot(hcat, c_ref[sb]) + d_ref[:, cols] * u_ref[:, cols]
        z_ref[:, cols] = jax.nn.gelu(y).astype(z_ref.dtype)


def s5_sample(u, h0r, h0i, a_re, a_im, b_blk, c_blk, d_row):
    rows = u.shape[0]
    nbatch = rows // DEC_SEQ
    nstate = N_SSM_GROUPS * SSM_STATE
    full = lambda shape: pl.BlockSpec(shape, lambda i: (0,) * len(shape))
    return pl.pallas_call(
        functools.partial(_s5_sample_kernel, nbatch=nbatch),
        grid=(1,),
        in_specs=[
            full((rows, D_MODEL)), full((nbatch, nstate)), full((nbatch, nstate)),
            full((1, nstate)), full((1, nstate)),
            full((N_SUPER, SUPER_IN, 2 * SUPER_STATE)),
            full((N_SUPER, 2 * SUPER_STATE, SUPER_IN)),
            full((1, D_MODEL)),
        ],
        out_specs=[full((rows, D_MODEL)), full((nbatch, nstate)), full((nbatch, nstate))],
        out_shape=[
            jax.ShapeDtypeStruct((rows, D_MODEL), BF16),
            jax.ShapeDtypeStruct((nbatch, nstate), F32),
            jax.ShapeDtypeStruct((nbatch, nstate), F32),
        ],
        scratch_shapes=[pltpu.VMEM((2 * SLABS * rows, LANES), F32)],
        compiler_params=_params("arbitrary"),
        name="s5_sample",
    )(u, h0r, h0i, a_re, a_im, b_blk, c_blk, d_row)


def _s5_weights(lam_re, lam_im, log_step, b_re, b_im, c_re, c_im):
    lam = lax.complex(lam_re.astype(F32), lam_im.astype(F32))
    dt = jnp.exp(log_step.astype(F32))[:, None]
    abar = jnp.exp(lam * dt)
    bbar = ((abar - 1.0) / lam)[..., None] * lax.complex(b_re.astype(F32), b_im.astype(F32))
    eye = jnp.eye(SSM_SUPER, dtype=F32)

    def b_layout(m):
        m = m.reshape(N_SUPER, SSM_SUPER, SSM_STATE, SSM_GROUP)
        return jnp.einsum('sgph,gk->sghkp', m, eye).reshape(N_SUPER, SUPER_IN, SUPER_STATE)

    def c_layout(m):
        m = m.reshape(N_SUPER, SSM_SUPER, SSM_GROUP, SSM_STATE)
        return jnp.einsum('sghp,gk->sgpkh', m, eye).reshape(N_SUPER, SUPER_STATE, SUPER_IN)

    b_blk = jnp.concatenate([b_layout(bbar.real), b_layout(bbar.imag)], axis=2).astype(BF16)
    c_blk = jnp.concatenate([c_layout(c_re.astype(F32)), -c_layout(c_im.astype(F32))],
                            axis=1).astype(BF16)
    return abar.real, abar.imag, b_blk, c_blk


def _ffn_prompt_kernel(x_ref, gpre_ref, wua_ref, wuv_ref, cwa_ref, cwv_ref, cba_ref, cbv_ref,
                       wd_ref, gpost_ref, o_ref, sa_ref, sv_ref,
                       hn_ref, ua_ref, uv_ref, act_ref, ha_ref, hv_ref,
                       *, nc, blocks_per_seq, rc, rm):
    i = pl.program_id(0)
    c = pl.program_id(1)
    tm, D = hn_ref.shape
    hdr = SUBLANES

    @pl.when(c == 0)
    def _():
        hn_ref[...] = _rms(x_ref[...], gpre_ref[...]).astype(BF16)
        o_ref[...] = jnp.zeros(o_ref.shape, F32)

    slot = i % 2

    @pl.when(i % blocks_per_seq == 0)
    def _():
        ha_ref[1 - slot, c] = jnp.zeros(ha_ref.shape[2:], F32)
        hv_ref[1 - slot, c] = jnp.zeros(hv_ref.shape[2:], F32)

    ua_ref[0:hdr, :] = ha_ref[1 - slot, c]
    uv_ref[0:hdr, :] = hv_ref[1 - slot, c]
    wa = wua_ref[...].astype(BF16)
    wv = wuv_ref[...].astype(BF16)
    wd = wd_ref[...].astype(BF16)

    def conv(buf_ref, w_ref, b_ref, r0):
        return (b_ref[...] + w_ref[0:1, :] * buf_ref[r0 + hdr - 2:r0 + hdr - 2 + rc, :]
                + w_ref[1:2, :] * buf_ref[r0 + hdr - 1:r0 + hdr - 1 + rc, :]
                + w_ref[2:3, :] * buf_ref[r0 + hdr:r0 + hdr + rc, :])

    def up_proj(k):
        hk = hn_ref[k * rm:(k + 1) * rm, :]
        ua_ref[hdr + k * rm:hdr + (k + 1) * rm, :] = jnp.dot(hk, wa, preferred_element_type=F32)
        uv_ref[hdr + k * rm:hdr + (k + 1) * rm, :] = jnp.dot(hk, wv, preferred_element_type=F32)

    ngroups = tm // rm
    up_proj(0)
    for k in range(ngroups):
        if k + 1 < ngroups:
            up_proj(k + 1)
        rows = slice(k * rm, (k + 1) * rm)
        for r0 in range(k * rm, (k + 1) * rm, rc):
            ca = conv(ua_ref, cwa_ref, cba_ref, r0)
            cv = conv(uv_ref, cwv_ref, cbv_ref, r0)
            act_ref[r0:r0 + rc, :] = (jax.nn.silu(ca) * cv).astype(BF16)
        o_ref[rows, :] += jnp.dot(act_ref[rows, :], wd, preferred_element_type=F32)

    ta = ua_ref[tm:tm + hdr, :]
    tv = uv_ref[tm:tm + hdr, :]
    ha_ref[slot, c] = ta
    hv_ref[slot, c] = tv
    sa_ref[c] = ta[hdr - 2:hdr, :]
    sv_ref[c] = tv[hdr - 2:hdr, :]

    @pl.when(c == nc - 1)
    def _():
        o_ref[...] = x_ref[...] + _rms(o_ref[...], gpost_ref[...])


FFN_ROW_CHUNK = 64
FFN_ROW_GROUP = 512
FFN_VMEM_LIMIT = V7X_VMEM_BYTES - 4 * 1024 * 1024


def ffn_prompt(x, layer, g_pre, w_up, w_layer, conv_w, conv_b, w_down, g_post, batch, tm, tf):
    M, D = x.shape
    nc = D_FF // tf
    bps = SEQ // tm
    conv_b3 = conv_b.reshape(DEPTH, 1, 2 * D_FF)
    y, sa, sv = pl.pallas_call(
        functools.partial(_ffn_prompt_kernel, nc=nc, blocks_per_seq=bps, rc=FFN_ROW_CHUNK,
                          rm=FFN_ROW_GROUP),
        grid=(M // tm, nc),
        in_specs=[
            pl.BlockSpec((tm, D), lambda i, c: (i, 0)),
            pl.BlockSpec((None, 1, D), lambda i, c: (layer, 0, 0)),
            pl.BlockSpec((None, D, tf), lambda i, c: (w_layer, 0, c)),
            pl.BlockSpec((None, D, tf), lambda i, c: (w_layer, 0, nc + c)),
            pl.BlockSpec((None, 3, tf), lambda i, c: (layer, 0, c)),
            pl.BlockSpec((None, 3, tf), lambda i, c: (layer, 0, nc + c)),
            pl.BlockSpec((None, 1, tf), lambda i, c: (layer, 0, c)),
            pl.BlockSpec((None, 1, tf), lambda i, c: (layer, 0, nc + c)),
            pl.BlockSpec((None, tf, D), lambda i, c: (w_layer, c, 0)),
            pl.BlockSpec((None, 1, D), lambda i, c: (layer, 0, 0)),
        ],
        out_specs=[
            _resident((tm, D), lambda i, c: (i, 0)),
            pl.BlockSpec((None, nc, 2, tf), lambda i, c: (i // bps, 0, 0, 0)),
            pl.BlockSpec((None, nc, 2, tf), lambda i, c: (i // bps, 0, 0, 0)),
        ],
        out_shape=[
            jax.ShapeDtypeStruct((M, D), F32),
            jax.ShapeDtypeStruct((batch, nc, 2, tf), F32),
            jax.ShapeDtypeStruct((batch, nc, 2, tf), F32),
        ],
        scratch_shapes=[
            pltpu.VMEM((tm, D), BF16),
            pltpu.VMEM((tm + SUBLANES, tf), F32),
            pltpu.VMEM((tm + SUBLANES, tf), F32),
            pltpu.VMEM((tm, tf), BF16),
            pltpu.VMEM((2, nc, SUBLANES, tf), F32),
            pltpu.VMEM((2, nc, SUBLANES, tf), F32),
        ],
        compiler_params=_params("arbitrary", "arbitrary", vmem=FFN_VMEM_LIMIT),
        name="ffn_prompt",
    )(x, g_pre.reshape(DEPTH, 1, D), w_up, w_up, conv_w, conv_w, conv_b3, conv_b3,
      w_down, g_post.reshape(DEPTH, 1, D))
    sa = sa.transpose(0, 2, 1, 3).reshape(batch, 2, D_FF)
    sv = sv.transpose(0, 2, 1, 3).reshape(batch, 2, D_FF)
    return y, jnp.concatenate([sa, sv], axis=-1)


def _ffn_sample_kernel(x_ref, gpre_ref, wua_ref, wuv_ref, cwa_ref, cwv_ref, cba_ref, cbv_ref,
                       wd_ref, gpost_ref, sta_ref, stv_ref, o_ref, na_ref, nv_ref, hn_ref, *, nc):
    c = pl.program_id(0)
    L = DEC_SEQ

    @pl.when(c == 0)
    def _():
        hn_ref[...] = _rms(x_ref[...], gpre_ref[...]).astype(BF16)

    hn = hn_ref[...]
    M = hn.shape[0]
    nb = M // L
    ua = jnp.dot(hn, wua_ref[...], preferred_element_type=F32)
    uv = jnp.dot(hn, wuv_ref[...], preferred_element_type=F32)
    tf = ua.shape[1]
    t = lax.broadcasted_iota(jnp.int32, (nb, L, tf), 1)

    def conv(u, st_ref, w_ref, b_ref):
        u3 = u.reshape(nb, L, tf)
        st = st_ref[...]
        prev2, prev1 = st[:, 0:1, :], st[:, 1:2, :]
        p1 = jnp.where(t == 0, prev1, pltpu.roll(u3, 1, 1))
        p2 = jnp.where(t == 0, prev2, jnp.where(t == 1, prev1, pltpu.roll(u3, 2, 1)))
        return (b_ref[...] + w_ref[0:1, :] * p2 + w_ref[1:2, :] * p1 + w_ref[2:3, :] * u3,
                u3[:, L - 2:L, :])

    ca, na = conv(ua, sta_ref, cwa_ref, cba_ref)
    cv, nv = conv(uv, stv_ref, cwv_ref, cbv_ref)
    na_ref[...] = na
    nv_ref[...] = nv
    act = (jax.nn.silu(ca) * cv).astype(BF16).reshape(M, tf)
    part = jnp.dot(act, wd_ref[...], preferred_element_type=F32)

    @pl.when(c == 0)
    def _():
        o_ref[...] = part

    @pl.when(c > 0)
    def _():
        o_ref[...] += part

    @pl.when(c == nc - 1)
    def _():
        o_ref[...] = x_ref[...] + _rms(o_ref[...], gpost_ref[...])


def ffn_sample(x, layer, g_pre, w_up, w_layer, conv_w, conv_b, w_down, g_post, conv_state, tf):
    M, D = x.shape
    nbatch = M // DEC_SEQ
    nc = D_FF // tf
    conv_b3 = conv_b.reshape(DEPTH, 1, 2 * D_FF)
    y, na, nv = pl.pallas_call(
        functools.partial(_ffn_sample_kernel, nc=nc),
        grid=(nc,),
        in_specs=[
            pl.BlockSpec((M, D), lambda c: (0, 0)),
            pl.BlockSpec((None, 1, D), lambda c: (layer, 0, 0)),
            pl.BlockSpec((None, D, tf), lambda c: (w_layer, 0, c)),
            pl.BlockSpec((None, D, tf), lambda c: (w_layer, 0, nc + c)),
            pl.BlockSpec((None, 3, tf), lambda c: (layer, 0, c)),
            pl.BlockSpec((None, 3, tf), lambda c: (layer, 0, nc + c)),
            pl.BlockSpec((None, 1, tf), lambda c: (layer, 0, c)),
            pl.BlockSpec((None, 1, tf), lambda c: (layer, 0, nc + c)),
            pl.BlockSpec((None, tf, D), lambda c: (w_layer, c, 0)),
            pl.BlockSpec((None, 1, D), lambda c: (layer, 0, 0)),
            pl.BlockSpec((None, nbatch, 2, tf), lambda c: (layer, 0, 0, c)),
            pl.BlockSpec((None, nbatch, 2, tf), lambda c: (layer, 0, 0, nc + c)),
        ],
        out_specs=[
            pl.BlockSpec((M, D), lambda c: (0, 0)),
            pl.BlockSpec((nbatch, 2, tf), lambda c: (0, 0, c)),
            pl.BlockSpec((nbatch, 2, tf), lambda c: (0, 0, c)),
        ],
        out_shape=[
            jax.ShapeDtypeStruct((M, D), F32),
            jax.ShapeDtypeStruct((nbatch, 2, D_FF), F32),
            jax.ShapeDtypeStruct((nbatch, 2, D_FF), F32),
        ],
        scratch_shapes=[pltpu.VMEM((M, D), BF16)],
        compiler_params=_params("arbitrary"),
        name="ffn_sample",
    )(x, g_pre.reshape(DEPTH, 1, D), w_up, w_up, conv_w, conv_w, conv_b3, conv_b3,
      w_down, g_post.reshape(DEPTH, 1, D), conv_state, conv_state)
    return y, jnp.concatenate([na, nv], axis=-1)


def kernel(x_prompt, x_sample, mem_prompt, cache_swa_k, cache_swa_v, state_ssm_re, state_ssm_im, state_ffn_conv, cache_mem_k, cache_mem_v, g_mix_pre, g_mix_post, w_qkv, w_attn_o, attn_sinks, w_ssm_in, ssm_lambda_re, ssm_lambda_im, ssm_log_step, ssm_b_re, ssm_b_im, ssm_c_re, ssm_c_im, ssm_d, w_ssm_glu, g_x_pre, g_x_post, g_mem, w_x_q, w_mem_k, w_mem_v, w_x_o, g_ffn_pre, g_ffn_post, w_ffn_up, ffn_conv_w, ffn_conv_b, w_ffn_down):
    B = x_prompt.shape[0]
    SB = x_sample.shape[0]
    xw = N_XHEADS * XHEAD_DIM
    nkc = N_KV_HEADS * HEAD_DIM
    xp = x_prompt.reshape(B * SEQ, D_MODEL)
    xs = x_sample.reshape(SB * DEC_SEQ, D_MODEL)
    mem = mem_prompt.reshape(B * N_MEM, D_MODEL)
    MS = SB * DEC_SEQ
    TMP = 1024
    cos_p, sin_p = _rope_tables(jnp.arange(SEQ))
    cos_s, sin_s = _rope_tables(PAST_LEN + jnp.arange(DEC_SEQ))
    ck_all = cache_swa_k.reshape(cache_swa_k.shape[0], SB, WINDOW, nkc)
    cv_all = cache_swa_v.reshape(cache_swa_v.shape[0], SB, WINDOW, nkc)
    cmk_all = cache_mem_k.reshape(DEPTH, SB, N_MEM, xw)
    cmv_all = cache_mem_v.reshape(DEPTH, SB, N_MEM, xw)
    w_qkv_bf = w_qkv.astype(BF16)
    w_attn_o_bf = w_attn_o.astype(BF16)
    w_ssm_in_bf = w_ssm_in.astype(BF16)
    w_glu_bf = w_ssm_glu.astype(BF16)
    w_x_q_bf = w_x_q.astype(BF16)
    w_x_o_bf = w_x_o.astype(BF16)
    w_mem_kv_bf = jnp.concatenate([w_mem_k, w_mem_v], axis=-1).astype(BF16)
    TD = 512

    swa_kp, swa_vp, swa_ks, swa_vs = [], [], [], []
    ssm_rp, ssm_ip, ssm_rs, ssm_is = [], [], [], []
    conv_p, conv_s, memk_p, memv_p = [], [], [], []
    for i in range(DEPTH):
        j = i // 2
        if i % 2 == 0:
            qkv_p = norm_matmul(xp, g_mix_pre, i, w_qkv_bf, j, TD)
            qkv_s = norm_matmul(xs, g_mix_pre, i, w_qkv_bf, j, MS)
            op, kp, vp, w_up_bf, w_down_bf = swa_prompt(qkv_p, attn_sinks, j, cos_p, sin_p, B,
                                                        w_ffn_up, w_ffn_down)
            sink_rows = jnp.repeat(attn_sinks[j].reshape(N_KV_HEADS, GQA_GROUP), DEC_SEQ,
                                   axis=1)[..., None]
            os_, kn, vn = swa_sample(qkv_s, ck_all, cv_all, j, sink_rows, cos_s, sin_s)
            swa_kp.append(kp.reshape(B, WINDOW, N_KV_HEADS, HEAD_DIM))
            swa_vp.append(vp.reshape(B, WINDOW, N_KV_HEADS, HEAD_DIM))
            swa_ks.append(kn.reshape(SB, WINDOW, N_KV_HEADS, HEAD_DIM))
            swa_vs.append(vn.reshape(SB, WINDOW, N_KV_HEADS, HEAD_DIM))
            xp = matmul_post(op, w_attn_o_bf, j, g_mix_post, i, xp, TD)
            xs = matmul_post(os_, w_attn_o_bf, j, g_mix_post, i, xs, MS)
        else:
            a_re, a_im, b_blk, c_blk = _s5_weights(
                ssm_lambda_re[j], ssm_lambda_im[j], ssm_log_step[j], ssm_b_re[j], ssm_b_im[j],
                ssm_c_re[j], ssm_c_im[j])
            d_row = ssm_d[j].reshape(1, D_MODEL)
            up_ = norm_matmul(xp, g_mix_pre, i, w_ssm_in_bf, j, TD)
            us_ = norm_matmul(xs, g_mix_pre, i, w_ssm_in_bf, j, MS)
            a_tiles = jnp.stack([a_re.reshape(N_SUPER, SUBLANES, LANES),
                                 a_im.reshape(N_SUPER, SUBLANES, LANES)], axis=1
                                ).reshape(2 * N_SUPER, SUBLANES, LANES)
            zero_state = jnp.zeros((B, N_SUPER, SUBLANES, LANES), F32)
            zp, rp, ip = s5_prompt(up_, zero_state, zero_state, a_tiles, b_blk, c_blk, d_row,
                                   B, 256)
            nstate = N_SSM_GROUPS * SSM_STATE
            zs, rn, im_ = s5_sample(us_, state_ssm_re[j].reshape(SB, nstate),
                                    state_ssm_im[j].reshape(SB, nstate),
                                    a_re.reshape(1, nstate), a_im.reshape(1, nstate),
                                    b_blk, c_blk, d_row)
            ssm_rp.append(rp.reshape(B, N_SSM_GROUPS, SSM_STATE))
            ssm_ip.append(ip.reshape(B, N_SSM_GROUPS, SSM_STATE))
            ssm_rs.append(rn.reshape(SB, N_SSM_GROUPS, SSM_STATE))
            ssm_is.append(im_.reshape(SB, N_SSM_GROUPS, SSM_STATE))
            xp = glu_post(zp, w_glu_bf, j, g_mix_post, i, xp, TD)
            xs = glu_post(zs, w_glu_bf, j, g_mix_post, i, xs, MS)
        mk, mv, mk_heads, mv_heads = mem_kv(mem, g_mem, w_mem_kv_bf, i)
        memk_p.append(mk_heads.reshape(B, N_MEM, N_XHEADS, XHEAD_DIM))
        memv_p.append(mv_heads.reshape(B, N_MEM, N_XHEADS, XHEAD_DIM))
        xp = xattn_prompt(xp, i, g_x_pre, w_x_q_bf, mk, mv, w_x_o_bf, g_x_post, B, 2 * TD)
        qs = norm_matmul(xs, g_x_pre, i, w_x_q_bf, i, MS)
        as_ = xattn_sample(qs, cmk_all, cmv_all, i)
        xs = matmul_post(as_, w_x_o_bf, i, g_x_post, i, xs, MS)
        xp, cp = ffn_prompt(xp, i, g_ffn_pre, w_up_bf, i % 2, ffn_conv_w, ffn_conv_b, w_down_bf,
                            g_ffn_post, B, TMP, 512)
        xs, cs = ffn_sample(xs, i, g_ffn_pre, w_up_bf, i % 2, ffn_conv_w, ffn_conv_b, w_down_bf,
                            g_ffn_post, state_ffn_conv, 1408)
        conv_p.append(cp)
        conv_s.append(cs)
    return (xp.reshape(B, SEQ, D_MODEL), xs.reshape(SB, DEC_SEQ, D_MODEL),
            jnp.stack(swa_kp), jnp.stack(swa_vp), jnp.stack(swa_ks), jnp.stack(swa_vs),
            jnp.stack(ssm_rp), jnp.stack(ssm_ip), jnp.stack(ssm_rs), jnp.stack(ssm_is),
            jnp.stack(conv_p), jnp.stack(conv_s), jnp.stack(memk_p), jnp.stack(memv_p))
```

```python
import functools

import jax
import jax.numpy as jnp
from jax import lax
from jax.experimental import pallas as pl
from jax.experimental.pallas import tpu as pltpu

D_MODEL = 2048
SEQ = 4096
DEPTH = 4
DEC_SEQ = 8
PAST_LEN = 16384
HEAD_DIM = 64
N_HEADS = 32
N_KV_HEADS = 4
GQA_GROUP = 8
WINDOW = 128
ROPE_THETA = 10000.0
SSM_GROUP = 16
N_SSM_GROUPS = 128
SSM_STATE = 64
N_MEM = 256
N_XHEADS = 4
XHEAD_DIM = 128
D_FF = 5632
NORM_EPS = 1e-6
NEG_INF = -1e30

F32 = jnp.float32
BF16 = jnp.bfloat16

V7X_VMEM_BYTES = 64 * 1024 * 1024
VMEM_LIMIT = V7X_VMEM_BYTES - 8 * 1024 * 1024
LANES = 128
SUBLANES = 8
SSM_SUPER = 16
N_SUPER = N_SSM_GROUPS // SSM_SUPER
SUPER_IN = SSM_SUPER * SSM_GROUP
SUPER_STATE = SSM_SUPER * SSM_STATE
SLABS = SUPER_STATE // LANES


def _params(*sem, vmem=VMEM_LIMIT):
    return pltpu.CompilerParams(dimension_semantics=sem, vmem_limit_bytes=vmem)


def _rms(x, g):
    ms = jnp.mean(x * x, axis=-1, keepdims=True)
    return x * lax.rsqrt(ms + NORM_EPS) * g


def _bdot(a, b):
    return jnp.dot(a.astype(BF16), b.astype(BF16), preferred_element_type=F32)


def _resident(shape, index_map):
    return pl.BlockSpec(shape, index_map, pipeline_mode=pl.Buffered(1))


ROW_GROUP = 256


def _row_groups(tm):
    rg = min(ROW_GROUP, tm)
    return [slice(k * rg, (k + 1) * rg) for k in range(tm // rg)]


def _norm_matmul_kernel(x_ref, g_ref, w_ref, o_ref):
    g = g_ref[...]
    for rows in _row_groups(x_ref.shape[0]):
        hn = _rms(x_ref[rows, :], g).astype(BF16)
        o_ref[rows, :] = jnp.dot(hn, w_ref[...], preferred_element_type=F32).astype(o_ref.dtype)


def norm_matmul(x, g, g_layer, w, w_layer, tm, out_dtype=F32):
    M, D = x.shape
    N = w.shape[-1]
    return pl.pallas_call(
        _norm_matmul_kernel,
        grid=(M // tm,),
        in_specs=[
            pl.BlockSpec((tm, D), lambda i: (i, 0)),
            pl.BlockSpec((None, 1, D), lambda i: (g_layer, 0, 0)),
            _resident((None, D, N), lambda i: (w_layer, 0, 0)),
        ],
        out_specs=pl.BlockSpec((tm, N), lambda i: (i, 0)),
        out_shape=jax.ShapeDtypeStruct((M, N), out_dtype),
        compiler_params=_params("arbitrary"),
        name="norm_matmul",
    )(x, g.reshape(g.shape[0], 1, D), w)


def _mem_kv_kernel(x_ref, g_ref, w_ref, k_ref, v_ref, kh_ref, vh_ref):
    xw = N_XHEADS * XHEAD_DIM
    hn = _rms(x_ref[...], g_ref[...]).astype(BF16)
    y = jnp.dot(hn, w_ref[...], preferred_element_type=F32)
    k_ref[...] = y[:, :xw]
    v_ref[...] = y[:, xw:]
    for h in range(N_XHEADS):
        kh_ref[:, h, :] = y[:, h * XHEAD_DIM:(h + 1) * XHEAD_DIM]
        vh_ref[:, h, :] = y[:, xw + h * XHEAD_DIM:xw + (h + 1) * XHEAD_DIM]


def mem_kv(mem, g, w, layer):
    M, D = mem.shape
    xw = N_XHEADS * XHEAD_DIM
    flat = jax.ShapeDtypeStruct((M, xw), F32)
    heads = jax.ShapeDtypeStruct((M, N_XHEADS, XHEAD_DIM), F32)
    return pl.pallas_call(
        _mem_kv_kernel,
        grid=(1,),
        in_specs=[
            pl.BlockSpec((M, D), lambda i: (0, 0)),
            pl.BlockSpec((None, 1, D), lambda i: (layer, 0, 0)),
            pl.BlockSpec((None, D, 2 * xw), lambda i: (layer, 0, 0)),
        ],
        out_specs=[
            pl.BlockSpec((M, xw), lambda i: (0, 0)),
            pl.BlockSpec((M, xw), lambda i: (0, 0)),
            pl.BlockSpec((M, N_XHEADS, XHEAD_DIM), lambda i: (0, 0, 0)),
            pl.BlockSpec((M, N_XHEADS, XHEAD_DIM), lambda i: (0, 0, 0)),
        ],
        out_shape=[flat, flat, heads, heads],
        compiler_params=_params("arbitrary"),
        name="mem_kv",
    )(mem, g.reshape(g.shape[0], 1, D), w)


def _matmul_post_kernel(a_ref, w_ref, g_ref, x_ref, o_ref):
    g = g_ref[...]
    for rows in _row_groups(x_ref.shape[0]):
        y = jnp.dot(a_ref[rows, :].astype(BF16), w_ref[...], preferred_element_type=F32)
        o_ref[rows, :] = x_ref[rows, :] + _rms(y, g)


def matmul_post(a, w, w_layer, g, g_layer, x, tm):
    M, K = a.shape
    D = w.shape[-1]
    return pl.pallas_call(
        _matmul_post_kernel,
        grid=(M // tm,),
        in_specs=[
            pl.BlockSpec((tm, K), lambda i: (i, 0)),
            _resident((None, K, D), lambda i: (w_layer, 0, 0)),
            pl.BlockSpec((None, 1, D), lambda i: (g_layer, 0, 0)),
            pl.BlockSpec((tm, D), lambda i: (i, 0)),
        ],
        out_specs=pl.BlockSpec((tm, D), lambda i: (i, 0)),
        out_shape=jax.ShapeDtypeStruct((M, D), F32),
        compiler_params=_params("arbitrary"),
        name="matmul_post",
    )(a, w, g.reshape(g.shape[0], 1, D), x)


GLU_COLS = 512


def _glu_post_kernel(z_ref, w_ref, g_ref, x_ref, o_ref, y_ref):
    D = o_ref.shape[1]
    g = g_ref[...]
    for rows in _row_groups(x_ref.shape[0]):
        z = z_ref[rows, :]
        for n in range(D // GLU_COLS):
            cols = slice(n * GLU_COLS, (n + 1) * GLU_COLS)
            gcols = slice(D + n * GLU_COLS, D + (n + 1) * GLU_COLS)
            val = jnp.dot(z, w_ref[:, cols], preferred_element_type=F32)
            gate = jnp.dot(z, w_ref[:, gcols], preferred_element_type=F32)
            y_ref[rows, cols] = val * jax.nn.sigmoid(gate)
        o_ref[rows, :] = x_ref[rows, :] + _rms(y_ref[rows, :], g)


def glu_post(z, w, w_layer, g, g_layer, x, tm):
    M, K = z.shape
    D = D_MODEL
    return pl.pallas_call(
        _glu_post_kernel,
        grid=(M // tm,),
        in_specs=[
            pl.BlockSpec((tm, K), lambda i: (i, 0)),
            _resident((None, K, 2 * D), lambda i: (w_layer, 0, 0)),
            pl.BlockSpec((None, 1, D), lambda i: (g_layer, 0, 0)),
            pl.BlockSpec((tm, D), lambda i: (i, 0)),
        ],
        out_specs=pl.BlockSpec((tm, D), lambda i: (i, 0)),
        out_shape=jax.ShapeDtypeStruct((M, D), F32),
        scratch_shapes=[pltpu.VMEM((tm, D), F32)],
        compiler_params=_params("arbitrary"),
        name="glu_post",
    )(z, w, g.reshape(g.shape[0], 1, D), x)


def _rope_chunk(xc, cos, sin_signed):
    lane = lax.broadcasted_iota(jnp.int32, xc.shape, 1)
    first_half = (lane & (HEAD_DIM - 1)) < (HEAD_DIM // 2)
    partner = jnp.where(first_half, pltpu.roll(xc, LANES - HEAD_DIM // 2, 1),
                        pltpu.roll(xc, HEAD_DIM // 2, 1))
    return xc * cos + partner * sin_signed


def _rope_tables(pos):
    half = HEAD_DIM // 2
    inv_freq = ROPE_THETA ** (-jnp.arange(half, dtype=F32) * 2.0 / HEAD_DIM)
    ang = pos.astype(F32)[:, None] * inv_freq[None, :]
    cos, sin = jnp.cos(ang), jnp.sin(ang)
    cos128 = jnp.tile(cos, (1, LANES // half))
    sin128 = jnp.tile(jnp.concatenate([-sin, sin], axis=1), (1, LANES // HEAD_DIM))
    return cos128, sin128


def _swa_prompt_kernel(sink_ref, q_ref, kv_ref, cos_ref, sin_ref, wu_ref, wd_ref,
                       o_ref, kc_ref, vc_ref, wub_ref, wdb_ref, kk_ref, vv_ref, *, layer):
    i = pl.program_id(1)
    blk = WINDOW
    wub_ref[...] = wu_ref[...].astype(BF16)
    wdb_ref[...] = wd_ref[...].astype(BF16)
    cos, sn = cos_ref[...], sin_ref[...]
    kv = kv_ref[...]
    nkc = N_KV_HEADS * HEAD_DIM
    k = jnp.concatenate([_rope_chunk(kv[:, c * LANES:(c + 1) * LANES], cos, sn)
                         for c in range(nkc // LANES)], axis=1)
    v = kv[:, nkc:2 * nkc]
    kc_ref[...] = k
    vc_ref[...] = v

    cur = i % 2
    prv = 1 - cur

    @pl.when(i == 0)
    def _():
        kk_ref[1] = jnp.zeros((blk, nkc), BF16)
        vv_ref[1] = jnp.zeros((blk, nkc), BF16)

    kb, vb = k.astype(BF16), v.astype(BF16)
    kk_ref[cur] = kb
    vv_ref[cur] = vb
    kk = jnp.concatenate([kk_ref[prv], kb], axis=0)
    vv = jnp.concatenate([vv_ref[prv], vb], axis=0)
    rows = GQA_GROUP * blk
    qi = lax.broadcasted_iota(jnp.int32, (rows, 2 * blk), 0) & (blk - 1)
    si = lax.broadcasted_iota(jnp.int32, (rows, 2 * blk), 1)
    d = blk + qi - si
    mask = (d >= 0) & (d < WINDOW) & ((si >= blk) | (i > 0))

    chunks_per_group = GQA_GROUP * HEAD_DIM // LANES

    def scores(kh):
        hs = slice(kh * HEAD_DIM, (kh + 1) * HEAD_DIM)
        pieces = []
        for c in range(kh * chunks_per_group, (kh + 1) * chunks_per_group):
            qc = (_rope_chunk(q_ref[:, c * LANES:(c + 1) * LANES], cos, sn)
                  * (HEAD_DIM ** -0.5)).astype(BF16)
            pieces += [qc[:, hh * HEAD_DIM:(hh + 1) * HEAD_DIM] for hh in range(LANES // HEAD_DIM)]
        qs = jnp.concatenate(pieces, axis=0)
        return lax.dot_general(qs, kk[:, hs], (((1,), (1,)), ((), ())),
                               preferred_element_type=F32)

    ahead = 2
    s_all = [scores(kh) for kh in range(min(ahead, N_KV_HEADS))]
    outs = []
    for kh in range(N_KV_HEADS):
        if kh + ahead < N_KV_HEADS:
            s_all.append(scores(kh + ahead))
        hs = slice(kh * HEAD_DIM, (kh + 1) * HEAD_DIM)
        sk = jnp.concatenate([jnp.full((blk, 1), sink_ref[layer, kh * GQA_GROUP + j], F32)
                              for j in range(GQA_GROUP)], axis=0)
        s = jnp.where(mask, s_all[kh], NEG_INF)
        mx = jnp.maximum(jnp.max(s, axis=1, keepdims=True), sk)
        p = jnp.exp(s - mx)
        den = jnp.sum(p, axis=1, keepdims=True) + jnp.exp(sk - mx)
        o = jnp.dot(p.astype(BF16), vv[:, hs], preferred_element_type=F32) / den
        outs += [o[j * blk:(j + 1) * blk, :] for j in range(GQA_GROUP)]
    o_ref[...] = jnp.concatenate(outs, axis=1).astype(o_ref.dtype)


def swa_prompt(qkv, sinks, layer, cos, sin, batch, w_up, w_down):
    blk = WINDOW
    nb = SEQ // blk
    nq = N_HEADS * HEAD_DIM
    nkv = 2 * N_KV_HEADS * HEAD_DIM
    per_layer = batch * nb // 2
    ur, dr = D_MODEL // per_layer, D_FF // per_layer

    def w_in(b, i):
        n = b * nb + i
        return (2 * layer + n // per_layer, n % per_layer, 0)

    def w_out(b, i):
        n = b * nb + i
        return (n // per_layer, n % per_layer, 0)

    o, kc, vc, w_up_bf, w_down_bf = pl.pallas_call(
        functools.partial(_swa_prompt_kernel, layer=layer),
        grid=(batch, nb),
        in_specs=[
            pl.BlockSpec(memory_space=pltpu.SMEM),
            pl.BlockSpec((blk, nq), lambda b, i: (b * nb + i, 0)),
            pl.BlockSpec((blk, nkv), lambda b, i: (b * nb + i, nq // nkv)),
            pl.BlockSpec((blk, LANES), lambda b, i: (i, 0)),
            pl.BlockSpec((blk, LANES), lambda b, i: (i, 0)),
            pl.BlockSpec((None, ur, 2 * D_FF), w_in),
            pl.BlockSpec((None, dr, D_MODEL), w_in),
        ],
        out_specs=[
            pl.BlockSpec((blk, nq), lambda b, i: (b * nb + i, 0)),
            pl.BlockSpec((None, blk, nkv // 2), lambda b, i: (b, 0, 0)),
            pl.BlockSpec((None, blk, nkv // 2), lambda b, i: (b, 0, 0)),
            pl.BlockSpec((None, ur, 2 * D_FF), w_out),
            pl.BlockSpec((None, dr, D_MODEL), w_out),
        ],
        out_shape=[
            jax.ShapeDtypeStruct((batch * SEQ, nq), BF16),
            jax.ShapeDtypeStruct((batch, blk, nkv // 2), F32),
            jax.ShapeDtypeStruct((batch, blk, nkv // 2), F32),
            jax.ShapeDtypeStruct((2, D_MODEL, 2 * D_FF), BF16),
            jax.ShapeDtypeStruct((2, D_FF, D_MODEL), BF16),
        ],
        scratch_shapes=[pltpu.VMEM((2, blk, nkv // 2), BF16),
                        pltpu.VMEM((2, blk, nkv // 2), BF16)],
        compiler_params=_params("arbitrary", "arbitrary"),
        name="swa_prompt",
    )(sinks, qkv, qkv, cos, sin, w_up, w_down)
    return o, kc, vc, w_up_bf, w_down_bf


def _swa_sample_kernel(sink_ref, qkv_ref, ck_ref, cv_ref, cos_ref, sin_ref,
                       o_ref, nk_ref, nv_ref, *, nb):
    L = DEC_SEQ
    wb = WINDOW
    nq = N_HEADS * HEAD_DIM
    nkc = N_KV_HEADS * HEAD_DIM
    cos, sn = cos_ref[...], sin_ref[...]
    rows = GQA_GROUP * L
    t_q = lax.broadcasted_iota(jnp.int32, (rows, wb), 0) % L
    c_k = lax.broadcasted_iota(jnp.int32, (rows, wb), 1)
    mask_c = c_k >= t_q + 1
    t_q2 = lax.broadcasted_iota(jnp.int32, (rows, L), 0) % L
    t_k2 = lax.broadcasted_iota(jnp.int32, (rows, L), 1)
    mask_n = t_k2 <= t_q2
    for b in range(nb):
        x = qkv_ref[b * L:(b + 1) * L, :]
        qr = jnp.concatenate([_rope_chunk(x[:, c * LANES:(c + 1) * LANES], cos, sn)
                              for c in range(nq // LANES)], axis=1) * (HEAD_DIM ** -0.5)
        kn = jnp.concatenate([_rope_chunk(x[:, nq + c * LANES:nq + (c + 1) * LANES], cos, sn)
                              for c in range(nkc // LANES)], axis=1)
        vn = x[:, nq + nkc:nq + 2 * nkc]
        ck = ck_ref[b]
        cv = cv_ref[b]
        nk_ref[b, 0:wb - L, :] = ck[L:wb, :]
        nk_ref[b, wb - L:wb, :] = kn
        nv_ref[b, 0:wb - L, :] = cv[L:wb, :]
        nv_ref[b, wb - L:wb, :] = vn
        ckb, cvb, knb, vnb = ck.astype(BF16), cv.astype(BF16), kn.astype(BF16), vn.astype(BF16)
        dn = (((1,), (1,)), ((), ()))
        scores = []
        for kh in range(N_KV_HEADS):
            hs = slice(kh * HEAD_DIM, (kh + 1) * HEAD_DIM)
            qs = jnp.concatenate(
                [qr[:, (kh * GQA_GROUP + j) * HEAD_DIM:(kh * GQA_GROUP + j + 1) * HEAD_DIM]
                 for j in range(GQA_GROUP)], axis=0).astype(BF16)
            scores.append((lax.dot_general(qs, ckb[:, hs], dn, preferred_element_type=F32),
                           lax.dot_general(qs, knb[:, hs], dn, preferred_element_type=F32)))
        outs = []
        for kh in range(N_KV_HEADS):
            hs = slice(kh * HEAD_DIM, (kh + 1) * HEAD_DIM)
            s_c = jnp.where(mask_c, scores[kh][0], NEG_INF)
            s_n = jnp.where(mask_n, scores[kh][1], NEG_INF)
            sk = sink_ref[kh]
            mx = jnp.maximum(jnp.maximum(jnp.max(s_c, axis=1, keepdims=True),
                                         jnp.max(s_n, axis=1, keepdims=True)), sk)
            p_c = jnp.exp(s_c - mx)
            p_n = jnp.exp(s_n - mx)
            den = (jnp.sum(p_c, axis=1, keepdims=True) + jnp.sum(p_n, axis=1, keepdims=True)
                   + jnp.exp(sk - mx))
            o = (jnp.dot(p_c.astype(BF16), cvb[:, hs], preferred_element_type=F32)
                 + jnp.dot(p_n.astype(BF16), vnb[:, hs], preferred_element_type=F32)) / den
            outs.extend(o[j * L:(j + 1) * L, :] for j in range(GQA_GROUP))
        o_ref[b * L:(b + 1) * L, :] = jnp.concatenate(outs, axis=1).astype(o_ref.dtype)


def swa_sample(qkv, cache_k, cache_v, layer, sink_rows, cos, sin, nb=8):
    nbatch = cache_k.shape[1]
    nq = N_HEADS * HEAD_DIM
    nkc = N_KV_HEADS * HEAD_DIM
    L = DEC_SEQ
    return pl.pallas_call(
        functools.partial(_swa_sample_kernel, nb=nb),
        grid=(nbatch // nb,),
        in_specs=[
            pl.BlockSpec((N_KV_HEADS, GQA_GROUP * L, 1), lambda g: (0, 0, 0)),
            pl.BlockSpec((nb * L, nq + 2 * nkc), lambda g: (g, 0)),
            pl.BlockSpec((None, nb, WINDOW, nkc), lambda g: (layer, g, 0, 0)),
            pl.BlockSpec((None, nb, WINDOW, nkc), lambda g: (layer, g, 0, 0)),
            pl.BlockSpec((L, LANES), lambda g: (0, 0)),
            pl.BlockSpec((L, LANES), lambda g: (0, 0)),
        ],
        out_specs=[
            pl.BlockSpec((nb * L, nq), lambda g: (g, 0)),
            pl.BlockSpec((nb, WINDOW, nkc), lambda g: (g, 0, 0)),
            pl.BlockSpec((nb, WINDOW, nkc), lambda g: (g, 0, 0)),
        ],
        out_shape=[
            jax.ShapeDtypeStruct((nbatch * L, nq), BF16),
            jax.ShapeDtypeStruct((nbatch, WINDOW, nkc), F32),
            jax.ShapeDtypeStruct((nbatch, WINDOW, nkc), F32),
        ],
        compiler_params=_params("arbitrary"),
        name="swa_sample",
    )(sink_rows, qkv, cache_k, cache_v, cos, sin)


_XHEADS = [slice(h * XHEAD_DIM, (h + 1) * XHEAD_DIM) for h in range(N_XHEADS)]


def _xattn_scores(q, mk):
    return [lax.dot_general((q[:, hs] * (XHEAD_DIM ** -0.5)).astype(BF16), mk[:, hs].astype(BF16),
                            (((1,), (1,)), ((), ())), preferred_element_type=F32)
            for hs in _XHEADS]


def _xattn_attend(scores, mv):
    outs = []
    for hs, s in zip(_XHEADS, scores):
        mx = jnp.max(s, axis=1, keepdims=True)
        p = jnp.exp(s - mx)
        den = jnp.sum(p, axis=1, keepdims=True)
        outs.append(jnp.dot(p.astype(BF16), mv[:, hs].astype(BF16),
                            preferred_element_type=F32) / den)
    return jnp.concatenate(outs, axis=1)


def _xattn_heads(q, mk, mv):
    return _xattn_attend(_xattn_scores(q, mk), mv)


def _xattn_prompt_kernel(x_ref, gpre_ref, wq_ref, mk_ref, mv_ref, wo_ref, gpost_ref, o_ref):
    mk = mk_ref[...].astype(BF16)
    mv = mv_ref[...].astype(BF16)
    gpre, gpost = gpre_ref[...], gpost_ref[...]
    groups = _row_groups(x_ref.shape[0])

    def project(rows):
        q = jnp.dot(_rms(x_ref[rows, :], gpre).astype(BF16), wq_ref[...],
                    preferred_element_type=F32)
        return _xattn_scores(q, mk)

    scores = project(groups[0])
    for k, rows in enumerate(groups):
        nxt = project(groups[k + 1]) if k + 1 < len(groups) else None
        a = _xattn_attend(scores, mv).astype(BF16)
        y = jnp.dot(a, wo_ref[...], preferred_element_type=F32)
        o_ref[rows, :] = x_ref[rows, :] + _rms(y, gpost)
        scores = nxt


def xattn_prompt(x, layer, g_pre, w_q, mk, mv, w_o, g_post, batch, tm):
    nq = SEQ // tm
    xw = N_XHEADS * XHEAD_DIM
    D = D_MODEL
    return pl.pallas_call(
        _xattn_prompt_kernel,
        grid=(batch, nq),
        in_specs=[
            pl.BlockSpec((tm, D), lambda b, i: (b * nq + i, 0)),
            pl.BlockSpec((None, 1, D), lambda b, i: (layer, 0, 0)),
            _resident((None, D, xw), lambda b, i: (layer, 0, 0)),
            pl.BlockSpec((N_MEM, xw), lambda b, i: (b, 0)),
            pl.BlockSpec((N_MEM, xw), lambda b, i: (b, 0)),
            _resident((None, xw, D), lambda b, i: (layer, 0, 0)),
            pl.BlockSpec((None, 1, D), lambda b, i: (layer, 0, 0)),
        ],
        out_specs=pl.BlockSpec((tm, D), lambda b, i: (b * nq + i, 0)),
        out_shape=jax.ShapeDtypeStruct((batch * SEQ, D), F32),
        compiler_params=_params("arbitrary", "arbitrary"),
        name="xattn_prompt",
    )(x, g_pre.reshape(DEPTH, 1, D), w_q, mk, mv, w_o, g_post.reshape(DEPTH, 1, D))


def _xattn_sample_kernel(q_ref, mk_ref, mv_ref, o_ref, *, nb):
    L = DEC_SEQ
    for b in range(nb):
        o_ref[b * L:(b + 1) * L, :] = _xattn_heads(
            q_ref[b * L:(b + 1) * L, :], mk_ref[b], mv_ref[b]).astype(o_ref.dtype)


def xattn_sample(q, cache_k, cache_v, layer, nb=8):
    nbatch = cache_k.shape[1]
    L = DEC_SEQ
    xw = N_XHEADS * XHEAD_DIM
    return pl.pallas_call(
        functools.partial(_xattn_sample_kernel, nb=nb),
        grid=(nbatch // nb,),
        in_specs=[
            pl.BlockSpec((nb * L, xw), lambda g: (g, 0)),
            pl.BlockSpec((None, nb, N_MEM, xw), lambda g: (layer, g, 0, 0)),
            pl.BlockSpec((None, nb, N_MEM, xw), lambda g: (layer, g, 0, 0)),
        ],
        out_specs=pl.BlockSpec((nb * L, xw), lambda g: (g, 0)),
        out_shape=jax.ShapeDtypeStruct((nbatch * L, xw), BF16),
        compiler_params=_params("arbitrary"),
        name="xattn_sample",
    )(q, cache_k, cache_v)


def _s5_prompt_kernel(u_ref, h0r_ref, h0i_ref, a_ref, b_ref, c_ref, d_ref,
                      z_ref, hr_ref, hi_ref, x_ref, hst_ref, *, T):
    c = pl.program_id(1)
    nslab = 2 * SLABS
    G = T // SUBLANES

    @pl.when(c == 0)
    def _():
        for sb in range(N_SUPER):
            hst_ref[2 * sb] = h0r_ref[sb]
            hst_ref[2 * sb + 1] = h0i_ref[sb]

    def slab_rows(sb, s):
        return pl.ds((sb * nslab + s) * S5_SLAB_PITCH, SUBLANES)

    for sb in range(N_SUPER):
        bu = _bdot(u_ref[:, sb * SUPER_IN:(sb + 1) * SUPER_IN], b_ref[sb])
        for s in range(nslab):
            x_ref[:, slab_rows(sb, s), :] = bu[:, s * LANES:(s + 1) * LANES].reshape(
                G, SUBLANES, LANES)

    a = [a_ref[k] for k in range(2 * N_SUPER)]
    h_init = tuple(hst_ref[k] for k in range(2 * N_SUPER))

    def step(t, h):
        g = lax.shift_right_logical(t, 3)
        r = lax.bitwise_and(t, SUBLANES - 1)
        new = []
        for sb in range(N_SUPER):
            ir = pl.ds((sb * nslab) * S5_SLAB_PITCH + r, SUBLANES, stride=S5_SLAB_PITCH)
            ii = pl.ds((sb * nslab + SLABS) * S5_SLAB_PITCH + r, SUBLANES, stride=S5_SLAB_PITCH)
            ar, ai = a[2 * sb], a[2 * sb + 1]
            hr, hi = h[2 * sb], h[2 * sb + 1]
            nr = ar * hr - ai * hi + x_ref[g, ir, :]
            ni = ar * hi + ai * hr + x_ref[g, ii, :]
            x_ref[g, ir, :] = nr
            x_ref[g, ii, :] = ni
            new += [nr, ni]
        return tuple(new)

    h_fin = lax.fori_loop(0, T, step, h_init)
    for sb in range(N_SUPER):
        hst_ref[2 * sb] = h_fin[2 * sb]
        hst_ref[2 * sb + 1] = h_fin[2 * sb + 1]
        hr_ref[sb] = h_fin[2 * sb]
        hi_ref[sb] = h_fin[2 * sb + 1]

    for sb in range(N_SUPER):
        hcat = jnp.concatenate([x_ref[:, slab_rows(sb, s), :].reshape(T, LANES)
                                for s in range(nslab)], axis=1)
        cols = slice(sb * SUPER_IN, (sb + 1) * SUPER_IN)
        y = _bdot(hcat, c_ref[sb]) + d_ref[:, cols] * u_ref[:, cols]
        z_ref[:, cols] = jax.nn.gelu(y).astype(z_ref.dtype)


S5_SLAB_PITCH = 12


def s5_prompt(u, h0r, h0i, a_tiles, b_blk, c_blk, d_row, batch, T):
    nc = SEQ // T
    st = (None, N_SUPER, SUBLANES, LANES)
    return pl.pallas_call(
        functools.partial(_s5_prompt_kernel, T=T),
        grid=(batch, nc),
        in_specs=[
            pl.BlockSpec((T, D_MODEL), lambda b, c: (b * nc + c, 0)),
            pl.BlockSpec(st, lambda b, c: (b, 0, 0, 0)),
            pl.BlockSpec(st, lambda b, c: (b, 0, 0, 0)),
            _resident((2 * N_SUPER, SUBLANES, LANES), lambda b, c: (0, 0, 0)),
            _resident((N_SUPER, SUPER_IN, 2 * SUPER_STATE), lambda b, c: (0, 0, 0)),
            _resident((N_SUPER, 2 * SUPER_STATE, SUPER_IN), lambda b, c: (0, 0, 0)),
            _resident((1, D_MODEL), lambda b, c: (0, 0)),
        ],
        out_specs=[
            pl.BlockSpec((T, D_MODEL), lambda b, c: (b * nc + c, 0)),
            pl.BlockSpec(st, lambda b, c: (b, 0, 0, 0)),
            pl.BlockSpec(st, lambda b, c: (b, 0, 0, 0)),
        ],
        out_shape=[
            jax.ShapeDtypeStruct((batch * SEQ, D_MODEL), BF16),
            jax.ShapeDtypeStruct((batch, N_SUPER, SUBLANES, LANES), F32),
            jax.ShapeDtypeStruct((batch, N_SUPER, SUBLANES, LANES), F32),
        ],
        scratch_shapes=[
            pltpu.VMEM((T // SUBLANES, N_SUPER * 2 * SLABS * S5_SLAB_PITCH, LANES), F32),
            pltpu.VMEM((2 * N_SUPER, SUBLANES, LANES), F32),
        ],
        compiler_params=_params("arbitrary", "arbitrary"),
        name="s5_prompt",
    )(u, h0r, h0i, a_tiles, b_blk, c_blk, d_row)


def _s5_sample_kernel(u_ref, h0r_ref, h0i_ref, ar_ref, ai_ref, b_ref, c_ref, d_ref,
                      z_ref, hr_ref, hi_ref, x_ref, *, nbatch):
    L = DEC_SEQ
    R = nbatch * L
    for sb in range(N_SUPER):
        cols = slice(sb * SUPER_IN, (sb + 1) * SUPER_IN)
        bu = _bdot(u_ref[:, cols], b_ref[sb])
        for s in range(2 * SLABS):
            x_ref[s * R:(s + 1) * R, :] = bu[:, s * LANES:(s + 1) * LANES]
        for s in range(SLABS):
            st = slice(sb * SUPER_STATE + s * LANES, sb * SUPER_STATE + (s + 1) * LANES)
            ar, ai = ar_ref[:, st], ai_ref[:, st]
            hr, hi = h0r_ref[:, st], h0i_ref[:, st]
            for t in range(L):
                rr = pl.ds(s * R + t, nbatch, stride=L)
                ri = pl.ds((SLABS + s) * R + t, nbatch, stride=L)
                nr = ar * hr - ai * hi + x_ref[rr, :]
                ni = ar * hi + ai * hr + x_ref[ri, :]
                x_ref[rr, :] = nr
                x_ref[ri, :] = ni
                hr, hi = nr, ni
            hr_ref[:, st] = hr
            hi_ref[:, st] = hi
        hcat = jnp.concatenate([x_ref[s * R:(s + 1) * R, :] for s in range(2 * SLABS)], axis=1)
        y = _bdot(hcat, c_ref[sb]) + d_ref[:, cols] * u_ref[:, cols]
        z_ref[:, cols] = jax.nn.gelu(y).astype(z_ref.dtype)


def s5_sample(u, h0r, h0i, a_re, a_im, b_blk, c_blk, d_row):
    rows = u.shape[0]
    nbatch = rows // DEC_SEQ
    nstate = N_SSM_GROUPS * SSM_STATE
    full = lambda shape: pl.BlockSpec(shape, lambda i: (0,) * len(shape))
    return pl.pallas_call(
        functools.partial(_s5_sample_kernel, nbatch=nbatch),
        grid=(1,),
        in_specs=[
            full((rows, D_MODEL)), full((nbatch, nstate)), full((nbatch, nstate)),
            full((1, nstate)), full((1, nstate)),
            full((N_SUPER, SUPER_IN, 2 * SUPER_STATE)),
            full((N_SUPER, 2 * SUPER_STATE, SUPER_IN)),
            full((1, D_MODEL)),
        ],
        out_specs=[full((rows, D_MODEL)), full((nbatch, nstate)), full((nbatch, nstate))],
        out_shape=[
            jax.ShapeDtypeStruct((rows, D_MODEL), BF16),
            jax.ShapeDtypeStruct((nbatch, nstate), F32),
            jax.ShapeDtypeStruct((nbatch, nstate), F32),
        ],
        scratch_shapes=[pltpu.VMEM((2 * SLABS * rows, LANES), F32)],
        compiler_params=_params("arbitrary"),
        name="s5_sample",
    )(u, h0r, h0i, a_re, a_im, b_blk, c_blk, d_row)


def _s5_weights(lam_re, lam_im, log_step, b_re, b_im, c_re, c_im):
    lam = lax.complex(lam_re.astype(F32), lam_im.astype(F32))
    dt = jnp.exp(log_step.astype(F32))[:, None]
    abar = jnp.exp(lam * dt)
    bbar = ((abar - 1.0) / lam)[..., None] * lax.complex(b_re.astype(F32), b_im.astype(F32))
    eye = jnp.eye(SSM_SUPER, dtype=F32)

    def b_layout(m):
        m = m.reshape(N_SUPER, SSM_SUPER, SSM_STATE, SSM_GROUP)
        return jnp.einsum('sgph,gk->sghkp', m, eye).reshape(N_SUPER, SUPER_IN, SUPER_STATE)

    def c_layout(m):
        m = m.reshape(N_SUPER, SSM_SUPER, SSM_GROUP, SSM_STATE)
        return jnp.einsum('sghp,gk->sgpkh', m, eye).reshape(N_SUPER, SUPER_STATE, SUPER_IN)

    b_blk = jnp.concatenate([b_layout(bbar.real), b_layout(bbar.imag)], axis=2).astype(BF16)
    c_blk = jnp.concatenate([c_layout(c_re.astype(F32)), -c_layout(c_im.astype(F32))],
                            axis=1).astype(BF16)
    return abar.real, abar.imag, b_blk, c_blk


def _ffn_prompt_kernel(x_ref, gpre_ref, wua_ref, wuv_ref, cwa_ref, cwv_ref, cba_ref, cbv_ref,
                       wd_ref, gpost_ref, o_ref, sa_ref, sv_ref,
                       hn_ref, ua_ref, uv_ref, act_ref, ha_ref, hv_ref,
                       *, nc, blocks_per_seq, rc, rm):
    i = pl.program_id(0)
    c = pl.program_id(1)
    tm, D = hn_ref.shape
    hdr = SUBLANES

    @pl.when(c == 0)
    def _():
        hn_ref[...] = _rms(x_ref[...], gpre_ref[...]).astype(BF16)
        o_ref[...] = jnp.zeros(o_ref.shape, F32)

    slot = i % 2

    @pl.when(i % blocks_per_seq == 0)
    def _():
        ha_ref[1 - slot, c] = jnp.zeros(ha_ref.shape[2:], F32)
        hv_ref[1 - slot, c] = jnp.zeros(hv_ref.shape[2:], F32)

    ua_ref[0:hdr, :] = ha_ref[1 - slot, c]
    uv_ref[0:hdr, :] = hv_ref[1 - slot, c]
    wa = wua_ref[...].astype(BF16)
    wv = wuv_ref[...].astype(BF16)
    wd = wd_ref[...].astype(BF16)

    def conv(buf_ref, w_ref, b_ref, r0):
        return (b_ref[...] + w_ref[0:1, :] * buf_ref[r0 + hdr - 2:r0 + hdr - 2 + rc, :]
                + w_ref[1:2, :] * buf_ref[r0 + hdr - 1:r0 + hdr - 1 + rc, :]
                + w_ref[2:3, :] * buf_ref[r0 + hdr:r0 + hdr + rc, :])

    hn = hn_ref[...]
    ua_ref[hdr:hdr + tm, :] = jnp.dot(hn, wa, preferred_element_type=F32)
    uv_ref[hdr:hdr + tm, :] = jnp.dot(hn, wv, preferred_element_type=F32)
    for k in range(tm // rm):
        rows = slice(k * rm, (k + 1) * rm)
        for r0 in range(k * rm, (k + 1) * rm, rc):
            ca = conv(ua_ref, cwa_ref, cba_ref, r0)
            cv = conv(uv_ref, cwv_ref, cbv_ref, r0)
            act_ref[r0:r0 + rc, :] = (jax.nn.silu(ca) * cv).astype(BF16)
        o_ref[rows, :] += jnp.dot(act_ref[rows, :], wd, preferred_element_type=F32)

    ta = ua_ref[tm:tm + hdr, :]
    tv = uv_ref[tm:tm + hdr, :]
    ha_ref[slot, c] = ta
    hv_ref[slot, c] = tv
    sa_ref[c] = ta[hdr - 2:hdr, :]
    sv_ref[c] = tv[hdr - 2:hdr, :]

    @pl.when(c == nc - 1)
    def _():
        o_ref[...] = x_ref[...] + _rms(o_ref[...], gpost_ref[...])


FFN_ROW_CHUNK = 64
FFN_ROW_GROUP = 512
FFN_VMEM_LIMIT = V7X_VMEM_BYTES - 4 * 1024 * 1024


def ffn_prompt(x, layer, g_pre, w_up, w_layer, conv_w, conv_b, w_down, g_post, batch, tm, tf):
    M, D = x.shape
    nc = D_FF // tf
    bps = SEQ // tm
    conv_b3 = conv_b.reshape(DEPTH, 1, 2 * D_FF)
    y, sa, sv = pl.pallas_call(
        functools.partial(_ffn_prompt_kernel, nc=nc, blocks_per_seq=bps, rc=FFN_ROW_CHUNK,
                          rm=FFN_ROW_GROUP),
        grid=(M // tm, nc),
        in_specs=[
            pl.BlockSpec((tm, D), lambda i, c: (i, 0)),
            pl.BlockSpec((None, 1, D), lambda i, c: (layer, 0, 0)),
            pl.BlockSpec((None, D, tf), lambda i, c: (w_layer, 0, c)),
            pl.BlockSpec((None, D, tf), lambda i, c: (w_layer, 0, nc + c)),
            pl.BlockSpec((None, 3, tf), lambda i, c: (layer, 0, c)),
            pl.BlockSpec((None, 3, tf), lambda i, c: (layer, 0, nc + c)),
            pl.BlockSpec((None, 1, tf), lambda i, c: (layer, 0, c)),
            pl.BlockSpec((None, 1, tf), lambda i, c: (layer, 0, nc + c)),
            pl.BlockSpec((None, tf, D), lambda i, c: (w_layer, c, 0)),
            pl.BlockSpec((None, 1, D), lambda i, c: (layer, 0, 0)),
        ],
        out_specs=[
            _resident((tm, D), lambda i, c: (i, 0)),
            pl.BlockSpec((None, nc, 2, tf), lambda i, c: (i // bps, 0, 0, 0)),
            pl.BlockSpec((None, nc, 2, tf), lambda i, c: (i // bps, 0, 0, 0)),
        ],
        out_shape=[
            jax.ShapeDtypeStruct((M, D), F32),
            jax.ShapeDtypeStruct((batch, nc, 2, tf), F32),
            jax.ShapeDtypeStruct((batch, nc, 2, tf), F32),
        ],
        scratch_shapes=[
            pltpu.VMEM((tm, D), BF16),
            pltpu.VMEM((tm + SUBLANES, tf), F32),
            pltpu.VMEM((tm + SUBLANES, tf), F32),
            pltpu.VMEM((tm, tf), BF16),
            pltpu.VMEM((2, nc, SUBLANES, tf), F32),
            pltpu.VMEM((2, nc, SUBLANES, tf), F32),
        ],
        compiler_params=_params("arbitrary", "arbitrary", vmem=FFN_VMEM_LIMIT),
        name="ffn_prompt",
    )(x, g_pre.reshape(DEPTH, 1, D), w_up, w_up, conv_w, conv_w, conv_b3, conv_b3,
      w_down, g_post.reshape(DEPTH, 1, D))
    sa = sa.transpose(0, 2, 1, 3).reshape(batch, 2, D_FF)
    sv = sv.transpose(0, 2, 1, 3).reshape(batch, 2, D_FF)
    return y, jnp.concatenate([sa, sv], axis=-1)


def _ffn_sample_kernel(x_ref, gpre_ref, wua_ref, wuv_ref, cwa_ref, cwv_ref, cba_ref, cbv_ref,
                       wd_ref, gpost_ref, sta_ref, stv_ref, o_ref, na_ref, nv_ref, hn_ref, *, nc):
    c = pl.program_id(0)
    L = DEC_SEQ

    @pl.when(c == 0)
    def _():
        hn_ref[...] = _rms(x_ref[...], gpre_ref[...]).astype(BF16)

    hn = hn_ref[...]
    M = hn.shape[0]
    nb = M // L
    ua = jnp.dot(hn, wua_ref[...], preferred_element_type=F32)
    uv = jnp.dot(hn, wuv_ref[...], preferred_element_type=F32)
    tf = ua.shape[1]
    t = lax.broadcasted_iota(jnp.int32, (nb, L, tf), 1)

    def conv(u, st_ref, w_ref, b_ref):
        u3 = u.reshape(nb, L, tf)
        st = st_ref[...]
        prev2, prev1 = st[:, 0:1, :], st[:, 1:2, :]
        p1 = jnp.where(t == 0, prev1, pltpu.roll(u3, 1, 1))
        p2 = jnp.where(t == 0, prev2, jnp.where(t == 1, prev1, pltpu.roll(u3, 2, 1)))
        return (b_ref[...] + w_ref[0:1, :] * p2 + w_ref[1:2, :] * p1 + w_ref[2:3, :] * u3,
                u3[:, L - 2:L, :])

    ca, na = conv(ua, sta_ref, cwa_ref, cba_ref)
    cv, nv = conv(uv, stv_ref, cwv_ref, cbv_ref)
    na_ref[...] = na
    nv_ref[...] = nv
    act = (jax.nn.silu(ca) * cv).astype(BF16).reshape(M, tf)
    part = jnp.dot(act, wd_ref[...], preferred_element_type=F32)

    @pl.when(c == 0)
    def _():
        o_ref[...] = part

    @pl.when(c > 0)
    def _():
        o_ref[...] += part

    @pl.when(c == nc - 1)
    def _():
        o_ref[...] = x_ref[...] + _rms(o_ref[...], gpost_ref[...])


def ffn_sample(x, layer, g_pre, w_up, w_layer, conv_w, conv_b, w_down, g_post, conv_state, tf):
    M, D = x.shape
    nbatch = M // DEC_SEQ
    nc = D_FF // tf
    conv_b3 = conv_b.reshape(DEPTH, 1, 2 * D_FF)
    y, na, nv = pl.pallas_call(
        functools.partial(_ffn_sample_kernel, nc=nc),
        grid=(nc,),
        in_specs=[
            pl.BlockSpec((M, D), lambda c: (0, 0)),
            pl.BlockSpec((None, 1, D), lambda c: (layer, 0, 0)),
            pl.BlockSpec((None, D, tf), lambda c: (w_layer, 0, c)),
            pl.BlockSpec((None, D, tf), lambda c: (w_layer, 0, nc + c)),
            pl.BlockSpec((None, 3, tf), lambda c: (layer, 0, c)),
            pl.BlockSpec((None, 3, tf), lambda c: (layer, 0, nc + c)),
            pl.BlockSpec((None, 1, tf), lambda c: (layer, 0, c)),
            pl.BlockSpec((None, 1, tf), lambda c: (layer, 0, nc + c)),
            pl.BlockSpec((None, tf, D), lambda c: (w_layer, c, 0)),
            pl.BlockSpec((None, 1, D), lambda c: (layer, 0, 0)),
            pl.BlockSpec((None, nbatch, 2, tf), lambda c: (layer, 0, 0, c)),
            pl.BlockSpec((None, nbatch, 2, tf), lambda c: (layer, 0, 0, nc + c)),
        ],
        out_specs=[
            pl.BlockSpec((M, D), lambda c: (0, 0)),
            pl.BlockSpec((nbatch, 2, tf), lambda c: (0, 0, c)),
            pl.BlockSpec((nbatch, 2, tf), lambda c: (0, 0, c)),
        ],
        out_shape=[
            jax.ShapeDtypeStruct((M, D), F32),
            jax.ShapeDtypeStruct((nbatch, 2, D_FF), F32),
            jax.ShapeDtypeStruct((nbatch, 2, D_FF), F32),
        ],
        scratch_shapes=[pltpu.VMEM((M, D), BF16)],
        compiler_params=_params("arbitrary"),
        name="ffn_sample",
    )(x, g_pre.reshape(DEPTH, 1, D), w_up, w_up, conv_w, conv_w, conv_b3, conv_b3,
      w_down, g_post.reshape(DEPTH, 1, D), conv_state, conv_state)
    return y, jnp.concatenate([na, nv], axis=-1)


def kernel(x_prompt, x_sample, mem_prompt, cache_swa_k, cache_swa_v, state_ssm_re, state_ssm_im, state_ffn_conv, cache_mem_k, cache_mem_v, g_mix_pre, g_mix_post, w_qkv, w_attn_o, attn_sinks, w_ssm_in, ssm_lambda_re, ssm_lambda_im, ssm_log_step, ssm_b_re, ssm_b_im, ssm_c_re, ssm_c_im, ssm_d, w_ssm_glu, g_x_pre, g_x_post, g_mem, w_x_q, w_mem_k, w_mem_v, w_x_o, g_ffn_pre, g_ffn_post, w_ffn_up, ffn_conv_w, ffn_conv_b, w_ffn_down):
    B = x_prompt.shape[0]
    SB = x_sample.shape[0]
    xw = N_XHEADS * XHEAD_DIM
    nkc = N_KV_HEADS * HEAD_DIM
    xp = x_prompt.reshape(B * SEQ, D_MODEL)
    xs = x_sample.reshape(SB * DEC_SEQ, D_MODEL)
    mem = mem_prompt.reshape(B * N_MEM, D_MODEL)
    MS = SB * DEC_SEQ
    TMP = 1024
    cos_p, sin_p = _rope_tables(jnp.arange(SEQ))
    cos_s, sin_s = _rope_tables(PAST_LEN + jnp.arange(DEC_SEQ))
    ck_all = cache_swa_k.reshape(cache_swa_k.shape[0], SB, WINDOW, nkc)
    cv_all = cache_swa_v.reshape(cache_swa_v.shape[0], SB, WINDOW, nkc)
    cmk_all = cache_mem_k.reshape(DEPTH, SB, N_MEM, xw)
    cmv_all = cache_mem_v.reshape(DEPTH, SB, N_MEM, xw)
    w_qkv_bf = w_qkv.astype(BF16)
    w_attn_o_bf = w_attn_o.astype(BF16)
    w_ssm_in_bf = w_ssm_in.astype(BF16)
    w_glu_bf = w_ssm_glu.astype(BF16)
    w_x_q_bf = w_x_q.astype(BF16)
    w_x_o_bf = w_x_o.astype(BF16)
    w_mem_kv_bf = jnp.concatenate([w_mem_k, w_mem_v], axis=-1).astype(BF16)
    TD = 512

    swa_kp, swa_vp, swa_ks, swa_vs = [], [], [], []
    ssm_rp, ssm_ip, ssm_rs, ssm_is = [], [], [], []
    conv_p, conv_s, memk_p, memv_p = [], [], [], []
    for i in range(DEPTH):
        j = i // 2
        if i % 2 == 0:
            qkv_p = norm_matmul(xp, g_mix_pre, i, w_qkv_bf, j, TD)
            qkv_s = norm_matmul(xs, g_mix_pre, i, w_qkv_bf, j, MS)
            op, kp, vp, w_up_bf, w_down_bf = swa_prompt(qkv_p, attn_sinks, j, cos_p, sin_p, B,
                                                        w_ffn_up, w_ffn_down)
            sink_rows = jnp.repeat(attn_sinks[j].reshape(N_KV_HEADS, GQA_GROUP), DEC_SEQ,
                                   axis=1)[..., None]
            os_, kn, vn = swa_sample(qkv_s, ck_all, cv_all, j, sink_rows, cos_s, sin_s)
            swa_kp.append(kp.reshape(B, WINDOW, N_KV_HEADS, HEAD_DIM))
            swa_vp.append(vp.reshape(B, WINDOW, N_KV_HEADS, HEAD_DIM))
            swa_ks.append(kn.reshape(SB, WINDOW, N_KV_HEADS, HEAD_DIM))
            swa_vs.append(vn.reshape(SB, WINDOW, N_KV_HEADS, HEAD_DIM))
            xp = matmul_post(op, w_attn_o_bf, j, g_mix_post, i, xp, TD)
            xs = matmul_post(os_, w_attn_o_bf, j, g_mix_post, i, xs, MS)
        else:
            a_re, a_im, b_blk, c_blk = _s5_weights(
                ssm_lambda_re[j], ssm_lambda_im[j], ssm_log_step[j], ssm_b_re[j], ssm_b_im[j],
                ssm_c_re[j], ssm_c_im[j])
            d_row = ssm_d[j].reshape(1, D_MODEL)
            up_ = norm_matmul(xp, g_mix_pre, i, w_ssm_in_bf, j, TD)
            us_ = norm_matmul(xs, g_mix_pre, i, w_ssm_in_bf, j, MS)
            a_tiles = jnp.stack([a_re.reshape(N_SUPER, SUBLANES, LANES),
                                 a_im.reshape(N_SUPER, SUBLANES, LANES)], axis=1
                                ).reshape(2 * N_SUPER, SUBLANES, LANES)
            zero_state = jnp.zeros((B, N_SUPER, SUBLANES, LANES), F32)
            zp, rp, ip = s5_prompt(up_, zero_state, zero_state, a_tiles, b_blk, c_blk, d_row,
                                   B, 256)
            nstate = N_SSM_GROUPS * SSM_STATE
            zs, rn, im_ = s5_sample(us_, state_ssm_re[j].reshape(SB, nstate),
                                    state_ssm_im[j].reshape(SB, nstate),
                                    a_re.reshape(1, nstate), a_im.reshape(1, nstate),
                                    b_blk, c_blk, d_row)
            ssm_rp.append(rp.reshape(B, N_SSM_GROUPS, SSM_STATE))
            ssm_ip.append(ip.reshape(B, N_SSM_GROUPS, SSM_STATE))
            ssm_rs.append(rn.reshape(SB, N_SSM_GROUPS, SSM_STATE))
            ssm_is.append(im_.reshape(SB, N_SSM_GROUPS, SSM_STATE))
            xp = glu_post(zp, w_glu_bf, j, g_mix_post, i, xp, TD)
            xs = glu_post(zs, w_glu_bf, j, g_mix_post, i, xs, MS)
        mk, mv, mk_heads, mv_heads = mem_kv(mem, g_mem, w_mem_kv_bf, i)
        memk_p.append(mk_heads.reshape(B, N_MEM, N_XHEADS, XHEAD_DIM))
        memv_p.append(mv_heads.reshape(B, N_MEM, N_XHEADS, XHEAD_DIM))
        xp = xattn_prompt(xp, i, g_x_pre, w_x_q_bf, mk, mv, w_x_o_bf, g_x_post, B, 2 * TD)
        qs = norm_matmul(xs, g_x_pre, i, w_x_q_bf, i, MS)
        as_ = xattn_sample(qs, cmk_all, cmv_all, i)
        xs = matmul_post(as_, w_x_o_bf, i, g_x_post, i, xs, MS)
        xp, cp = ffn_prompt(xp, i, g_ffn_pre, w_up_bf, i % 2, ffn_conv_w, ffn_conv_b, w_down_bf,
                            g_ffn_post, B, TMP, 512)
        xs, cs = ffn_sample(xs, i, g_ffn_pre, w_up_bf, i % 2, ffn_conv_w, ffn_conv_b, w_down_bf,
                            g_ffn_post, state_ffn_conv, 1408)
        conv_p.append(cp)
        conv_s.append(cs)
    return (xp.reshape(B, SEQ, D_MODEL), xs.reshape(SB, DEC_SEQ, D_MODEL),
            jnp.stack(swa_kp), jnp.stack(swa_vp), jnp.stack(swa_ks), jnp.stack(swa_vs),
            jnp.stack(ssm_rp), jnp.stack(ssm_ip), jnp.stack(ssm_rs), jnp.stack(ssm_is),
            jnp.stack(conv_p), jnp.stack(conv_s), jnp.stack(memk_p), jnp.stack(memv_p))
```

```python
import functools

import jax
import jax.numpy as jnp
from jax import lax
from jax.experimental import pallas as pl
from jax.experimental.pallas import tpu as pltpu

D_MODEL = 2048
SEQ = 4096
DEPTH = 4
DEC_SEQ = 8
PAST_LEN = 16384
HEAD_DIM = 64
N_HEADS = 32
N_KV_HEADS = 4
GQA_GROUP = 8
WINDOW = 128
ROPE_THETA = 10000.0
SSM_GROUP = 16
N_SSM_GROUPS = 128
SSM_STATE = 64
N_MEM = 256
N_XHEADS = 4
XHEAD_DIM = 128
D_FF = 5632
NORM_EPS = 1e-6
NEG_INF = -1e30
LOG2E = 1.4426950408889634

F32 = jnp.float32
BF16 = jnp.bfloat16

V7X_VMEM_BYTES = 64 * 1024 * 1024
VMEM_LIMIT = V7X_VMEM_BYTES - 8 * 1024 * 1024
LANES = 128
SUBLANES = 8
SSM_SUPER = 16
N_SUPER = N_SSM_GROUPS // SSM_SUPER
SUPER_IN = SSM_SUPER * SSM_GROUP
SUPER_STATE = SSM_SUPER * SSM_STATE
SLABS = SUPER_STATE // LANES


def _params(*sem, vmem=VMEM_LIMIT):
    return pltpu.CompilerParams(dimension_semantics=sem, vmem_limit_bytes=vmem)


def _rms(x, g):
    ms = jnp.mean(x * x, axis=-1, keepdims=True)
    return x * lax.rsqrt(ms + NORM_EPS) * g


def _bdot(a, b):
    return jnp.dot(a.astype(BF16), b.astype(BF16), preferred_element_type=F32)


def _resident(shape, index_map):
    return pl.BlockSpec(shape, index_map, pipeline_mode=pl.Buffered(1))


ROW_GROUP = 256


def _row_groups(tm):
    rg = min(ROW_GROUP, tm)
    return [slice(k * rg, (k + 1) * rg) for k in range(tm // rg)]


def _norm_matmul_kernel(x_ref, g_ref, w_ref, o_ref):
    g = g_ref[...]
    for rows in _row_groups(x_ref.shape[0]):
        hn = _rms(x_ref[rows, :], g).astype(BF16)
        o_ref[rows, :] = jnp.dot(hn, w_ref[...], preferred_element_type=F32).astype(o_ref.dtype)


def norm_matmul(x, g, g_layer, w, w_layer, tm, out_dtype=F32):
    M, D = x.shape
    N = w.shape[-1]
    return pl.pallas_call(
        _norm_matmul_kernel,
        grid=(M // tm,),
        in_specs=[
            pl.BlockSpec((tm, D), lambda i: (i, 0)),
            pl.BlockSpec((None, 1, D), lambda i: (g_layer, 0, 0)),
            _resident((None, D, N), lambda i: (w_layer, 0, 0)),
        ],
        out_specs=pl.BlockSpec((tm, N), lambda i: (i, 0)),
        out_shape=jax.ShapeDtypeStruct((M, N), out_dtype),
        compiler_params=_params("arbitrary"),
        name="norm_matmul",
    )(x, g.reshape(g.shape[0], 1, D), w)


def _mem_kv_kernel(x_ref, g_ref, w_ref, k_ref, v_ref, kh_ref, vh_ref):
    xw = N_XHEADS * XHEAD_DIM
    hn = _rms(x_ref[...], g_ref[...]).astype(BF16)
    y = jnp.dot(hn, w_ref[...], preferred_element_type=F32)
    k_ref[...] = y[:, :xw]
    v_ref[...] = y[:, xw:]
    for h in range(N_XHEADS):
        kh_ref[:, h, :] = y[:, h * XHEAD_DIM:(h + 1) * XHEAD_DIM]
        vh_ref[:, h, :] = y[:, xw + h * XHEAD_DIM:xw + (h + 1) * XHEAD_DIM]


def mem_kv(mem, g, w, layer):
    M, D = mem.shape
    xw = N_XHEADS * XHEAD_DIM
    flat = jax.ShapeDtypeStruct((M, xw), F32)
    heads = jax.ShapeDtypeStruct((M, N_XHEADS, XHEAD_DIM), F32)
    return pl.pallas_call(
        _mem_kv_kernel,
        grid=(1,),
        in_specs=[
            pl.BlockSpec((M, D), lambda i: (0, 0)),
            pl.BlockSpec((None, 1, D), lambda i: (layer, 0, 0)),
            pl.BlockSpec((None, D, 2 * xw), lambda i: (layer, 0, 0)),
        ],
        out_specs=[
            pl.BlockSpec((M, xw), lambda i: (0, 0)),
            pl.BlockSpec((M, xw), lambda i: (0, 0)),
            pl.BlockSpec((M, N_XHEADS, XHEAD_DIM), lambda i: (0, 0, 0)),
            pl.BlockSpec((M, N_XHEADS, XHEAD_DIM), lambda i: (0, 0, 0)),
        ],
        out_shape=[flat, flat, heads, heads],
        compiler_params=_params("arbitrary"),
        name="mem_kv",
    )(mem, g.reshape(g.shape[0], 1, D), w)


def _matmul_post_kernel(a_ref, w_ref, g_ref, x_ref, o_ref):
    g = g_ref[...]
    for rows in _row_groups(x_ref.shape[0]):
        y = jnp.dot(a_ref[rows, :].astype(BF16), w_ref[...], preferred_element_type=F32)
        o_ref[rows, :] = x_ref[rows, :] + _rms(y, g)


def matmul_post(a, w, w_layer, g, g_layer, x, tm):
    M, K = a.shape
    D = w.shape[-1]
    return pl.pallas_call(
        _matmul_post_kernel,
        grid=(M // tm,),
        in_specs=[
            pl.BlockSpec((tm, K), lambda i: (i, 0)),
            _resident((None, K, D), lambda i: (w_layer, 0, 0)),
            pl.BlockSpec((None, 1, D), lambda i: (g_layer, 0, 0)),
            pl.BlockSpec((tm, D), lambda i: (i, 0)),
        ],
        out_specs=pl.BlockSpec((tm, D), lambda i: (i, 0)),
        out_shape=jax.ShapeDtypeStruct((M, D), F32),
        compiler_params=_params("arbitrary"),
        name="matmul_post",
    )(a, w, g.reshape(g.shape[0], 1, D), x)


GLU_COLS = 512


def _glu_post_kernel(z_ref, w_ref, g_ref, x_ref, o_ref, y_ref):
    D = o_ref.shape[1]
    g = g_ref[...]
    for rows in _row_groups(x_ref.shape[0]):
        z = z_ref[rows, :]
        for n in range(D // GLU_COLS):
            cols = slice(n * GLU_COLS, (n + 1) * GLU_COLS)
            gcols = slice(D + n * GLU_COLS, D + (n + 1) * GLU_COLS)
            val = jnp.dot(z, w_ref[:, cols], preferred_element_type=F32)
            gate = jnp.dot(z, w_ref[:, gcols], preferred_element_type=F32)
            y_ref[rows, cols] = val * jax.nn.sigmoid(gate)
        o_ref[rows, :] = x_ref[rows, :] + _rms(y_ref[rows, :], g)


def glu_post(z, w, w_layer, g, g_layer, x, tm):
    M, K = z.shape
    D = D_MODEL
    return pl.pallas_call(
        _glu_post_kernel,
        grid=(M // tm,),
        in_specs=[
            pl.BlockSpec((tm, K), lambda i: (i, 0)),
            _resident((None, K, 2 * D), lambda i: (w_layer, 0, 0)),
            pl.BlockSpec((None, 1, D), lambda i: (g_layer, 0, 0)),
            pl.BlockSpec((tm, D), lambda i: (i, 0)),
        ],
        out_specs=pl.BlockSpec((tm, D), lambda i: (i, 0)),
        out_shape=jax.ShapeDtypeStruct((M, D), F32),
        scratch_shapes=[pltpu.VMEM((tm, D), F32)],
        compiler_params=_params("arbitrary"),
        name="glu_post",
    )(z, w, g.reshape(g.shape[0], 1, D), x)


def _rope_chunk(xc, cos, sin_signed):
    lane = lax.broadcasted_iota(jnp.int32, xc.shape, 1)
    first_half = (lane & (HEAD_DIM - 1)) < (HEAD_DIM // 2)
    partner = jnp.where(first_half, pltpu.roll(xc, LANES - HEAD_DIM // 2, 1),
                        pltpu.roll(xc, HEAD_DIM // 2, 1))
    return xc * cos + partner * sin_signed


def _rope_tables(pos):
    half = HEAD_DIM // 2
    inv_freq = ROPE_THETA ** (-jnp.arange(half, dtype=F32) * 2.0 / HEAD_DIM)
    ang = pos.astype(F32)[:, None] * inv_freq[None, :]
    cos, sin = jnp.cos(ang), jnp.sin(ang)
    cos128 = jnp.tile(cos, (1, LANES // half))
    sin128 = jnp.tile(jnp.concatenate([-sin, sin], axis=1), (1, LANES // HEAD_DIM))
    return cos128, sin128


def _swa_prompt_kernel(sink_ref, q_ref, kv_ref, cos_ref, sin_ref, wu_ref, wd_ref,
                       o_ref, kc_ref, vc_ref, wub_ref, wdb_ref, kk_ref, vv_ref, *, layer):
    i = pl.program_id(1)
    blk = WINDOW
    wub_ref[...] = wu_ref[...].astype(BF16)
    wdb_ref[...] = wd_ref[...].astype(BF16)
    cos, sn = cos_ref[...], sin_ref[...]
    kv = kv_ref[...]
    nkc = N_KV_HEADS * HEAD_DIM
    k = jnp.concatenate([_rope_chunk(kv[:, c * LANES:(c + 1) * LANES], cos, sn)
                         for c in range(nkc // LANES)], axis=1)
    v = kv[:, nkc:2 * nkc]
    kc_ref[...] = k
    vc_ref[...] = v

    cur = i % 2
    prv = 1 - cur

    @pl.when(i == 0)
    def _():
        kk_ref[1] = jnp.zeros((blk, nkc), BF16)
        vv_ref[1] = jnp.zeros((blk, nkc), BF16)

    kb, vb = k.astype(BF16), v.astype(BF16)
    kk_ref[cur] = kb
    vv_ref[cur] = vb
    kk = jnp.concatenate([kk_ref[prv], kb], axis=0)
    vv = jnp.concatenate([vv_ref[prv], vb], axis=0)
    qi = lax.broadcasted_iota(jnp.int32, (blk, 2 * blk), 0)
    si = lax.broadcasted_iota(jnp.int32, (blk, 2 * blk), 1)
    d = blk + qi - si
    mask = (d >= 0) & (d < WINDOW) & ((si >= blk) | (i > 0))
    bias = jnp.where(mask, 0.0, NEG_INF)[None]
    ones = jnp.ones((2 * blk, HEAD_DIM), BF16)

    chunks_per_group = GQA_GROUP * HEAD_DIM // LANES

    def scores(kh):
        hs = slice(kh * HEAD_DIM, (kh + 1) * HEAD_DIM)
        pieces = []
        for c in range(kh * chunks_per_group, (kh + 1) * chunks_per_group):
            qc = (_rope_chunk(q_ref[:, c * LANES:(c + 1) * LANES], cos, sn)
                  * (HEAD_DIM ** -0.5 * LOG2E)).astype(BF16)
            pieces += [qc[:, hh * HEAD_DIM:(hh + 1) * HEAD_DIM] for hh in range(LANES // HEAD_DIM)]
        qs = jnp.concatenate(pieces, axis=0)
        return lax.dot_general(qs, kk[:, hs], (((1,), (1,)), ((), ())),
                               preferred_element_type=F32)

    ahead = 3
    s_all = [scores(kh) for kh in range(min(ahead, N_KV_HEADS))]
    outs = []
    for kh in range(N_KV_HEADS):
        if kh + ahead < N_KV_HEADS:
            s_all.append(scores(kh + ahead))
        hs = slice(kh * HEAD_DIM, (kh + 1) * HEAD_DIM)
        sk = jnp.concatenate([jnp.full((1, blk, 1), sink_ref[layer, kh * GQA_GROUP + j] * LOG2E, F32)
                              for j in range(GQA_GROUP)], axis=0)
        s = s_all[kh].reshape(GQA_GROUP, blk, 2 * blk) + bias
        mx = jnp.maximum(jnp.max(s, axis=2, keepdims=True), sk)
        p = jnp.exp2(s - mx).astype(BF16).reshape(GQA_GROUP * blk, 2 * blk)
        pv = jnp.dot(p, jnp.concatenate([vv[:, hs], ones], axis=1), preferred_element_type=F32)
        den = pv[:, HEAD_DIM:] + jnp.exp2(sk - mx).reshape(GQA_GROUP * blk, 1)
        o = pv[:, :HEAD_DIM] / den
        outs += [o[j * blk:(j + 1) * blk, :] for j in range(GQA_GROUP)]
    o_ref[...] = jnp.concatenate(outs, axis=1).astype(o_ref.dtype)


def swa_prompt(qkv, sinks, layer, cos, sin, batch, w_up, w_down):
    blk = WINDOW
    nb = SEQ // blk
    nq = N_HEADS * HEAD_DIM
    nkv = 2 * N_KV_HEADS * HEAD_DIM
    per_layer = batch * nb // 2
    ur, dr = D_MODEL // per_layer, D_FF // per_layer

    def w_in(b, i):
        n = b * nb + i
        return (2 * layer + n // per_layer, n % per_layer, 0)

    def w_out(b, i):
        n = b * nb + i
        return (n // per_layer, n % per_layer, 0)

    o, kc, vc, w_up_bf, w_down_bf = pl.pallas_call(
        functools.partial(_swa_prompt_kernel, layer=layer),
        grid=(batch, nb),
        in_specs=[
            pl.BlockSpec(memory_space=pltpu.SMEM),
            pl.BlockSpec((blk, nq), lambda b, i: (b * nb + i, 0)),
            pl.BlockSpec((blk, nkv), lambda b, i: (b * nb + i, nq // nkv)),
            pl.BlockSpec((blk, LANES), lambda b, i: (i, 0)),
            pl.BlockSpec((blk, LANES), lambda b, i: (i, 0)),
            pl.BlockSpec((None, ur, 2 * D_FF), w_in),
            pl.BlockSpec((None, dr, D_MODEL), w_in),
        ],
        out_specs=[
            pl.BlockSpec((blk, nq), lambda b, i: (b * nb + i, 0)),
            pl.BlockSpec((None, blk, nkv // 2), lambda b, i: (b, 0, 0)),
            pl.BlockSpec((None, blk, nkv // 2), lambda b, i: (b, 0, 0)),
            pl.BlockSpec((None, ur, 2 * D_FF), w_out),
            pl.BlockSpec((None, dr, D_MODEL), w_out),
        ],
        out_shape=[
            jax.ShapeDtypeStruct((batch * SEQ, nq), BF16),
            jax.ShapeDtypeStruct((batch, blk, nkv // 2), F32),
            jax.ShapeDtypeStruct((batch, blk, nkv // 2), F32),
            jax.ShapeDtypeStruct((2, D_MODEL, 2 * D_FF), BF16),
            jax.ShapeDtypeStruct((2, D_FF, D_MODEL), BF16),
        ],
        scratch_shapes=[pltpu.VMEM((2, blk, nkv // 2), BF16),
                        pltpu.VMEM((2, blk, nkv // 2), BF16)],
        compiler_params=_params("arbitrary", "arbitrary"),
        name="swa_prompt",
    )(sinks, qkv, qkv, cos, sin, w_up, w_down)
    return o, kc, vc, w_up_bf, w_down_bf


def _swa_sample_kernel(sink_ref, qkv_ref, ck_ref, cv_ref, cos_ref, sin_ref,
                       o_ref, nk_ref, nv_ref, *, nb):
    L = DEC_SEQ
    wb = WINDOW
    nq = N_HEADS * HEAD_DIM
    nkc = N_KV_HEADS * HEAD_DIM
    cos, sn = cos_ref[...], sin_ref[...]
    rows = GQA_GROUP * L
    t_q = lax.broadcasted_iota(jnp.int32, (rows, wb), 0) % L
    c_k = lax.broadcasted_iota(jnp.int32, (rows, wb), 1)
    mask_c = c_k >= t_q + 1
    t_q2 = lax.broadcasted_iota(jnp.int32, (rows, L), 0) % L
    t_k2 = lax.broadcasted_iota(jnp.int32, (rows, L), 1)
    mask_n = t_k2 <= t_q2
    for b in range(nb):
        x = qkv_ref[b * L:(b + 1) * L, :]
        qr = jnp.concatenate([_rope_chunk(x[:, c * LANES:(c + 1) * LANES], cos, sn)
                              for c in range(nq // LANES)], axis=1) * (HEAD_DIM ** -0.5)
        kn = jnp.concatenate([_rope_chunk(x[:, nq + c * LANES:nq + (c + 1) * LANES], cos, sn)
                              for c in range(nkc // LANES)], axis=1)
        vn = x[:, nq + nkc:nq + 2 * nkc]
        ck = ck_ref[b]
        cv = cv_ref[b]
        nk_ref[b, 0:wb - L, :] = ck[L:wb, :]
        nk_ref[b, wb - L:wb, :] = kn
        nv_ref[b, 0:wb - L, :] = cv[L:wb, :]
        nv_ref[b, wb - L:wb, :] = vn
        ckb, cvb, knb, vnb = ck.astype(BF16), cv.astype(BF16), kn.astype(BF16), vn.astype(BF16)
        dn = (((1,), (1,)), ((), ()))
        scores = []
        for kh in range(N_KV_HEADS):
            hs = slice(kh * HEAD_DIM, (kh + 1) * HEAD_DIM)
            qs = jnp.concatenate(
                [qr[:, (kh * GQA_GROUP + j) * HEAD_DIM:(kh * GQA_GROUP + j + 1) * HEAD_DIM]
                 for j in range(GQA_GROUP)], axis=0).astype(BF16)
            scores.append((lax.dot_general(qs, ckb[:, hs], dn, preferred_element_type=F32),
                           lax.dot_general(qs, knb[:, hs], dn, preferred_element_type=F32)))
        outs = []
        for kh in range(N_KV_HEADS):
            hs = slice(kh * HEAD_DIM, (kh + 1) * HEAD_DIM)
            s_c = jnp.where(mask_c, scores[kh][0], NEG_INF)
            s_n = jnp.where(mask_n, scores[kh][1], NEG_INF)
            sk = sink_ref[kh]
            mx = jnp.maximum(jnp.maximum(jnp.max(s_c, axis=1, keepdims=True),
                                         jnp.max(s_n, axis=1, keepdims=True)), sk)
            p_c = jnp.exp(s_c - mx)
            p_n = jnp.exp(s_n - mx)
            den = (jnp.sum(p_c, axis=1, keepdims=True) + jnp.sum(p_n, axis=1, keepdims=True)
                   + jnp.exp(sk - mx))
            o = (jnp.dot(p_c.astype(BF16), cvb[:, hs], preferred_element_type=F32)
                 + jnp.dot(p_n.astype(BF16), vnb[:, hs], preferred_element_type=F32)) / den
            outs.extend(o[j * L:(j + 1) * L, :] for j in range(GQA_GROUP))
        o_ref[b * L:(b + 1) * L, :] = jnp.concatenate(outs, axis=1).astype(o_ref.dtype)


def swa_sample(qkv, cache_k, cache_v, layer, sink_rows, cos, sin, nb=8):
    nbatch = cache_k.shape[1]
    nq = N_HEADS * HEAD_DIM
    nkc = N_KV_HEADS * HEAD_DIM
    L = DEC_SEQ
    return pl.pallas_call(
        functools.partial(_swa_sample_kernel, nb=nb),
        grid=(nbatch // nb,),
        in_specs=[
            pl.BlockSpec((N_KV_HEADS, GQA_GROUP * L, 1), lambda g: (0, 0, 0)),
            pl.BlockSpec((nb * L, nq + 2 * nkc), lambda g: (g, 0)),
            pl.BlockSpec((None, nb, WINDOW, nkc), lambda g: (layer, g, 0, 0)),
            pl.BlockSpec((None, nb, WINDOW, nkc), lambda g: (layer, g, 0, 0)),
            pl.BlockSpec((L, LANES), lambda g: (0, 0)),
            pl.BlockSpec((L, LANES), lambda g: (0, 0)),
        ],
        out_specs=[
            pl.BlockSpec((nb * L, nq), lambda g: (g, 0)),
            pl.BlockSpec((nb, WINDOW, nkc), lambda g: (g, 0, 0)),
            pl.BlockSpec((nb, WINDOW, nkc), lambda g: (g, 0, 0)),
        ],
        out_shape=[
            jax.ShapeDtypeStruct((nbatch * L, nq), BF16),
            jax.ShapeDtypeStruct((nbatch, WINDOW, nkc), F32),
            jax.ShapeDtypeStruct((nbatch, WINDOW, nkc), F32),
        ],
        compiler_params=_params("arbitrary"),
        name="swa_sample",
    )(sink_rows, qkv, cache_k, cache_v, cos, sin)


_XHEADS = [slice(h * XHEAD_DIM, (h + 1) * XHEAD_DIM) for h in range(N_XHEADS)]


def _xattn_scores(q, mk):
    return [lax.dot_general((q[:, hs] * (XHEAD_DIM ** -0.5)).astype(BF16), mk[:, hs].astype(BF16),
                            (((1,), (1,)), ((), ())), preferred_element_type=F32)
            for hs in _XHEADS]


def _xattn_attend(scores, mv):
    outs = []
    for hs, s in zip(_XHEADS, scores):
        mx = jnp.max(s, axis=1, keepdims=True)
        p = jnp.exp(s - mx)
        den = jnp.sum(p, axis=1, keepdims=True)
        outs.append(jnp.dot(p.astype(BF16), mv[:, hs].astype(BF16),
                            preferred_element_type=F32) / den)
    return jnp.concatenate(outs, axis=1)


def _xattn_heads(q, mk, mv):
    return _xattn_attend(_xattn_scores(q, mk), mv)


def _xattn_prompt_kernel(x_ref, gpre_ref, wq_ref, mk_ref, mv_ref, wo_ref, gpost_ref, o_ref):
    mk = mk_ref[...].astype(BF16)
    mv = mv_ref[...].astype(BF16)
    gpre, gpost = gpre_ref[...], gpost_ref[...]
    groups = _row_groups(x_ref.shape[0])

    def project(rows):
        q = jnp.dot(_rms(x_ref[rows, :], gpre).astype(BF16), wq_ref[...],
                    preferred_element_type=F32)
        return _xattn_scores(q, mk)

    scores = project(groups[0])
    for k, rows in enumerate(groups):
        nxt = project(groups[k + 1]) if k + 1 < len(groups) else None
        a = _xattn_attend(scores, mv).astype(BF16)
        y = jnp.dot(a, wo_ref[...], preferred_element_type=F32)
        o_ref[rows, :] = x_ref[rows, :] + _rms(y, gpost)
        scores = nxt


def xattn_prompt(x, layer, g_pre, w_q, mk, mv, w_o, g_post, batch, tm):
    nq = SEQ // tm
    xw = N_XHEADS * XHEAD_DIM
    D = D_MODEL
    return pl.pallas_call(
        _xattn_prompt_kernel,
        grid=(batch, nq),
        in_specs=[
            pl.BlockSpec((tm, D), lambda b, i: (b * nq + i, 0)),
            pl.BlockSpec((None, 1, D), lambda b, i: (layer, 0, 0)),
            _resident((None, D, xw), lambda b, i: (layer, 0, 0)),
            pl.BlockSpec((N_MEM, xw), lambda b, i: (b, 0)),
            pl.BlockSpec((N_MEM, xw), lambda b, i: (b, 0)),
            _resident((None, xw, D), lambda b, i: (layer, 0, 0)),
            pl.BlockSpec((None, 1, D), lambda b, i: (layer, 0, 0)),
        ],
        out_specs=pl.BlockSpec((tm, D), lambda b, i: (b * nq + i, 0)),
        out_shape=jax.ShapeDtypeStruct((batch * SEQ, D), F32),
        compiler_params=_params("arbitrary", "arbitrary"),
        name="xattn_prompt",
    )(x, g_pre.reshape(DEPTH, 1, D), w_q, mk, mv, w_o, g_post.reshape(DEPTH, 1, D))


def _xattn_sample_kernel(q_ref, mk_ref, mv_ref, o_ref, *, nb):
    L = DEC_SEQ
    for b in range(nb):
        o_ref[b * L:(b + 1) * L, :] = _xattn_heads(
            q_ref[b * L:(b + 1) * L, :], mk_ref[b], mv_ref[b]).astype(o_ref.dtype)


def xattn_sample(q, cache_k, cache_v, layer, nb=8):
    nbatch = cache_k.shape[1]
    L = DEC_SEQ
    xw = N_XHEADS * XHEAD_DIM
    return pl.pallas_call(
        functools.partial(_xattn_sample_kernel, nb=nb),
        grid=(nbatch // nb,),
        in_specs=[
            pl.BlockSpec((nb * L, xw), lambda g: (g, 0)),
            pl.BlockSpec((None, nb, N_MEM, xw), lambda g: (layer, g, 0, 0)),
            pl.BlockSpec((None, nb, N_MEM, xw), lambda g: (layer, g, 0, 0)),
        ],
        out_specs=pl.BlockSpec((nb * L, xw), lambda g: (g, 0)),
        out_shape=jax.ShapeDtypeStruct((nbatch * L, xw), BF16),
        compiler_params=_params("arbitrary"),
        name="xattn_sample",
    )(q, cache_k, cache_v)


def _s5_prompt_kernel(u_ref, h0r_ref, h0i_ref, a_ref, b_ref, c_ref, d_ref,
                      z_ref, hr_ref, hi_ref, x_ref, hst_ref, *, T):
    c = pl.program_id(1)
    nslab = 2 * SLABS
    G = T // SUBLANES

    @pl.when(c == 0)
    def _():
        for sb in range(N_SUPER):
            hst_ref[2 * sb] = h0r_ref[sb]
            hst_ref[2 * sb + 1] = h0i_ref[sb]

    def slab_rows(sb, s):
        return pl.ds((sb * nslab + s) * S5_SLAB_PITCH, SUBLANES)

    for sb in range(N_SUPER):
        bu = _bdot(u_ref[:, sb * SUPER_IN:(sb + 1) * SUPER_IN], b_ref[sb])
        for s in range(nslab):
            x_ref[:, slab_rows(sb, s), :] = bu[:, s * LANES:(s + 1) * LANES].reshape(
                G, SUBLANES, LANES)

    a = [a_ref[k] for k in range(2 * N_SUPER)]
    h_init = tuple(hst_ref[k] for k in range(2 * N_SUPER))

    def step(t, h):
        g = lax.shift_right_logical(t, 3)
        r = lax.bitwise_and(t, SUBLANES - 1)
        new = []
        for sb in range(N_SUPER):
            ir = pl.ds((sb * nslab) * S5_SLAB_PITCH + r, SUBLANES, stride=S5_SLAB_PITCH)
            ii = pl.ds((sb * nslab + SLABS) * S5_SLAB_PITCH + r, SUBLANES, stride=S5_SLAB_PITCH)
            ar, ai = a[2 * sb], a[2 * sb + 1]
            hr, hi = h[2 * sb], h[2 * sb + 1]
            nr = ar * hr - ai * hi + x_ref[g, ir, :]
            ni = ar * hi + ai * hr + x_ref[g, ii, :]
            x_ref[g, ir, :] = nr
            x_ref[g, ii, :] = ni
            new += [nr, ni]
        return tuple(new)

    h_fin = lax.fori_loop(0, T, step, h_init)
    for sb in range(N_SUPER):
        hst_ref[2 * sb] = h_fin[2 * sb]
        hst_ref[2 * sb + 1] = h_fin[2 * sb + 1]
        hr_ref[sb] = h_fin[2 * sb]
        hi_ref[sb] = h_fin[2 * sb + 1]

    for sb in range(N_SUPER):
        hcat = jnp.concatenate([x_ref[:, slab_rows(sb, s), :].reshape(T, LANES)
                                for s in range(nslab)], axis=1)
        cols = slice(sb * SUPER_IN, (sb + 1) * SUPER_IN)
        y = _bdot(hcat, c_ref[sb]) + d_ref[:, cols] * u_ref[:, cols]
        z_ref[:, cols] = jax.nn.gelu(y).astype(z_ref.dtype)


S5_SLAB_PITCH = 12


def s5_prompt(u, h0r, h0i, a_tiles, b_blk, c_blk, d_row, batch, T):
    nc = SEQ // T
    st = (None, N_SUPER, SUBLANES, LANES)
    return pl.pallas_call(
        functools.partial(_s5_prompt_kernel, T=T),
        grid=(batch, nc),
        in_specs=[
            pl.BlockSpec((T, D_MODEL), lambda b, c: (b * nc + c, 0)),
            pl.BlockSpec(st, lambda b, c: (b, 0, 0, 0)),
            pl.BlockSpec(st, lambda b, c: (b, 0, 0, 0)),
            _resident((2 * N_SUPER, SUBLANES, LANES), lambda b, c: (0, 0, 0)),
            _resident((N_SUPER, SUPER_IN, 2 * SUPER_STATE), lambda b, c: (0, 0, 0)),
            _resident((N_SUPER, 2 * SUPER_STATE, SUPER_IN), lambda b, c: (0, 0, 0)),
            _resident((1, D_MODEL), lambda b, c: (0, 0)),
        ],
        out_specs=[
            pl.BlockSpec((T, D_MODEL), lambda b, c: (b * nc + c, 0)),
            pl.BlockSpec(st, lambda b, c: (b, 0, 0, 0)),
            pl.BlockSpec(st, lambda b, c: (b, 0, 0, 0)),
        ],
        out_shape=[
            jax.ShapeDtypeStruct((batch * SEQ, D_MODEL), BF16),
            jax.ShapeDtypeStruct((batch, N_SUPER, SUBLANES, LANES), F32),
            jax.ShapeDtypeStruct((batch, N_SUPER, SUBLANES, LANES), F32),
        ],
        scratch_shapes=[
            pltpu.VMEM((T // SUBLANES, N_SUPER * 2 * SLABS * S5_SLAB_PITCH, LANES), F32),
            pltpu.VMEM((2 * N_SUPER, SUBLANES, LANES), F32),
        ],
        compiler_params=_params("arbitrary", "arbitrary"),
        name="s5_prompt",
    )(u, h0r, h0i, a_tiles, b_blk, c_blk, d_row)


def _s5_sample_kernel(u_ref, h0r_ref, h0i_ref, ar_ref, ai_ref, b_ref, c_ref, d_ref,
                      z_ref, hr_ref, hi_ref, x_ref, *, nbatch):
    L = DEC_SEQ
    R = nbatch * L
    for sb in range(N_SUPER):
        cols = slice(sb * SUPER_IN, (sb + 1) * SUPER_IN)
        bu = _bdot(u_ref[:, cols], b_ref[sb])
        for s in range(2 * SLABS):
            x_ref[s * R:(s + 1) * R, :] = bu[:, s * LANES:(s + 1) * LANES]
        for s in range(SLABS):
            st = slice(sb * SUPER_STATE + s * LANES, sb * SUPER_STATE + (s + 1) * LANES)
            ar, ai = ar_ref[:, st], ai_ref[:, st]
            hr, hi = h0r_ref[:, st], h0i_ref[:, st]
            for t in range(L):
                rr = pl.ds(s * R + t, nbatch, stride=L)
                ri = pl.ds((SLABS + s) * R + t, nbatch, stride=L)
                nr = ar * hr - ai * hi + x_ref[rr, :]
                ni = ar * hi + ai * hr + x_ref[ri, :]
                x_ref[rr, :] = nr
                x_ref[ri, :] = ni
                hr, hi = nr, ni
            hr_ref[:, st] = hr
            hi_ref[:, st] = hi
        hcat = jnp.concatenate([x_ref[s * R:(s + 1) * R, :] for s in range(2 * SLABS)], axis=1)
        y = _bdot(hcat, c_ref[sb]) + d_ref[:, cols] * u_ref[:, cols]
        z_ref[:, cols] = jax.nn.gelu(y).astype(z_ref.dtype)


def s5_sample(u, h0r, h0i, a_re, a_im, b_blk, c_blk, d_row):
    rows = u.shape[0]
    nbatch = rows // DEC_SEQ
    nstate = N_SSM_GROUPS * SSM_STATE
    full = lambda shape: pl.BlockSpec(shape, lambda i: (0,) * len(shape))
    return pl.pallas_call(
        functools.partial(_s5_sample_kernel, nbatch=nbatch),
        grid=(1,),
        in_specs=[
            full((rows, D_MODEL)), full((nbatch, nstate)), full((nbatch, nstate)),
            full((1, nstate)), full((1, nstate)),
            full((N_SUPER, SUPER_IN, 2 * SUPER_STATE)),
            full((N_SUPER, 2 * SUPER_STATE, SUPER_IN)),
            full((1, D_MODEL)),
        ],
        out_specs=[full((rows, D_MODEL)), full((nbatch, nstate)), full((nbatch, nstate))],
        out_shape=[
            jax.ShapeDtypeStruct((rows, D_MODEL), BF16),
            jax.ShapeDtypeStruct((nbatch, nstate), F32),
            jax.ShapeDtypeStruct((nbatch, nstate), F32),
        ],
        scratch_shapes=[pltpu.VMEM((2 * SLABS * rows, LANES), F32)],
        compiler_params=_params("arbitrary"),
        name="s5_sample",
    )(u, h0r, h0i, a_re, a_im, b_blk, c_blk, d_row)


def _s5_weights(lam_re, lam_im, log_step, b_re, b_im, c_re, c_im):
    lam = lax.complex(lam_re.astype(F32), lam_im.astype(F32))
    dt = jnp.exp(log_step.astype(F32))[:, None]
    abar = jnp.exp(lam * dt)
    bbar = ((abar - 1.0) / lam)[..., None] * lax.complex(b_re.astype(F32), b_im.astype(F32))
    eye = jnp.eye(SSM_SUPER, dtype=F32)

    def b_layout(m):
        m = m.reshape(N_SUPER, SSM_SUPER, SSM_STATE, SSM_GROUP)
        return jnp.einsum('sgph,gk->sghkp', m, eye).reshape(N_SUPER, SUPER_IN, SUPER_STATE)

    def c_layout(m):
        m = m.reshape(N_SUPER, SSM_SUPER, SSM_GROUP, SSM_STATE)
        return jnp.einsum('sghp,gk->sgpkh', m, eye).reshape(N_SUPER, SUPER_STATE, SUPER_IN)

    b_blk = jnp.concatenate([b_layout(bbar.real), b_layout(bbar.imag)], axis=2).astype(BF16)
    c_blk = jnp.concatenate([c_layout(c_re.astype(F32)), -c_layout(c_im.astype(F32))],
                            axis=1).astype(BF16)
    return abar.real, abar.imag, b_blk, c_blk


def _ffn_prompt_kernel(x_ref, gpre_ref, wua_ref, wuv_ref, cwa_ref, cwv_ref, cba_ref, cbv_ref,
                       wd_ref, gpost_ref, o_ref, sa_ref, sv_ref,
                       hn_ref, ua_ref, uv_ref, act_ref, ha_ref, hv_ref,
                       *, nc, blocks_per_seq, rc, rm):
    i = pl.program_id(0)
    c = pl.program_id(1)
    tm, D = hn_ref.shape
    hdr = SUBLANES

    @pl.when(c == 0)
    def _():
        hn_ref[...] = _rms(x_ref[...], gpre_ref[...]).astype(BF16)
        o_ref[...] = jnp.zeros(o_ref.shape, F32)

    slot = i % 2

    @pl.when(i % blocks_per_seq == 0)
    def _():
        ha_ref[1 - slot, c] = jnp.zeros(ha_ref.shape[2:], F32)
        hv_ref[1 - slot, c] = jnp.zeros(hv_ref.shape[2:], F32)

    ua_ref[0:hdr, :] = ha_ref[1 - slot, c]
    uv_ref[0:hdr, :] = hv_ref[1 - slot, c]
    wa = wua_ref[...].astype(BF16)
    wv = wuv_ref[...].astype(BF16)
    wd = wd_ref[...].astype(BF16)

    def conv(buf_ref, w_ref, b_ref, r0):
        return (b_ref[...] + w_ref[0:1, :] * buf_ref[r0 + hdr - 2:r0 + hdr - 2 + rc, :]
                + w_ref[1:2, :] * buf_ref[r0 + hdr - 1:r0 + hdr - 1 + rc, :]
                + w_ref[2:3, :] * buf_ref[r0 + hdr:r0 + hdr + rc, :])

    def up_proj(k):
        hk = hn_ref[k * rm:(k + 1) * rm, :]
        ua_ref[hdr + k * rm:hdr + (k + 1) * rm, :] = jnp.dot(hk, wa, preferred_element_type=F32)
        uv_ref[hdr + k * rm:hdr + (k + 1) * rm, :] = jnp.dot(hk, wv, preferred_element_type=F32)

    ngroups = tm // rm
    up_proj(0)
    for k in range(ngroups):
        if k + 1 < ngroups:
            up_proj(k + 1)
        rows = slice(k * rm, (k + 1) * rm)
        for r0 in range(k * rm, (k + 1) * rm, rc):
            ca = conv(ua_ref, cwa_ref, cba_ref, r0)
            cv = conv(uv_ref, cwv_ref, cbv_ref, r0)
            act_ref[r0:r0 + rc, :] = (jax.nn.silu(ca) * cv).astype(BF16)
        o_ref[rows, :] += jnp.dot(act_ref[rows, :], wd, preferred_element_type=F32)

    ta = ua_ref[tm:tm + hdr, :]
    tv = uv_ref[tm:tm + hdr, :]
    ha_ref[slot, c] = ta
    hv_ref[slot, c] = tv
    sa_ref[c] = ta[hdr - 2:hdr, :]
    sv_ref[c] = tv[hdr - 2:hdr, :]

    @pl.when(c == nc - 1)
    def _():
        o_ref[...] = x_ref[...] + _rms(o_ref[...], gpost_ref[...])


FFN_ROW_CHUNK = 64
FFN_ROW_GROUP = 512
FFN_VMEM_LIMIT = V7X_VMEM_BYTES - 4 * 1024 * 1024


def ffn_prompt(x, layer, g_pre, w_up, w_layer, conv_w, conv_b, w_down, g_post, batch, tm, tf):
    M, D = x.shape
    nc = D_FF // tf
    bps = SEQ // tm
    conv_b3 = conv_b.reshape(DEPTH, 1, 2 * D_FF)
    y, sa, sv = pl.pallas_call(
        functools.partial(_ffn_prompt_kernel, nc=nc, blocks_per_seq=bps, rc=FFN_ROW_CHUNK,
                          rm=FFN_ROW_GROUP),
        grid=(M // tm, nc),
        in_specs=[
            pl.BlockSpec((tm, D), lambda i, c: (i, 0)),
            pl.BlockSpec((None, 1, D), lambda i, c: (layer, 0, 0)),
            pl.BlockSpec((None, D, tf), lambda i, c: (w_layer, 0, c)),
            pl.BlockSpec((None, D, tf), lambda i, c: (w_layer, 0, nc + c)),
            pl.BlockSpec((None, 3, tf), lambda i, c: (layer, 0, c)),
            pl.BlockSpec((None, 3, tf), lambda i, c: (layer, 0, nc + c)),
            pl.BlockSpec((None, 1, tf), lambda i, c: (layer, 0, c)),
            pl.BlockSpec((None, 1, tf), lambda i, c: (layer, 0, nc + c)),
            pl.BlockSpec((None, tf, D), lambda i, c: (w_layer, c, 0)),
            pl.BlockSpec((None, 1, D), lambda i, c: (layer, 0, 0)),
        ],
        out_specs=[
            _resident((tm, D), lambda i, c: (i, 0)),
            pl.BlockSpec((None, nc, 2, tf), lambda i, c: (i // bps, 0, 0, 0)),
            pl.BlockSpec((None, nc, 2, tf), lambda i, c: (i // bps, 0, 0, 0)),
        ],
        out_shape=[
            jax.ShapeDtypeStruct((M, D), F32),
            jax.ShapeDtypeStruct((batch, nc, 2, tf), F32),
            jax.ShapeDtypeStruct((batch, nc, 2, tf), F32),
        ],
        scratch_shapes=[
            pltpu.VMEM((tm, D), BF16),
            pltpu.VMEM((tm + SUBLANES, tf), F32),
            pltpu.VMEM((tm + SUBLANES, tf), F32),
            pltpu.VMEM((tm, tf), BF16),
            pltpu.VMEM((2, nc, SUBLANES, tf), F32),
            pltpu.VMEM((2, nc, SUBLANES, tf), F32),
        ],
        compiler_params=_params("arbitrary", "arbitrary", vmem=FFN_VMEM_LIMIT),
        name="ffn_prompt",
    )(x, g_pre.reshape(DEPTH, 1, D), w_up, w_up, conv_w, conv_w, conv_b3, conv_b3,
      w_down, g_post.reshape(DEPTH, 1, D))
    sa = sa.transpose(0, 2, 1, 3).reshape(batch, 2, D_FF)
    sv = sv.transpose(0, 2, 1, 3).reshape(batch, 2, D_FF)
    return y, jnp.concatenate([sa, sv], axis=-1)


def _ffn_sample_kernel(x_ref, gpre_ref, wua_ref, wuv_ref, cwa_ref, cwv_ref, cba_ref, cbv_ref,
                       wd_ref, gpost_ref, sta_ref, stv_ref, o_ref, na_ref, nv_ref, hn_ref, *, nc):
    c = pl.program_id(0)
    L = DEC_SEQ

    @pl.when(c == 0)
    def _():
        hn_ref[...] = _rms(x_ref[...], gpre_ref[...]).astype(BF16)

    hn = hn_ref[...]
    M = hn.shape[0]
    nb = M // L
    ua = jnp.dot(hn, wua_ref[...], preferred_element_type=F32)
    uv = jnp.dot(hn, wuv_ref[...], preferred_element_type=F32)
    tf = ua.shape[1]
    t = lax.broadcasted_iota(jnp.int32, (nb, L, tf), 1)

    def conv(u, st_ref, w_ref, b_ref):
        u3 = u.reshape(nb, L, tf)
        st = st_ref[...]
        prev2, prev1 = st[:, 0:1, :], st[:, 1:2, :]
        p1 = jnp.where(t == 0, prev1, pltpu.roll(u3, 1, 1))
        p2 = jnp.where(t == 0, prev2, jnp.where(t == 1, prev1, pltpu.roll(u3, 2, 1)))
        return (b_ref[...] + w_ref[0:1, :] * p2 + w_ref[1:2, :] * p1 + w_ref[2:3, :] * u3,
                u3[:, L - 2:L, :])

    ca, na = conv(ua, sta_ref, cwa_ref, cba_ref)
    cv, nv = conv(uv, stv_ref, cwv_ref, cbv_ref)
    na_ref[...] = na
    nv_ref[...] = nv
    act = (jax.nn.silu(ca) * cv).astype(BF16).reshape(M, tf)
    part = jnp.dot(act, wd_ref[...], preferred_element_type=F32)

    @pl.when(c == 0)
    def _():
        o_ref[...] = part

    @pl.when(c > 0)
    def _():
        o_ref[...] += part

    @pl.when(c == nc - 1)
    def _():
        o_ref[...] = x_ref[...] + _rms(o_ref[...], gpost_ref[...])


def ffn_sample(x, layer, g_pre, w_up, w_layer, conv_w, conv_b, w_down, g_post, conv_state, tf):
    M, D = x.shape
    nbatch = M // DEC_SEQ
    nc = D_FF // tf
    conv_b3 = conv_b.reshape(DEPTH, 1, 2 * D_FF)
    y, na, nv = pl.pallas_call(
        functools.partial(_ffn_sample_kernel, nc=nc),
        grid=(nc,),
        in_specs=[
            pl.BlockSpec((M, D), lambda c: (0, 0)),
            pl.BlockSpec((None, 1, D), lambda c: (layer, 0, 0)),
            pl.BlockSpec((None, D, tf), lambda c: (w_layer, 0, c)),
            pl.BlockSpec((None, D, tf), lambda c: (w_layer, 0, nc + c)),
            pl.BlockSpec((None, 3, tf), lambda c: (layer, 0, c)),
            pl.BlockSpec((None, 3, tf), lambda c: (layer, 0, nc + c)),
            pl.BlockSpec((None, 1, tf), lambda c: (layer, 0, c)),
            pl.BlockSpec((None, 1, tf), lambda c: (layer, 0, nc + c)),
            pl.BlockSpec((None, tf, D), lambda c: (w_layer, c, 0)),
            pl.BlockSpec((None, 1, D), lambda c: (layer, 0, 0)),
            pl.BlockSpec((None, nbatch, 2, tf), lambda c: (layer, 0, 0, c)),
            pl.BlockSpec((None, nbatch, 2, tf), lambda c: (layer, 0, 0, nc + c)),
        ],
        out_specs=[
            pl.BlockSpec((M, D), lambda c: (0, 0)),
            pl.BlockSpec((nbatch, 2, tf), lambda c: (0, 0, c)),
            pl.BlockSpec((nbatch, 2, tf), lambda c: (0, 0, c)),
        ],
        out_shape=[
            jax.ShapeDtypeStruct((M, D), F32),
            jax.ShapeDtypeStruct((nbatch, 2, D_FF), F32),
            jax.ShapeDtypeStruct((nbatch, 2, D_FF), F32),
        ],
        scratch_shapes=[pltpu.VMEM((M, D), BF16)],
        compiler_params=_params("arbitrary"),
        name="ffn_sample",
    )(x, g_pre.reshape(DEPTH, 1, D), w_up, w_up, conv_w, conv_w, conv_b3, conv_b3,
      w_down, g_post.reshape(DEPTH, 1, D), conv_state, conv_state)
    return y, jnp.concatenate([na, nv], axis=-1)


def kernel(x_prompt, x_sample, mem_prompt, cache_swa_k, cache_swa_v, state_ssm_re, state_ssm_im, state_ffn_conv, cache_mem_k, cache_mem_v, g_mix_pre, g_mix_post, w_qkv, w_attn_o, attn_sinks, w_ssm_in, ssm_lambda_re, ssm_lambda_im, ssm_log_step, ssm_b_re, ssm_b_im, ssm_c_re, ssm_c_im, ssm_d, w_ssm_glu, g_x_pre, g_x_post, g_mem, w_x_q, w_mem_k, w_mem_v, w_x_o, g_ffn_pre, g_ffn_post, w_ffn_up, ffn_conv_w, ffn_conv_b, w_ffn_down):
    B = x_prompt.shape[0]
    SB = x_sample.shape[0]
    xw = N_XHEADS * XHEAD_DIM
    nkc = N_KV_HEADS * HEAD_DIM
    xp = x_prompt.reshape(B * SEQ, D_MODEL)
    xs = x_sample.reshape(SB * DEC_SEQ, D_MODEL)
    mem = mem_prompt.reshape(B * N_MEM, D_MODEL)
    MS = SB * DEC_SEQ
    TMP = 1024
    cos_p, sin_p = _rope_tables(jnp.arange(SEQ))
    cos_s, sin_s = _rope_tables(PAST_LEN + jnp.arange(DEC_SEQ))
    ck_all = cache_swa_k.reshape(cache_swa_k.shape[0], SB, WINDOW, nkc)
    cv_all = cache_swa_v.reshape(cache_swa_v.shape[0], SB, WINDOW, nkc)
    cmk_all = cache_mem_k.reshape(DEPTH, SB, N_MEM, xw)
    cmv_all = cache_mem_v.reshape(DEPTH, SB, N_MEM, xw)
    w_qkv_bf = w_qkv.astype(BF16)
    w_attn_o_bf = w_attn_o.astype(BF16)
    w_ssm_in_bf = w_ssm_in.astype(BF16)
    w_glu_bf = w_ssm_glu.astype(BF16)
    w_x_q_bf = w_x_q.astype(BF16)
    w_x_o_bf = w_x_o.astype(BF16)
    w_mem_kv_bf = jnp.concatenate([w_mem_k, w_mem_v], axis=-1).astype(BF16)
    TD = 512

    swa_kp, swa_vp, swa_ks, swa_vs = [], [], [], []
    ssm_rp, ssm_ip, ssm_rs, ssm_is = [], [], [], []
    conv_p, conv_s, memk_p, memv_p = [], [], [], []
    for i in range(DEPTH):
        j = i // 2
        if i % 2 == 0:
            qkv_p = norm_matmul(xp, g_mix_pre, i, w_qkv_bf, j, TD)
            qkv_s = norm_matmul(xs, g_mix_pre, i, w_qkv_bf, j, MS)
            op, kp, vp, w_up_bf, w_down_bf = swa_prompt(qkv_p, attn_sinks, j, cos_p, sin_p, B,
                                                        w_ffn_up, w_ffn_down)
            sink_rows = jnp.repeat(attn_sinks[j].reshape(N_KV_HEADS, GQA_GROUP), DEC_SEQ,
                                   axis=1)[..., None]
            os_, kn, vn = swa_sample(qkv_s, ck_all, cv_all, j, sink_rows, cos_s, sin_s)
            swa_kp.append(kp.reshape(B, WINDOW, N_KV_HEADS, HEAD_DIM))
            swa_vp.append(vp.reshape(B, WINDOW, N_KV_HEADS, HEAD_DIM))
            swa_ks.append(kn.reshape(SB, WINDOW, N_KV_HEADS, HEAD_DIM))
            swa_vs.append(vn.reshape(SB, WINDOW, N_KV_HEADS, HEAD_DIM))
            xp = matmul_post(op, w_attn_o_bf, j, g_mix_post, i, xp, TD)
            xs = matmul_post(os_, w_attn_o_bf, j, g_mix_post, i, xs, MS)
        else:
            a_re, a_im, b_blk, c_blk = _s5_weights(
                ssm_lambda_re[j], ssm_lambda_im[j], ssm_log_step[j], ssm_b_re[j], ssm_b_im[j],
                ssm_c_re[j], ssm_c_im[j])
            d_row = ssm_d[j].reshape(1, D_MODEL)
            up_ = norm_matmul(xp, g_mix_pre, i, w_ssm_in_bf, j, TD)
            us_ = norm_matmul(xs, g_mix_pre, i, w_ssm_in_bf, j, MS)
            a_tiles = jnp.stack([a_re.reshape(N_SUPER, SUBLANES, LANES),
                                 a_im.reshape(N_SUPER, SUBLANES, LANES)], axis=1
                                ).reshape(2 * N_SUPER, SUBLANES, LANES)
            zero_state = jnp.zeros((B, N_SUPER, SUBLANES, LANES), F32)
            zp, rp, ip = s5_prompt(up_, zero_state, zero_state, a_tiles, b_blk, c_blk, d_row,
                                   B, 256)
            nstate = N_SSM_GROUPS * SSM_STATE
            zs, rn, im_ = s5_sample(us_, state_ssm_re[j].reshape(SB, nstate),
                                    state_ssm_im[j].reshape(SB, nstate),
                                    a_re.reshape(1, nstate), a_im.reshape(1, nstate),
                                    b_blk, c_blk, d_row)
            ssm_rp.append(rp.reshape(B, N_SSM_GROUPS, SSM_STATE))
            ssm_ip.append(ip.reshape(B, N_SSM_GROUPS, SSM_STATE))
            ssm_rs.append(rn.reshape(SB, N_SSM_GROUPS, SSM_STATE))
            ssm_is.append(im_.reshape(SB, N_SSM_GROUPS, SSM_STATE))
            xp = glu_post(zp, w_glu_bf, j, g_mix_post, i, xp, TD)
            xs = glu_post(zs, w_glu_bf, j, g_mix_post, i, xs, MS)
        mk, mv, mk_heads, mv_heads = mem_kv(mem, g_mem, w_mem_kv_bf, i)
        memk_p.append(mk_heads.reshape(B, N_MEM, N_XHEADS, XHEAD_DIM))
        memv_p.append(mv_heads.reshape(B, N_MEM, N_XHEADS, XHEAD_DIM))
        xp = xattn_prompt(xp, i, g_x_pre, w_x_q_bf, mk, mv, w_x_o_bf, g_x_post, B, 2 * TD)
        qs = norm_matmul(xs, g_x_pre, i, w_x_q_bf, i, MS)
        as_ = xattn_sample(qs, cmk_all, cmv_all, i)
        xs = matmul_post(as_, w_x_o_bf, i, g_x_post, i, xs, MS)
        xp, cp = ffn_prompt(xp, i, g_ffn_pre, w_up_bf, i % 2, ffn_conv_w, ffn_conv_b, w_down_bf,
                            g_ffn_post, B, TMP, 512)
        xs, cs = ffn_sample(xs, i, g_ffn_pre, w_up_bf, i % 2, ffn_conv_w, ffn_conv_b, w_down_bf,
                            g_ffn_post, state_ffn_conv, 1408)
        conv_p.append(cp)
        conv_s.append(cs)
    return (xp.reshape(B, SEQ, D_MODEL), xs.reshape(SB, DEC_SEQ, D_MODEL),
            jnp.stack(swa_kp), jnp.stack(swa_vp), jnp.stack(swa_ks), jnp.stack(swa_vs),
            jnp.stack(ssm_rp), jnp.stack(ssm_ip), jnp.stack(ssm_rs), jnp.stack(ssm_is),
            jnp.stack(conv_p), jnp.stack(conv_s), jnp.stack(memk_p), jnp.stack(memv_p))
```
